```python
import math
import jax
import jax.numpy as jnp
from jax import lax
import numpy as np

D_MODEL = 1024
BATCH = 16
SEQ = 2048
DEPTH = 2

N_MIXERS = 2
N_ATTN_LAYERS = (DEPTH + N_MIXERS - 1) // N_MIXERS
N_MLSTM_LAYERS = DEPTH // N_MIXERS
DA_HEADS = 8
DA_HEAD_DIM = 64
DA_WIDTH = DA_HEADS * 2 * DA_HEAD_DIM
DA_QBLOCK = 128
N_BUCKETS = 32
MAX_DISTANCE = 128
ML_INNER = 2 * D_MODEL
ML_HEADS = 4
ML_HEAD_DIM = ML_INNER // ML_HEADS
ML_CONV = 5
ML_QKV_BLOCK = 4
ML_CHUNK = 128
N_GROUPS = 4
EXPERTS_PER_GROUP = 8
N_EXPERTS = N_GROUPS * EXPERTS_PER_GROUP
TOP_K_IN_GROUP = 2
D_EXPERT = D_MODEL // 4
EPS = 1e-6

kernel_name = 'hybrid_diffattn_mlstm_hmoe_encoder'


def rmsnorm(x, g):
    xf = x.astype(jnp.float32)
    y = xf * lax.rsqrt(jnp.mean(xf * xf, axis=-1, keepdims=True) + EPS)
    return (y * g.astype(jnp.float32)).astype(x.dtype)


def t5_bucket(rel):
    nb = N_BUCKETS // 2
    max_exact = nb // 2
    ret = jnp.where(rel > 0, nb, 0)
    n = jnp.abs(rel)
    nf = jnp.maximum(n, 1).astype(jnp.float32)
    large = max_exact + (jnp.log(nf / max_exact) / math.log(MAX_DISTANCE / max_exact)
                         * (nb - max_exact)).astype(jnp.int32)
    large = jnp.minimum(large, nb - 1)
    return ret + jnp.where(n < max_exact, n, large)


def diff_attention(h, w_in, w_out, q_gain, k_gain, lam_q1, lam_k1, lam_q2, lam_k2, subln_g, rel_table, layer_idx):
    B, S, _ = h.shape
    H, dh = DA_HEADS, DA_HEAD_DIM
    q, k, v = jnp.split(h @ w_in, 3, axis=-1)
    q = rmsnorm(q.reshape(B, S, H, 2, dh), q_gain)
    k = rmsnorm(k.reshape(B, S, H, 2, dh), k_gain)
    q = q.transpose(3, 0, 2, 1, 4) * dh ** -0.5
    k = k.transpose(3, 0, 2, 1, 4)
    v = v.reshape(B, S, H, 2 * dh).transpose(0, 2, 1, 3)
    lam_init = 0.8 - 0.6 * math.exp(-0.3 * layer_idx)
    lam = (jnp.exp(jnp.sum(lam_q1 * lam_k1).astype(jnp.float32))
           - jnp.exp(jnp.sum(lam_q2 * lam_k2).astype(jnp.float32)) + lam_init)
    kpos = jnp.arange(S)

    def block(i):
        start = i * DA_QBLOCK
        qb = lax.dynamic_slice_in_dim(q, start, DA_QBLOCK, axis=3)
        qpos = start + jnp.arange(DA_QBLOCK)
        bias = rel_table[t5_bucket(kpos[None, :] - qpos[:, None])]
        bias = bias.transpose(2, 0, 1).astype(jnp.float32)
        s = jnp.einsum('mbhqd,mbhkd->mbhqk', qb, k).astype(jnp.float32) + bias
        a = jax.nn.softmax(s, axis=-1)
        attn = a[0] - lam * a[1]
        return jnp.einsum('bhqk,bhkd->bhqd', attn.astype(v.dtype), v)

    o = lax.map(block, jnp.arange(S // DA_QBLOCK))
    o = o.transpose(1, 0, 3, 2, 4).reshape(B, S, H, 2 * dh)
    o = rmsnorm(o, subln_g) * (1.0 - lam_init)
    return o.reshape(B, S, DA_WIDTH) @ w_out


def conv_centred(x, w, b):
    pad = w.shape[0] // 2
    y = lax.conv_general_dilated(x, w[:, None, :], window_strides=(1,), padding=[(pad, pad)],
                                 dimension_numbers=('NWC', 'WIO', 'NWC'),
                                 feature_group_count=x.shape[-1])
    return y + b


def blockdiag(x, w):
    B, S, C = x.shape
    nb, blk, _ = w.shape
    return jnp.einsum('bsni,nio->bsno', x.reshape(B, S, nb, blk), w).reshape(B, S, C)


def mlstm_scan(q, k, v, i_pre, f_pre):
    B, S, H, dk = q.shape
    dv = v.shape[-1]
    L = ML_CHUNK
    NC = S // L
    f32 = jnp.float32

    def to_chunks(t):
        t = t.reshape((B, NC, L, H) + t.shape[3:])
        return jnp.moveaxis(t, (1, 3), (0, 2))

    qc = to_chunks(q.astype(f32))
    kc = to_chunks(k.astype(f32) * dk ** -0.5)
    vc = to_chunks(v.astype(f32))
    lfc = to_chunks(jax.nn.log_sigmoid(f_pre.astype(f32)))
    igc = to_chunks(i_pre.astype(f32))
    causal = jnp.tril(jnp.ones((L, L), dtype=bool))

    def step(carry, xs):
        C, n, m = carry
        qb, kb, vb, lf, ii = xs
        b = jnp.cumsum(lf, axis=-1)
        dmat = jnp.where(causal, b[..., :, None] - b[..., None, :] + ii[..., None, :], -jnp.inf)
        inter = b + m[..., None]
        m_t = jnp.maximum(inter, jnp.max(dmat, axis=-1))
        w_intra = jnp.exp(dmat - m_t[..., None])
        w_inter = jnp.exp(inter - m_t)
        s = jnp.einsum('bhtd,bhsd->bhts', qb, kb) * w_intra
        num = (w_inter[..., None] * jnp.einsum('bhtk,bhkv->bhtv', qb, C)
               + jnp.einsum('bhts,bhsv->bhtv', s, vb))
        den = w_inter * jnp.einsum('bhtk,bhk->bht', qb, n) + jnp.sum(s, axis=-1)
        h = num / jnp.maximum(jnp.abs(den), jnp.exp(-m_t))[..., None]
        bL = b[..., -1]
        g = bL[..., None] - b + ii
        m_new = jnp.maximum(bL + m, jnp.max(g, axis=-1))
        decay = jnp.exp(bL + m - m_new)
        kw = kb * jnp.exp(g - m_new[..., None])[..., None]
        C_new = decay[..., None, None] * C + jnp.einsum('bhsk,bhsv->bhkv', kw, vb)
        n_new = decay[..., None] * n + jnp.sum(kw, axis=2)
        return (C_new, n_new, m_new), h

    init = (jnp.zeros((B, H, dk, dv), f32), jnp.zeros((B, H, dk), f32), jnp.zeros((B, H), f32))
    _, hs = lax.scan(step, init, (qc, kc, vc, lfc, igc))
    return hs.transpose(1, 0, 3, 2, 4).reshape(B, S, H, dv)


def mlstm_mixer(h, w_in, conv_w, conv_b, wq, wk, wv, gate_w, gate_b, outnorm_g, skip, w_out):
    B, S, _ = h.shape
    H, dv = ML_HEADS, ML_HEAD_DIM
    xm, z = jnp.split(h @ w_in, 2, axis=-1)
    xc = jax.nn.silu(conv_centred(xm, conv_w, conv_b))
    q = blockdiag(xc, wq)
    k = blockdiag(xc, wk)
    v = blockdiag(xm, wv)
    pre = (jnp.einsum('bsc,dcg->dbsg', q, gate_w[:, 0]) + jnp.einsum('bsc,dcg->dbsg', k, gate_w[:, 1])
           + jnp.einsum('bsc,dcg->dbsg', v, gate_w[:, 2]) + gate_b[:, None, None, :])
    heads = lambda t: t.reshape(B, S, H, dv)
    flip = lambda t: jnp.flip(t, axis=1)
    qh, kh, vh = heads(q), heads(k), heads(v)
    h_fwd = mlstm_scan(qh, kh, vh, pre[0, ..., :H], pre[0, ..., H:])
    h_bwd = flip(mlstm_scan(flip(qh), flip(kh), flip(vh), flip(pre[1, ..., :H]), flip(pre[1, ..., H:])))
    hh = rmsnorm((h_fwd + h_bwd).astype(h.dtype), outnorm_g)
    y = (hh.reshape(B, S, ML_INNER) + skip * xc) * jax.nn.silu(z)
    return y @ w_out


def hier_moe(h, w_group, w_router, w1, w3, w2):
    B, S, D = h.shape
    t = h.reshape(B * S, D)
    g_logits = (t @ w_group).astype(jnp.float32)
    g_prob = jax.nn.softmax(g_logits, axis=-1)
    g_sel = jnp.argmax(g_logits, axis=-1)
    p_g = jnp.take_along_axis(g_prob, g_sel[:, None], axis=-1)
    e_logits = (t @ w_router).astype(jnp.float32).reshape(-1, N_GROUPS, EXPERTS_PER_GROUP)
    e_logits = jnp.take_along_axis(e_logits, g_sel[:, None, None], axis=1)[:, 0]
    top_p, top_i = lax.top_k(jax.nn.softmax(e_logits, axis=-1), TOP_K_IN_GROUP)
    weights = p_g * top_p / jnp.sum(top_p, axis=-1, keepdims=True)
    expert_id = g_sel[:, None] * EXPERTS_PER_GROUP + top_i
    gates = jnp.sum(jax.nn.one_hot(expert_id, N_EXPERTS, dtype=jnp.float32) * weights[..., None], axis=1)

    def expert(acc, xs):
        a, b_, c_, gcol = xs
        y = (jax.nn.silu(t @ a) * (t @ b_)) @ c_
        return acc + gcol[:, None].astype(t.dtype) * y, None

    y, _ = lax.scan(expert, jnp.zeros_like(t), (w1, w3, w2, gates.T))
    return y.reshape(B, S, D)


def setup_inputs(seed: int = 0) -> dict:
    key = jax.random.key(seed)
    ks = iter(jax.random.split(key, 48))
    nrm = lambda shape, scale: scale * jax.random.normal(next(ks), shape, jnp.float32)
    NA, NM = N_ATTN_LAYERS, N_MLSTM_LAYERS
    H, dh = DA_HEADS, DA_HEAD_DIM
    MH, dv = ML_HEADS, ML_HEAD_DIM
    nblk = ML_INNER // ML_QKV_BLOCK
    f_bias = jnp.linspace(3.0, 6.0, MH, dtype=jnp.float32)
    ml_gate_b = jnp.concatenate([nrm((NM, 2, MH), 0.1), f_bias + nrm((NM, 2, MH), 0.1)], axis=-1)
    return {
        'x': nrm((BATCH, SEQ, D_MODEL), 1.0),
        'c': nrm((BATCH, D_MODEL), 1.0),
        'rel_table': nrm((N_BUCKETS, H), 0.5),
        'ada_w': nrm((DEPTH, D_MODEL, 6 * D_MODEL), 0.5 * D_MODEL ** -0.5),
        'ada_b': nrm((DEPTH, 6 * D_MODEL), 0.02),
        'norm_mix_g': 1.0 + nrm((DEPTH, D_MODEL), 0.02),
        'norm_ffn_g': 1.0 + nrm((DEPTH, D_MODEL), 0.02),
        'da_w_in': nrm((NA, D_MODEL, 3 * DA_WIDTH), D_MODEL ** -0.5),
        'da_w_out': nrm((NA, DA_WIDTH, D_MODEL), DA_WIDTH ** -0.5),
        'da_q_gain': 1.0 + nrm((NA, dh), 0.02),
        'da_k_gain': 1.0 + nrm((NA, dh), 0.02),
        'da_lam_q1': nrm((NA, dh), 0.1),
        'da_lam_k1': nrm((NA, dh), 0.1),
        'da_lam_q2': nrm((NA, dh), 0.1),
        'da_lam_k2': nrm((NA, dh), 0.1),
        'da_subln_g': 1.0 + nrm((NA, 2 * dh), 0.02),
        'ml_w_in': nrm((NM, D_MODEL, 2 * ML_INNER), D_MODEL ** -0.5),
        'ml_conv_w': nrm((NM, ML_CONV, ML_INNER), ML_CONV ** -0.5),
        'ml_conv_b': nrm((NM, ML_INNER), 0.01),
        'ml_wq': nrm((NM, nblk, ML_QKV_BLOCK, ML_QKV_BLOCK), ML_QKV_BLOCK ** -0.5),
        'ml_wk': nrm((NM, nblk, ML_QKV_BLOCK, ML_QKV_BLOCK), ML_QKV_BLOCK ** -0.5),
        'ml_wv': nrm((NM, nblk, ML_QKV_BLOCK, ML_QKV_BLOCK), ML_QKV_BLOCK ** -0.5),
        'ml_gate_w': nrm((NM, 2, 3, ML_INNER, 2 * MH), (3 * ML_INNER) ** -0.5),
        'ml_gate_b': ml_gate_b,
        'ml_outnorm_g': 1.0 + nrm((NM, MH, dv), 0.02),
        'ml_skip': 1.0 + nrm((NM, ML_INNER), 0.02),
        'ml_w_out': nrm((NM, ML_INNER, D_MODEL), ML_INNER ** -0.5),
        'moe_w_group': nrm((DEPTH, D_MODEL, N_GROUPS), D_MODEL ** -0.5),
        'moe_w_router': nrm((DEPTH, D_MODEL, N_EXPERTS), D_MODEL ** -0.5),
        'moe_w1': nrm((DEPTH, N_EXPERTS, D_MODEL, D_EXPERT), D_MODEL ** -0.5),
        'moe_w3': nrm((DEPTH, N_EXPERTS, D_MODEL, D_EXPERT), D_MODEL ** -0.5),
        'moe_w2': nrm((DEPTH, N_EXPERTS, D_EXPERT, D_MODEL), D_EXPERT ** -0.5),
    }


def reference(x, c, rel_table, ada_w, ada_b, norm_mix_g, norm_ffn_g,
              da_w_in, da_w_out, da_q_gain, da_k_gain, da_lam_q1, da_lam_k1, da_lam_q2, da_lam_k2, da_subln_g,
              ml_w_in, ml_conv_w, ml_conv_b, ml_wq, ml_wk, ml_wv, ml_gate_w, ml_gate_b, ml_outnorm_g, ml_skip, ml_w_out,
              moe_w_group, moe_w_router, moe_w1, moe_w3, moe_w2):
    c_act = jax.nn.silu(c)
    for i in range(DEPTH):
        mod = (c_act @ ada_w[i] + ada_b[i])[:, None, :]
        sh1, sc1, g1, sh2, sc2, g2 = jnp.split(mod, 6, axis=-1)
        hm = rmsnorm(x, norm_mix_g[i]) * (1.0 + sc1) + sh1
        j = i // N_MIXERS
        if i % N_MIXERS == 0:
            y = diff_attention(hm, da_w_in[j], da_w_out[j], da_q_gain[j], da_k_gain[j],
                               da_lam_q1[j], da_lam_k1[j], da_lam_q2[j], da_lam_k2[j], da_subln_g[j],
                               rel_table, i)
        else:
            y = mlstm_mixer(hm, ml_w_in[j], ml_conv_w[j], ml_conv_b[j], ml_wq[j], ml_wk[j], ml_wv[j],
                            ml_gate_w[j], ml_gate_b[j], ml_outnorm_g[j], ml_skip[j], ml_w_out[j])
        x = x + g1 * y
        hf = rmsnorm(x, norm_ffn_g[i]) * (1.0 + sc2) + sh2
        x = x + g2 * hier_moe(hf, moe_w_group[i], moe_w_router[i], moe_w1[i], moe_w3[i], moe_w2[i])
    return x
```

```python
import functools
import math

import jax
import jax.numpy as jnp
from jax import lax
from jax.experimental import pallas as pl
from jax.experimental.pallas import tpu as pltpu

EPS = 1e-6
N_MIXERS = 2
DA_HEADS = 8
DA_HEAD_DIM = 64
N_BUCKETS = 32
MAX_DISTANCE = 128
ML_HEADS = 4
ML_CHUNK = 128
ML_QKV_BLOCK = 4
N_GROUPS = 4
EXPERTS_PER_GROUP = 8
N_EXPERTS = N_GROUPS * EXPERTS_PER_GROUP

LANES = 128
VMEM_LIMIT = 48 * 1024 * 1024
ATTN_Q_TILE = 256
ROW_TILE = 512
MOE_ROW_TILE = 512
MOE_TOKEN_TILE = 256
MLSTM_COL_TILE = 256

F32 = jnp.float32
BF16 = jnp.bfloat16


def _params(*sem):
    return pltpu.CompilerParams(dimension_semantics=sem, vmem_limit_bytes=VMEM_LIMIT)


def _silu(x):
    return x / (1.0 + jnp.exp(-x))


def _log_sigmoid(x):
    return jnp.minimum(x, 0.0) - jnp.log(1.0 + jnp.exp(-jnp.abs(x)))


def _modulated_norm(x, g, sc, sh):
    ms = jnp.mean(x * x, axis=-1, keepdims=True)
    return x * lax.rsqrt(ms + EPS) * g * (1.0 + sc) + sh


def _ada_kernel(c_ref, w_ref, b_ref, o_ref):
    ca = _silu(c_ref[...]).astype(BF16)
    o_ref[0] = jnp.dot(ca, w_ref[0].astype(BF16), preferred_element_type=F32) + b_ref[0]


def _ada_mod(c, ada_w, ada_b):
    depth, d, n = ada_w.shape
    b = c.shape[0]
    tn = 1536
    return pl.pallas_call(
        _ada_kernel,
        grid=(depth, n // tn),
        in_specs=[
            pl.BlockSpec((b, d), lambda i, j: (0, 0)),
            pl.BlockSpec((1, d, tn), lambda i, j: (i, 0, j)),
            pl.BlockSpec((1, 1, tn), lambda i, j: (i, 0, j)),
        ],
        out_specs=pl.BlockSpec((1, b, tn), lambda i, j: (i, 0, j)),
        out_shape=jax.ShapeDtypeStruct((depth, b, n), F32),
        compiler_params=_params("parallel", "parallel"),
    )(c, ada_w, ada_b.reshape(depth, 1, n))


def _prenorm_matmul_kernel(x_ref, g_ref, sc_ref, sh_ref, w_ref, o_ref, *, n_chunk):
    y = _modulated_norm(x_ref[0], g_ref[...], sc_ref[0], sh_ref[0]).astype(BF16)
    n = w_ref.shape[1]
    for n0 in range(0, n, n_chunk):
        o_ref[0, :, n0:n0 + n_chunk] = jnp.dot(
            y, w_ref[:, n0:n0 + n_chunk], preferred_element_type=F32).astype(o_ref.dtype)


def _prenorm_matmul(x, g, sc, sh, w):
    b, s, d = x.shape
    n = w.shape[1]
    ts = min(ROW_TILE, s)
    return pl.pallas_call(
        functools.partial(_prenorm_matmul_kernel, n_chunk=1024),
        grid=(b, s // ts),
        in_specs=[
            pl.BlockSpec((1, ts, d), lambda i, j: (i, j, 0)),
            pl.BlockSpec((1, d), lambda i, j: (0, 0)),
            pl.BlockSpec((1, 1, d), lambda i, j: (i, 0, 0)),
            pl.BlockSpec((1, 1, d), lambda i, j: (i, 0, 0)),
            pl.BlockSpec((d, n), lambda i, j: (0, 0)),
        ],
        out_specs=pl.BlockSpec((1, ts, n), lambda i, j: (i, j, 0)),
        out_shape=jax.ShapeDtypeStruct((b, s, n), BF16),
        compiler_params=_params("parallel", "parallel"),
    )(x, g.reshape(1, d), sc, sh, w)


def _proj_residual_kernel(a_ref, w_ref, x_ref, gate_ref, o_ref):
    y = jnp.dot(a_ref[0], w_ref[...], preferred_element_type=F32)
    o_ref[0] = x_ref[0] + gate_ref[0] * y


def _proj_residual(a, w, x, gate):
    b, s, k = a.shape
    d = w.shape[1]
    ts = min(ROW_TILE, s)
    return pl.pallas_call(
        _proj_residual_kernel,
        grid=(b, s // ts),
        in_specs=[
            pl.BlockSpec((1, ts, k), lambda i, j: (i, j, 0)),
            pl.BlockSpec((k, d), lambda i, j: (0, 0)),
            pl.BlockSpec((1, ts, d), lambda i, j: (i, j, 0)),
            pl.BlockSpec((1, 1, d), lambda i, j: (i, 0, 0)),
        ],
        out_specs=pl.BlockSpec((1, ts, d), lambda i, j: (i, j, 0)),
        out_shape=jax.ShapeDtypeStruct((b, s, d), F32),
        compiler_params=_params("parallel", "parallel"),
    )(a, w, x, gate)


def _t5_bucket(rel):
    nb = N_BUCKETS // 2
    max_exact = nb // 2
    ret = jnp.where(rel > 0, nb, 0)
    n = jnp.abs(rel)
    nf = jnp.maximum(n, 1).astype(F32)
    large = max_exact + (jnp.log(nf / max_exact) / math.log(MAX_DISTANCE / max_exact)
                         * (nb - max_exact)).astype(jnp.int32)
    large = jnp.minimum(large, nb - 1)
    return ret + jnp.where(n < max_exact, n, large)


def _bias_band(rel_table, s, tq):
    h = rel_table.shape[1]
    delta = jnp.arange(-(s - 1), s)
    t = rel_table[_t5_bucket(delta)].T.astype(F32)
    u = jnp.concatenate([t, jnp.zeros((h, 1), F32)], axis=1)
    skew = jnp.tile(u, (1, tq))[:, :tq * (2 * s - 1)].reshape(h, tq, 2 * s - 1)
    band = skew[:, :, tq - 1:tq - 1 + 2 * s - tq]
    nb = (2 * s - tq) // LANES
    return band.reshape(h, tq, nb, LANES).transpose(0, 2, 1, 3)


def _group_rms(x, gain, half):
    sq = x * x
    lane = lax.broadcasted_iota(jnp.int32, x.shape, 1)
    lo = lane < half
    s_lo = jnp.sum(jnp.where(lo, sq, 0.0), axis=-1, keepdims=True)
    s_all = jnp.sum(sq, axis=-1, keepdims=True)
    ms = jnp.where(lo, s_lo, s_all - s_lo) * (1.0 / half)
    return x * lax.rsqrt(ms + EPS) * gain


def _diff_attn_kernel(lam_ref, q_ref, k_ref, v_ref, qg_ref, kg_ref, sg_ref, band_ref, o_ref, kn_ref,
                      *, tq, s, dh, out_scale):
    qi = pl.program_id(2)
    nq = pl.num_programs(2)

    @pl.when(qi == 0)
    def _():
        kn_ref[...] = _group_rms(k_ref[0].astype(F32), kg_ref[...], dh).astype(BF16)

    lam = lam_ref[0]
    q = _group_rms(q_ref[0].astype(F32), qg_ref[...], dh) * dh ** -0.5
    lane = lax.broadcasted_iota(jnp.int32, q.shape, 1)
    q0 = jnp.where(lane < dh, q, 0.0).astype(BF16)
    q1 = jnp.where(lane < dh, 0.0, q).astype(BF16)
    qq = jnp.concatenate([q0, q1], axis=0)
    sc = lax.dot_general(qq, kn_ref[...], (((1,), (1,)), ((), ())), preferred_element_type=F32)
    jb0 = (nq - 1 - qi) * (tq // LANES)
    nkb = s // LANES
    bias = jnp.concatenate([band_ref[0, jb0 + kb] for kb in range(nkb)], axis=1)
    s0 = sc[:tq] + bias
    s1 = sc[tq:] + bias
    p0 = jnp.exp(s0 - jnp.max(s0, axis=-1, keepdims=True))
    p1 = jnp.exp(s1 - jnp.max(s1, axis=-1, keepdims=True))
    r0 = 1.0 / jnp.sum(p0, axis=-1, keepdims=True)
    r1 = lam / jnp.sum(p1, axis=-1, keepdims=True)
    attn = (p0 * r0 - p1 * r1).astype(BF16)
    o = jnp.dot(attn, v_ref[0], preferred_element_type=F32)
    ms = jnp.mean(o * o, axis=-1, keepdims=True)
    o_ref[0] = (o * lax.rsqrt(ms + EPS) * sg_ref[...] * out_scale).astype(o_ref.dtype)


def _diff_attention(qkv, q_gain, k_gain, subln_g, lam, lam_init, rel_table):
    b, s, _ = qkv.shape
    h, dh = DA_HEADS, DA_HEAD_DIM
    tq = min(ATTN_Q_TILE, s)
    band = _bias_band(rel_table, s, tq)
    nb = band.shape[1]
    qg = jnp.tile(q_gain, 2).reshape(1, 2 * dh)
    kg = jnp.tile(k_gain, 2).reshape(1, 2 * dh)
    kern = functools.partial(_diff_attn_kernel, tq=tq, s=s, dh=dh, out_scale=1.0 - lam_init)
    grid_spec = pltpu.PrefetchScalarGridSpec(
        num_scalar_prefetch=1,
        grid=(b, h, s // tq),
        in_specs=[
            pl.BlockSpec((1, tq, 2 * dh), lambda bi, hi, qi, lam: (bi, qi, hi)),
            pl.BlockSpec((1, s, 2 * dh), lambda bi, hi, qi, lam: (bi, 0, h + hi)),
            pl.BlockSpec((1, s, 2 * dh), lambda bi, hi, qi, lam: (bi, 0, 2 * h + hi)),
            pl.BlockSpec((1, 2 * dh), lambda bi, hi, qi, lam: (0, 0)),
            pl.BlockSpec((1, 2 * dh), lambda bi, hi, qi, lam: (0, 0)),
            pl.BlockSpec((1, 2 * dh), lambda bi, hi, qi, lam: (0, 0)),
            pl.BlockSpec((1, nb, tq, LANES), lambda bi, hi, qi, lam: (hi, 0, 0, 0)),
        ],
        out_specs=pl.BlockSpec((1, tq, 2 * dh), lambda bi, hi, qi, lam: (bi, qi, hi)),
        scratch_shapes=[pltpu.VMEM((s, 2 * dh), BF16)],
    )
    return pl.pallas_call(
        kern,
        grid_spec=grid_spec,
        out_shape=jax.ShapeDtypeStruct((b, s, h * 2 * dh), BF16),
        compiler_params=_params("parallel", "parallel", "arbitrary"),
    )(lam.reshape(1), qkv, qkv, qkv, qg, kg, subln_g.reshape(1, 2 * dh), band)


def _mlstm_pre_kernel(xm_ref, cw_ref, cb_ref, wq_ref, wk_ref, wv_ref, gq_ref, gk_ref, gv_ref, gb_ref,
                      q_ref, k_ref, v_ref, xc_ref, pre_ref, pad_ref, *, s, halo):
    j = pl.program_id(1)
    xm = xm_ref[0].astype(F32)
    cbw = xm.shape[1]
    pad_ref[0:8, :] = jnp.zeros((8, cbw), F32)
    pad_ref[8 + s:16 + s, :] = jnp.zeros((8, cbw), F32)
    pad_ref[8:8 + s, :] = xm
    acc = jnp.zeros_like(xm) + cb_ref[...]
    for t in range(2 * halo + 1):
        acc = acc + cw_ref[t:t + 1, :] * pad_ref[8 - halo + t:8 - halo + t + s, :]
    xc = _silu(acc)
    xcb = xc.astype(BF16)
    xmb = xm_ref[0]
    q = jnp.dot(xcb, wq_ref[0], preferred_element_type=F32).astype(BF16)
    k = jnp.dot(xcb, wk_ref[0], preferred_element_type=F32).astype(BF16)
    v = jnp.dot(xmb, wv_ref[0], preferred_element_type=F32).astype(BF16)
    q_ref[0] = q
    k_ref[0] = k
    v_ref[0] = v
    xc_ref[0] = xcb
    part = (jnp.dot(q, gq_ref[...], preferred_element_type=F32)
            + jnp.dot(k, gk_ref[...], preferred_element_type=F32)
            + jnp.dot(v, gv_ref[...], preferred_element_type=F32))

    @pl.when(j == 0)
    def _():
        pre_ref[0] = part + gb_ref[...]

    @pl.when(j != 0)
    def _():
        pre_ref[0] += part


def _blockdiag_dense(w, cb):
    nblk, blk, _ = w.shape
    per = cb // blk
    w4 = w.reshape(nblk // per, per, blk, blk)
    eye = jnp.eye(per, dtype=w.dtype)
    return jnp.einsum('jnio,nm->jnimo', w4, eye).reshape(nblk // per, cb, cb).astype(BF16)


def _mlstm_pre(up, conv_w, conv_b, wq, wk, wv, gate_w, gate_b):
    b, s, c2 = up.shape
    c = c2 // 2
    cb = MLSTM_COL_TILE
    ncb = c // cb
    kw = conv_w.shape[0]
    ng = gate_w.shape[0] * gate_w.shape[-1]

    def gate_mat(i):
        g = gate_w[:, i].transpose(1, 0, 2).reshape(c, ng)
        return jnp.pad(g, ((0, 0), (0, LANES - ng))).astype(BF16)

    gb = jnp.pad(gate_b.reshape(1, ng), ((0, 0), (0, LANES - ng)))
    col = lambda i, j: (i, 0, j)
    out_bf = jax.ShapeDtypeStruct((b, s, c), BF16)
    return pl.pallas_call(
        functools.partial(_mlstm_pre_kernel, s=s, halo=kw // 2),
        grid=(b, ncb),
        in_specs=[
            pl.BlockSpec((1, s, cb), col),
            pl.BlockSpec((kw, cb), lambda i, j: (0, j)),
            pl.BlockSpec((1, cb), lambda i, j: (0, j)),
            pl.BlockSpec((1, cb, cb), lambda i, j: (j, 0, 0)),
            pl.BlockSpec((1, cb, cb), lambda i, j: (j, 0, 0)),
            pl.BlockSpec((1, cb, cb), lambda i, j: (j, 0, 0)),
            pl.BlockSpec((cb, LANES), lambda i, j: (j, 0)),
            pl.BlockSpec((cb, LANES), lambda i, j: (j, 0)),
            pl.BlockSpec((cb, LANES), lambda i, j: (j, 0)),
            pl.BlockSpec((1, LANES), lambda i, j: (0, 0)),
        ],
        out_specs=[
            pl.BlockSpec((1, s, cb), col),
            pl.BlockSpec((1, s, cb), col),
            pl.BlockSpec((1, s, cb), col),
            pl.BlockSpec((1, s, cb), col),
            pl.BlockSpec((1, s, LANES), lambda i, j: (i, 0, 0)),
        ],
        out_shape=[out_bf, out_bf, out_bf, out_bf, jax.ShapeDtypeStruct((b, s, LANES), F32)],
        scratch_shapes=[pltpu.VMEM((s + 16, cb), F32)],
        compiler_params=_params("parallel", "arbitrary"),
    )(up, conv_w, conv_b.reshape(1, c), _blockdiag_dense(wq, cb), _blockdiag_dense(wk, cb),
      _blockdiag_dense(wv, cb), gate_mat(0), gate_mat(1), gate_mat(2), gb)


def _mlstm_scan_kernel(q_ref, k_ref, v_ref, gcol_ref, grow_ref, og_ref, o_ref,
                       cf_ref, cbk_ref, hf_ref, hb_ref, *, L, nc, dk):
    scale = dk ** -0.5
    row = lax.broadcasted_iota(jnp.int32, (L, L), 0)
    coli = lax.broadcasted_iota(jnp.int32, (L, L), 1)
    lower = coli <= row
    upper = coli >= row
    cf_ref[...] = jnp.zeros_like(cf_ref)
    cbk_ref[...] = jnp.zeros_like(cbk_ref)

    def chunk_step(c, c_ref, h_ref, n, m, fwd):
        mask, mask_t = (lower, upper) if fwd else (upper, lower)
        gi = 0 if fwd else 2
        sl = pl.ds(pl.multiple_of(c * L, L), L)
        qb = q_ref[0, sl, :]
        kb = k_ref[0, sl, :]
        vb = v_ref[0, sl, :]
        gc = gcol_ref[0, 0, c]
        gr = grow_ref[0, 0, c]
        ii_col = gc[:, gi:gi + 1]
        lf_col = _log_sigmoid(gc[:, gi + 1:gi + 2])
        ii_row = gr[gi:gi + 1, :]
        lf_row = _log_sigmoid(gr[gi + 1:gi + 2, :])
        b_col = jnp.sum(jnp.where(mask, lf_row, 0.0), axis=1, keepdims=True)
        b_row = jnp.sum(jnp.where(mask_t, lf_col, 0.0), axis=0, keepdims=True)
        dmat = jnp.where(mask, b_col - b_row + ii_row, -jnp.inf)
        inter = b_col + m
        m_t = jnp.maximum(inter, jnp.max(dmat, axis=1, keepdims=True))
        w_intra = jnp.exp(dmat - m_t)
        w_inter = jnp.exp(inter - m_t)
        sqk = lax.dot_general(qb, kb, (((1,), (1,)), ((), ())), preferred_element_type=F32)
        sw = sqk * scale * w_intra
        cmat = c_ref[...]
        num = (w_inter * jnp.dot(qb, cmat.astype(BF16), preferred_element_type=F32)
               + jnp.dot(sw.astype(BF16), vb, preferred_element_type=F32))
        den = (w_inter * jnp.sum(qb.astype(F32) * n, axis=1, keepdims=True)
               + jnp.sum(sw, axis=1, keepdims=True))
        h_ref[sl, :] = num / jnp.maximum(jnp.abs(den), jnp.exp(-m_t))
        b_tot = jnp.sum(lf_row, axis=1, keepdims=True)
        g_col = b_tot - b_col + ii_col
        m_new = jnp.maximum(b_tot + m, jnp.max(g_col, axis=0, keepdims=True))
        decay = jnp.exp(b_tot + m - m_new)
        kw = kb.astype(F32) * scale * jnp.exp(g_col - m_new)
        c_ref[...] = decay * cmat + lax.dot_general(
            kw.astype(BF16), vb, (((0,), (0,)), ((), ())), preferred_element_type=F32)
        n_new = decay * n + jnp.sum(kw, axis=0, keepdims=True)
        return n_new, m_new

    def body(i, carry):
        nf, mf, nb_, mb = carry
        nf, mf = chunk_step(i, cf_ref, hf_ref, nf, mf, True)
        nb_, mb = chunk_step(nc - 1 - i, cbk_ref, hb_ref, nb_, mb, False)
        return nf, mf, nb_, mb

    z_n = jnp.zeros((1, dk), F32)
    z_m = jnp.zeros((1, 1), F32)
    lax.fori_loop(0, nc, body, (z_n, z_m, z_n, z_m))

    def finish(c, carry):
        sl = pl.ds(pl.multiple_of(c * L, L), L)
        hs = hf_ref[sl, :] + hb_ref[sl, :]
        ms = jnp.mean(hs * hs, axis=-1, keepdims=True)
        o_ref[0, sl, :] = (hs * lax.rsqrt(ms + EPS) * og_ref[0]).astype(o_ref.dtype)
        return carry

    lax.fori_loop(0, nc, finish, 0)


def _mlstm_scan(q, k, v, pre, outnorm_g):
    b, s, c = q.shape
    hh = ML_HEADS
    dk = c // hh
    L = min(ML_CHUNK, s)
    nc = s // L
    idx = jnp.array([[0 * 2 * hh + h, 0 * 2 * hh + hh + h, 2 * hh + h, 2 * hh + hh + h] for h in range(hh)])
    g = pre[:, :, idx]
    gcol = g.transpose(0, 2, 1, 3).reshape(b, hh, nc, L, 4)
    grow = gcol.transpose(0, 1, 2, 4, 3)
    head = lambda i, j: (i, 0, j)
    return pl.pallas_call(
        functools.partial(_mlstm_scan_kernel, L=L, nc=nc, dk=dk),
        grid=(b, hh),
        in_specs=[
            pl.BlockSpec((1, s, dk), head),
            pl.BlockSpec((1, s, dk), head),
            pl.BlockSpec((1, s, dk), head),
            pl.BlockSpec((1, 1, nc, L, 4), lambda i, j: (i, j, 0, 0, 0)),
            pl.BlockSpec((1, 1, nc, 4, L), lambda i, j: (i, j, 0, 0, 0)),
            pl.BlockSpec((1, 1, dk), lambda i, j: (j, 0, 0)),
        ],
        out_specs=pl.BlockSpec((1, s, dk), head),
        out_shape=jax.ShapeDtypeStruct((b, s, c), BF16),
        scratch_shapes=[pltpu.VMEM((dk, dk), F32), pltpu.VMEM((dk, dk), F32),
                        pltpu.VMEM((s, dk), F32), pltpu.VMEM((s, dk), F32)],
        compiler_params=_params("parallel", "parallel"),
    )(q, k, v, gcol, grow, outnorm_g.reshape(hh, 1, dk))


def _mlstm_out_kernel(hn_ref, xc_ref, z_ref, skip_ref, w_ref, x_ref, gate_ref, o_ref):
    a = (hn_ref[0].astype(F32) + skip_ref[...] * xc_ref[0].astype(F32)) * _silu(z_ref[0].astype(F32))
    y = jnp.dot(a.astype(BF16), w_ref[...], preferred_element_type=F32)
    o_ref[0] = x_ref[0] + gate_ref[0] * y


def _mlstm_out(hn, xc, up, skip, w, x, gate):
    b, s, c = hn.shape
    d = w.shape[1]
    ts = min(ROW_TILE, s)
    row = lambda i, j: (i, j, 0)
    return pl.pallas_call(
        _mlstm_out_kernel,
        grid=(b, s // ts),
        in_specs=[
            pl.BlockSpec((1, ts, c), row),
            pl.BlockSpec((1, ts, c), row),
            pl.BlockSpec((1, ts, c), lambda i, j: (i, j, 1)),
            pl.BlockSpec((1, c), lambda i, j: (0, 0)),
            pl.BlockSpec((c, d), lambda i, j: (0, 0)),
            pl.BlockSpec((1, ts, d), row),
            pl.BlockSpec((1, 1, d), lambda i, j: (i, 0, 0)),
        ],
        out_specs=pl.BlockSpec((1, ts, d), row),
        out_shape=jax.ShapeDtypeStruct((b, s, d), F32),
        compiler_params=_params("parallel", "parallel"),
    )(hn, xc, up, skip.reshape(1, c), w, x, gate)


def _router_kernel(x_ref, g_ref, sc_ref, sh_ref, w_ref, route_ref, wgt_ref, cnt_ref, run_ref):
    first = (pl.program_id(0) == 0) & (pl.program_id(1) == 0)

    @pl.when(first)
    def _():
        run_ref[...] = jnp.zeros_like(run_ref)

    hf = _modulated_norm(x_ref[0], g_ref[...], sc_ref[0], sh_ref[0])
    logits = jnp.dot(hf, w_ref[...], preferred_element_type=F32, precision=lax.Precision.HIGHEST)
    tm = logits.shape[0]
    lane = lax.broadcasted_iota(jnp.int32, logits.shape, 1).astype(F32)
    neg = -jnp.inf
    is_g = lane < N_GROUPS
    gl = jnp.where(is_g, logits, neg)
    gmax = jnp.max(gl, axis=-1, keepdims=True)
    g_sel = jnp.min(jnp.where(gl == gmax, lane, float(LANES)), axis=-1, keepdims=True)
    p_g = 1.0 / jnp.sum(jnp.where(is_g, jnp.exp(gl - gmax), 0.0), axis=-1, keepdims=True)
    e_lane = lane - N_GROUPS
    in_grp = (e_lane >= g_sel * EXPERTS_PER_GROUP) & (e_lane < (g_sel + 1) * EXPERTS_PER_GROUP)
    el = jnp.where(in_grp, logits, neg)
    emax = jnp.max(el, axis=-1, keepdims=True)
    i1 = jnp.min(jnp.where(el == emax, e_lane, float(LANES)), axis=-1, keepdims=True)
    el2 = jnp.where(e_lane == i1, neg, el)
    emax2 = jnp.max(el2, axis=-1, keepdims=True)
    i2 = jnp.min(jnp.where(el2 == emax2, e_lane, float(LANES)), axis=-1, keepdims=True)
    t2 = jnp.exp(emax2 - emax)
    w1 = p_g / (1.0 + t2)
    w2 = p_g * t2 / (1.0 + t2)
    a = jnp.where(e_lane == i1, 1.0, jnp.where(e_lane == i2, 1.0, 0.0))
    r = lax.broadcasted_iota(jnp.int32, (tm, tm), 0)
    cc = lax.broadcasted_iota(jnp.int32, (tm, tm), 1)
    tri = jnp.where(cc < r, 1.0, 0.0).astype(BF16)
    before = jnp.dot(tri, a.astype(BF16), preferred_element_type=F32) + run_ref[...]
    rank1 = jnp.sum(jnp.where(e_lane == i1, before, 0.0), axis=-1, keepdims=True)
    rank2 = jnp.sum(jnp.where(e_lane == i2, before, 0.0), axis=-1, keepdims=True)
    run_new = run_ref[...] + jnp.sum(a, axis=0, keepdims=True)
    run_ref[...] = run_new
    cnt_ref[...] = run_new.astype(jnp.int32)
    route = jnp.where(lane == 0, i1, jnp.where(lane == 1, i2, jnp.where(
        lane == 2, rank1, jnp.where(lane == 3, rank2, 0.0))))
    route_ref[0] = route.astype(jnp.int32)
    wgt_ref[0] = jnp.where(lane == 0, w1, jnp.where(lane == 1, w2, 0.0))


def _router(x, g, sc, sh, w_group, w_router):
    b, s, d = x.shape
    ts = min(ROW_TILE, s)
    w = jnp.concatenate([w_group, w_router], axis=1)
    w = jnp.pad(w, ((0, 0), (0, LANES - w.shape[1])))
    row = lambda i, j: (i, j, 0)
    return pl.pallas_call(
        _router_kernel,
        grid=(b, s // ts),
        in_specs=[
            pl.BlockSpec((1, ts, d), row),
            pl.BlockSpec((1, d), lambda i, j: (0, 0)),
            pl.BlockSpec((1, 1, d), lambda i, j: (i, 0, 0)),
            pl.BlockSpec((1, 1, d), lambda i, j: (i, 0, 0)),
            pl.BlockSpec((d, LANES), lambda i, j: (0, 0)),
        ],
        out_specs=[
            pl.BlockSpec((1, ts, LANES), row),
            pl.BlockSpec((1, ts, LANES), row),
            pl.BlockSpec((1, LANES), lambda i, j: (0, 0)),
        ],
        out_shape=[jax.ShapeDtypeStruct((b, s, LANES), jnp.int32),
                   jax.ShapeDtypeStruct((b, s, LANES), F32),
                   jax.ShapeDtypeStruct((1, LANES), jnp.int32)],
        scratch_shapes=[pltpu.VMEM((1, LANES), F32)],
        compiler_params=_params("arbitrary", "arbitrary"),
    )(x, g.reshape(1, d), sc, sh, w)


def _row_copy(src, src_row, dst, dst_row, sem):
    return pltpu.make_async_copy(src.at[pl.ds(src_row, 1)], dst.at[pl.ds(dst_row, 1)], sem)


def _dispatch_kernel(dest_ref, x_ref, g_ref, sc_ref, sh_ref, init_ref, xs_ref, buf_ref, sem):
    del init_ref
    tm = buf_ref.shape[0]
    buf_ref[...] = _modulated_norm(x_ref[0], g_ref[...], sc_ref[0], sh_ref[0])

    def issue(r, carry):
        _row_copy(buf_ref, r, xs_ref, dest_ref[0, 0, 2 * r], sem.at[0]).start()
        _row_copy(buf_ref, r, xs_ref, dest_ref[0, 0, 2 * r + 1], sem.at[1]).start()
        return carry

    lax.fori_loop(0, tm, issue, 0)

    def drain(r, carry):
        _row_copy(buf_ref, r, xs_ref, dest_ref[0, 0, 2 * r], sem.at[0]).wait()
        _row_copy(buf_ref, r, xs_ref, dest_ref[0, 0, 2 * r + 1], sem.at[1]).wait()
        return carry

    lax.fori_loop(0, tm, drain, 0)


def _dispatch(x, g, sc, sh, dest, n_rows):
    b, s, d = x.shape
    tm = min(MOE_TOKEN_TILE, s)
    nt = s // tm
    dest3 = dest.reshape(b * nt, 1, 2 * tm)
    row = lambda i, j: (i, j, 0)
    return pl.pallas_call(
        _dispatch_kernel,
        grid=(b, nt),
        in_specs=[
            pl.BlockSpec((1, 1, 2 * tm), lambda i, j: (i * nt + j, 0, 0), memory_space=pltpu.SMEM),
            pl.BlockSpec((1, tm, d), row),
            pl.BlockSpec((1, d), lambda i, j: (0, 0)),
            pl.BlockSpec((1, 1, d), lambda i, j: (i, 0, 0)),
            pl.BlockSpec((1, 1, d), lambda i, j: (i, 0, 0)),
            pl.BlockSpec(memory_space=pl.ANY),
        ],
        out_specs=pl.BlockSpec(memory_space=pl.ANY),
        out_shape=jax.ShapeDtypeStruct((n_rows, d), F32),
        scratch_shapes=[pltpu.VMEM((tm, d), F32), pltpu.SemaphoreType.DMA((2,))],
        input_output_aliases={5: 0},
        compiler_params=_params("arbitrary", "arbitrary"),
    )(dest3, x, g.reshape(1, d), sc, sh, jnp.zeros((n_rows, d), F32))


def _expert_kernel(te_ref, nu_ref, xs_ref, w1_ref, w3_ref, w2_ref, ys_ref):
    i = pl.program_id(0)

    @pl.when(i < nu_ref[0])
    def _():
        xb = xs_ref[...].astype(BF16)
        a = jnp.dot(xb, w1_ref[0], preferred_element_type=F32)
        bb = jnp.dot(xb, w3_ref[0], preferred_element_type=F32)
        hmid = (_silu(a) * bb).astype(BF16)
        ys_ref[...] = jnp.dot(hmid, w2_ref[0], preferred_element_type=F32)

    @pl.when(i >= nu_ref[0])
    def _():
        ys_ref[...] = jnp.zeros_like(ys_ref)


def _expert_ffn(xs, tile_expert, n_used, w1, w3, w2):
    p, d = xs.shape
    de = w1.shape[2]
    tm = MOE_ROW_TILE
    grid_spec = pltpu.PrefetchScalarGridSpec(
        num_scalar_prefetch=2,
        grid=(p // tm,),
        in_specs=[
            pl.BlockSpec((tm, d), lambda i, te, nu: (i, 0)),
            pl.BlockSpec((1, d, de), lambda i, te, nu: (te[i], 0, 0)),
            pl.BlockSpec((1, d, de), lambda i, te, nu: (te[i], 0, 0)),
            pl.BlockSpec((1, de, d), lambda i, te, nu: (te[i], 0, 0)),
        ],
        out_specs=pl.BlockSpec((tm, d), lambda i, te, nu: (i, 0)),
    )
    return pl.pallas_call(
        _expert_kernel,
        grid_spec=grid_spec,
        out_shape=jax.ShapeDtypeStruct((p, d), F32),
        compiler_params=_params("arbitrary"),
    )(tile_expert, n_used, xs, w1, w3, w2)


def _combine_kernel(dest_ref, ys_ref, wgt_ref, x_ref, gate_ref, o_ref, y0_ref, y1_ref, sem):
    tm = y0_ref.shape[0]

    def issue(r, carry):
        _row_copy(ys_ref, dest_ref[0, 0, 2 * r], y0_ref, r, sem.at[0]).start()
        _row_copy(ys_ref, dest_ref[0, 0, 2 * r + 1], y1_ref, r, sem.at[1]).start()
        return carry

    lax.fori_loop(0, tm, issue, 0)

    def drain(r, carry):
        _row_copy(ys_ref, dest_ref[0, 0, 2 * r], y0_ref, r, sem.at[0]).wait()
        _row_copy(ys_ref, dest_ref[0, 0, 2 * r + 1], y1_ref, r, sem.at[1]).wait()
        return carry

    lax.fori_loop(0, tm, drain, 0)
    wgt = wgt_ref[0]
    y = wgt[:, 0:1] * y0_ref[...] + wgt[:, 1:2] * y1_ref[...]
    o_ref[0] = x_ref[0] + gate_ref[0] * y


def _combine(ys, dest, wgt, x, gate):
    b, s, d = x.shape
    tm = min(MOE_TOKEN_TILE, s)
    nt = s // tm
    dest3 = dest.reshape(b * nt, 1, 2 * tm)
    row = lambda i, j: (i, j, 0)
    return pl.pallas_call(
        _combine_kernel,
        grid=(b, nt),
        in_specs=[
            pl.BlockSpec((1, 1, 2 * tm), lambda i, j: (i * nt + j, 0, 0), memory_space=pltpu.SMEM),
            pl.BlockSpec(memory_space=pl.ANY),
            pl.BlockSpec((1, tm, LANES), row),
            pl.BlockSpec((1, tm, d), row),
            pl.BlockSpec((1, 1, d), lambda i, j: (i, 0, 0)),
        ],
        out_specs=pl.BlockSpec((1, tm, d), row),
        out_shape=jax.ShapeDtypeStruct((b, s, d), F32),
        scratch_shapes=[pltpu.VMEM((tm, d), F32), pltpu.VMEM((tm, d), F32),
                        pltpu.SemaphoreType.DMA((2,))],
        compiler_params=_params("arbitrary", "arbitrary"),
    )(dest3, ys, wgt, x, gate)


def _hier_moe_residual(x, g, sc, sh, gate, w_group, w_router, w1, w3, w2):
    b, s, d = x.shape
    t = b * s
    route, wgt, cnt = _router(x, g, sc, sh, w_group, w_router)
    tm = MOE_ROW_TILE
    counts = cnt[0, N_GROUPS:N_GROUPS + N_EXPERTS]
    tiles = (counts + tm - 1) // tm
    tile_end = jnp.cumsum(tiles)
    base = (tile_end - tiles) * tm
    n_tiles = (2 * t) // tm + N_EXPERTS
    tile_expert = jnp.minimum(
        jnp.searchsorted(tile_end, jnp.arange(n_tiles), side='right'), N_EXPERTS - 1).astype(jnp.int32)
    n_used = tile_end[-1:].astype(jnp.int32)
    dest = (base[route[..., 0:2]] + route[..., 2:4]).astype(jnp.int32)
    xs = _dispatch(x, g, sc, sh, dest, n_tiles * tm)
    ys = _expert_ffn(xs, tile_expert, n_used, w1.astype(BF16), w3.astype(BF16), w2.astype(BF16))
    return _combine(ys, dest, wgt, x, gate)


def kernel(x, c, rel_table, ada_w, ada_b, norm_mix_g, norm_ffn_g, da_w_in, da_w_out, da_q_gain, da_k_gain, da_lam_q1, da_lam_k1, da_lam_q2, da_lam_k2, da_subln_g, ml_w_in, ml_conv_w, ml_conv_b, ml_wq, ml_wk, ml_wv, ml_gate_w, ml_gate_b, ml_outnorm_g, ml_skip, ml_w_out, moe_w_group, moe_w_router, moe_w1, moe_w3, moe_w2):
    depth = ada_w.shape[0]
    d = x.shape[-1]
    mod = _ada_mod(c, ada_w, ada_b)
    for i in range(depth):
        sh1, sc1, g1, sh2, sc2, g2 = [mod[i, :, None, k * d:(k + 1) * d] for k in range(6)]
        j = i // N_MIXERS
        if i % N_MIXERS == 0:
            qkv = _prenorm_matmul(x, norm_mix_g[i], sc1, sh1, da_w_in[j].astype(BF16))
            lam_init = 0.8 - 0.6 * math.exp(-0.3 * i)
            lam = (jnp.exp(jnp.sum(da_lam_q1[j] * da_lam_k1[j])) - jnp.exp(jnp.sum(da_lam_q2[j] * da_lam_k2[j]))
                   + lam_init).astype(F32)
            o = _diff_attention(qkv, da_q_gain[j], da_k_gain[j], da_subln_g[j], lam, lam_init, rel_table)
            x = _proj_residual(o, da_w_out[j].astype(BF16), x, g1)
        else:
            up = _prenorm_matmul(x, norm_mix_g[i], sc1, sh1, ml_w_in[j].astype(BF16))
            q, k, v, xc, pre = _mlstm_pre(up, ml_conv_w[j], ml_conv_b[j], ml_wq[j], ml_wk[j], ml_wv[j],
                                          ml_gate_w[j], ml_gate_b[j])
            hn = _mlstm_scan(q, k, v, pre, ml_outnorm_g[j])
            x = _mlstm_out(hn, xc, up, ml_skip[j], ml_w_out[j].astype(BF16), x, g1)
        x = _hier_moe_residual(x, norm_ffn_g[i], sc2, sh2, g2, moe_w_group[i], moe_w_router[i],
                               moe_w1[i], moe_w3[i], moe_w2[i])
    return x
```

```python
import functools
import math

import jax
import jax.numpy as jnp
from jax import lax
from jax.experimental import pallas as pl
from jax.experimental.pallas import tpu as pltpu

EPS = 1e-6
N_MIXERS = 2
DA_HEADS = 8
DA_HEAD_DIM = 64
N_BUCKETS = 32
MAX_DISTANCE = 128
ML_HEADS = 4
ML_CHUNK = 128
ML_QKV_BLOCK = 4
N_GROUPS = 4
EXPERTS_PER_GROUP = 8
N_EXPERTS = N_GROUPS * EXPERTS_PER_GROUP

LANES = 128
VMEM_LIMIT = 48 * 1024 * 1024
ATTN_Q_TILE = 256
ROW_TILE = 512
MOE_ROW_TILE = 512
MOE_TOKEN_TILE = 256
MLSTM_COL_TILE = 256

F32 = jnp.float32
BF16 = jnp.bfloat16


def _params(*sem):
    return pltpu.CompilerParams(dimension_semantics=sem, vmem_limit_bytes=VMEM_LIMIT)


def _silu(x):
    return x / (1.0 + jnp.exp(-x))


def _log_sigmoid(x):
    return jnp.minimum(x, 0.0) - jnp.log(1.0 + jnp.exp(-jnp.abs(x)))


def _modulated_norm(x, g, sc, sh):
    ms = jnp.mean(x * x, axis=-1, keepdims=True)
    return x * lax.rsqrt(ms + EPS) * g * (1.0 + sc) + sh


def _ada_kernel(c_ref, w_ref, b_ref, o_ref):
    ca = _silu(c_ref[...]).astype(BF16)
    o_ref[0] = jnp.dot(ca, w_ref[0].astype(BF16), preferred_element_type=F32) + b_ref[0]


def _ada_mod(c, ada_w, ada_b):
    depth, d, n = ada_w.shape
    b = c.shape[0]
    tn = 1536
    return pl.pallas_call(
        _ada_kernel,
        grid=(depth, n // tn),
        in_specs=[
            pl.BlockSpec((b, d), lambda i, j: (0, 0)),
            pl.BlockSpec((1, d, tn), lambda i, j: (i, 0, j)),
            pl.BlockSpec((1, 1, tn), lambda i, j: (i, 0, j)),
        ],
        out_specs=pl.BlockSpec((1, b, tn), lambda i, j: (i, 0, j)),
        out_shape=jax.ShapeDtypeStruct((depth, b, n), F32),
        compiler_params=_params("parallel", "parallel"),
    )(c, ada_w, ada_b.reshape(depth, 1, n))


def _prenorm_matmul_kernel(x_ref, g_ref, sc_ref, sh_ref, w_ref, o_ref, *, n_chunk):
    y = _modulated_norm(x_ref[0], g_ref[...], sc_ref[0], sh_ref[0]).astype(BF16)
    n = w_ref.shape[1]
    for n0 in range(0, n, n_chunk):
        o_ref[0, :, n0:n0 + n_chunk] = jnp.dot(
            y, w_ref[:, n0:n0 + n_chunk], preferred_element_type=F32).astype(o_ref.dtype)


def _prenorm_matmul(x, g, sc, sh, w):
    b, s, d = x.shape
    n = w.shape[1]
    ts = min(ROW_TILE, s)
    return pl.pallas_call(
        functools.partial(_prenorm_matmul_kernel, n_chunk=1024),
        grid=(b, s // ts),
        in_specs=[
            pl.BlockSpec((1, ts, d), lambda i, j: (i, j, 0)),
            pl.BlockSpec((1, d), lambda i, j: (0, 0)),
            pl.BlockSpec((1, 1, d), lambda i, j: (i, 0, 0)),
            pl.BlockSpec((1, 1, d), lambda i, j: (i, 0, 0)),
            pl.BlockSpec((d, n), lambda i, j: (0, 0)),
        ],
        out_specs=pl.BlockSpec((1, ts, n), lambda i, j: (i, j, 0)),
        out_shape=jax.ShapeDtypeStruct((b, s, n), BF16),
        compiler_params=_params("parallel", "parallel"),
    )(x, g.reshape(1, d), sc, sh, w)


def _proj_residual_kernel(a_ref, w_ref, x_ref, gate_ref, o_ref):
    y = jnp.dot(a_ref[0], w_ref[...], preferred_element_type=F32)
    o_ref[0] = x_ref[0] + gate_ref[0] * y


def _proj_residual(a, w, x, gate):
    b, s, k = a.shape
    d = w.shape[1]
    ts = min(ROW_TILE, s)
    return pl.pallas_call(
        _proj_residual_kernel,
        grid=(b, s // ts),
        in_specs=[
            pl.BlockSpec((1, ts, k), lambda i, j: (i, j, 0)),
            pl.BlockSpec((k, d), lambda i, j: (0, 0)),
            pl.BlockSpec((1, ts, d), lambda i, j: (i, j, 0)),
            pl.BlockSpec((1, 1, d), lambda i, j: (i, 0, 0)),
        ],
        out_specs=pl.BlockSpec((1, ts, d), lambda i, j: (i, j, 0)),
        out_shape=jax.ShapeDtypeStruct((b, s, d), F32),
        compiler_params=_params("parallel", "parallel"),
    )(a, w, x, gate)


def _t5_bucket(rel):
    nb = N_BUCKETS // 2
    max_exact = nb // 2
    ret = jnp.where(rel > 0, nb, 0)
    n = jnp.abs(rel)
    nf = jnp.maximum(n, 1).astype(F32)
    large = max_exact + (jnp.log(nf / max_exact) / math.log(MAX_DISTANCE / max_exact)
                         * (nb - max_exact)).astype(jnp.int32)
    large = jnp.minimum(large, nb - 1)
    return ret + jnp.where(n < max_exact, n, large)


def _bias_band_kernel(t_ref, o_ref, *, tq, nb):
    u = jnp.broadcast_to(t_ref[0], (tq, t_ref.shape[2]))
    y = pltpu.roll(u, 1, 1, stride=1, stride_axis=0)
    for jb in range(nb):
        o_ref[0, jb] = y[:, tq + jb * LANES:tq + (jb + 1) * LANES]


def _bias_band(rel_table, s, tq):
    h = rel_table.shape[1]
    delta = jnp.arange(-(s - 1), s)
    t = rel_table[_t5_bucket(delta)].T.astype(F32)
    u = jnp.concatenate([t, jnp.zeros((h, 1), F32)], axis=1).reshape(h, 1, 2 * s)
    nb = (2 * s - tq) // LANES
    return pl.pallas_call(
        functools.partial(_bias_band_kernel, tq=tq, nb=nb),
        grid=(h,),
        in_specs=[pl.BlockSpec((1, 1, 2 * s), lambda i: (i, 0, 0))],
        out_specs=pl.BlockSpec((1, nb, tq, LANES), lambda i: (i, 0, 0, 0)),
        out_shape=jax.ShapeDtypeStruct((h, nb, tq, LANES), F32),
        compiler_params=_params("parallel"),
    )(u)


def _group_rms(x, gain, half):
    sq = x * x
    lane = lax.broadcasted_iota(jnp.int32, x.shape, 1)
    lo = lane < half
    s_lo = jnp.sum(jnp.where(lo, sq, 0.0), axis=-1, keepdims=True)
    s_all = jnp.sum(sq, axis=-1, keepdims=True)
    ms = jnp.where(lo, s_lo, s_all - s_lo) * (1.0 / half)
    return x * lax.rsqrt(ms + EPS) * gain


def _diff_attn_kernel(lam_ref, q_ref, k_ref, v_ref, qg_ref, kg_ref, sg_ref, band_ref, o_ref, kn_ref,
                      *, tq, s, dh, out_scale):
    qi = pl.program_id(2)
    nq = pl.num_programs(2)

    @pl.when(qi == 0)
    def _():
        kn_ref[...] = _group_rms(k_ref[0].astype(F32), kg_ref[...], dh).astype(BF16)

    lam = lam_ref[0]
    q = _group_rms(q_ref[0].astype(F32), qg_ref[...], dh) * dh ** -0.5
    lane = lax.broadcasted_iota(jnp.int32, q.shape, 1)
    q0 = jnp.where(lane < dh, q, 0.0).astype(BF16)
    q1 = jnp.where(lane < dh, 0.0, q).astype(BF16)
    qq = jnp.concatenate([q0, q1], axis=0)
    sc = lax.dot_general(qq, kn_ref[...], (((1,), (1,)), ((), ())), preferred_element_type=F32)
    jb0 = (nq - 1 - qi) * (tq // LANES)
    nkb = s // LANES
    bias = jnp.concatenate([band_ref[0, jb0 + kb] for kb in range(nkb)], axis=1)
    s0 = sc[:tq] + bias
    s1 = sc[tq:] + bias
    p0 = jnp.exp(s0 - jnp.max(s0, axis=-1, keepdims=True))
    p1 = jnp.exp(s1 - jnp.max(s1, axis=-1, keepdims=True))
    r0 = 1.0 / jnp.sum(p0, axis=-1, keepdims=True)
    r1 = lam / jnp.sum(p1, axis=-1, keepdims=True)
    attn = (p0 * r0 - p1 * r1).astype(BF16)
    o = jnp.dot(attn, v_ref[0], preferred_element_type=F32)
    ms = jnp.mean(o * o, axis=-1, keepdims=True)
    o_ref[0] = (o * lax.rsqrt(ms + EPS) * sg_ref[...] * out_scale).astype(o_ref.dtype)


def _diff_attention(qkv, q_gain, k_gain, subln_g, lam, lam_init, rel_table):
    b, s, _ = qkv.shape
    h, dh = DA_HEADS, DA_HEAD_DIM
    tq = min(ATTN_Q_TILE, s)
    band = _bias_band(rel_table, s, tq)
    nb = band.shape[1]
    qg = jnp.tile(q_gain, 2).reshape(1, 2 * dh)
    kg = jnp.tile(k_gain, 2).reshape(1, 2 * dh)
    kern = functools.partial(_diff_attn_kernel, tq=tq, s=s, dh=dh, out_scale=1.0 - lam_init)
    grid_spec = pltpu.PrefetchScalarGridSpec(
        num_scalar_prefetch=1,
        grid=(b, h, s // tq),
        in_specs=[
            pl.BlockSpec((1, tq, 2 * dh), lambda bi, hi, qi, lam: (bi, qi, hi)),
            pl.BlockSpec((1, s, 2 * dh), lambda bi, hi, qi, lam: (bi, 0, h + hi)),
            pl.BlockSpec((1, s, 2 * dh), lambda bi, hi, qi, lam: (bi, 0, 2 * h + hi)),
            pl.BlockSpec((1, 2 * dh), lambda bi, hi, qi, lam: (0, 0)),
            pl.BlockSpec((1, 2 * dh), lambda bi, hi, qi, lam: (0, 0)),
            pl.BlockSpec((1, 2 * dh), lambda bi, hi, qi, lam: (0, 0)),
            pl.BlockSpec((1, nb, tq, LANES), lambda bi, hi, qi, lam: (hi, 0, 0, 0)),
        ],
        out_specs=pl.BlockSpec((1, tq, 2 * dh), lambda bi, hi, qi, lam: (bi, qi, hi)),
        scratch_shapes=[pltpu.VMEM((s, 2 * dh), BF16)],
    )
    return pl.pallas_call(
        kern,
        grid_spec=grid_spec,
        out_shape=jax.ShapeDtypeStruct((b, s, h * 2 * dh), BF16),
        compiler_params=_params("parallel", "parallel", "arbitrary"),
    )(lam.reshape(1), qkv, qkv, qkv, qg, kg, subln_g.reshape(1, 2 * dh), band)


def _mlstm_pre_kernel(xm_ref, cw_ref, cb_ref, wq_ref, wk_ref, wv_ref, gq_ref, gk_ref, gv_ref, gb_ref,
                      q_ref, k_ref, v_ref, xc_ref, pre_ref, pad_ref, *, s, halo):
    j = pl.program_id(1)
    xm = xm_ref[0].astype(F32)
    cbw = xm.shape[1]
    pad_ref[0:8, :] = jnp.zeros((8, cbw), F32)
    pad_ref[8 + s:16 + s, :] = jnp.zeros((8, cbw), F32)
    pad_ref[8:8 + s, :] = xm
    acc = jnp.zeros_like(xm) + cb_ref[...]
    for t in range(2 * halo + 1):
        acc = acc + cw_ref[t:t + 1, :] * pad_ref[8 - halo + t:8 - halo + t + s, :]
    xc = _silu(acc)
    xcb = xc.astype(BF16)
    xmb = xm_ref[0]
    q = jnp.dot(xcb, wq_ref[0], preferred_element_type=F32).astype(BF16)
    k = jnp.dot(xcb, wk_ref[0], preferred_element_type=F32).astype(BF16)
    v = jnp.dot(xmb, wv_ref[0], preferred_element_type=F32).astype(BF16)
    q_ref[0] = q
    k_ref[0] = k
    v_ref[0] = v
    xc_ref[0] = xcb
    part = (jnp.dot(q, gq_ref[...], preferred_element_type=F32)
            + jnp.dot(k, gk_ref[...], preferred_element_type=F32)
            + jnp.dot(v, gv_ref[...], preferred_element_type=F32))

    @pl.when(j == 0)
    def _():
        pre_ref[0] = part + gb_ref[...]

    @pl.when(j != 0)
    def _():
        pre_ref[0] += part


def _blockdiag_dense(w, cb):
    nblk, blk, _ = w.shape
    per = cb // blk
    w4 = w.reshape(nblk // per, per, blk, blk)
    eye = jnp.eye(per, dtype=w.dtype)
    return jnp.einsum('jnio,nm->jnimo', w4, eye).reshape(nblk // per, cb, cb).astype(BF16)


def _mlstm_pre(up, conv_w, conv_b, wq, wk, wv, gate_w, gate_b):
    b, s, c2 = up.shape
    c = c2 // 2
    cb = MLSTM_COL_TILE
    ncb = c // cb
    kw = conv_w.shape[0]
    ng = gate_w.shape[0] * gate_w.shape[-1]

    def gate_mat(i):
        g = gate_w[:, i].transpose(1, 0, 2).reshape(c, ng)
        return jnp.pad(g, ((0, 0), (0, LANES - ng))).astype(BF16)

    gb = jnp.pad(gate_b.reshape(1, ng), ((0, 0), (0, LANES - ng)))
    col = lambda i, j: (i, 0, j)
    out_bf = jax.ShapeDtypeStruct((b, s, c), BF16)
    return pl.pallas_call(
        functools.partial(_mlstm_pre_kernel, s=s, halo=kw // 2),
        grid=(b, ncb),
        in_specs=[
            pl.BlockSpec((1, s, cb), col),
            pl.BlockSpec((kw, cb), lambda i, j: (0, j)),
            pl.BlockSpec((1, cb), lambda i, j: (0, j)),
            pl.BlockSpec((1, cb, cb), lambda i, j: (j, 0, 0)),
            pl.BlockSpec((1, cb, cb), lambda i, j: (j, 0, 0)),
            pl.BlockSpec((1, cb, cb), lambda i, j: (j, 0, 0)),
            pl.BlockSpec((cb, LANES), lambda i, j: (j, 0)),
            pl.BlockSpec((cb, LANES), lambda i, j: (j, 0)),
            pl.BlockSpec((cb, LANES), lambda i, j: (j, 0)),
            pl.BlockSpec((1, LANES), lambda i, j: (0, 0)),
        ],
        out_specs=[
            pl.BlockSpec((1, s, cb), col),
            pl.BlockSpec((1, s, cb), col),
            pl.BlockSpec((1, s, cb), col),
            pl.BlockSpec((1, s, cb), col),
            pl.BlockSpec((1, s, LANES), lambda i, j: (i, 0, 0)),
        ],
        out_shape=[out_bf, out_bf, out_bf, out_bf, jax.ShapeDtypeStruct((b, s, LANES), F32)],
        scratch_shapes=[pltpu.VMEM((s + 16, cb), F32)],
        compiler_params=_params("parallel", "arbitrary"),
    )(up, conv_w, conv_b.reshape(1, c), _blockdiag_dense(wq, cb), _blockdiag_dense(wk, cb),
      _blockdiag_dense(wv, cb), gate_mat(0), gate_mat(1), gate_mat(2), gb)


def _mlstm_scan_kernel(q_ref, k_ref, v_ref, gcol_ref, grow_ref, og_ref, o_ref,
                       cf_ref, cbk_ref, hf_ref, hb_ref, *, L, nc, dk):
    scale = dk ** -0.5
    row = lax.broadcasted_iota(jnp.int32, (L, L), 0)
    coli = lax.broadcasted_iota(jnp.int32, (L, L), 1)
    lower = coli <= row
    upper = coli >= row
    cf_ref[...] = jnp.zeros_like(cf_ref)
    cbk_ref[...] = jnp.zeros_like(cbk_ref)

    def chunk_step(c, c_ref, h_ref, n, m, fwd):
        mask, mask_t = (lower, upper) if fwd else (upper, lower)
        gi = 0 if fwd else 2
        sl = pl.ds(pl.multiple_of(c * L, L), L)
        qb = q_ref[0, sl, :]
        kb = k_ref[0, sl, :]
        vb = v_ref[0, sl, :]
        gc = gcol_ref[0, 0, c]
        gr = grow_ref[0, 0, c]
        ii_col = gc[:, gi:gi + 1]
        lf_col = _log_sigmoid(gc[:, gi + 1:gi + 2])
        ii_row = gr[gi:gi + 1, :]
        lf_row = _log_sigmoid(gr[gi + 1:gi + 2, :])
        b_col = jnp.sum(jnp.where(mask, lf_row, 0.0), axis=1, keepdims=True)
        b_row = jnp.sum(jnp.where(mask_t, lf_col, 0.0), axis=0, keepdims=True)
        dmat = jnp.where(mask, b_col - b_row + ii_row, -jnp.inf)
        inter = b_col + m
        m_t = jnp.maximum(inter, jnp.max(dmat, axis=1, keepdims=True))
        w_intra = jnp.exp(dmat - m_t)
        w_inter = jnp.exp(inter - m_t)
        sqk = lax.dot_general(qb, kb, (((1,), (1,)), ((), ())), preferred_element_type=F32)
        sw = sqk * scale * w_intra
        cmat = c_ref[...]
        num = (w_inter * jnp.dot(qb, cmat.astype(BF16), preferred_element_type=F32)
               + jnp.dot(sw.astype(BF16), vb, preferred_element_type=F32))
        den = (w_inter * jnp.sum(qb.astype(F32) * n, axis=1, keepdims=True)
               + jnp.sum(sw, axis=1, keepdims=True))
        h_ref[sl, :] = num / jnp.maximum(jnp.abs(den), jnp.exp(-m_t))
        b_tot = jnp.sum(lf_row, axis=1, keepdims=True)
        g_col = b_tot - b_col + ii_col
        m_new = jnp.maximum(b_tot + m, jnp.max(g_col, axis=0, keepdims=True))
        decay = jnp.exp(b_tot + m - m_new)
        kw = kb.astype(F32) * scale * jnp.exp(g_col - m_new)
        c_ref[...] = decay * cmat + lax.dot_general(
            kw.astype(BF16), vb, (((0,), (0,)), ((), ())), preferred_element_type=F32)
        n_new = decay * n + jnp.sum(kw, axis=0, keepdims=True)
        return n_new, m_new

    def body(i, carry):
        nf, mf, nb_, mb = carry
        nf, mf = chunk_step(i, cf_ref, hf_ref, nf, mf, True)
        nb_, mb = chunk_step(nc - 1 - i, cbk_ref, hb_ref, nb_, mb, False)
        return nf, mf, nb_, mb

    z_n = jnp.zeros((1, dk), F32)
    z_m = jnp.zeros((1, 1), F32)
    lax.fori_loop(0, nc, body, (z_n, z_m, z_n, z_m))

    def finish(c, carry):
        sl = pl.ds(pl.multiple_of(c * L, L), L)
        hs = hf_ref[sl, :] + hb_ref[sl, :]
        ms = jnp.mean(hs * hs, axis=-1, keepdims=True)
        o_ref[0, sl, :] = (hs * lax.rsqrt(ms + EPS) * og_ref[0]).astype(o_ref.dtype)
        return carry

    lax.fori_loop(0, nc, finish, 0)


def _mlstm_scan(q, k, v, pre, outnorm_g):
    b, s, c = q.shape
    hh = ML_HEADS
    dk = c // hh
    L = min(ML_CHUNK, s)
    nc = s // L
    idx = jnp.array([[0 * 2 * hh + h, 0 * 2 * hh + hh + h, 2 * hh + h, 2 * hh + hh + h] for h in range(hh)])
    g = pre[:, :, idx]
    gcol = g.transpose(0, 2, 1, 3).reshape(b, hh, nc, L, 4)
    grow = gcol.transpose(0, 1, 2, 4, 3)
    head = lambda i, j: (i, 0, j)
    return pl.pallas_call(
        functools.partial(_mlstm_scan_kernel, L=L, nc=nc, dk=dk),
        grid=(b, hh),
        in_specs=[
            pl.BlockSpec((1, s, dk), head),
            pl.BlockSpec((1, s, dk), head),
            pl.BlockSpec((1, s, dk), head),
            pl.BlockSpec((1, 1, nc, L, 4), lambda i, j: (i, j, 0, 0, 0)),
            pl.BlockSpec((1, 1, nc, 4, L), lambda i, j: (i, j, 0, 0, 0)),
            pl.BlockSpec((1, 1, dk), lambda i, j: (j, 0, 0)),
        ],
        out_specs=pl.BlockSpec((1, s, dk), head),
        out_shape=jax.ShapeDtypeStruct((b, s, c), BF16),
        scratch_shapes=[pltpu.VMEM((dk, dk), F32), pltpu.VMEM((dk, dk), F32),
                        pltpu.VMEM((s, dk), F32), pltpu.VMEM((s, dk), F32)],
        compiler_params=_params("parallel", "parallel"),
    )(q, k, v, gcol, grow, outnorm_g.reshape(hh, 1, dk))


def _mlstm_out_kernel(hn_ref, xc_ref, z_ref, skip_ref, w_ref, x_ref, gate_ref, o_ref):
    a = (hn_ref[0].astype(F32) + skip_ref[...] * xc_ref[0].astype(F32)) * _silu(z_ref[0].astype(F32))
    y = jnp.dot(a.astype(BF16), w_ref[...], preferred_element_type=F32)
    o_ref[0] = x_ref[0] + gate_ref[0] * y


def _mlstm_out(hn, xc, up, skip, w, x, gate):
    b, s, c = hn.shape
    d = w.shape[1]
    ts = min(ROW_TILE, s)
    row = lambda i, j: (i, j, 0)
    return pl.pallas_call(
        _mlstm_out_kernel,
        grid=(b, s // ts),
        in_specs=[
            pl.BlockSpec((1, ts, c), row),
            pl.BlockSpec((1, ts, c), row),
            pl.BlockSpec((1, ts, c), lambda i, j: (i, j, 1)),
            pl.BlockSpec((1, c), lambda i, j: (0, 0)),
            pl.BlockSpec((c, d), lambda i, j: (0, 0)),
            pl.BlockSpec((1, ts, d), row),
            pl.BlockSpec((1, 1, d), lambda i, j: (i, 0, 0)),
        ],
        out_specs=pl.BlockSpec((1, ts, d), row),
        out_shape=jax.ShapeDtypeStruct((b, s, d), F32),
        compiler_params=_params("parallel", "parallel"),
    )(hn, xc, up, skip.reshape(1, c), w, x, gate)


def _router_kernel(x_ref, g_ref, sc_ref, sh_ref, w_ref, route_ref, wgt_ref, cnt_ref, run_ref):
    first = (pl.program_id(0) == 0) & (pl.program_id(1) == 0)

    @pl.when(first)
    def _():
        run_ref[...] = jnp.zeros_like(run_ref)

    hf = _modulated_norm(x_ref[0], g_ref[...], sc_ref[0], sh_ref[0])
    logits = jnp.dot(hf, w_ref[...], preferred_element_type=F32, precision=lax.Precision.HIGHEST)
    tm = logits.shape[0]
    lane = lax.broadcasted_iota(jnp.int32, logits.shape, 1).astype(F32)
    neg = -jnp.inf
    is_g = lane < N_GROUPS
    gl = jnp.where(is_g, logits, neg)
    gmax = jnp.max(gl, axis=-1, keepdims=True)
    g_sel = jnp.min(jnp.where(gl == gmax, lane, float(LANES)), axis=-1, keepdims=True)
    p_g = 1.0 / jnp.sum(jnp.where(is_g, jnp.exp(gl - gmax), 0.0), axis=-1, keepdims=True)
    e_lane = lane - N_GROUPS
    in_grp = (e_lane >= g_sel * EXPERTS_PER_GROUP) & (e_lane < (g_sel + 1) * EXPERTS_PER_GROUP)
    el = jnp.where(in_grp, logits, neg)
    emax = jnp.max(el, axis=-1, keepdims=True)
    i1 = jnp.min(jnp.where(el == emax, e_lane, float(LANES)), axis=-1, keepdims=True)
    el2 = jnp.where(e_lane == i1, neg, el)
    emax2 = jnp.max(el2, axis=-1, keepdims=True)
    i2 = jnp.min(jnp.where(el2 == emax2, e_lane, float(LANES)), axis=-1, keepdims=True)
    t2 = jnp.exp(emax2 - emax)
    w1 = p_g / (1.0 + t2)
    w2 = p_g * t2 / (1.0 + t2)
    a = jnp.where(e_lane == i1, 1.0, jnp.where(e_lane == i2, 1.0, 0.0))
    r = lax.broadcasted_iota(jnp.int32, (tm, tm), 0)
    cc = lax.broadcasted_iota(jnp.int32, (tm, tm), 1)
    tri = jnp.where(cc < r, 1.0, 0.0).astype(BF16)
    before = jnp.dot(tri, a.astype(BF16), preferred_element_type=F32) + run_ref[...]
    rank1 = jnp.sum(jnp.where(e_lane == i1, before, 0.0), axis=-1, keepdims=True)
    rank2 = jnp.sum(jnp.where(e_lane == i2, before, 0.0), axis=-1, keepdims=True)
    run_new = run_ref[...] + jnp.sum(a, axis=0, keepdims=True)
    run_ref[...] = run_new
    cnt_ref[...] = run_new.astype(jnp.int32)
    route = jnp.where(lane == 0, i1, jnp.where(lane == 1, i2, jnp.where(
        lane == 2, rank1, jnp.where(lane == 3, rank2, 0.0))))
    route_ref[0] = route.astype(jnp.int32)
    wgt_ref[0] = jnp.where(lane == 0, w1, jnp.where(lane == 1, w2, 0.0))


def _router(x, g, sc, sh, w_group, w_router):
    b, s, d = x.shape
    ts = min(ROW_TILE, s)
    w = jnp.concatenate([w_group, w_router], axis=1)
    w = jnp.pad(w, ((0, 0), (0, LANES - w.shape[1])))
    row = lambda i, j: (i, j, 0)
    return pl.pallas_call(
        _router_kernel,
        grid=(b, s // ts),
        in_specs=[
            pl.BlockSpec((1, ts, d), row),
            pl.BlockSpec((1, d), lambda i, j: (0, 0)),
            pl.BlockSpec((1, 1, d), lambda i, j: (i, 0, 0)),
            pl.BlockSpec((1, 1, d), lambda i, j: (i, 0, 0)),
            pl.BlockSpec((d, LANES), lambda i, j: (0, 0)),
        ],
        out_specs=[
            pl.BlockSpec((1, ts, LANES), row),
            pl.BlockSpec((1, ts, LANES), row),
            pl.BlockSpec((1, LANES), lambda i, j: (0, 0)),
        ],
        out_shape=[jax.ShapeDtypeStruct((b, s, LANES), jnp.int32),
                   jax.ShapeDtypeStruct((b, s, LANES), F32),
                   jax.ShapeDtypeStruct((1, LANES), jnp.int32)],
        scratch_shapes=[pltpu.VMEM((1, LANES), F32)],
        compiler_params=_params("arbitrary", "arbitrary"),
    )(x, g.reshape(1, d), sc, sh, w)


def _row_copy(src, src_row, dst, dst_row, sem):
    return pltpu.make_async_copy(src.at[pl.ds(src_row, 1)], dst.at[pl.ds(dst_row, 1)], sem)


def _dispatch_kernel(last_ref, dest_ref, x_ref, g_ref, sc_ref, sh_ref, xs_ref, buf_ref, zero_ref, sem, zsem):
    tm = buf_ref.shape[0]
    zrows = zero_ref.shape[0]

    @pl.when((pl.program_id(0) == 0) & (pl.program_id(1) == 0))
    def _():
        zero_ref[...] = jnp.zeros_like(zero_ref)

        def fill(e):
            start = pl.multiple_of(last_ref[e], zrows)
            return pltpu.make_async_copy(zero_ref, xs_ref.at[pl.ds(start, zrows)], zsem)

        for e in range(N_EXPERTS):
            @pl.when(last_ref[e] >= 0)
            def _():
                fill(e).start()
        for e in range(N_EXPERTS):
            @pl.when(last_ref[e] >= 0)
            def _():
                fill(e).wait()

        def tail(k):
            start = pl.multiple_of(k * zrows, zrows)
            return pltpu.make_async_copy(zero_ref, xs_ref.at[pl.ds(start, zrows)], zsem)

        n_used = last_ref[N_EXPERTS]
        n_tiles = xs_ref.shape[0] // zrows
        lax.fori_loop(n_used, n_tiles, lambda k, c: (tail(k).start(), c)[1], 0)
        lax.fori_loop(n_used, n_tiles, lambda k, c: (tail(k).wait(), c)[1], 0)

    buf_ref[...] = _modulated_norm(x_ref[0], g_ref[...], sc_ref[0], sh_ref[0])

    def issue(r, carry):
        _row_copy(buf_ref, r, xs_ref, dest_ref[0, 0, 2 * r], sem.at[0]).start()
        _row_copy(buf_ref, r, xs_ref, dest_ref[0, 0, 2 * r + 1], sem.at[1]).start()
        return carry

    lax.fori_loop(0, tm, issue, 0)

    def drain(r, carry):
        _row_copy(buf_ref, r, xs_ref, dest_ref[0, 0, 2 * r], sem.at[0]).wait()
        _row_copy(buf_ref, r, xs_ref, dest_ref[0, 0, 2 * r + 1], sem.at[1]).wait()
        return carry

    lax.fori_loop(0, tm, drain, 0)


def _dispatch(x, g, sc, sh, dest, last_tile_row, n_rows):
    b, s, d = x.shape
    tm = min(MOE_TOKEN_TILE, s)
    nt = s // tm
    dest3 = dest.reshape(b * nt, 1, 2 * tm)
    row = lambda i, j, last: (i, j, 0)
    grid_spec = pltpu.PrefetchScalarGridSpec(
        num_scalar_prefetch=1,
        grid=(b, nt),
        in_specs=[
            pl.BlockSpec((1, 1, 2 * tm), lambda i, j, last: (i * nt + j, 0, 0), memory_space=pltpu.SMEM),
            pl.BlockSpec((1, tm, d), row),
            pl.BlockSpec((1, d), lambda i, j, last: (0, 0)),
            pl.BlockSpec((1, 1, d), lambda i, j, last: (i, 0, 0)),
            pl.BlockSpec((1, 1, d), lambda i, j, last: (i, 0, 0)),
        ],
        out_specs=pl.BlockSpec(memory_space=pl.ANY),
        scratch_shapes=[pltpu.VMEM((tm, d), F32), pltpu.VMEM((MOE_ROW_TILE, d), F32),
                        pltpu.SemaphoreType.DMA((2,)), pltpu.SemaphoreType.DMA(())],
    )
    return pl.pallas_call(
        _dispatch_kernel,
        grid_spec=grid_spec,
        out_shape=jax.ShapeDtypeStruct((n_rows, d), F32),
        compiler_params=_params("arbitrary", "arbitrary"),
    )(last_tile_row, dest3, x, g.reshape(1, d), sc, sh)


def _expert_kernel(te_ref, nu_ref, xs_ref, w1_ref, w3_ref, w2_ref, ys_ref):
    i = pl.program_id(0)

    @pl.when(i < nu_ref[0])
    def _():
        xb = xs_ref[...].astype(BF16)
        a = jnp.dot(xb, w1_ref[0], preferred_element_type=F32)
        bb = jnp.dot(xb, w3_ref[0], preferred_element_type=F32)
        hmid = (_silu(a) * bb).astype(BF16)
        ys_ref[...] = jnp.dot(hmid, w2_ref[0], preferred_element_type=F32)

    @pl.when(i >= nu_ref[0])
    def _():
        ys_ref[...] = jnp.zeros_like(ys_ref)


def _expert_ffn(xs, tile_expert, n_used, w1, w3, w2):
    p, d = xs.shape
    de = w1.shape[2]
    tm = MOE_ROW_TILE
    grid_spec = pltpu.PrefetchScalarGridSpec(
        num_scalar_prefetch=2,
        grid=(p // tm,),
        in_specs=[
            pl.BlockSpec((tm, d), lambda i, te, nu: (jnp.minimum(i, nu[0] - 1), 0)),
            pl.BlockSpec((1, d, de), lambda i, te, nu: (te[i], 0, 0)),
            pl.BlockSpec((1, d, de), lambda i, te, nu: (te[i], 0, 0)),
            pl.BlockSpec((1, de, d), lambda i, te, nu: (te[i], 0, 0)),
        ],
        out_specs=pl.BlockSpec((tm, d), lambda i, te, nu: (i, 0)),
    )
    return pl.pallas_call(
        _expert_kernel,
        grid_spec=grid_spec,
        out_shape=jax.ShapeDtypeStruct((p, d), F32),
        compiler_params=_params("arbitrary"),
    )(tile_expert, n_used, xs, w1, w3, w2)


def _combine_kernel(dest_ref, ys_ref, wgt_ref, x_ref, gate_ref, o_ref, y0_ref, y1_ref, sem):
    tm = y0_ref.shape[0]

    def issue(r, carry):
        _row_copy(ys_ref, dest_ref[0, 0, 2 * r], y0_ref, r, sem.at[0]).start()
        _row_copy(ys_ref, dest_ref[0, 0, 2 * r + 1], y1_ref, r, sem.at[1]).start()
        return carry

    lax.fori_loop(0, tm, issue, 0)

    def drain(r, carry):
        _row_copy(ys_ref, dest_ref[0, 0, 2 * r], y0_ref, r, sem.at[0]).wait()
        _row_copy(ys_ref, dest_ref[0, 0, 2 * r + 1], y1_ref, r, sem.at[1]).wait()
        return carry

    lax.fori_loop(0, tm, drain, 0)
    wgt = wgt_ref[0]
    y = wgt[:, 0:1] * y0_ref[...] + wgt[:, 1:2] * y1_ref[...]
    o_ref[0] = x_ref[0] + gate_ref[0] * y


def _combine(ys, dest, wgt, x, gate):
    b, s, d = x.shape
    tm = min(MOE_TOKEN_TILE, s)
    nt = s // tm
    dest3 = dest.reshape(b * nt, 1, 2 * tm)
    row = lambda i, j: (i, j, 0)
    return pl.pallas_call(
        _combine_kernel,
        grid=(b, nt),
        in_specs=[
            pl.BlockSpec((1, 1, 2 * tm), lambda i, j: (i * nt + j, 0, 0), memory_space=pltpu.SMEM),
            pl.BlockSpec(memory_space=pl.ANY),
            pl.BlockSpec((1, tm, LANES), row),
            pl.BlockSpec((1, tm, d), row),
            pl.BlockSpec((1, 1, d), lambda i, j: (i, 0, 0)),
        ],
        out_specs=pl.BlockSpec((1, tm, d), row),
        out_shape=jax.ShapeDtypeStruct((b, s, d), F32),
        scratch_shapes=[pltpu.VMEM((tm, d), F32), pltpu.VMEM((tm, d), F32),
                        pltpu.SemaphoreType.DMA((2,))],
        compiler_params=_params("arbitrary", "arbitrary"),
    )(dest3, ys, wgt, x, gate)


def _hier_moe_residual(x, g, sc, sh, gate, w_group, w_router, w1, w3, w2):
    b, s, d = x.shape
    t = b * s
    route, wgt, cnt = _router(x, g, sc, sh, w_group, w_router)
    tm = MOE_ROW_TILE
    counts = cnt[0, N_GROUPS:N_GROUPS + N_EXPERTS]
    tiles = (counts + tm - 1) // tm
    tile_end = jnp.cumsum(tiles)
    base = (tile_end - tiles) * tm
    n_tiles = (2 * t) // tm + N_EXPERTS
    tile_expert = jnp.minimum(
        jnp.sum(tile_end[None, :] <= jnp.arange(n_tiles)[:, None], axis=1), N_EXPERTS - 1).astype(jnp.int32)
    n_used = tile_end[-1:].astype(jnp.int32)
    experts = jnp.arange(N_EXPERTS)
    pick_base = jnp.sum(jnp.where(route[..., 0:2, None] == experts, base, 0), axis=-1)
    dest = (pick_base + route[..., 2:4]).astype(jnp.int32)
    last_tile_row = jnp.where(tiles > 0, (tile_end - 1) * tm, -1).astype(jnp.int32)
    last_tile_row = jnp.concatenate([last_tile_row, n_used])
    xs = _dispatch(x, g, sc, sh, dest, last_tile_row, n_tiles * tm)
    ys = _expert_ffn(xs, tile_expert, n_used, w1.astype(BF16), w3.astype(BF16), w2.astype(BF16))
    return _combine(ys, dest, wgt, x, gate)


def kernel(x, c, rel_table, ada_w, ada_b, norm_mix_g, norm_ffn_g, da_w_in, da_w_out, da_q_gain, da_k_gain, da_lam_q1, da_lam_k1, da_lam_q2, da_lam_k2, da_subln_g, ml_w_in, ml_conv_w, ml_conv_b, ml_wq, ml_wk, ml_wv, ml_gate_w, ml_gate_b, ml_outnorm_g, ml_skip, ml_w_out, moe_w_group, moe_w_router, moe_w1, moe_w3, moe_w2):
    depth = ada_w.shape[0]
    d = x.shape[-1]
    mod = _ada_mod(c, ada_w, ada_b)
    for i in range(depth):
        sh1, sc1, g1, sh2, sc2, g2 = [mod[i, :, None, k * d:(k + 1) * d] for k in range(6)]
        j = i // N_MIXERS
        if i % N_MIXERS == 0:
            qkv = _prenorm_matmul(x, norm_mix_g[i], sc1, sh1, da_w_in[j].astype(BF16))
            lam_init = 0.8 - 0.6 * math.exp(-0.3 * i)
            lam = (jnp.exp(jnp.sum(da_lam_q1[j] * da_lam_k1[j])) - jnp.exp(jnp.sum(da_lam_q2[j] * da_lam_k2[j]))
                   + lam_init).astype(F32)
            o = _diff_attention(qkv, da_q_gain[j], da_k_gain[j], da_subln_g[j], lam, lam_init, rel_table)
            x = _proj_residual(o, da_w_out[j].astype(BF16), x, g1)
        else:
            up = _prenorm_matmul(x, norm_mix_g[i], sc1, sh1, ml_w_in[j].astype(BF16))
            q, k, v, xc, pre = _mlstm_pre(up, ml_conv_w[j], ml_conv_b[j], ml_wq[j], ml_wk[j], ml_wv[j],
                                          ml_gate_w[j], ml_gate_b[j])
            hn = _mlstm_scan(q, k, v, pre, ml_outnorm_g[j])
            x = _mlstm_out(hn, xc, up, ml_skip[j], ml_w_out[j].astype(BF16), x, g1)
        x = _hier_moe_residual(x, norm_ffn_g[i], sc2, sh2, g2, moe_w_group[i], moe_w_router[i],
                               moe_w1[i], moe_w3[i], moe_w2[i])
    return x
```

```python
import functools
import math

import jax
import jax.numpy as jnp
from jax import lax
from jax.experimental import pallas as pl
from jax.experimental.pallas import tpu as pltpu

EPS = 1e-6
N_MIXERS = 2
DA_HEADS = 8
DA_HEAD_DIM = 64
N_BUCKETS = 32
MAX_DISTANCE = 128
ML_HEADS = 4
ML_CHUNK = 256
ML_QKV_BLOCK = 4
N_GROUPS = 4
EXPERTS_PER_GROUP = 8
N_EXPERTS = N_GROUPS * EXPERTS_PER_GROUP

LANES = 128
VMEM_LIMIT = 48 * 1024 * 1024
ATTN_Q_TILE = 512
ATTN_ROW_BLOCK = 128
ROW_TILE = 512
MOE_ROW_TILE = 512
MOE_TOKEN_TILE = 256
MLSTM_COL_TILE = 256

F32 = jnp.float32
BF16 = jnp.bfloat16
LOG2E = math.log2(math.e)


def _params(*sem):
    return pltpu.CompilerParams(dimension_semantics=sem, vmem_limit_bytes=VMEM_LIMIT)


def _silu(x):
    return x / (1.0 + jnp.exp(-x))


def _log_sigmoid(x):
    return jnp.minimum(x, 0.0) - jnp.log(1.0 + jnp.exp(-jnp.abs(x)))


def _modulated_norm(x, g, sc, sh):
    ms = jnp.mean(x * x, axis=-1, keepdims=True)
    return x * lax.rsqrt(ms + EPS) * g * (1.0 + sc) + sh


def _ada_kernel(c_ref, w_ref, b_ref, o_ref):
    ca = _silu(c_ref[...]).astype(BF16)
    o_ref[0] = jnp.dot(ca, w_ref[0].astype(BF16), preferred_element_type=F32) + b_ref[0]


def _ada_mod(c, ada_w, ada_b):
    depth, d, n = ada_w.shape
    b = c.shape[0]
    tn = 1536
    return pl.pallas_call(
        _ada_kernel,
        grid=(depth, n // tn),
        in_specs=[
            pl.BlockSpec((b, d), lambda i, j: (0, 0)),
            pl.BlockSpec((1, d, tn), lambda i, j: (i, 0, j)),
            pl.BlockSpec((1, 1, tn), lambda i, j: (i, 0, j)),
        ],
        out_specs=pl.BlockSpec((1, b, tn), lambda i, j: (i, 0, j)),
        out_shape=jax.ShapeDtypeStruct((depth, b, n), F32),
        compiler_params=_params("parallel", "parallel"),
    )(c, ada_w, ada_b.reshape(depth, 1, n))


def _prenorm_matmul_kernel(x_ref, g_ref, sc_ref, sh_ref, w_ref, o_ref, *, n_chunk):
    y = _modulated_norm(x_ref[0], g_ref[...], sc_ref[0], sh_ref[0]).astype(BF16)
    n = w_ref.shape[1]
    for n0 in range(0, n, n_chunk):
        o_ref[0, :, n0:n0 + n_chunk] = jnp.dot(
            y, w_ref[:, n0:n0 + n_chunk], preferred_element_type=F32).astype(o_ref.dtype)


def _prenorm_matmul(x, g, sc, sh, w):
    b, s, d = x.shape
    n = w.shape[1]
    ts = min(ROW_TILE, s)
    return pl.pallas_call(
        functools.partial(_prenorm_matmul_kernel, n_chunk=1024),
        grid=(b, s // ts),
        in_specs=[
            pl.BlockSpec((1, ts, d), lambda i, j: (i, j, 0)),
            pl.BlockSpec((1, d), lambda i, j: (0, 0)),
            pl.BlockSpec((1, 1, d), lambda i, j: (i, 0, 0)),
            pl.BlockSpec((1, 1, d), lambda i, j: (i, 0, 0)),
            pl.BlockSpec((d, n), lambda i, j: (0, 0)),
        ],
        out_specs=pl.BlockSpec((1, ts, n), lambda i, j: (i, j, 0)),
        out_shape=jax.ShapeDtypeStruct((b, s, n), BF16),
        compiler_params=_params("parallel", "parallel"),
    )(x, g.reshape(1, d), sc, sh, w)


def _proj_residual_kernel(a_ref, w_ref, x_ref, gate_ref, o_ref):
    y = jnp.dot(a_ref[0], w_ref[...], preferred_element_type=F32)
    o_ref[0] = x_ref[0] + gate_ref[0] * y


def _proj_residual(a, w, x, gate):
    b, s, k = a.shape
    d = w.shape[1]
    ts = min(ROW_TILE, s)
    return pl.pallas_call(
        _proj_residual_kernel,
        grid=(b, s // ts),
        in_specs=[
            pl.BlockSpec((1, ts, k), lambda i, j: (i, j, 0)),
            pl.BlockSpec((k, d), lambda i, j: (0, 0)),
            pl.BlockSpec((1, ts, d), lambda i, j: (i, j, 0)),
            pl.BlockSpec((1, 1, d), lambda i, j: (i, 0, 0)),
        ],
        out_specs=pl.BlockSpec((1, ts, d), lambda i, j: (i, j, 0)),
        out_shape=jax.ShapeDtypeStruct((b, s, d), F32),
        compiler_params=_params("parallel", "parallel"),
    )(a, w, x, gate)


def _t5_bucket(rel):
    nb = N_BUCKETS // 2
    max_exact = nb // 2
    ret = jnp.where(rel > 0, nb, 0)
    n = jnp.abs(rel)
    nf = jnp.maximum(n, 1).astype(F32)
    large = max_exact + (jnp.log(nf / max_exact) / math.log(MAX_DISTANCE / max_exact)
                         * (nb - max_exact)).astype(jnp.int32)
    large = jnp.minimum(large, nb - 1)
    return ret + jnp.where(n < max_exact, n, large)


def _bias_band_kernel(t_ref, o_ref, *, tq, nb):
    u = jnp.broadcast_to(t_ref[0], (tq, t_ref.shape[2]))
    y = pltpu.roll(u, 1, 1, stride=1, stride_axis=0)
    for jb in range(nb):
        o_ref[0, jb] = y[:, tq + jb * LANES:tq + (jb + 1) * LANES]


def _bias_band(rel_table, s, tq):
    h = rel_table.shape[1]
    delta = jnp.arange(-(s - 1), s)
    t = rel_table[_t5_bucket(delta)].T.astype(F32) * LOG2E
    u = jnp.concatenate([t, jnp.zeros((h, 1), F32)], axis=1).reshape(h, 1, 2 * s)
    nb = (2 * s - tq) // LANES
    return pl.pallas_call(
        functools.partial(_bias_band_kernel, tq=tq, nb=nb),
        grid=(h,),
        in_specs=[pl.BlockSpec((1, 1, 2 * s), lambda i: (i, 0, 0))],
        out_specs=pl.BlockSpec((1, nb, tq, LANES), lambda i: (i, 0, 0, 0)),
        out_shape=jax.ShapeDtypeStruct((h, nb, tq, LANES), F32),
        compiler_params=_params("parallel"),
    )(u)


def _group_rms(x, gain, half):
    sq = x * x
    lane = lax.broadcasted_iota(jnp.int32, x.shape, 1)
    lo = lane < half
    s_lo = jnp.sum(jnp.where(lo, sq, 0.0), axis=-1, keepdims=True)
    s_all = jnp.sum(sq, axis=-1, keepdims=True)
    ms = jnp.where(lo, s_lo, s_all - s_lo) * (1.0 / half)
    return x * lax.rsqrt(ms + EPS) * gain


def _diff_attn_kernel(lam_ref, q_ref, k_ref, v_ref, qg_ref, kg_ref, sg_ref, band_ref, o_ref, kn_ref, ve_ref,
                      *, tq, s, dh, out_scale):
    qi = pl.program_id(2)
    nq = pl.num_programs(2)
    dv = 2 * dh

    @pl.when(qi == 0)
    def _():
        kn_ref[...] = _group_rms(k_ref[0].astype(F32), kg_ref[...], dh).astype(BF16)
        ve_ref[:, :dv] = v_ref[0]
        ve_ref[:, dv:] = jnp.ones((s, dv), BF16)

    lam = lam_ref[0]
    q = _group_rms(q_ref[0].astype(F32), qg_ref[...], dh) * (dh ** -0.5 * LOG2E)
    lane = lax.broadcasted_iota(jnp.int32, q.shape, 1)
    q0 = jnp.where(lane < dh, q, 0.0).astype(BF16)
    q1 = jnp.where(lane < dh, 0.0, q).astype(BF16)
    jb0 = (nq - 1 - qi) * (tq // LANES)
    nkb = s // LANES

    def softmax_av(qm, r0, r1):
        sc = lax.dot_general(qm[r0:r1], kn_ref[...], (((1,), (1,)), ((), ())), preferred_element_type=F32)
        sc = sc + jnp.concatenate([band_ref[0, jb0 + kb, r0:r1, :] for kb in range(nkb)], axis=1)
        p = jnp.exp2(sc - jnp.max(sc, axis=-1, keepdims=True)).astype(BF16)
        oe = jnp.dot(p, ve_ref[...], preferred_element_type=F32)
        return oe[:, :dv], oe[:, dv:dv + 1]

    rb = ATTN_ROW_BLOCK
    for r0 in range(0, tq, rb):
        n0, l0 = softmax_av(q0, r0, r0 + rb)
        n1, l1 = softmax_av(q1, r0, r0 + rb)
        o = n0 * (1.0 / l0) - n1 * (lam / l1)
        ms = jnp.mean(o * o, axis=-1, keepdims=True)
        o_ref[0, r0:r0 + rb, :] = (o * lax.rsqrt(ms + EPS) * sg_ref[...] * out_scale).astype(o_ref.dtype)


def _diff_attention(qkv, q_gain, k_gain, subln_g, lam, lam_init, rel_table):
    b, s, _ = qkv.shape
    h, dh = DA_HEADS, DA_HEAD_DIM
    tq = min(ATTN_Q_TILE, s)
    band = _bias_band(rel_table, s, tq)
    nb = band.shape[1]
    qg = jnp.tile(q_gain, 2).reshape(1, 2 * dh)
    kg = jnp.tile(k_gain, 2).reshape(1, 2 * dh)
    kern = functools.partial(_diff_attn_kernel, tq=tq, s=s, dh=dh, out_scale=1.0 - lam_init)
    grid_spec = pltpu.PrefetchScalarGridSpec(
        num_scalar_prefetch=1,
        grid=(b, h, s // tq),
        in_specs=[
            pl.BlockSpec((1, tq, 2 * dh), lambda bi, hi, qi, lam: (bi, qi, hi)),
            pl.BlockSpec((1, s, 2 * dh), lambda bi, hi, qi, lam: (bi, 0, h + hi)),
            pl.BlockSpec((1, s, 2 * dh), lambda bi, hi, qi, lam: (bi, 0, 2 * h + hi)),
            pl.BlockSpec((1, 2 * dh), lambda bi, hi, qi, lam: (0, 0)),
            pl.BlockSpec((1, 2 * dh), lambda bi, hi, qi, lam: (0, 0)),
            pl.BlockSpec((1, 2 * dh), lambda bi, hi, qi, lam: (0, 0)),
            pl.BlockSpec((1, nb, tq, LANES), lambda bi, hi, qi, lam: (hi, 0, 0, 0)),
        ],
        out_specs=pl.BlockSpec((1, tq, 2 * dh), lambda bi, hi, qi, lam: (bi, qi, hi)),
        scratch_shapes=[pltpu.VMEM((s, 2 * dh), BF16), pltpu.VMEM((s, 4 * dh), BF16)],
    )
    return pl.pallas_call(
        kern,
        grid_spec=grid_spec,
        out_shape=jax.ShapeDtypeStruct((b, s, h * 2 * dh), BF16),
        compiler_params=_params("parallel", "parallel", "arbitrary"),
    )(lam.reshape(1), qkv, qkv, qkv, qg, kg, subln_g.reshape(1, 2 * dh), band)


def _mlstm_pre_kernel(xm_ref, cw_ref, cb_ref, wq_ref, wk_ref, wv_ref, gq_ref, gk_ref, gv_ref, gb_ref,
                      q_ref, k_ref, v_ref, xc_ref, pre_ref, pad_ref, *, s, halo):
    j = pl.program_id(1)
    xm = xm_ref[0].astype(F32)
    cbw = xm.shape[1]
    pad_ref[0:8, :] = jnp.zeros((8, cbw), F32)
    pad_ref[8 + s:16 + s, :] = jnp.zeros((8, cbw), F32)
    pad_ref[8:8 + s, :] = xm
    acc = jnp.zeros_like(xm) + cb_ref[...]
    for t in range(2 * halo + 1):
        acc = acc + cw_ref[t:t + 1, :] * pad_ref[8 - halo + t:8 - halo + t + s, :]
    xc = _silu(acc)
    xcb = xc.astype(BF16)
    xmb = xm_ref[0]
    q = jnp.dot(xcb, wq_ref[0], preferred_element_type=F32).astype(BF16)
    k = jnp.dot(xcb, wk_ref[0], preferred_element_type=F32).astype(BF16)
    v = jnp.dot(xmb, wv_ref[0], preferred_element_type=F32).astype(BF16)
    q_ref[0] = q
    k_ref[0] = k
    v_ref[0] = v
    xc_ref[0] = xcb
    part = (jnp.dot(q, gq_ref[...], preferred_element_type=F32)
            + jnp.dot(k, gk_ref[...], preferred_element_type=F32)
            + jnp.dot(v, gv_ref[...], preferred_element_type=F32))

    @pl.when(j == 0)
    def _():
        pre_ref[0] = part + gb_ref[...]

    @pl.when(j != 0)
    def _():
        pre_ref[0] += part


def _blockdiag_dense(w, cb):
    nblk, blk, _ = w.shape
    per = cb // blk
    w4 = w.reshape(nblk // per, per, blk, blk)
    eye = jnp.eye(per, dtype=w.dtype)
    return jnp.einsum('jnio,nm->jnimo', w4, eye).reshape(nblk // per, cb, cb).astype(BF16)


def _mlstm_pre(up, conv_w, conv_b, wq, wk, wv, gate_w, gate_b):
    b, s, c2 = up.shape
    c = c2 // 2
    cb = MLSTM_COL_TILE
    ncb = c // cb
    kw = conv_w.shape[0]
    ng = gate_w.shape[0] * gate_w.shape[-1]

    def gate_mat(i):
        g = gate_w[:, i].transpose(1, 0, 2).reshape(c, ng)
        return jnp.pad(g, ((0, 0), (0, LANES - ng))).astype(BF16)

    gb = jnp.pad(gate_b.reshape(1, ng), ((0, 0), (0, LANES - ng)))
    col = lambda i, j: (i, 0, j)
    out_bf = jax.ShapeDtypeStruct((b, s, c), BF16)
    return pl.pallas_call(
        functools.partial(_mlstm_pre_kernel, s=s, halo=kw // 2),
        grid=(b, ncb),
        in_specs=[
            pl.BlockSpec((1, s, cb), col),
            pl.BlockSpec((kw, cb), lambda i, j: (0, j)),
            pl.BlockSpec((1, cb), lambda i, j: (0, j)),
            pl.BlockSpec((1, cb, cb), lambda i, j: (j, 0, 0)),
            pl.BlockSpec((1, cb, cb), lambda i, j: (j, 0, 0)),
            pl.BlockSpec((1, cb, cb), lambda i, j: (j, 0, 0)),
            pl.BlockSpec((cb, LANES), lambda i, j: (j, 0)),
            pl.BlockSpec((cb, LANES), lambda i, j: (j, 0)),
            pl.BlockSpec((cb, LANES), lambda i, j: (j, 0)),
            pl.BlockSpec((1, LANES), lambda i, j: (0, 0)),
        ],
        out_specs=[
            pl.BlockSpec((1, s, cb), col),
            pl.BlockSpec((1, s, cb), col),
            pl.BlockSpec((1, s, cb), col),
            pl.BlockSpec((1, s, cb), col),
            pl.BlockSpec((1, s, LANES), lambda i, j: (i, 0, 0)),
        ],
        out_shape=[out_bf, out_bf, out_bf, out_bf, jax.ShapeDtypeStruct((b, s, LANES), F32)],
        scratch_shapes=[pltpu.VMEM((s + 16, cb), F32)],
        compiler_params=_params("parallel", "arbitrary"),
    )(up, conv_w, conv_b.reshape(1, c), _blockdiag_dense(wq, cb), _blockdiag_dense(wk, cb),
      _blockdiag_dense(wv, cb), gate_mat(0), gate_mat(1), gate_mat(2), gb)


def _mlstm_scan_kernel(q_ref, k_ref, v_ref, gcol_ref, grow_ref, og_ref, o_ref,
                       cf_ref, cbk_ref, hf_ref, hb_ref, *, L, nc, dk):
    scale = dk ** -0.5
    row = lax.broadcasted_iota(jnp.int32, (L, L), 0)
    coli = lax.broadcasted_iota(jnp.int32, (L, L), 1)
    lower = coli <= row
    upper = coli >= row
    cf_ref[...] = jnp.zeros_like(cf_ref)
    cbk_ref[...] = jnp.zeros_like(cbk_ref)

    def chunk_step(c, c_ref, h_ref, n, m, fwd):
        mask, mask_t = (lower, upper) if fwd else (upper, lower)
        gi = 0 if fwd else 2
        sl = pl.ds(pl.multiple_of(c * L, L), L)
        qb = q_ref[0, sl, :]
        kb = k_ref[0, sl, :]
        vb = v_ref[0, sl, :]
        gc = gcol_ref[0, 0, c]
        gr = grow_ref[0, 0, c]
        ii_col = gc[:, gi:gi + 1]
        lf_col = _log_sigmoid(gc[:, gi + 1:gi + 2])
        ii_row = gr[gi:gi + 1, :]
        lf_row = _log_sigmoid(gr[gi + 1:gi + 2, :])
        b_col = jnp.sum(jnp.where(mask, lf_row, 0.0), axis=1, keepdims=True)
        b_row = jnp.sum(jnp.where(mask_t, lf_col, 0.0), axis=0, keepdims=True)
        dmat = jnp.where(mask, b_col - b_row + ii_row, -jnp.inf)
        inter = b_col + m
        m_t = jnp.maximum(inter, jnp.max(dmat, axis=1, keepdims=True))
        w_intra = jnp.exp(dmat - m_t)
        w_inter = jnp.exp(inter - m_t)
        sqk = lax.dot_general(qb, kb, (((1,), (1,)), ((), ())), preferred_element_type=F32)
        sw = sqk * scale * w_intra
        cmat = c_ref[...]
        num = (w_inter * jnp.dot(qb, cmat.astype(BF16), preferred_element_type=F32)
               + jnp.dot(sw.astype(BF16), vb, preferred_element_type=F32))
        den = (w_inter * jnp.sum(qb.astype(F32) * n, axis=1, keepdims=True)
               + jnp.sum(sw, axis=1, keepdims=True))
        h_ref[sl, :] = num / jnp.maximum(jnp.abs(den), jnp.exp(-m_t))
        b_tot = jnp.sum(lf_row, axis=1, keepdims=True)
        g_col = b_tot - b_col + ii_col
        m_new = jnp.maximum(b_tot + m, jnp.max(g_col, axis=0, keepdims=True))
        decay = jnp.exp(b_tot + m - m_new)
        kw = kb.astype(F32) * scale * jnp.exp(g_col - m_new)
        c_ref[...] = decay * cmat + lax.dot_general(
            kw.astype(BF16), vb, (((0,), (0,)), ((), ())), preferred_element_type=F32)
        n_new = decay * n + jnp.sum(kw, axis=0, keepdims=True)
        return n_new, m_new

    def body(i, carry):
        nf, mf, nb_, mb = carry
        nf, mf = chunk_step(i, cf_ref, hf_ref, nf, mf, True)
        nb_, mb = chunk_step(nc - 1 - i, cbk_ref, hb_ref, nb_, mb, False)
        return nf, mf, nb_, mb

    z_n = jnp.zeros((1, dk), F32)
    z_m = jnp.zeros((1, 1), F32)
    lax.fori_loop(0, nc, body, (z_n, z_m, z_n, z_m))

    def finish(c, carry):
        sl = pl.ds(pl.multiple_of(c * L, L), L)
        hs = hf_ref[sl, :] + hb_ref[sl, :]
        ms = jnp.mean(hs * hs, axis=-1, keepdims=True)
        o_ref[0, sl, :] = (hs * lax.rsqrt(ms + EPS) * og_ref[0]).astype(o_ref.dtype)
        return carry

    lax.fori_loop(0, nc, finish, 0)


def _mlstm_scan(q, k, v, pre, outnorm_g):
    b, s, c = q.shape
    hh = ML_HEADS
    dk = c // hh
    L = min(ML_CHUNK, s)
    nc = s // L
    idx = jnp.array([[0 * 2 * hh + h, 0 * 2 * hh + hh + h, 2 * hh + h, 2 * hh + hh + h] for h in range(hh)])
    g = pre[:, :, idx]
    gcol = g.transpose(0, 2, 1, 3).reshape(b, hh, nc, L, 4)
    grow = gcol.transpose(0, 1, 2, 4, 3)
    head = lambda i, j: (i, 0, j)
    return pl.pallas_call(
        functools.partial(_mlstm_scan_kernel, L=L, nc=nc, dk=dk),
        grid=(b, hh),
        in_specs=[
            pl.BlockSpec((1, s, dk), head),
            pl.BlockSpec((1, s, dk), head),
            pl.BlockSpec((1, s, dk), head),
            pl.BlockSpec((1, 1, nc, L, 4), lambda i, j: (i, j, 0, 0, 0)),
            pl.BlockSpec((1, 1, nc, 4, L), lambda i, j: (i, j, 0, 0, 0)),
            pl.BlockSpec((1, 1, dk), lambda i, j: (j, 0, 0)),
        ],
        out_specs=pl.BlockSpec((1, s, dk), head),
        out_shape=jax.ShapeDtypeStruct((b, s, c), BF16),
        scratch_shapes=[pltpu.VMEM((dk, dk), F32), pltpu.VMEM((dk, dk), F32),
                        pltpu.VMEM((s, dk), F32), pltpu.VMEM((s, dk), F32)],
        compiler_params=_params("parallel", "parallel"),
    )(q, k, v, gcol, grow, outnorm_g.reshape(hh, 1, dk))


def _mlstm_out_kernel(hn_ref, xc_ref, z_ref, skip_ref, w_ref, x_ref, gate_ref, o_ref):
    a = (hn_ref[0].astype(F32) + skip_ref[...] * xc_ref[0].astype(F32)) * _silu(z_ref[0].astype(F32))
    y = jnp.dot(a.astype(BF16), w_ref[...], preferred_element_type=F32)
    o_ref[0] = x_ref[0] + gate_ref[0] * y


def _mlstm_out(hn, xc, up, skip, w, x, gate):
    b, s, c = hn.shape
    d = w.shape[1]
    ts = min(ROW_TILE, s)
    row = lambda i, j: (i, j, 0)
    return pl.pallas_call(
        _mlstm_out_kernel,
        grid=(b, s // ts),
        in_specs=[
            pl.BlockSpec((1, ts, c), row),
            pl.BlockSpec((1, ts, c), row),
            pl.BlockSpec((1, ts, c), lambda i, j: (i, j, 1)),
            pl.BlockSpec((1, c), lambda i, j: (0, 0)),
            pl.BlockSpec((c, d), lambda i, j: (0, 0)),
            pl.BlockSpec((1, ts, d), row),
            pl.BlockSpec((1, 1, d), lambda i, j: (i, 0, 0)),
        ],
        out_specs=pl.BlockSpec((1, ts, d), row),
        out_shape=jax.ShapeDtypeStruct((b, s, d), F32),
        compiler_params=_params("parallel", "parallel"),
    )(hn, xc, up, skip.reshape(1, c), w, x, gate)


def _router_kernel(x_ref, g_ref, sc_ref, sh_ref, w_ref, route_ref, wgt_ref, cnt_ref, run_ref):
    first = (pl.program_id(0) == 0) & (pl.program_id(1) == 0)

    @pl.when(first)
    def _():
        run_ref[...] = jnp.zeros_like(run_ref)

    hf = _modulated_norm(x_ref[0], g_ref[...], sc_ref[0], sh_ref[0])
    logits = jnp.dot(hf, w_ref[...], preferred_element_type=F32, precision=lax.Precision.HIGHEST)
    tm = logits.shape[0]
    lane = lax.broadcasted_iota(jnp.int32, logits.shape, 1).astype(F32)
    neg = -jnp.inf
    is_g = lane < N_GROUPS
    gl = jnp.where(is_g, logits, neg)
    gmax = jnp.max(gl, axis=-1, keepdims=True)
    g_sel = jnp.min(jnp.where(gl == gmax, lane, float(LANES)), axis=-1, keepdims=True)
    p_g = 1.0 / jnp.sum(jnp.where(is_g, jnp.exp(gl - gmax), 0.0), axis=-1, keepdims=True)
    e_lane = lane - N_GROUPS
    in_grp = (e_lane >= g_sel * EXPERTS_PER_GROUP) & (e_lane < (g_sel + 1) * EXPERTS_PER_GROUP)
    el = jnp.where(in_grp, logits, neg)
    emax = jnp.max(el, axis=-1, keepdims=True)
    i1 = jnp.min(jnp.where(el == emax, e_lane, float(LANES)), axis=-1, keepdims=True)
    el2 = jnp.where(e_lane == i1, neg, el)
    emax2 = jnp.max(el2, axis=-1, keepdims=True)
    i2 = jnp.min(jnp.where(el2 == emax2, e_lane, float(LANES)), axis=-1, keepdims=True)
    t2 = jnp.exp(emax2 - emax)
    w1 = p_g / (1.0 + t2)
    w2 = p_g * t2 / (1.0 + t2)
    a = jnp.where(e_lane == i1, 1.0, jnp.where(e_lane == i2, 1.0, 0.0))
    r = lax.broadcasted_iota(jnp.int32, (tm, tm), 0)
    cc = lax.broadcasted_iota(jnp.int32, (tm, tm), 1)
    tri = jnp.where(cc < r, 1.0, 0.0).astype(BF16)
    before = jnp.dot(tri, a.astype(BF16), preferred_element_type=F32) + run_ref[...]
    rank1 = jnp.sum(jnp.where(e_lane == i1, before, 0.0), axis=-1, keepdims=True)
    rank2 = jnp.sum(jnp.where(e_lane == i2, before, 0.0), axis=-1, keepdims=True)
    run_new = run_ref[...] + jnp.sum(a, axis=0, keepdims=True)
    run_ref[...] = run_new
    cnt_ref[...] = run_new.astype(jnp.int32)
    route = jnp.where(lane == 0, i1, jnp.where(lane == 1, i2, jnp.where(
        lane == 2, rank1, jnp.where(lane == 3, rank2, 0.0))))
    route_ref[0] = route.astype(jnp.int32)
    wgt_ref[0] = jnp.where(lane == 0, w1, jnp.where(lane == 1, w2, 0.0))


def _router(x, g, sc, sh, w_group, w_router):
    b, s, d = x.shape
    ts = min(ROW_TILE, s)
    w = jnp.concatenate([w_group, w_router], axis=1)
    w = jnp.pad(w, ((0, 0), (0, LANES - w.shape[1])))
    row = lambda i, j: (i, j, 0)
    return pl.pallas_call(
        _router_kernel,
        grid=(b, s // ts),
        in_specs=[
            pl.BlockSpec((1, ts, d), row),
            pl.BlockSpec((1, d), lambda i, j: (0, 0)),
            pl.BlockSpec((1, 1, d), lambda i, j: (i, 0, 0)),
            pl.BlockSpec((1, 1, d), lambda i, j: (i, 0, 0)),
            pl.BlockSpec((d, LANES), lambda i, j: (0, 0)),
        ],
        out_specs=[
            pl.BlockSpec((1, ts, LANES), row),
            pl.BlockSpec((1, ts, LANES), row),
            pl.BlockSpec((1, LANES), lambda i, j: (0, 0)),
        ],
        out_shape=[jax.ShapeDtypeStruct((b, s, LANES), jnp.int32),
                   jax.ShapeDtypeStruct((b, s, LANES), F32),
                   jax.ShapeDtypeStruct((1, LANES), jnp.int32)],
        scratch_shapes=[pltpu.VMEM((1, LANES), F32)],
        compiler_params=_params("arbitrary", "arbitrary"),
    )(x, g.reshape(1, d), sc, sh, w)


def _row_copy(src, src_row, dst, dst_row, sem):
    return pltpu.make_async_copy(src.at[pl.ds(src_row, 1)], dst.at[pl.ds(dst_row, 1)], sem)


def _dispatch_kernel(last_ref, dest_ref, x_ref, g_ref, sc_ref, sh_ref, xs_ref, buf_ref, zero_ref, sem, zsem):
    tm = buf_ref.shape[0]
    zrows = zero_ref.shape[0]

    @pl.when((pl.program_id(0) == 0) & (pl.program_id(1) == 0))
    def _():
        zero_ref[...] = jnp.zeros_like(zero_ref)

        def fill(e):
            start = pl.multiple_of(last_ref[e], zrows)
            return pltpu.make_async_copy(zero_ref, xs_ref.at[pl.ds(start, zrows)], zsem)

        for e in range(N_EXPERTS):
            @pl.when(last_ref[e] >= 0)
            def _():
                fill(e).start()
        for e in range(N_EXPERTS):
            @pl.when(last_ref[e] >= 0)
            def _():
                fill(e).wait()

        def tail(k):
            start = pl.multiple_of(k * zrows, zrows)
            return pltpu.make_async_copy(zero_ref, xs_ref.at[pl.ds(start, zrows)], zsem)

        n_used = last_ref[N_EXPERTS]
        n_tiles = xs_ref.shape[0] // zrows
        lax.fori_loop(n_used, n_tiles, lambda k, c: (tail(k).start(), c)[1], 0)
        lax.fori_loop(n_used, n_tiles, lambda k, c: (tail(k).wait(), c)[1], 0)

    buf_ref[...] = _modulated_norm(x_ref[0], g_ref[...], sc_ref[0], sh_ref[0])

    def issue(r, carry):
        _row_copy(buf_ref, r, xs_ref, dest_ref[0, 0, 2 * r], sem.at[0]).start()
        _row_copy(buf_ref, r, xs_ref, dest_ref[0, 0, 2 * r + 1], sem.at[1]).start()
        return carry

    lax.fori_loop(0, tm, issue, 0)

    def drain(r, carry):
        _row_copy(buf_ref, r, xs_ref, dest_ref[0, 0, 2 * r], sem.at[0]).wait()
        _row_copy(buf_ref, r, xs_ref, dest_ref[0, 0, 2 * r + 1], sem.at[1]).wait()
        return carry

    lax.fori_loop(0, tm, drain, 0)


def _dispatch(x, g, sc, sh, dest, last_tile_row, n_rows):
    b, s, d = x.shape
    tm = min(MOE_TOKEN_TILE, s)
    nt = s // tm
    dest3 = dest.reshape(b * nt, 1, 2 * tm)
    row = lambda i, j, last: (i, j, 0)
    grid_spec = pltpu.PrefetchScalarGridSpec(
        num_scalar_prefetch=1,
        grid=(b, nt),
        in_specs=[
            pl.BlockSpec((1, 1, 2 * tm), lambda i, j, last: (i * nt + j, 0, 0), memory_space=pltpu.SMEM),
            pl.BlockSpec((1, tm, d), row),
            pl.BlockSpec((1, d), lambda i, j, last: (0, 0)),
            pl.BlockSpec((1, 1, d), lambda i, j, last: (i, 0, 0)),
            pl.BlockSpec((1, 1, d), lambda i, j, last: (i, 0, 0)),
        ],
        out_specs=pl.BlockSpec(memory_space=pl.ANY),
        scratch_shapes=[pltpu.VMEM((tm, d), F32), pltpu.VMEM((MOE_ROW_TILE, d), F32),
                        pltpu.SemaphoreType.DMA((2,)), pltpu.SemaphoreType.DMA(())],
    )
    return pl.pallas_call(
        _dispatch_kernel,
        grid_spec=grid_spec,
        out_shape=jax.ShapeDtypeStruct((n_rows, d), F32),
        compiler_params=_params("arbitrary", "arbitrary"),
    )(last_tile_row, dest3, x, g.reshape(1, d), sc, sh)


def _expert_kernel(te_ref, nu_ref, xs_ref, w1_ref, w3_ref, w2_ref, ys_ref):
    i = pl.program_id(0)

    @pl.when(i < nu_ref[0])
    def _():
        xb = xs_ref[...].astype(BF16)
        a = jnp.dot(xb, w1_ref[0], preferred_element_type=F32)
        bb = jnp.dot(xb, w3_ref[0], preferred_element_type=F32)
        hmid = (_silu(a) * bb).astype(BF16)
        ys_ref[...] = jnp.dot(hmid, w2_ref[0], preferred_element_type=F32)

    @pl.when(i >= nu_ref[0])
    def _():
        ys_ref[...] = jnp.zeros_like(ys_ref)


def _expert_ffn(xs, tile_expert, n_used, w1, w3, w2):
    p, d = xs.shape
    de = w1.shape[2]
    tm = MOE_ROW_TILE
    grid_spec = pltpu.PrefetchScalarGridSpec(
        num_scalar_prefetch=2,
        grid=(p // tm,),
        in_specs=[
            pl.BlockSpec((tm, d), lambda i, te, nu: (jnp.maximum(jnp.minimum(i, nu[0] - 1), 0), 0)),
            pl.BlockSpec((1, d, de), lambda i, te, nu: (te[i], 0, 0)),
            pl.BlockSpec((1, d, de), lambda i, te, nu: (te[i], 0, 0)),
            pl.BlockSpec((1, de, d), lambda i, te, nu: (te[i], 0, 0)),
        ],
        out_specs=pl.BlockSpec((tm, d), lambda i, te, nu: (i, 0)),
    )
    return pl.pallas_call(
        _expert_kernel,
        grid_spec=grid_spec,
        out_shape=jax.ShapeDtypeStruct((p, d), F32),
        compiler_params=_params("arbitrary"),
    )(tile_expert, n_used, xs, w1, w3, w2)


def _combine_kernel(dest_ref, ys_ref, wgt_ref, x_ref, gate_ref, o_ref, y0_ref, y1_ref, sem):
    tm = y0_ref.shape[0]

    def issue(r, carry):
        _row_copy(ys_ref, dest_ref[0, 0, 2 * r], y0_ref, r, sem.at[0]).start()
        _row_copy(ys_ref, dest_ref[0, 0, 2 * r + 1], y1_ref, r, sem.at[1]).start()
        return carry

    lax.fori_loop(0, tm, issue, 0)

    def drain(r, carry):
        _row_copy(ys_ref, dest_ref[0, 0, 2 * r], y0_ref, r, sem.at[0]).wait()
        _row_copy(ys_ref, dest_ref[0, 0, 2 * r + 1], y1_ref, r, sem.at[1]).wait()
        return carry

    lax.fori_loop(0, tm, drain, 0)
    wgt = wgt_ref[0]
    y = wgt[:, 0:1] * y0_ref[...] + wgt[:, 1:2] * y1_ref[...]
    o_ref[0] = x_ref[0] + gate_ref[0] * y


def _combine(ys, dest, wgt, x, gate):
    b, s, d = x.shape
    tm = min(MOE_TOKEN_TILE, s)
    nt = s // tm
    dest3 = dest.reshape(b * nt, 1, 2 * tm)
    row = lambda i, j: (i, j, 0)
    return pl.pallas_call(
        _combine_kernel,
        grid=(b, nt),
        in_specs=[
            pl.BlockSpec((1, 1, 2 * tm), lambda i, j: (i * nt + j, 0, 0), memory_space=pltpu.SMEM),
            pl.BlockSpec(memory_space=pl.ANY),
            pl.BlockSpec((1, tm, LANES), row),
            pl.BlockSpec((1, tm, d), row),
            pl.BlockSpec((1, 1, d), lambda i, j: (i, 0, 0)),
        ],
        out_specs=pl.BlockSpec((1, tm, d), row),
        out_shape=jax.ShapeDtypeStruct((b, s, d), F32),
        scratch_shapes=[pltpu.VMEM((tm, d), F32), pltpu.VMEM((tm, d), F32),
                        pltpu.SemaphoreType.DMA((2,))],
        compiler_params=_params("arbitrary", "arbitrary"),
    )(dest3, ys, wgt, x, gate)


def _hier_moe_residual(x, g, sc, sh, gate, w_group, w_router, w1, w3, w2):
    b, s, d = x.shape
    t = b * s
    route, wgt, cnt = _router(x, g, sc, sh, w_group, w_router)
    tm = MOE_ROW_TILE
    counts = cnt[0, N_GROUPS:N_GROUPS + N_EXPERTS]
    tiles = (counts + tm - 1) // tm
    tile_end = jnp.cumsum(tiles)
    base = (tile_end - tiles) * tm
    n_tiles = (2 * t) // tm + N_EXPERTS
    tile_expert = jnp.minimum(
        jnp.sum(tile_end[None, :] <= jnp.arange(n_tiles)[:, None], axis=1), N_EXPERTS - 1).astype(jnp.int32)
    n_used = tile_end[-1:].astype(jnp.int32)
    experts = jnp.arange(N_EXPERTS)
    pick_base = jnp.sum(jnp.where(route[..., 0:2, None] == experts, base, 0), axis=-1)
    dest = (pick_base + route[..., 2:4]).astype(jnp.int32)
    last_tile_row = jnp.where(tiles > 0, (tile_end - 1) * tm, -1).astype(jnp.int32)
    last_tile_row = jnp.concatenate([last_tile_row, n_used])
    xs = _dispatch(x, g, sc, sh, dest, last_tile_row, n_tiles * tm)
    ys = _expert_ffn(xs, tile_expert, n_used, w1.astype(BF16), w3.astype(BF16), w2.astype(BF16))
    return _combine(ys, dest, wgt, x, gate)


def kernel(x, c, rel_table, ada_w, ada_b, norm_mix_g, norm_ffn_g, da_w_in, da_w_out, da_q_gain, da_k_gain, da_lam_q1, da_lam_k1, da_lam_q2, da_lam_k2, da_subln_g, ml_w_in, ml_conv_w, ml_conv_b, ml_wq, ml_wk, ml_wv, ml_gate_w, ml_gate_b, ml_outnorm_g, ml_skip, ml_w_out, moe_w_group, moe_w_router, moe_w1, moe_w3, moe_w2):
    depth = ada_w.shape[0]
    d = x.shape[-1]
    mod = _ada_mod(c, ada_w, ada_b)
    for i in range(depth):
        sh1, sc1, g1, sh2, sc2, g2 = [mod[i, :, None, k * d:(k + 1) * d] for k in range(6)]
        j = i // N_MIXERS
        if i % N_MIXERS == 0:
            qkv = _prenorm_matmul(x, norm_mix_g[i], sc1, sh1, da_w_in[j].astype(BF16))
            lam_init = 0.8 - 0.6 * math.exp(-0.3 * i)
            lam = (jnp.exp(jnp.sum(da_lam_q1[j] * da_lam_k1[j])) - jnp.exp(jnp.sum(da_lam_q2[j] * da_lam_k2[j]))
                   + lam_init).astype(F32)
            o = _diff_attention(qkv, da_q_gain[j], da_k_gain[j], da_subln_g[j], lam, lam_init, rel_table)
            x = _proj_residual(o, da_w_out[j].astype(BF16), x, g1)
        else:
            up = _prenorm_matmul(x, norm_mix_g[i], sc1, sh1, ml_w_in[j].astype(BF16))
            q, k, v, xc, pre = _mlstm_pre(up, ml_conv_w[j], ml_conv_b[j], ml_wq[j], ml_wk[j], ml_wv[j],
                                          ml_gate_w[j], ml_gate_b[j])
            hn = _mlstm_scan(q, k, v, pre, ml_outnorm_g[j])
            x = _mlstm_out(hn, xc, up, ml_skip[j], ml_w_out[j].astype(BF16), x, g1)
        x = _hier_moe_residual(x, norm_ffn_g[i], sc2, sh2, g2, moe_w_group[i], moe_w_router[i],
                               moe_w1[i], moe_w3[i], moe_w2[i])
    return x
```

```python
import functools
import math

import jax
import jax.numpy as jnp
from jax import lax
from jax.experimental import pallas as pl
from jax.experimental.pallas import tpu as pltpu

EPS = 1e-6
N_MIXERS = 2
DA_HEADS = 8
DA_HEAD_DIM = 64
N_BUCKETS = 32
MAX_DISTANCE = 128
ML_HEADS = 4
ML_CHUNK = 256
ML_QKV_BLOCK = 4
N_GROUPS = 4
EXPERTS_PER_GROUP = 8
N_EXPERTS = N_GROUPS * EXPERTS_PER_GROUP

LANES = 128
SUBLANES = 8
VMEM_LIMIT = 48 * 1024 * 1024
ATTN_Q_TILE = 512
ATTN_ROW_BLOCK = 128
ROW_TILE = 512
MOE_ROW_TILE = 512
MOE_TOKEN_TILE = 512
MLSTM_COL_TILE = 256

F32 = jnp.float32
BF16 = jnp.bfloat16
LOG2E = math.log2(math.e)


def _params(*sem):
    return pltpu.CompilerParams(dimension_semantics=sem, vmem_limit_bytes=VMEM_LIMIT)


def _silu(x):
    return x / (1.0 + jnp.exp(-x))


def _log_sigmoid(x):
    return jnp.minimum(x, 0.0) - jnp.log(1.0 + jnp.exp(-jnp.abs(x)))


def _modulated_norm(x, g, sc, sh):
    ms = jnp.mean(x * x, axis=-1, keepdims=True)
    return x * lax.rsqrt(ms + EPS) * g * (1.0 + sc) + sh


def _ada_kernel(c_ref, w_ref, b_ref, o_ref):
    ca = _silu(c_ref[...]).astype(BF16)
    o_ref[0] = jnp.dot(ca, w_ref[0].astype(BF16), preferred_element_type=F32) + b_ref[0]


def _ada_mod(c, ada_w, ada_b):
    depth, d, n = ada_w.shape
    b = c.shape[0]
    tn = 1536
    return pl.pallas_call(
        _ada_kernel,
        grid=(depth, n // tn),
        in_specs=[
            pl.BlockSpec((b, d), lambda i, j: (0, 0)),
            pl.BlockSpec((1, d, tn), lambda i, j: (i, 0, j)),
            pl.BlockSpec((1, 1, tn), lambda i, j: (i, 0, j)),
        ],
        out_specs=pl.BlockSpec((1, b, tn), lambda i, j: (i, 0, j)),
        out_shape=jax.ShapeDtypeStruct((depth, b, n), F32),
        compiler_params=_params("parallel", "parallel"),
    )(c, ada_w, ada_b.reshape(depth, 1, n))


def _prenorm_matmul_kernel(x_ref, g_ref, sc_ref, sh_ref, w_ref, o_ref, *, n_chunk):
    y = _modulated_norm(x_ref[0], g_ref[...], sc_ref[0], sh_ref[0]).astype(BF16)
    n = w_ref.shape[1]
    for n0 in range(0, n, n_chunk):
        o_ref[0, :, n0:n0 + n_chunk] = jnp.dot(
            y, w_ref[:, n0:n0 + n_chunk], preferred_element_type=F32).astype(o_ref.dtype)


def _prenorm_matmul(x, g, sc, sh, w):
    b, s, d = x.shape
    n = w.shape[1]
    ts = min(ROW_TILE, s)
    return pl.pallas_call(
        functools.partial(_prenorm_matmul_kernel, n_chunk=1024),
        grid=(b, s // ts),
        in_specs=[
            pl.BlockSpec((1, ts, d), lambda i, j: (i, j, 0)),
            pl.BlockSpec((1, d), lambda i, j: (0, 0)),
            pl.BlockSpec((1, 1, d), lambda i, j: (i, 0, 0)),
            pl.BlockSpec((1, 1, d), lambda i, j: (i, 0, 0)),
            pl.BlockSpec((d, n), lambda i, j: (0, 0)),
        ],
        out_specs=pl.BlockSpec((1, ts, n), lambda i, j: (i, j, 0)),
        out_shape=jax.ShapeDtypeStruct((b, s, n), BF16),
        compiler_params=_params("parallel", "parallel"),
    )(x, g.reshape(1, d), sc, sh, w)


def _proj_residual_kernel(a_ref, w_ref, x_ref, gate_ref, o_ref):
    y = jnp.dot(a_ref[0], w_ref[...], preferred_element_type=F32)
    o_ref[0] = x_ref[0] + gate_ref[0] * y


def _proj_residual(a, w, x, gate):
    b, s, k = a.shape
    d = w.shape[1]
    ts = min(ROW_TILE, s)
    return pl.pallas_call(
        _proj_residual_kernel,
        grid=(b, s // ts),
        in_specs=[
            pl.BlockSpec((1, ts, k), lambda i, j: (i, j, 0)),
            pl.BlockSpec((k, d), lambda i, j: (0, 0)),
            pl.BlockSpec((1, ts, d), lambda i, j: (i, j, 0)),
            pl.BlockSpec((1, 1, d), lambda i, j: (i, 0, 0)),
        ],
        out_specs=pl.BlockSpec((1, ts, d), lambda i, j: (i, j, 0)),
        out_shape=jax.ShapeDtypeStruct((b, s, d), F32),
        compiler_params=_params("parallel", "parallel"),
    )(a, w, x, gate)


def _t5_bucket(rel):
    nb = N_BUCKETS // 2
    max_exact = nb // 2
    ret = jnp.where(rel > 0, nb, 0)
    n = jnp.abs(rel)
    nf = jnp.maximum(n, 1).astype(F32)
    large = max_exact + (jnp.log(nf / max_exact) / math.log(MAX_DISTANCE / max_exact)
                         * (nb - max_exact)).astype(jnp.int32)
    large = jnp.minimum(large, nb - 1)
    return ret + jnp.where(n < max_exact, n, large)


def _bias_band_kernel(t_ref, o_ref, *, tq, nb):
    u = jnp.broadcast_to(t_ref[0], (tq, t_ref.shape[2]))
    y = pltpu.roll(u, 1, 1, stride=1, stride_axis=0)
    for jb in range(nb):
        o_ref[0, jb] = y[:, tq + jb * LANES:tq + (jb + 1) * LANES]


def _bias_band(rel_table, s, tq):
    h = rel_table.shape[1]
    delta = jnp.arange(-(s - 1), s)
    t = rel_table[_t5_bucket(delta)].T.astype(F32) * LOG2E
    u = jnp.concatenate([t, jnp.zeros((h, 1), F32)], axis=1).reshape(h, 1, 2 * s)
    nb = (2 * s - tq) // LANES
    return pl.pallas_call(
        functools.partial(_bias_band_kernel, tq=tq, nb=nb),
        grid=(h,),
        in_specs=[pl.BlockSpec((1, 1, 2 * s), lambda i: (i, 0, 0))],
        out_specs=pl.BlockSpec((1, nb, tq, LANES), lambda i: (i, 0, 0, 0)),
        out_shape=jax.ShapeDtypeStruct((h, nb, tq, LANES), F32),
        compiler_params=_params("parallel"),
    )(u)


def _group_rms(x, gain, half):
    sq = x * x
    lane = lax.broadcasted_iota(jnp.int32, x.shape, 1)
    lo = lane < half
    s_lo = jnp.sum(jnp.where(lo, sq, 0.0), axis=-1, keepdims=True)
    s_all = jnp.sum(sq, axis=-1, keepdims=True)
    ms = jnp.where(lo, s_lo, s_all - s_lo) * (1.0 / half)
    return x * lax.rsqrt(ms + EPS) * gain


def _diff_attn_kernel(lam_ref, q_ref, k_ref, v_ref, qg_ref, kg_ref, sg_ref, band_ref, o_ref, kn_ref, ve_ref,
                      *, tq, s, dh, out_scale):
    qi = pl.program_id(2)
    nq = pl.num_programs(2)
    dv = 2 * dh

    @pl.when(qi == 0)
    def _():
        kn_ref[...] = _group_rms(k_ref[0].astype(F32), kg_ref[...], dh).astype(BF16)
        ve_ref[:, :dv] = v_ref[0]
        ve_ref[:, dv:] = jnp.ones((s, dv), BF16)

    lam = lam_ref[0]
    q = _group_rms(q_ref[0].astype(F32), qg_ref[...], dh) * (dh ** -0.5 * LOG2E)
    lane = lax.broadcasted_iota(jnp.int32, q.shape, 1)
    q0 = jnp.where(lane < dh, q, 0.0).astype(BF16)
    q1 = jnp.where(lane < dh, 0.0, q).astype(BF16)
    jb0 = (nq - 1 - qi) * (tq // LANES)
    nkb = s // LANES

    def softmax_av(qm, r0, r1):
        sc = lax.dot_general(qm[r0:r1], kn_ref[...], (((1,), (1,)), ((), ())), preferred_element_type=F32)
        sc = sc + jnp.concatenate([band_ref[0, jb0 + kb, r0:r1, :] for kb in range(nkb)], axis=1)
        p = jnp.exp2(sc - jnp.max(sc, axis=-1, keepdims=True)).astype(BF16)
        oe = jnp.dot(p, ve_ref[...], preferred_element_type=F32)
        return oe[:, :dv], oe[:, dv:dv + 1]

    rb = ATTN_ROW_BLOCK
    for r0 in range(0, tq, rb):
        n0, l0 = softmax_av(q0, r0, r0 + rb)
        n1, l1 = softmax_av(q1, r0, r0 + rb)
        o = n0 * (1.0 / l0) - n1 * (lam / l1)
        ms = jnp.mean(o * o, axis=-1, keepdims=True)
        o_ref[0, r0:r0 + rb, :] = (o * lax.rsqrt(ms + EPS) * sg_ref[...] * out_scale).astype(o_ref.dtype)


def _diff_attention(qkv, q_gain, k_gain, subln_g, lam, lam_init, rel_table):
    b, s, _ = qkv.shape
    h, dh = DA_HEADS, DA_HEAD_DIM
    tq = min(ATTN_Q_TILE, s)
    band = _bias_band(rel_table, s, tq)
    nb = band.shape[1]
    qg = jnp.tile(q_gain, 2).reshape(1, 2 * dh)
    kg = jnp.tile(k_gain, 2).reshape(1, 2 * dh)
    kern = functools.partial(_diff_attn_kernel, tq=tq, s=s, dh=dh, out_scale=1.0 - lam_init)
    grid_spec = pltpu.PrefetchScalarGridSpec(
        num_scalar_prefetch=1,
        grid=(b, h, s // tq),
        in_specs=[
            pl.BlockSpec((1, tq, 2 * dh), lambda bi, hi, qi, lam: (bi, qi, hi)),
            pl.BlockSpec((1, s, 2 * dh), lambda bi, hi, qi, lam: (bi, 0, h + hi)),
            pl.BlockSpec((1, s, 2 * dh), lambda bi, hi, qi, lam: (bi, 0, 2 * h + hi)),
            pl.BlockSpec((1, 2 * dh), lambda bi, hi, qi, lam: (0, 0)),
            pl.BlockSpec((1, 2 * dh), lambda bi, hi, qi, lam: (0, 0)),
            pl.BlockSpec((1, 2 * dh), lambda bi, hi, qi, lam: (0, 0)),
            pl.BlockSpec((1, nb, tq, LANES), lambda bi, hi, qi, lam: (hi, 0, 0, 0)),
        ],
        out_specs=pl.BlockSpec((1, tq, 2 * dh), lambda bi, hi, qi, lam: (bi, qi, hi)),
        scratch_shapes=[pltpu.VMEM((s, 2 * dh), BF16), pltpu.VMEM((s, 4 * dh), BF16)],
    )
    return pl.pallas_call(
        kern,
        grid_spec=grid_spec,
        out_shape=jax.ShapeDtypeStruct((b, s, h * 2 * dh), BF16),
        compiler_params=_params("parallel", "parallel", "arbitrary"),
    )(lam.reshape(1), qkv, qkv, qkv, qg, kg, subln_g.reshape(1, 2 * dh), band)


def _mlstm_pre_kernel(xm_ref, cw_ref, cb_ref, wq_ref, wk_ref, wv_ref, gq_ref, gk_ref, gv_ref, gb_ref,
                      q_ref, k_ref, v_ref, xc_ref, pre_ref, pad_ref, *, s, halo):
    j = pl.program_id(1)
    xm = xm_ref[0].astype(F32)
    cbw = xm.shape[1]
    pad_ref[0:8, :] = jnp.zeros((8, cbw), F32)
    pad_ref[8 + s:16 + s, :] = jnp.zeros((8, cbw), F32)
    pad_ref[8:8 + s, :] = xm
    acc = jnp.zeros_like(xm) + cb_ref[...]
    for t in range(2 * halo + 1):
        acc = acc + cw_ref[t:t + 1, :] * pad_ref[8 - halo + t:8 - halo + t + s, :]
    xc = _silu(acc)
    xcb = xc.astype(BF16)
    xmb = xm_ref[0]
    q = jnp.dot(xcb, wq_ref[0], preferred_element_type=F32).astype(BF16)
    k = jnp.dot(xcb, wk_ref[0], preferred_element_type=F32).astype(BF16)
    v = jnp.dot(xmb, wv_ref[0], preferred_element_type=F32).astype(BF16)
    q_ref[0] = q
    k_ref[0] = k
    v_ref[0] = v
    xc_ref[0] = xcb
    part = (jnp.dot(q, gq_ref[...], preferred_element_type=F32)
            + jnp.dot(k, gk_ref[...], preferred_element_type=F32)
            + jnp.dot(v, gv_ref[...], preferred_element_type=F32))

    @pl.when(j == 0)
    def _():
        pre_ref[0] = part + gb_ref[...]

    @pl.when(j != 0)
    def _():
        pre_ref[0] += part


def _blockdiag_dense(w, cb):
    nblk, blk, _ = w.shape
    per = cb // blk
    w4 = w.reshape(nblk // per, per, blk, blk)
    eye = jnp.eye(per, dtype=w.dtype)
    return jnp.einsum('jnio,nm->jnimo', w4, eye).reshape(nblk // per, cb, cb).astype(BF16)


def _mlstm_pre(up, conv_w, conv_b, wq, wk, wv, gate_w, gate_b):
    b, s, c2 = up.shape
    c = c2 // 2
    cb = MLSTM_COL_TILE
    ncb = c // cb
    kw = conv_w.shape[0]
    ng = gate_w.shape[0] * gate_w.shape[-1]

    def gate_mat(i):
        g = gate_w[:, i].transpose(1, 0, 2).reshape(c, ng)
        return jnp.pad(g, ((0, 0), (0, LANES - ng))).astype(BF16)

    gb = jnp.pad(gate_b.reshape(1, ng), ((0, 0), (0, LANES - ng)))
    col = lambda i, j: (i, 0, j)
    out_bf = jax.ShapeDtypeStruct((b, s, c), BF16)
    return pl.pallas_call(
        functools.partial(_mlstm_pre_kernel, s=s, halo=kw // 2),
        grid=(b, ncb),
        in_specs=[
            pl.BlockSpec((1, s, cb), col),
            pl.BlockSpec((kw, cb), lambda i, j: (0, j)),
            pl.BlockSpec((1, cb), lambda i, j: (0, j)),
            pl.BlockSpec((1, cb, cb), lambda i, j: (j, 0, 0)),
            pl.BlockSpec((1, cb, cb), lambda i, j: (j, 0, 0)),
            pl.BlockSpec((1, cb, cb), lambda i, j: (j, 0, 0)),
            pl.BlockSpec((cb, LANES), lambda i, j: (j, 0)),
            pl.BlockSpec((cb, LANES), lambda i, j: (j, 0)),
            pl.BlockSpec((cb, LANES), lambda i, j: (j, 0)),
            pl.BlockSpec((1, LANES), lambda i, j: (0, 0)),
        ],
        out_specs=[
            pl.BlockSpec((1, s, cb), col),
            pl.BlockSpec((1, s, cb), col),
            pl.BlockSpec((1, s, cb), col),
            pl.BlockSpec((1, s, cb), col),
            pl.BlockSpec((1, s, LANES), lambda i, j: (i, 0, 0)),
        ],
        out_shape=[out_bf, out_bf, out_bf, out_bf, jax.ShapeDtypeStruct((b, s, LANES), F32)],
        scratch_shapes=[pltpu.VMEM((s + 16, cb), F32)],
        compiler_params=_params("parallel", "arbitrary"),
    )(up, conv_w, conv_b.reshape(1, c), _blockdiag_dense(wq, cb), _blockdiag_dense(wk, cb),
      _blockdiag_dense(wv, cb), gate_mat(0), gate_mat(1), gate_mat(2), gb)


def _mlstm_scan_kernel(q_ref, k_ref, v_ref, gcol_ref, grow_ref, og_ref, o_ref,
                       cf_ref, cbk_ref, hf_ref, hb_ref, *, L, nc, dk):
    scale = dk ** -0.5
    row = lax.broadcasted_iota(jnp.int32, (L, L), 0)
    coli = lax.broadcasted_iota(jnp.int32, (L, L), 1)
    lower = coli <= row
    upper = coli >= row
    cf_ref[...] = jnp.zeros_like(cf_ref)
    cbk_ref[...] = jnp.zeros_like(cbk_ref)

    def chunk_step(c, c_ref, h_ref, n, m, fwd):
        mask, mask_t = (lower, upper) if fwd else (upper, lower)
        gi = 0 if fwd else 2
        sl = pl.ds(pl.multiple_of(c * L, L), L)
        qb = q_ref[0, sl, :]
        kb = k_ref[0, sl, :]
        vb = v_ref[0, sl, :]
        gc = gcol_ref[0, 0, c]
        gr = grow_ref[0, 0, c]
        ii_col = gc[:, gi:gi + 1]
        lf_col = _log_sigmoid(gc[:, gi + 1:gi + 2])
        ii_row = gr[gi:gi + 1, :]
        lf_row = _log_sigmoid(gr[gi + 1:gi + 2, :])
        b_col = jnp.sum(jnp.where(mask, lf_row, 0.0), axis=1, keepdims=True)
        b_row = jnp.sum(jnp.where(mask_t, lf_col, 0.0), axis=0, keepdims=True)
        dmat = jnp.where(mask, b_col - b_row + ii_row, -jnp.inf)
        inter = b_col + m
        m_t = jnp.maximum(inter, jnp.max(dmat, axis=1, keepdims=True))
        w_intra = jnp.exp(dmat - m_t)
        w_inter = jnp.exp(inter - m_t)
        sqk = lax.dot_general(qb, kb, (((1,), (1,)), ((), ())), preferred_element_type=F32)
        sw = sqk * scale * w_intra
        cmat = c_ref[...]
        num = (w_inter * jnp.dot(qb, cmat.astype(BF16), preferred_element_type=F32)
               + jnp.dot(sw.astype(BF16), vb, preferred_element_type=F32))
        den = (w_inter * jnp.sum(qb.astype(F32) * n, axis=1, keepdims=True)
               + jnp.sum(sw, axis=1, keepdims=True))
        h_ref[sl, :] = num / jnp.maximum(jnp.abs(den), jnp.exp(-m_t))
        b_tot = jnp.sum(lf_row, axis=1, keepdims=True)
        g_col = b_tot - b_col + ii_col
        m_new = jnp.maximum(b_tot + m, jnp.max(g_col, axis=0, keepdims=True))
        decay = jnp.exp(b_tot + m - m_new)
        kw = kb.astype(F32) * scale * jnp.exp(g_col - m_new)
        c_ref[...] = decay * cmat + lax.dot_general(
            kw.astype(BF16), vb, (((0,), (0,)), ((), ())), preferred_element_type=F32)
        n_new = decay * n + jnp.sum(kw, axis=0, keepdims=True)
        return n_new, m_new

    def body(i, carry):
        nf, mf, nb_, mb = carry
        nf, mf = chunk_step(i, cf_ref, hf_ref, nf, mf, True)
        nb_, mb = chunk_step(nc - 1 - i, cbk_ref, hb_ref, nb_, mb, False)
        return nf, mf, nb_, mb

    z_n = jnp.zeros((1, dk), F32)
    z_m = jnp.zeros((1, 1), F32)
    lax.fori_loop(0, nc, body, (z_n, z_m, z_n, z_m))

    def finish(c, carry):
        sl = pl.ds(pl.multiple_of(c * L, L), L)
        hs = hf_ref[sl, :] + hb_ref[sl, :]
        ms = jnp.mean(hs * hs, axis=-1, keepdims=True)
        o_ref[0, sl, :] = (hs * lax.rsqrt(ms + EPS) * og_ref[0]).astype(o_ref.dtype)
        return carry

    lax.fori_loop(0, nc, finish, 0)


def _mlstm_scan(q, k, v, pre, outnorm_g):
    b, s, c = q.shape
    hh = ML_HEADS
    dk = c // hh
    L = min(ML_CHUNK, s)
    nc = s // L
    idx = jnp.array([[0 * 2 * hh + h, 0 * 2 * hh + hh + h, 2 * hh + h, 2 * hh + hh + h] for h in range(hh)])
    g = pre[:, :, idx]
    gcol = g.transpose(0, 2, 1, 3).reshape(b, hh, nc, L, 4)
    grow = gcol.transpose(0, 1, 2, 4, 3)
    head = lambda i, j: (i, 0, j)
    return pl.pallas_call(
        functools.partial(_mlstm_scan_kernel, L=L, nc=nc, dk=dk),
        grid=(b, hh),
        in_specs=[
            pl.BlockSpec((1, s, dk), head),
            pl.BlockSpec((1, s, dk), head),
            pl.BlockSpec((1, s, dk), head),
            pl.BlockSpec((1, 1, nc, L, 4), lambda i, j: (i, j, 0, 0, 0)),
            pl.BlockSpec((1, 1, nc, 4, L), lambda i, j: (i, j, 0, 0, 0)),
            pl.BlockSpec((1, 1, dk), lambda i, j: (j, 0, 0)),
        ],
        out_specs=pl.BlockSpec((1, s, dk), head),
        out_shape=jax.ShapeDtypeStruct((b, s, c), BF16),
        scratch_shapes=[pltpu.VMEM((dk, dk), F32), pltpu.VMEM((dk, dk), F32),
                        pltpu.VMEM((s, dk), F32), pltpu.VMEM((s, dk), F32)],
        compiler_params=_params("parallel", "parallel"),
    )(q, k, v, gcol, grow, outnorm_g.reshape(hh, 1, dk))


def _mlstm_out_kernel(hn_ref, xc_ref, z_ref, skip_ref, w_ref, x_ref, gate_ref, o_ref):
    a = (hn_ref[0].astype(F32) + skip_ref[...] * xc_ref[0].astype(F32)) * _silu(z_ref[0].astype(F32))
    y = jnp.dot(a.astype(BF16), w_ref[...], preferred_element_type=F32)
    o_ref[0] = x_ref[0] + gate_ref[0] * y


def _mlstm_out(hn, xc, up, skip, w, x, gate):
    b, s, c = hn.shape
    d = w.shape[1]
    ts = min(ROW_TILE, s)
    row = lambda i, j: (i, j, 0)
    return pl.pallas_call(
        _mlstm_out_kernel,
        grid=(b, s // ts),
        in_specs=[
            pl.BlockSpec((1, ts, c), row),
            pl.BlockSpec((1, ts, c), row),
            pl.BlockSpec((1, ts, c), lambda i, j: (i, j, 1)),
            pl.BlockSpec((1, c), lambda i, j: (0, 0)),
            pl.BlockSpec((c, d), lambda i, j: (0, 0)),
            pl.BlockSpec((1, ts, d), row),
            pl.BlockSpec((1, 1, d), lambda i, j: (i, 0, 0)),
        ],
        out_specs=pl.BlockSpec((1, ts, d), row),
        out_shape=jax.ShapeDtypeStruct((b, s, d), F32),
        compiler_params=_params("parallel", "parallel"),
    )(hn, xc, up, skip.reshape(1, c), w, x, gate)


def _router_kernel(x_ref, g_ref, sc_ref, sh_ref, w_ref, lpc_ref, lpr_ref, wgt_ref, stats_ref, cnt_ref, run_ref):
    first = (pl.program_id(0) == 0) & (pl.program_id(1) == 0)

    @pl.when(first)
    def _():
        run_ref[...] = jnp.zeros_like(run_ref)

    hf = _modulated_norm(x_ref[0], g_ref[...], sc_ref[0], sh_ref[0])
    logits = jnp.dot(hf, w_ref[...], preferred_element_type=F32, precision=lax.Precision.HIGHEST)
    tm = logits.shape[0]
    lane = lax.broadcasted_iota(jnp.int32, logits.shape, 1).astype(F32)
    neg = -jnp.inf
    is_g = lane < N_GROUPS
    gl = jnp.where(is_g, logits, neg)
    gmax = jnp.max(gl, axis=-1, keepdims=True)
    g_sel = jnp.min(jnp.where(gl == gmax, lane, float(LANES)), axis=-1, keepdims=True)
    p_g = 1.0 / jnp.sum(jnp.where(is_g, jnp.exp(gl - gmax), 0.0), axis=-1, keepdims=True)
    e_lane = lane - N_GROUPS
    in_grp = (e_lane >= g_sel * EXPERTS_PER_GROUP) & (e_lane < (g_sel + 1) * EXPERTS_PER_GROUP)
    el = jnp.where(in_grp, logits, neg)
    emax = jnp.max(el, axis=-1, keepdims=True)
    i1 = jnp.min(jnp.where(el == emax, e_lane, float(LANES)), axis=-1, keepdims=True)
    el2 = jnp.where(e_lane == i1, neg, el)
    emax2 = jnp.max(el2, axis=-1, keepdims=True)
    i2 = jnp.min(jnp.where(el2 == emax2, e_lane, float(LANES)), axis=-1, keepdims=True)
    t2 = jnp.exp(emax2 - emax)
    w1 = p_g / (1.0 + t2)
    w2 = p_g * t2 / (1.0 + t2)
    a = jnp.where(e_lane == i1, 1.0, jnp.where(e_lane == i2, 1.0, 0.0))
    r = lax.broadcasted_iota(jnp.int32, (tm, tm), 0)
    cc = lax.broadcasted_iota(jnp.int32, (tm, tm), 1)
    tri = jnp.where(cc < r, 1.0, 0.0).astype(BF16)
    rank = jnp.dot(tri, a.astype(BF16), preferred_element_type=F32)
    k8 = jnp.ceil(jnp.sum(a, axis=0, keepdims=True) * (1.0 / SUBLANES))
    ur = lax.broadcasted_iota(jnp.int32, (LANES, LANES), 0)
    uc = lax.broadcasted_iota(jnp.int32, (LANES, LANES), 1)
    upper = jnp.where(ur < uc, 1.0, 0.0).astype(BF16)
    k8_rows = jnp.broadcast_to(k8, (SUBLANES, LANES)).astype(BF16)
    off = jnp.dot(k8_rows, upper, preferred_element_type=F32)[0:1] * SUBLANES
    pos = off + rank
    lp0 = jnp.sum(jnp.where(e_lane == i1, pos, 0.0), axis=-1, keepdims=True)
    lp1 = jnp.sum(jnp.where(e_lane == i2, pos, 0.0), axis=-1, keepdims=True)
    run_old = run_ref[...]
    run_new = run_old + k8 * SUBLANES
    run_ref[...] = run_new
    cnt_ref[...] = run_new.astype(jnp.int32)
    srow = lax.broadcasted_iota(jnp.int32, (SUBLANES, LANES), 0)
    stats_ref[0] = jnp.where(srow == 0, k8, jnp.where(srow == 1, off, jnp.where(
        srow == 2, run_old, 0.0))).astype(jnp.int32)
    lp = jnp.where(lane == 0, lp0, jnp.where(lane == 1, lp1, 0.0))
    lpc_ref[0] = lp.astype(jnp.int32)
    lpr_ref[0, 0] = lp.T[:SUBLANES].astype(jnp.int32)
    wgt_ref[0] = jnp.where(lane == 0, w1, jnp.where(lane == 1, w2, 0.0))


def _router(x, g, sc, sh, w_group, w_router):
    b, s, d = x.shape
    ts = min(MOE_TOKEN_TILE, s)
    nt = s // ts
    w = jnp.concatenate([w_group, w_router], axis=1)
    w = jnp.pad(w, ((0, 0), (0, LANES - w.shape[1])))
    row = lambda i, j: (i, j, 0)
    return pl.pallas_call(
        _router_kernel,
        grid=(b, nt),
        in_specs=[
            pl.BlockSpec((1, ts, d), row),
            pl.BlockSpec((1, d), lambda i, j: (0, 0)),
            pl.BlockSpec((1, 1, d), lambda i, j: (i, 0, 0)),
            pl.BlockSpec((1, 1, d), lambda i, j: (i, 0, 0)),
            pl.BlockSpec((d, LANES), lambda i, j: (0, 0)),
        ],
        out_specs=[
            pl.BlockSpec((1, ts, LANES), row),
            pl.BlockSpec((1, 1, SUBLANES, ts), lambda i, j: (i, j, 0, 0)),
            pl.BlockSpec((1, ts, LANES), row),
            pl.BlockSpec((1, SUBLANES, LANES), lambda i, j: (i * nt + j, 0, 0)),
            pl.BlockSpec((1, LANES), lambda i, j: (0, 0)),
        ],
        out_shape=[jax.ShapeDtypeStruct((b, s, LANES), jnp.int32),
                   jax.ShapeDtypeStruct((b, nt, SUBLANES, ts), jnp.int32),
                   jax.ShapeDtypeStruct((b, s, LANES), F32),
                   jax.ShapeDtypeStruct((b * nt, SUBLANES, LANES), jnp.int32),
                   jax.ShapeDtypeStruct((1, LANES), jnp.int32)],
        scratch_shapes=[pltpu.VMEM((1, LANES), F32)],
        compiler_params=_params("arbitrary", "arbitrary"),
    )(x, g.reshape(1, d), sc, sh, w)


def _chunk_copies(meta_ref, local_ref, hbm_ref, sem, *, to_hbm, start, max_groups):
    def per_expert(e, carry):
        k = meta_ref[0, 0, e]
        lo = meta_ref[0, 0, N_EXPERTS + e]
        hi = meta_ref[0, 0, 2 * N_EXPERTS + e]
        done = jnp.int32(0)
        for bit in range(max_groups.bit_length()):
            rows = SUBLANES << bit
            take = (k >> bit) & 1

            @pl.when(take == 1)
            def _():
                loc = local_ref.at[pl.ds(pl.multiple_of(lo + done, SUBLANES), rows)]
                far = hbm_ref.at[pl.ds(pl.multiple_of(hi + done, SUBLANES), rows)]
                cp = pltpu.make_async_copy(loc, far, sem) if to_hbm else pltpu.make_async_copy(far, loc, sem)
                if start:
                    cp.start()
                else:
                    cp.wait()

            done = done + take * rows
        return carry

    lax.fori_loop(0, N_EXPERTS, per_expert, 0)


def _dispatch_kernel(last_ref, meta_ref, lpr_ref, x_ref, g_ref, sc_ref, sh_ref, xs_ref, loc_ref, zero_ref,
                     sem, zsem):
    tm = x_ref.shape[1]
    zrows = zero_ref.shape[0]

    @pl.when((pl.program_id(0) == 0) & (pl.program_id(1) == 0))
    def _():
        zero_ref[...] = jnp.zeros_like(zero_ref)

        def fill(e):
            start = pl.multiple_of(last_ref[e], zrows)
            return pltpu.make_async_copy(zero_ref, xs_ref.at[pl.ds(start, zrows)], zsem)

        for e in range(N_EXPERTS):
            @pl.when(last_ref[e] >= 0)
            def _():
                fill(e).start()
        for e in range(N_EXPERTS):
            @pl.when(last_ref[e] >= 0)
            def _():
                fill(e).wait()

        def tail(k):
            start = pl.multiple_of(k * zrows, zrows)
            return pltpu.make_async_copy(zero_ref, xs_ref.at[pl.ds(start, zrows)], zsem)

        n_used = last_ref[N_EXPERTS]
        n_tiles = xs_ref.shape[0] // zrows
        lax.fori_loop(n_used, n_tiles, lambda k, c: (tail(k).start(), c)[1], 0)
        lax.fori_loop(n_used, n_tiles, lambda k, c: (tail(k).wait(), c)[1], 0)

    hf = _modulated_norm(x_ref[0], g_ref[...], sc_ref[0], sh_ref[0]).astype(BF16)
    rows = lax.broadcasted_iota(jnp.int32, (loc_ref.shape[0], tm), 0)
    sel = jnp.where(rows == lpr_ref[0, 0, 0:1, :], 1.0,
                    jnp.where(rows == lpr_ref[0, 0, 1:2, :], 1.0, 0.0)).astype(BF16)
    loc_ref[...] = jnp.dot(sel, hf, preferred_element_type=F32)
    copies = functools.partial(_chunk_copies, meta_ref, loc_ref, xs_ref, sem, to_hbm=True,
                               max_groups=tm // SUBLANES)
    copies(start=True)
    copies(start=False)


def _local_rows(tm):
    return 2 * tm + N_EXPERTS * SUBLANES


def _dispatch(x, g, sc, sh, meta, lpr, last_tile_row, n_rows):
    b, s, d = x.shape
    tm = min(MOE_TOKEN_TILE, s)
    nt = s // tm
    row = lambda i, j, last: (i, j, 0)
    grid_spec = pltpu.PrefetchScalarGridSpec(
        num_scalar_prefetch=1,
        grid=(b, nt),
        in_specs=[
            pl.BlockSpec((1, 1, LANES), lambda i, j, last: (i * nt + j, 0, 0), memory_space=pltpu.SMEM),
            pl.BlockSpec((1, 1, SUBLANES, tm), lambda i, j, last: (i, j, 0, 0)),
            pl.BlockSpec((1, tm, d), row),
            pl.BlockSpec((1, d), lambda i, j, last: (0, 0)),
            pl.BlockSpec((1, 1, d), lambda i, j, last: (i, 0, 0)),
            pl.BlockSpec((1, 1, d), lambda i, j, last: (i, 0, 0)),
        ],
        out_specs=pl.BlockSpec(memory_space=pl.ANY),
        scratch_shapes=[pltpu.VMEM((_local_rows(tm), d), F32), pltpu.VMEM((MOE_ROW_TILE, d), F32),
                        pltpu.SemaphoreType.DMA(()), pltpu.SemaphoreType.DMA(())],
    )
    return pl.pallas_call(
        _dispatch_kernel,
        grid_spec=grid_spec,
        out_shape=jax.ShapeDtypeStruct((n_rows, d), F32),
        compiler_params=_params("arbitrary", "arbitrary"),
    )(last_tile_row, meta, lpr, x, g.reshape(1, d), sc, sh)


def _expert_kernel(te_ref, nu_ref, xs_ref, w1_ref, w3_ref, w2_ref, ys_ref):
    i = pl.program_id(0)

    @pl.when(i < nu_ref[0])
    def _():
        xb = xs_ref[...].astype(BF16)
        a = jnp.dot(xb, w1_ref[0], preferred_element_type=F32)
        bb = jnp.dot(xb, w3_ref[0], preferred_element_type=F32)
        hmid = (_silu(a) * bb).astype(BF16)
        ys_ref[...] = jnp.dot(hmid, w2_ref[0], preferred_element_type=F32)

    @pl.when(i >= nu_ref[0])
    def _():
        ys_ref[...] = jnp.zeros_like(ys_ref)


def _expert_ffn(xs, tile_expert, n_used, w1, w3, w2):
    p, d = xs.shape
    de = w1.shape[2]
    tm = MOE_ROW_TILE
    grid_spec = pltpu.PrefetchScalarGridSpec(
        num_scalar_prefetch=2,
        grid=(p // tm,),
        in_specs=[
            pl.BlockSpec((tm, d), lambda i, te, nu: (jnp.maximum(jnp.minimum(i, nu[0] - 1), 0), 0)),
            pl.BlockSpec((1, d, de), lambda i, te, nu: (te[i], 0, 0)),
            pl.BlockSpec((1, d, de), lambda i, te, nu: (te[i], 0, 0)),
            pl.BlockSpec((1, de, d), lambda i, te, nu: (te[i], 0, 0)),
        ],
        out_specs=pl.BlockSpec((tm, d), lambda i, te, nu: (i, 0)),
    )
    return pl.pallas_call(
        _expert_kernel,
        grid_spec=grid_spec,
        out_shape=jax.ShapeDtypeStruct((p, d), F32),
        compiler_params=_params("arbitrary"),
    )(tile_expert, n_used, xs, w1, w3, w2)


def _combine_kernel(meta_ref, ys_ref, lpc_ref, wgt_ref, x_ref, gate_ref, o_ref, loc_ref, sem):
    tm = x_ref.shape[1]
    loc_ref[2 * tm:, :] = jnp.zeros((loc_ref.shape[0] - 2 * tm, loc_ref.shape[1]), F32)
    copies = functools.partial(_chunk_copies, meta_ref, loc_ref, ys_ref, sem, to_hbm=False,
                               max_groups=tm // SUBLANES)
    copies(start=True)
    copies(start=False)
    cols = lax.broadcasted_iota(jnp.int32, (tm, loc_ref.shape[0]), 1)
    lp = lpc_ref[0]
    wgt = wgt_ref[0]
    wm = jnp.where(cols == lp[:, 0:1], wgt[:, 0:1], jnp.where(cols == lp[:, 1:2], wgt[:, 1:2], 0.0))
    w_hi = wm.astype(BF16)
    w_lo = (wm - w_hi.astype(F32)).astype(BF16)
    yb = loc_ref[...].astype(BF16)
    y = jnp.dot(w_hi, yb, preferred_element_type=F32) + jnp.dot(w_lo, yb, preferred_element_type=F32)
    o_ref[0] = x_ref[0] + gate_ref[0] * y


def _combine(ys, meta, lpc, wgt, x, gate):
    b, s, d = x.shape
    tm = min(MOE_TOKEN_TILE, s)
    nt = s // tm
    row = lambda i, j: (i, j, 0)
    return pl.pallas_call(
        _combine_kernel,
        grid=(b, nt),
        in_specs=[
            pl.BlockSpec((1, 1, LANES), lambda i, j: (i * nt + j, 0, 0), memory_space=pltpu.SMEM),
            pl.BlockSpec(memory_space=pl.ANY),
            pl.BlockSpec((1, tm, LANES), row),
            pl.BlockSpec((1, tm, LANES), row),
            pl.BlockSpec((1, tm, d), row),
            pl.BlockSpec((1, 1, d), lambda i, j: (i, 0, 0)),
        ],
        out_specs=pl.BlockSpec((1, tm, d), row),
        out_shape=jax.ShapeDtypeStruct((b, s, d), F32),
        scratch_shapes=[pltpu.VMEM((_local_rows(tm), d), F32), pltpu.SemaphoreType.DMA(())],
        compiler_params=_params("arbitrary", "arbitrary"),
    )(meta, ys, lpc, wgt, x, gate)


def _hier_moe_residual(x, g, sc, sh, gate, w_group, w_router, w1, w3, w2):
    b, s, d = x.shape
    t = b * s
    lpc, lpr, wgt, stats, cnt = _router(x, g, sc, sh, w_group, w_router)
    tm = MOE_ROW_TILE
    ex = slice(N_GROUPS, N_GROUPS + N_EXPERTS)
    counts = cnt[0, ex]
    tiles = (counts + tm - 1) // tm
    tile_end = jnp.cumsum(tiles)
    base = (tile_end - tiles) * tm
    n_blocks = stats.shape[0]
    n_tiles = (2 * t + n_blocks * N_EXPERTS * SUBLANES) // tm + N_EXPERTS
    tile_expert = jnp.minimum(
        jnp.sum(tile_end[None, :] <= jnp.arange(n_tiles)[:, None], axis=1), N_EXPERTS - 1).astype(jnp.int32)
    n_used = tile_end[-1:].astype(jnp.int32)
    meta = jnp.concatenate([stats[:, 0, ex], stats[:, 1, ex], base[None, :] + stats[:, 2, ex],
                            jnp.zeros((n_blocks, LANES - 3 * N_EXPERTS), jnp.int32)], axis=1)
    meta = meta.astype(jnp.int32).reshape(n_blocks, 1, LANES)
    last_tile_row = jnp.where(tiles > 0, (tile_end - 1) * tm, -1).astype(jnp.int32)
    last_tile_row = jnp.concatenate([last_tile_row, n_used])
    xs = _dispatch(x, g, sc, sh, meta, lpr, last_tile_row, n_tiles * tm)
    ys = _expert_ffn(xs, tile_expert, n_used, w1.astype(BF16), w3.astype(BF16), w2.astype(BF16))
    return _combine(ys, meta, lpc, wgt, x, gate)


def kernel(x, c, rel_table, ada_w, ada_b, norm_mix_g, norm_ffn_g, da_w_in, da_w_out, da_q_gain, da_k_gain, da_lam_q1, da_lam_k1, da_lam_q2, da_lam_k2, da_subln_g, ml_w_in, ml_conv_w, ml_conv_b, ml_wq, ml_wk, ml_wv, ml_gate_w, ml_gate_b, ml_outnorm_g, ml_skip, ml_w_out, moe_w_group, moe_w_router, moe_w1, moe_w3, moe_w2):
    depth = ada_w.shape[0]
    d = x.shape[-1]
    mod = _ada_mod(c, ada_w, ada_b)
    for i in range(depth):
        sh1, sc1, g1, sh2, sc2, g2 = [mod[i, :, None, k * d:(k + 1) * d] for k in range(6)]
        j = i // N_MIXERS
        if i % N_MIXERS == 0:
            qkv = _prenorm_matmul(x, norm_mix_g[i], sc1, sh1, da_w_in[j].astype(BF16))
            lam_init = 0.8 - 0.6 * math.exp(-0.3 * i)
            lam = (jnp.exp(jnp.sum(da_lam_q1[j] * da_lam_k1[j])) - jnp.exp(jnp.sum(da_lam_q2[j] * da_lam_k2[j]))
                   + lam_init).astype(F32)
            o = _diff_attention(qkv, da_q_gain[j], da_k_gain[j], da_subln_g[j], lam, lam_init, rel_table)
            x = _proj_residual(o, da_w_out[j].astype(BF16), x, g1)
        else:
            up = _prenorm_matmul(x, norm_mix_g[i], sc1, sh1, ml_w_in[j].astype(BF16))
            q, k, v, xc, pre = _mlstm_pre(up, ml_conv_w[j], ml_conv_b[j], ml_wq[j], ml_wk[j], ml_wv[j],
                                          ml_gate_w[j], ml_gate_b[j])
            hn = _mlstm_scan(q, k, v, pre, ml_outnorm_g[j])
            x = _mlstm_out(hn, xc, up, ml_skip[j], ml_w_out[j].astype(BF16), x, g1)
        x = _hier_moe_residual(x, norm_ffn_g[i], sc2, sh2, g2, moe_w_group[i], moe_w_router[i],
                               moe_w1[i], moe_w3[i], moe_w2[i])
    return x
```

```python
import functools
import math

import jax
import jax.numpy as jnp
from jax import lax
from jax.experimental import pallas as pl
from jax.experimental.pallas import tpu as pltpu

EPS = 1e-6
N_MIXERS = 2
DA_HEADS = 8
DA_HEAD_DIM = 64
N_BUCKETS = 32
MAX_DISTANCE = 128
ML_HEADS = 4
ML_CHUNK = 256
ML_QKV_BLOCK = 4
N_GROUPS = 4
EXPERTS_PER_GROUP = 8
N_EXPERTS = N_GROUPS * EXPERTS_PER_GROUP

LANES = 128
SUBLANES = 8
VMEM_LIMIT = 48 * 1024 * 1024
ATTN_Q_TILE = 2048
ATTN_ROW_BLOCK = 128
ROW_TILE = 512
MOE_ROW_TILE = 512
MOE_TOKEN_TILE = 512
MLSTM_COL_TILE = 256
MLSTM_ROW_CHUNK = 512

F32 = jnp.float32
BF16 = jnp.bfloat16
LOG2E = math.log2(math.e)


def _params(*sem):
    return pltpu.CompilerParams(dimension_semantics=sem, vmem_limit_bytes=VMEM_LIMIT)


def _silu(x):
    return x / (1.0 + jnp.exp(-x))


def _log_sigmoid(x):
    return jnp.minimum(x, 0.0) - jnp.log(1.0 + jnp.exp(-jnp.abs(x)))


def _modulated_norm(x, g, sc, sh):
    ms = jnp.mean(x * x, axis=-1, keepdims=True)
    return x * lax.rsqrt(ms + EPS) * g * (1.0 + sc) + sh


def _ada_kernel(c_ref, w_ref, b_ref, o_ref):
    ca = _silu(c_ref[...]).astype(BF16)
    o_ref[0] = jnp.dot(ca, w_ref[0].astype(BF16), preferred_element_type=F32) + b_ref[0]


def _ada_mod(c, ada_w, ada_b):
    depth, d, n = ada_w.shape
    b = c.shape[0]
    tn = 1536
    return pl.pallas_call(
        _ada_kernel,
        grid=(depth, n // tn),
        in_specs=[
            pl.BlockSpec((b, d), lambda i, j: (0, 0)),
            pl.BlockSpec((1, d, tn), lambda i, j: (i, 0, j)),
            pl.BlockSpec((1, 1, tn), lambda i, j: (i, 0, j)),
        ],
        out_specs=pl.BlockSpec((1, b, tn), lambda i, j: (i, 0, j)),
        out_shape=jax.ShapeDtypeStruct((depth, b, n), F32),
        compiler_params=_params("parallel", "parallel"),
    )(c, ada_w, ada_b.reshape(depth, 1, n))


def _prenorm_matmul_kernel(x_ref, g_ref, sc_ref, sh_ref, w_ref, o_ref, *, n_chunk):
    y = _modulated_norm(x_ref[0], g_ref[...], sc_ref[0], sh_ref[0]).astype(BF16)
    n = w_ref.shape[1]
    for n0 in range(0, n, n_chunk):
        o_ref[0, :, n0:n0 + n_chunk] = jnp.dot(
            y, w_ref[:, n0:n0 + n_chunk], preferred_element_type=F32).astype(o_ref.dtype)


def _prenorm_matmul(x, g, sc, sh, w):
    b, s, d = x.shape
    n = w.shape[1]
    ts = min(ROW_TILE, s)
    return pl.pallas_call(
        functools.partial(_prenorm_matmul_kernel, n_chunk=1024),
        grid=(b, s // ts),
        in_specs=[
            pl.BlockSpec((1, ts, d), lambda i, j: (i, j, 0)),
            pl.BlockSpec((1, d), lambda i, j: (0, 0)),
            pl.BlockSpec((1, 1, d), lambda i, j: (i, 0, 0)),
            pl.BlockSpec((1, 1, d), lambda i, j: (i, 0, 0)),
            pl.BlockSpec((d, n), lambda i, j: (0, 0)),
        ],
        out_specs=pl.BlockSpec((1, ts, n), lambda i, j: (i, j, 0)),
        out_shape=jax.ShapeDtypeStruct((b, s, n), BF16),
        compiler_params=_params("parallel", "parallel"),
    )(x, g.reshape(1, d), sc, sh, w)


def _proj_residual_kernel(a_ref, w_ref, x_ref, gate_ref, o_ref):
    y = jnp.dot(a_ref[0], w_ref[...], preferred_element_type=F32)
    o_ref[0] = x_ref[0] + gate_ref[0] * y


def _proj_residual(a, w, x, gate):
    b, s, k = a.shape
    d = w.shape[1]
    ts = min(ROW_TILE, s)
    return pl.pallas_call(
        _proj_residual_kernel,
        grid=(b, s // ts),
        in_specs=[
            pl.BlockSpec((1, ts, k), lambda i, j: (i, j, 0)),
            pl.BlockSpec((k, d), lambda i, j: (0, 0)),
            pl.BlockSpec((1, ts, d), lambda i, j: (i, j, 0)),
            pl.BlockSpec((1, 1, d), lambda i, j: (i, 0, 0)),
        ],
        out_specs=pl.BlockSpec((1, ts, d), lambda i, j: (i, j, 0)),
        out_shape=jax.ShapeDtypeStruct((b, s, d), F32),
        compiler_params=_params("parallel", "parallel"),
    )(a, w, x, gate)


def _t5_bucket(rel):
    nb = N_BUCKETS // 2
    max_exact = nb // 2
    ret = jnp.where(rel > 0, nb, 0)
    n = jnp.abs(rel)
    nf = jnp.maximum(n, 1).astype(F32)
    large = max_exact + (jnp.log(nf / max_exact) / math.log(MAX_DISTANCE / max_exact)
                         * (nb - max_exact)).astype(jnp.int32)
    large = jnp.minimum(large, nb - 1)
    return ret + jnp.where(n < max_exact, n, large)


def _bias_band_kernel(t_ref, o_ref, *, tq, nb):
    u = jnp.broadcast_to(t_ref[0], (tq, t_ref.shape[2]))
    y = pltpu.roll(u, 1, 1, stride=1, stride_axis=0)
    for jb in range(nb):
        o_ref[0, jb] = y[:, tq + jb * LANES:tq + (jb + 1) * LANES]


def _bias_band(rel_table, s, tq):
    h = rel_table.shape[1]
    delta = jnp.arange(-(s - 1), s)
    t = rel_table[_t5_bucket(delta)].T.astype(F32) * LOG2E
    u = jnp.concatenate([t, jnp.zeros((h, 1), F32)], axis=1).reshape(h, 1, 2 * s)
    nb = (2 * s - tq) // LANES
    return pl.pallas_call(
        functools.partial(_bias_band_kernel, tq=tq, nb=nb),
        grid=(h,),
        in_specs=[pl.BlockSpec((1, 1, 2 * s), lambda i: (i, 0, 0))],
        out_specs=pl.BlockSpec((1, nb, tq, LANES), lambda i: (i, 0, 0, 0)),
        out_shape=jax.ShapeDtypeStruct((h, nb, tq, LANES), F32),
        compiler_params=_params("parallel"),
    )(u)


def _group_rms(x, gain, half):
    sq = x * x
    lane = lax.broadcasted_iota(jnp.int32, x.shape, 1)
    lo = lane < half
    s_lo = jnp.sum(jnp.where(lo, sq, 0.0), axis=-1, keepdims=True)
    s_all = jnp.sum(sq, axis=-1, keepdims=True)
    ms = jnp.where(lo, s_lo, s_all - s_lo) * (1.0 / half)
    return x * lax.rsqrt(ms + EPS) * gain


def _diff_attn_kernel(lam_ref, q_ref, k_ref, v_ref, qg_ref, kg_ref, sg_ref, band_ref, o_ref, kn_ref, ve_ref,
                      *, tq, s, dh, out_scale):
    qi = pl.program_id(2)
    nq = pl.num_programs(2)
    dv = 2 * dh

    @pl.when(qi == 0)
    def _():
        kn_ref[...] = _group_rms(k_ref[0].astype(F32), kg_ref[...], dh).astype(BF16)
        ve_ref[:, :dv] = v_ref[0]
        ve_ref[:, dv:] = jnp.ones((s, dv), BF16)

    lam = lam_ref[0]
    q = _group_rms(q_ref[0].astype(F32), qg_ref[...], dh) * (dh ** -0.5 * LOG2E)
    lane = lax.broadcasted_iota(jnp.int32, q.shape, 1)
    q0 = jnp.where(lane < dh, q, 0.0).astype(BF16)
    q1 = jnp.where(lane < dh, 0.0, q).astype(BF16)
    jb0 = (nq - 1 - qi) * (tq // LANES)
    nkb = s // LANES

    def softmax_av(qm, r0, r1):
        sc = lax.dot_general(qm[r0:r1], kn_ref[...], (((1,), (1,)), ((), ())), preferred_element_type=F32)
        sc = sc + jnp.concatenate([band_ref[0, jb0 + kb, r0:r1, :] for kb in range(nkb)], axis=1)
        p = jnp.exp2(sc - jnp.max(sc, axis=-1, keepdims=True)).astype(BF16)
        oe = jnp.dot(p, ve_ref[...], preferred_element_type=F32)
        return oe[:, :dv], oe[:, dv:dv + 1]

    rb = ATTN_ROW_BLOCK
    for r0 in range(0, tq, rb):
        n0, l0 = softmax_av(q0, r0, r0 + rb)
        n1, l1 = softmax_av(q1, r0, r0 + rb)
        o = n0 * (1.0 / l0) - n1 * (lam / l1)
        ms = jnp.mean(o * o, axis=-1, keepdims=True)
        o_ref[0, r0:r0 + rb, :] = (o * lax.rsqrt(ms + EPS) * sg_ref[...] * out_scale).astype(o_ref.dtype)


def _diff_attention(qkv, q_gain, k_gain, subln_g, lam, lam_init, rel_table):
    b, s, _ = qkv.shape
    h, dh = DA_HEADS, DA_HEAD_DIM
    tq = min(ATTN_Q_TILE, s)
    band = _bias_band(rel_table, s, tq)
    nb = band.shape[1]
    qg = jnp.tile(q_gain, 2).reshape(1, 2 * dh)
    kg = jnp.tile(k_gain, 2).reshape(1, 2 * dh)
    kern = functools.partial(_diff_attn_kernel, tq=tq, s=s, dh=dh, out_scale=1.0 - lam_init)
    grid_spec = pltpu.PrefetchScalarGridSpec(
        num_scalar_prefetch=1,
        grid=(b, h, s // tq),
        in_specs=[
            pl.BlockSpec((1, tq, 2 * dh), lambda bi, hi, qi, lam: (bi, qi, hi)),
            pl.BlockSpec((1, s, 2 * dh), lambda bi, hi, qi, lam: (bi, 0, h + hi)),
            pl.BlockSpec((1, s, 2 * dh), lambda bi, hi, qi, lam: (bi, 0, 2 * h + hi)),
            pl.BlockSpec((1, 2 * dh), lambda bi, hi, qi, lam: (0, 0)),
            pl.BlockSpec((1, 2 * dh), lambda bi, hi, qi, lam: (0, 0)),
            pl.BlockSpec((1, 2 * dh), lambda bi, hi, qi, lam: (0, 0)),
            pl.BlockSpec((1, nb, tq, LANES), lambda bi, hi, qi, lam: (hi, 0, 0, 0),
                         pipeline_mode=pl.Buffered(1)),
        ],
        out_specs=pl.BlockSpec((1, tq, 2 * dh), lambda bi, hi, qi, lam: (bi, qi, hi)),
        scratch_shapes=[pltpu.VMEM((s, 2 * dh), BF16), pltpu.VMEM((s, 4 * dh), BF16)],
    )
    return pl.pallas_call(
        kern,
        grid_spec=grid_spec,
        out_shape=jax.ShapeDtypeStruct((b, s, h * 2 * dh), BF16),
        compiler_params=_params("parallel", "parallel", "arbitrary"),
    )(lam.reshape(1), qkv, qkv, qkv, qg, kg, subln_g.reshape(1, 2 * dh), band)


def _mlstm_pre_kernel(xm_ref, cw_ref, cb_ref, wq_ref, wk_ref, wv_ref, gq_ref, gk_ref, gv_ref, gb_ref,
                      q_ref, k_ref, v_ref, xc_ref, pre_ref, pad_ref, *, s, halo):
    j = pl.program_id(1)
    xm = xm_ref[0].astype(F32)
    cbw = xm.shape[1]
    pad_ref[0:8, :] = jnp.zeros((8, cbw), F32)
    pad_ref[8 + s:16 + s, :] = jnp.zeros((8, cbw), F32)
    pad_ref[8:8 + s, :] = xm

    @pl.when(j == 0)
    def _():
        pre_ref[0] = jnp.broadcast_to(gb_ref[...], pre_ref.shape[1:])

    rc = min(MLSTM_ROW_CHUNK, s)
    for r0 in range(0, s, rc):
        rows = slice(r0, r0 + rc)
        acc = cb_ref[...] + cw_ref[0:1, :] * pad_ref[8 - halo + r0:8 - halo + r0 + rc, :]
        for t in range(1, 2 * halo + 1):
            acc = acc + cw_ref[t:t + 1, :] * pad_ref[8 - halo + t + r0:8 - halo + t + r0 + rc, :]
        xcb = _silu(acc).astype(BF16)
        q = jnp.dot(xcb, wq_ref[0], preferred_element_type=F32).astype(BF16)
        k = jnp.dot(xcb, wk_ref[0], preferred_element_type=F32).astype(BF16)
        v = jnp.dot(xm_ref[0, rows, :], wv_ref[0], preferred_element_type=F32).astype(BF16)
        q_ref[0, rows, :] = q
        k_ref[0, rows, :] = k
        v_ref[0, rows, :] = v
        xc_ref[0, rows, :] = xcb
        pre_ref[0, rows, :] += (jnp.dot(q, gq_ref[...], preferred_element_type=F32)
                                + jnp.dot(k, gk_ref[...], preferred_element_type=F32)
                                + jnp.dot(v, gv_ref[...], preferred_element_type=F32))


def _blockdiag_dense(w, cb):
    nblk, blk, _ = w.shape
    per = cb // blk
    w4 = w.reshape(nblk // per, per, blk, blk)
    eye = jnp.eye(per, dtype=w.dtype)
    return jnp.einsum('jnio,nm->jnimo', w4, eye).reshape(nblk // per, cb, cb).astype(BF16)


def _mlstm_pre(up, conv_w, conv_b, wq, wk, wv, gate_w, gate_b):
    b, s, c2 = up.shape
    c = c2 // 2
    cb = MLSTM_COL_TILE
    ncb = c // cb
    kw = conv_w.shape[0]
    ng = gate_w.shape[0] * gate_w.shape[-1]

    def gate_mat(i):
        g = gate_w[:, i].transpose(1, 0, 2).reshape(c, ng)
        return jnp.pad(g, ((0, 0), (0, LANES - ng))).astype(BF16)

    gb = jnp.pad(gate_b.reshape(1, ng), ((0, 0), (0, LANES - ng)))
    col = lambda i, j: (i, 0, j)
    out_bf = jax.ShapeDtypeStruct((b, s, c), BF16)
    return pl.pallas_call(
        functools.partial(_mlstm_pre_kernel, s=s, halo=kw // 2),
        grid=(b, ncb),
        in_specs=[
            pl.BlockSpec((1, s, cb), col),
            pl.BlockSpec((kw, cb), lambda i, j: (0, j)),
            pl.BlockSpec((1, cb), lambda i, j: (0, j)),
            pl.BlockSpec((1, cb, cb), lambda i, j: (j, 0, 0)),
            pl.BlockSpec((1, cb, cb), lambda i, j: (j, 0, 0)),
            pl.BlockSpec((1, cb, cb), lambda i, j: (j, 0, 0)),
            pl.BlockSpec((cb, LANES), lambda i, j: (j, 0)),
            pl.BlockSpec((cb, LANES), lambda i, j: (j, 0)),
            pl.BlockSpec((cb, LANES), lambda i, j: (j, 0)),
            pl.BlockSpec((1, LANES), lambda i, j: (0, 0)),
        ],
        out_specs=[
            pl.BlockSpec((1, s, cb), col),
            pl.BlockSpec((1, s, cb), col),
            pl.BlockSpec((1, s, cb), col),
            pl.BlockSpec((1, s, cb), col),
            pl.BlockSpec((1, s, LANES), lambda i, j: (i, 0, 0)),
        ],
        out_shape=[out_bf, out_bf, out_bf, out_bf, jax.ShapeDtypeStruct((b, s, LANES), F32)],
        scratch_shapes=[pltpu.VMEM((s + 16, cb), F32)],
        compiler_params=_params("parallel", "arbitrary"),
    )(up, conv_w, conv_b.reshape(1, c), _blockdiag_dense(wq, cb), _blockdiag_dense(wk, cb),
      _blockdiag_dense(wv, cb), gate_mat(0), gate_mat(1), gate_mat(2), gb)


def _mlstm_scan_kernel(q_ref, k_ref, v_ref, gcol_ref, grow_ref, og_ref, o_ref,
                       cf_ref, cbk_ref, hf_ref, hb_ref, *, L, nc, dk):
    scale = dk ** -0.5
    row = lax.broadcasted_iota(jnp.int32, (L, L), 0)
    coli = lax.broadcasted_iota(jnp.int32, (L, L), 1)
    lower = coli <= row
    upper = coli >= row
    cf_ref[...] = jnp.zeros_like(cf_ref)
    cbk_ref[...] = jnp.zeros_like(cbk_ref)

    def chunk_step(c, c_ref, h_ref, n, m, fwd):
        mask, mask_t = (lower, upper) if fwd else (upper, lower)
        gi = 0 if fwd else 2
        sl = pl.ds(pl.multiple_of(c * L, L), L)
        qb = q_ref[0, sl, :]
        kb = k_ref[0, sl, :]
        vb = v_ref[0, sl, :]
        gc = gcol_ref[0, 0, c]
        gr = grow_ref[0, 0, c]
        ii_col = gc[:, gi:gi + 1]
        lf_col = _log_sigmoid(gc[:, gi + 1:gi + 2])
        ii_row = gr[gi:gi + 1, :]
        lf_row = _log_sigmoid(gr[gi + 1:gi + 2, :])
        b_col = jnp.sum(jnp.where(mask, lf_row, 0.0), axis=1, keepdims=True)
        b_row = jnp.sum(jnp.where(mask_t, lf_col, 0.0), axis=0, keepdims=True)
        dmat = jnp.where(mask, b_col - b_row + ii_row, -jnp.inf)
        inter = b_col + m
        m_t = jnp.maximum(inter, jnp.max(dmat, axis=1, keepdims=True))
        w_intra = jnp.exp(dmat - m_t)
        w_inter = jnp.exp(inter - m_t)
        sqk = lax.dot_general(qb, kb, (((1,), (1,)), ((), ())), preferred_element_type=F32)
        sw = sqk * scale * w_intra
        cmat = c_ref[...]
        num = (w_inter * jnp.dot(qb, cmat.astype(BF16), preferred_element_type=F32)
               + jnp.dot(sw.astype(BF16), vb, preferred_element_type=F32))
        den = (w_inter * jnp.sum(qb.astype(F32) * n, axis=1, keepdims=True)
               + jnp.sum(sw, axis=1, keepdims=True))
        h_ref[sl, :] = num / jnp.maximum(jnp.abs(den), jnp.exp(-m_t))
        b_tot = jnp.sum(lf_row, axis=1, keepdims=True)
        g_col = b_tot - b_col + ii_col
        m_new = jnp.maximum(b_tot + m, jnp.max(g_col, axis=0, keepdims=True))
        decay = jnp.exp(b_tot + m - m_new)
        kw = kb.astype(F32) * scale * jnp.exp(g_col - m_new)
        c_ref[...] = decay * cmat + lax.dot_general(
            kw.astype(BF16), vb, (((0,), (0,)), ((), ())), preferred_element_type=F32)
        n_new = decay * n + jnp.sum(kw, axis=0, keepdims=True)
        return n_new, m_new

    def body(i, carry):
        nf, mf, nb_, mb = carry
        nf, mf = chunk_step(i, cf_ref, hf_ref, nf, mf, True)
        nb_, mb = chunk_step(nc - 1 - i, cbk_ref, hb_ref, nb_, mb, False)
        return nf, mf, nb_, mb

    z_n = jnp.zeros((1, dk), F32)
    z_m = jnp.zeros((1, 1), F32)
    lax.fori_loop(0, nc, body, (z_n, z_m, z_n, z_m))

    def finish(c, carry):
        sl = pl.ds(pl.multiple_of(c * L, L), L)
        hs = hf_ref[sl, :] + hb_ref[sl, :]
        ms = jnp.mean(hs * hs, axis=-1, keepdims=True)
        o_ref[0, sl, :] = (hs * lax.rsqrt(ms + EPS) * og_ref[0]).astype(o_ref.dtype)
        return carry

    lax.fori_loop(0, nc, finish, 0)


def _mlstm_scan(q, k, v, pre, outnorm_g):
    b, s, c = q.shape
    hh = ML_HEADS
    dk = c // hh
    L = min(ML_CHUNK, s)
    nc = s // L
    idx = jnp.array([[0 * 2 * hh + h, 0 * 2 * hh + hh + h, 2 * hh + h, 2 * hh + hh + h] for h in range(hh)])
    g = pre[:, :, idx]
    gcol = g.transpose(0, 2, 1, 3).reshape(b, hh, nc, L, 4)
    grow = gcol.transpose(0, 1, 2, 4, 3)
    head = lambda i, j: (i, 0, j)
    return pl.pallas_call(
        functools.partial(_mlstm_scan_kernel, L=L, nc=nc, dk=dk),
        grid=(b, hh),
        in_specs=[
            pl.BlockSpec((1, s, dk), head),
            pl.BlockSpec((1, s, dk), head),
            pl.BlockSpec((1, s, dk), head),
            pl.BlockSpec((1, 1, nc, L, 4), lambda i, j: (i, j, 0, 0, 0)),
            pl.BlockSpec((1, 1, nc, 4, L), lambda i, j: (i, j, 0, 0, 0)),
            pl.BlockSpec((1, 1, dk), lambda i, j: (j, 0, 0)),
        ],
        out_specs=pl.BlockSpec((1, s, dk), head),
        out_shape=jax.ShapeDtypeStruct((b, s, c), BF16),
        scratch_shapes=[pltpu.VMEM((dk, dk), F32), pltpu.VMEM((dk, dk), F32),
                        pltpu.VMEM((s, dk), F32), pltpu.VMEM((s, dk), F32)],
        compiler_params=_params("parallel", "parallel"),
    )(q, k, v, gcol, grow, outnorm_g.reshape(hh, 1, dk))


def _mlstm_out_kernel(hn_ref, xc_ref, z_ref, skip_ref, w_ref, x_ref, gate_ref, o_ref):
    a = (hn_ref[0].astype(F32) + skip_ref[...] * xc_ref[0].astype(F32)) * _silu(z_ref[0].astype(F32))
    y = jnp.dot(a.astype(BF16), w_ref[...], preferred_element_type=F32)
    o_ref[0] = x_ref[0] + gate_ref[0] * y


def _mlstm_out(hn, xc, up, skip, w, x, gate):
    b, s, c = hn.shape
    d = w.shape[1]
    ts = min(ROW_TILE, s)
    row = lambda i, j: (i, j, 0)
    return pl.pallas_call(
        _mlstm_out_kernel,
        grid=(b, s // ts),
        in_specs=[
            pl.BlockSpec((1, ts, c), row),
            pl.BlockSpec((1, ts, c), row),
            pl.BlockSpec((1, ts, c), lambda i, j: (i, j, 1)),
            pl.BlockSpec((1, c), lambda i, j: (0, 0)),
            pl.BlockSpec((c, d), lambda i, j: (0, 0)),
            pl.BlockSpec((1, ts, d), row),
            pl.BlockSpec((1, 1, d), lambda i, j: (i, 0, 0)),
        ],
        out_specs=pl.BlockSpec((1, ts, d), row),
        out_shape=jax.ShapeDtypeStruct((b, s, d), F32),
        compiler_params=_params("parallel", "parallel"),
    )(hn, xc, up, skip.reshape(1, c), w, x, gate)


def _router_kernel(x_ref, g_ref, sc_ref, sh_ref, w_ref, lpc_ref, lpr_ref, wgt_ref, stats_ref, cnt_ref, run_ref):
    first = (pl.program_id(0) == 0) & (pl.program_id(1) == 0)

    @pl.when(first)
    def _():
        run_ref[...] = jnp.zeros_like(run_ref)

    hf = _modulated_norm(x_ref[0], g_ref[...], sc_ref[0], sh_ref[0])
    logits = jnp.dot(hf, w_ref[...], preferred_element_type=F32, precision=lax.Precision.HIGHEST)
    tm = logits.shape[0]
    lane = lax.broadcasted_iota(jnp.int32, logits.shape, 1).astype(F32)
    neg = -jnp.inf
    is_g = lane < N_GROUPS
    gl = jnp.where(is_g, logits, neg)
    gmax = jnp.max(gl, axis=-1, keepdims=True)
    g_sel = jnp.min(jnp.where(gl == gmax, lane, float(LANES)), axis=-1, keepdims=True)
    p_g = 1.0 / jnp.sum(jnp.where(is_g, jnp.exp(gl - gmax), 0.0), axis=-1, keepdims=True)
    e_lane = lane - N_GROUPS
    in_grp = (e_lane >= g_sel * EXPERTS_PER_GROUP) & (e_lane < (g_sel + 1) * EXPERTS_PER_GROUP)
    el = jnp.where(in_grp, logits, neg)
    emax = jnp.max(el, axis=-1, keepdims=True)
    i1 = jnp.min(jnp.where(el == emax, e_lane, float(LANES)), axis=-1, keepdims=True)
    el2 = jnp.where(e_lane == i1, neg, el)
    emax2 = jnp.max(el2, axis=-1, keepdims=True)
    i2 = jnp.min(jnp.where(el2 == emax2, e_lane, float(LANES)), axis=-1, keepdims=True)
    t2 = jnp.exp(emax2 - emax)
    w1 = p_g / (1.0 + t2)
    w2 = p_g * t2 / (1.0 + t2)
    a = jnp.where(e_lane == i1, 1.0, jnp.where(e_lane == i2, 1.0, 0.0))
    r = lax.broadcasted_iota(jnp.int32, (tm, tm), 0)
    cc = lax.broadcasted_iota(jnp.int32, (tm, tm), 1)
    tri = jnp.where(cc < r, 1.0, 0.0).astype(BF16)
    rank = jnp.dot(tri, a.astype(BF16), preferred_element_type=F32)
    k8 = jnp.ceil(jnp.sum(a, axis=0, keepdims=True) * (1.0 / SUBLANES))
    ur = lax.broadcasted_iota(jnp.int32, (LANES, LANES), 0)
    uc = lax.broadcasted_iota(jnp.int32, (LANES, LANES), 1)
    upper = jnp.where(ur < uc, 1.0, 0.0).astype(BF16)
    k8_rows = jnp.broadcast_to(k8, (SUBLANES, LANES)).astype(BF16)
    off = jnp.dot(k8_rows, upper, preferred_element_type=F32)[0:1] * SUBLANES
    pos = off + rank
    lp0 = jnp.sum(jnp.where(e_lane == i1, pos, 0.0), axis=-1, keepdims=True)
    lp1 = jnp.sum(jnp.where(e_lane == i2, pos, 0.0), axis=-1, keepdims=True)
    run_old = run_ref[...]
    run_new = run_old + k8 * SUBLANES
    run_ref[...] = run_new
    cnt_ref[...] = run_new.astype(jnp.int32)
    srow = lax.broadcasted_iota(jnp.int32, (SUBLANES, LANES), 0)
    stats_ref[0] = jnp.where(srow == 0, k8, jnp.where(srow == 1, off, jnp.where(
        srow == 2, run_old, 0.0))).astype(jnp.int32)
    lp = jnp.where(lane == 0, lp0, jnp.where(lane == 1, lp1, 0.0))
    lpc_ref[0] = lp.astype(jnp.int32)
    lpr_ref[0, 0] = lp.T[:SUBLANES].astype(jnp.int32)
    wgt_ref[0] = jnp.where(lane == 0, w1, jnp.where(lane == 1, w2, 0.0))


def _router(x, g, sc, sh, w_group, w_router):
    b, s, d = x.shape
    ts = min(MOE_TOKEN_TILE, s)
    nt = s // ts
    w = jnp.concatenate([w_group, w_router], axis=1)
    w = jnp.pad(w, ((0, 0), (0, LANES - w.shape[1])))
    row = lambda i, j: (i, j, 0)
    return pl.pallas_call(
        _router_kernel,
        grid=(b, nt),
        in_specs=[
            pl.BlockSpec((1, ts, d), row),
            pl.BlockSpec((1, d), lambda i, j: (0, 0)),
            pl.BlockSpec((1, 1, d), lambda i, j: (i, 0, 0)),
            pl.BlockSpec((1, 1, d), lambda i, j: (i, 0, 0)),
            pl.BlockSpec((d, LANES), lambda i, j: (0, 0)),
        ],
        out_specs=[
            pl.BlockSpec((1, ts, LANES), row),
            pl.BlockSpec((1, 1, SUBLANES, ts), lambda i, j: (i, j, 0, 0)),
            pl.BlockSpec((1, ts, LANES), row),
            pl.BlockSpec((1, SUBLANES, LANES), lambda i, j: (i * nt + j, 0, 0)),
            pl.BlockSpec((1, LANES), lambda i, j: (0, 0)),
        ],
        out_shape=[jax.ShapeDtypeStruct((b, s, LANES), jnp.int32),
                   jax.ShapeDtypeStruct((b, nt, SUBLANES, ts), jnp.int32),
                   jax.ShapeDtypeStruct((b, s, LANES), F32),
                   jax.ShapeDtypeStruct((b * nt, SUBLANES, LANES), jnp.int32),
                   jax.ShapeDtypeStruct((1, LANES), jnp.int32)],
        scratch_shapes=[pltpu.VMEM((1, LANES), F32)],
        compiler_params=_params("arbitrary", "arbitrary"),
    )(x, g.reshape(1, d), sc, sh, w)


def _chunk_copies(meta_ref, local_ref, hbm_ref, sem, *, to_hbm, start, max_groups):
    def per_expert(e, carry):
        k = meta_ref[0, 0, e]
        lo = meta_ref[0, 0, N_EXPERTS + e]
        hi = meta_ref[0, 0, 2 * N_EXPERTS + e]
        done = jnp.int32(0)
        for bit in range(max_groups.bit_length()):
            rows = SUBLANES << bit
            take = (k >> bit) & 1

            @pl.when(take == 1)
            def _():
                loc = local_ref.at[pl.ds(pl.multiple_of(lo + done, SUBLANES), rows)]
                far = hbm_ref.at[pl.ds(pl.multiple_of(hi + done, SUBLANES), rows)]
                cp = pltpu.make_async_copy(loc, far, sem) if to_hbm else pltpu.make_async_copy(far, loc, sem)
                if start:
                    cp.start()
                else:
                    cp.wait()

            done = done + take * rows
        return carry

    lax.fori_loop(0, N_EXPERTS, per_expert, 0)


def _dispatch_kernel(last_ref, meta_ref, lpr_ref, x_ref, g_ref, sc_ref, sh_ref, xs_ref, loc_ref, zero_ref,
                     sem, zsem):
    tm = x_ref.shape[1]
    zrows = zero_ref.shape[0]

    @pl.when((pl.program_id(0) == 0) & (pl.program_id(1) == 0))
    def _():
        zero_ref[...] = jnp.zeros_like(zero_ref)

        def fill(e):
            start = pl.multiple_of(last_ref[e], zrows)
            return pltpu.make_async_copy(zero_ref, xs_ref.at[pl.ds(start, zrows)], zsem)

        for e in range(N_EXPERTS):
            @pl.when(last_ref[e] >= 0)
            def _():
                fill(e).start()
        for e in range(N_EXPERTS):
            @pl.when(last_ref[e] >= 0)
            def _():
                fill(e).wait()

        def tail(k):
            start = pl.multiple_of(k * zrows, zrows)
            return pltpu.make_async_copy(zero_ref, xs_ref.at[pl.ds(start, zrows)], zsem)

        n_used = last_ref[N_EXPERTS]
        n_tiles = xs_ref.shape[0] // zrows
        lax.fori_loop(n_used, n_tiles, lambda k, c: (tail(k).start(), c)[1], 0)
        lax.fori_loop(n_used, n_tiles, lambda k, c: (tail(k).wait(), c)[1], 0)

    hf = _modulated_norm(x_ref[0], g_ref[...], sc_ref[0], sh_ref[0]).astype(BF16)
    rows = lax.broadcasted_iota(jnp.int32, (loc_ref.shape[0], tm), 0)
    sel = jnp.where(rows == lpr_ref[0, 0, 0:1, :], 1.0,
                    jnp.where(rows == lpr_ref[0, 0, 1:2, :], 1.0, 0.0)).astype(BF16)
    loc_ref[...] = jnp.dot(sel, hf, preferred_element_type=F32)
    copies = functools.partial(_chunk_copies, meta_ref, loc_ref, xs_ref, sem, to_hbm=True,
                               max_groups=tm // SUBLANES)
    copies(start=True)
    copies(start=False)


def _local_rows(tm):
    return 2 * tm + N_EXPERTS * SUBLANES


def _dispatch(x, g, sc, sh, meta, lpr, last_tile_row, n_rows):
    b, s, d = x.shape
    tm = min(MOE_TOKEN_TILE, s)
    nt = s // tm
    row = lambda i, j, last: (i, j, 0)
    grid_spec = pltpu.PrefetchScalarGridSpec(
        num_scalar_prefetch=1,
        grid=(b, nt),
        in_specs=[
            pl.BlockSpec((1, 1, LANES), lambda i, j, last: (i * nt + j, 0, 0), memory_space=pltpu.SMEM),
            pl.BlockSpec((1, 1, SUBLANES, tm), lambda i, j, last: (i, j, 0, 0)),
            pl.BlockSpec((1, tm, d), row),
            pl.BlockSpec((1, d), lambda i, j, last: (0, 0)),
            pl.BlockSpec((1, 1, d), lambda i, j, last: (i, 0, 0)),
            pl.BlockSpec((1, 1, d), lambda i, j, last: (i, 0, 0)),
        ],
        out_specs=pl.BlockSpec(memory_space=pl.ANY),
        scratch_shapes=[pltpu.VMEM((_local_rows(tm), d), F32), pltpu.VMEM((MOE_ROW_TILE, d), F32),
                        pltpu.SemaphoreType.DMA(()), pltpu.SemaphoreType.DMA(())],
    )
    return pl.pallas_call(
        _dispatch_kernel,
        grid_spec=grid_spec,
        out_shape=jax.ShapeDtypeStruct((n_rows, d), F32),
        compiler_params=_params("arbitrary", "arbitrary"),
    )(last_tile_row, meta, lpr, x, g.reshape(1, d), sc, sh)


def _expert_kernel(te_ref, nu_ref, xs_ref, w1_ref, w3_ref, w2_ref, ys_ref):
    i = pl.program_id(0)

    @pl.when(i < nu_ref[0])
    def _():
        xb = xs_ref[...].astype(BF16)
        a = jnp.dot(xb, w1_ref[0, 0].astype(BF16), preferred_element_type=F32)
        bb = jnp.dot(xb, w3_ref[0, 0].astype(BF16), preferred_element_type=F32)
        hmid = (_silu(a) * bb).astype(BF16)
        ys_ref[...] = jnp.dot(hmid, w2_ref[0, 0].astype(BF16), preferred_element_type=F32)

    @pl.when(i >= nu_ref[0])
    def _():
        ys_ref[...] = jnp.zeros_like(ys_ref)


def _expert_ffn(xs, tile_expert, n_used, w1, w3, w2, layer):
    p, d = xs.shape
    de = w1.shape[3]
    tm = MOE_ROW_TILE
    grid_spec = pltpu.PrefetchScalarGridSpec(
        num_scalar_prefetch=2,
        grid=(p // tm,),
        in_specs=[
            pl.BlockSpec((tm, d), lambda i, te, nu: (jnp.maximum(jnp.minimum(i, nu[0] - 1), 0), 0)),
            pl.BlockSpec((1, 1, d, de), lambda i, te, nu: (layer, te[i], 0, 0)),
            pl.BlockSpec((1, 1, d, de), lambda i, te, nu: (layer, te[i], 0, 0)),
            pl.BlockSpec((1, 1, de, d), lambda i, te, nu: (layer, te[i], 0, 0)),
        ],
        out_specs=pl.BlockSpec((tm, d), lambda i, te, nu: (i, 0)),
    )
    return pl.pallas_call(
        _expert_kernel,
        grid_spec=grid_spec,
        out_shape=jax.ShapeDtypeStruct((p, d), F32),
        compiler_params=_params("arbitrary"),
    )(tile_expert, n_used, xs, w1, w3, w2)


def _combine_kernel(meta_ref, ys_ref, lpc_ref, wgt_ref, x_ref, gate_ref, o_ref, loc_ref, sem):
    tm = x_ref.shape[1]
    loc_ref[2 * tm:, :] = jnp.zeros((loc_ref.shape[0] - 2 * tm, loc_ref.shape[1]), F32)
    copies = functools.partial(_chunk_copies, meta_ref, loc_ref, ys_ref, sem, to_hbm=False,
                               max_groups=tm // SUBLANES)
    copies(start=True)
    copies(start=False)
    cols = lax.broadcasted_iota(jnp.int32, (tm, loc_ref.shape[0]), 1)
    lp = lpc_ref[0]
    wgt = wgt_ref[0]
    wm = jnp.where(cols == lp[:, 0:1], wgt[:, 0:1], jnp.where(cols == lp[:, 1:2], wgt[:, 1:2], 0.0))
    w_hi = wm.astype(BF16)
    w_lo = (wm - w_hi.astype(F32)).astype(BF16)
    yb = loc_ref[...].astype(BF16)
    y = jnp.dot(w_hi, yb, preferred_element_type=F32) + jnp.dot(w_lo, yb, preferred_element_type=F32)
    o_ref[0] = x_ref[0] + gate_ref[0] * y


def _combine(ys, meta, lpc, wgt, x, gate):
    b, s, d = x.shape
    tm = min(MOE_TOKEN_TILE, s)
    nt = s // tm
    row = lambda i, j: (i, j, 0)
    return pl.pallas_call(
        _combine_kernel,
        grid=(b, nt),
        in_specs=[
            pl.BlockSpec((1, 1, LANES), lambda i, j: (i * nt + j, 0, 0), memory_space=pltpu.SMEM),
            pl.BlockSpec(memory_space=pl.ANY),
            pl.BlockSpec((1, tm, LANES), row),
            pl.BlockSpec((1, tm, LANES), row),
            pl.BlockSpec((1, tm, d), row),
            pl.BlockSpec((1, 1, d), lambda i, j: (i, 0, 0)),
        ],
        out_specs=pl.BlockSpec((1, tm, d), row),
        out_shape=jax.ShapeDtypeStruct((b, s, d), F32),
        scratch_shapes=[pltpu.VMEM((_local_rows(tm), d), F32), pltpu.SemaphoreType.DMA(())],
        compiler_params=_params("arbitrary", "arbitrary"),
    )(meta, ys, lpc, wgt, x, gate)


def _hier_moe_residual(x, g, sc, sh, gate, w_group, w_router, w1, w3, w2, layer):
    b, s, d = x.shape
    t = b * s
    lpc, lpr, wgt, stats, cnt = _router(x, g, sc, sh, w_group, w_router)
    tm = MOE_ROW_TILE
    ex = slice(N_GROUPS, N_GROUPS + N_EXPERTS)
    counts = cnt[0, ex]
    tiles = (counts + tm - 1) // tm
    tile_end = jnp.cumsum(tiles)
    base = (tile_end - tiles) * tm
    n_blocks = stats.shape[0]
    n_tiles = (2 * t + n_blocks * N_EXPERTS * SUBLANES) // tm + N_EXPERTS
    tile_expert = jnp.minimum(
        jnp.sum(tile_end[None, :] <= jnp.arange(n_tiles)[:, None], axis=1), N_EXPERTS - 1).astype(jnp.int32)
    n_used = tile_end[-1:].astype(jnp.int32)
    meta = jnp.concatenate([stats[:, 0, ex], stats[:, 1, ex], base[None, :] + stats[:, 2, ex],
                            jnp.zeros((n_blocks, LANES - 3 * N_EXPERTS), jnp.int32)], axis=1)
    meta = meta.astype(jnp.int32).reshape(n_blocks, 1, LANES)
    last_tile_row = jnp.where(tiles > 0, (tile_end - 1) * tm, -1).astype(jnp.int32)
    last_tile_row = jnp.concatenate([last_tile_row, n_used])
    xs = _dispatch(x, g, sc, sh, meta, lpr, last_tile_row, n_tiles * tm)
    ys = _expert_ffn(xs, tile_expert, n_used, w1, w3, w2, layer)
    return _combine(ys, meta, lpc, wgt, x, gate)


def kernel(x, c, rel_table, ada_w, ada_b, norm_mix_g, norm_ffn_g, da_w_in, da_w_out, da_q_gain, da_k_gain, da_lam_q1, da_lam_k1, da_lam_q2, da_lam_k2, da_subln_g, ml_w_in, ml_conv_w, ml_conv_b, ml_wq, ml_wk, ml_wv, ml_gate_w, ml_gate_b, ml_outnorm_g, ml_skip, ml_w_out, moe_w_group, moe_w_router, moe_w1, moe_w3, moe_w2):
    depth = ada_w.shape[0]
    d = x.shape[-1]
    mod = _ada_mod(c, ada_w, ada_b)
    for i in range(depth):
        sh1, sc1, g1, sh2, sc2, g2 = [mod[i, :, None, k * d:(k + 1) * d] for k in range(6)]
        j = i // N_MIXERS
        if i % N_MIXERS == 0:
            qkv = _prenorm_matmul(x, norm_mix_g[i], sc1, sh1, da_w_in[j].astype(BF16))
            lam_init = 0.8 - 0.6 * math.exp(-0.3 * i)
            lam = (jnp.exp(jnp.sum(da_lam_q1[j] * da_lam_k1[j])) - jnp.exp(jnp.sum(da_lam_q2[j] * da_lam_k2[j]))
                   + lam_init).astype(F32)
            o = _diff_attention(qkv, da_q_gain[j], da_k_gain[j], da_subln_g[j], lam, lam_init, rel_table)
            x = _proj_residual(o, da_w_out[j].astype(BF16), x, g1)
        else:
            up = _prenorm_matmul(x, norm_mix_g[i], sc1, sh1, ml_w_in[j].astype(BF16))
            q, k, v, xc, pre = _mlstm_pre(up, ml_conv_w[j], ml_conv_b[j], ml_wq[j], ml_wk[j], ml_wv[j],
                                          ml_gate_w[j], ml_gate_b[j])
            hn = _mlstm_scan(q, k, v, pre, ml_outnorm_g[j])
            x = _mlstm_out(hn, xc, up, ml_skip[j], ml_w_out[j].astype(BF16), x, g1)
        x = _hier_moe_residual(x, norm_ffn_g[i], sc2, sh2, g2, moe_w_group[i], moe_w_router[i],
                               moe_w1, moe_w3, moe_w2, i)
    return x
```

```python
import functools
import math

import jax
import jax.numpy as jnp
from jax import lax
from jax.experimental import pallas as pl
from jax.experimental.pallas import tpu as pltpu

EPS = 1e-6
N_MIXERS = 2
DA_HEADS = 8
DA_HEAD_DIM = 64
N_BUCKETS = 32
MAX_DISTANCE = 128
ML_HEADS = 4
ML_CHUNK = 256
ML_QKV_BLOCK = 4
N_GROUPS = 4
EXPERTS_PER_GROUP = 8
N_EXPERTS = N_GROUPS * EXPERTS_PER_GROUP

LANES = 128
SUBLANES = 8
VMEM_LIMIT = 48 * 1024 * 1024
ATTN_Q_TILE = 1024
ATTN_ROW_BLOCK = 128
ROW_TILE = 512
MOE_ROW_TILE = 512
MOE_TOKEN_TILE = 512
MLSTM_COL_TILE = 256
MLSTM_ROW_CHUNK = 512

F32 = jnp.float32
BF16 = jnp.bfloat16
LOG2E = math.log2(math.e)


def _params(*sem):
    return pltpu.CompilerParams(dimension_semantics=sem, vmem_limit_bytes=VMEM_LIMIT)


def _silu(x):
    return x / (1.0 + jnp.exp(-x))


def _log_sigmoid(x):
    return jnp.minimum(x, 0.0) - jnp.log(1.0 + jnp.exp(-jnp.abs(x)))


def _modulated_norm(x, g, sc, sh):
    ms = jnp.mean(x * x, axis=-1, keepdims=True)
    return x * lax.rsqrt(ms + EPS) * g * (1.0 + sc) + sh


def _ada_kernel(c_ref, w_ref, b_ref, o_ref):
    ca = _silu(c_ref[...]).astype(BF16)
    o_ref[0] = jnp.dot(ca, w_ref[0].astype(BF16), preferred_element_type=F32) + b_ref[0]


def _ada_mod(c, ada_w, ada_b):
    depth, d, n = ada_w.shape
    b = c.shape[0]
    tn = 1536
    return pl.pallas_call(
        _ada_kernel,
        grid=(depth, n // tn),
        in_specs=[
            pl.BlockSpec((b, d), lambda i, j: (0, 0)),
            pl.BlockSpec((1, d, tn), lambda i, j: (i, 0, j)),
            pl.BlockSpec((1, 1, tn), lambda i, j: (i, 0, j)),
        ],
        out_specs=pl.BlockSpec((1, b, tn), lambda i, j: (i, 0, j)),
        out_shape=jax.ShapeDtypeStruct((depth, b, n), F32),
        compiler_params=_params("parallel", "parallel"),
    )(c, ada_w, ada_b.reshape(depth, 1, n))


def _prenorm_matmul_kernel(x_ref, g_ref, sc_ref, sh_ref, w_ref, o_ref, *, n_chunk):
    y = _modulated_norm(x_ref[0], g_ref[...], sc_ref[0], sh_ref[0]).astype(BF16)
    n = w_ref.shape[1]
    for n0 in range(0, n, n_chunk):
        o_ref[0, :, n0:n0 + n_chunk] = jnp.dot(
            y, w_ref[:, n0:n0 + n_chunk], preferred_element_type=F32).astype(o_ref.dtype)


def _prenorm_matmul(x, g, sc, sh, w):
    b, s, d = x.shape
    n = w.shape[1]
    ts = min(ROW_TILE, s)
    return pl.pallas_call(
        functools.partial(_prenorm_matmul_kernel, n_chunk=1024),
        grid=(b, s // ts),
        in_specs=[
            pl.BlockSpec((1, ts, d), lambda i, j: (i, j, 0)),
            pl.BlockSpec((1, d), lambda i, j: (0, 0)),
            pl.BlockSpec((1, 1, d), lambda i, j: (i, 0, 0)),
            pl.BlockSpec((1, 1, d), lambda i, j: (i, 0, 0)),
            pl.BlockSpec((d, n), lambda i, j: (0, 0)),
        ],
        out_specs=pl.BlockSpec((1, ts, n), lambda i, j: (i, j, 0)),
        out_shape=jax.ShapeDtypeStruct((b, s, n), BF16),
        compiler_params=_params("parallel", "parallel"),
    )(x, g.reshape(1, d), sc, sh, w)


def _proj_residual_kernel(a_ref, w_ref, x_ref, gate_ref, o_ref):
    y = jnp.dot(a_ref[0], w_ref[...], preferred_element_type=F32)
    o_ref[0] = x_ref[0] + gate_ref[0] * y


def _proj_residual(a, w, x, gate):
    b, s, k = a.shape
    d = w.shape[1]
    ts = min(ROW_TILE, s)
    return pl.pallas_call(
        _proj_residual_kernel,
        grid=(b, s // ts),
        in_specs=[
            pl.BlockSpec((1, ts, k), lambda i, j: (i, j, 0)),
            pl.BlockSpec((k, d), lambda i, j: (0, 0)),
            pl.BlockSpec((1, ts, d), lambda i, j: (i, j, 0)),
            pl.BlockSpec((1, 1, d), lambda i, j: (i, 0, 0)),
        ],
        out_specs=pl.BlockSpec((1, ts, d), lambda i, j: (i, j, 0)),
        out_shape=jax.ShapeDtypeStruct((b, s, d), F32),
        compiler_params=_params("parallel", "parallel"),
    )(a, w, x, gate)


def _t5_bucket(rel):
    nb = N_BUCKETS // 2
    max_exact = nb // 2
    ret = jnp.where(rel > 0, nb, 0)
    n = jnp.abs(rel)
    nf = jnp.maximum(n, 1).astype(F32)
    large = max_exact + (jnp.log(nf / max_exact) / math.log(MAX_DISTANCE / max_exact)
                         * (nb - max_exact)).astype(jnp.int32)
    large = jnp.minimum(large, nb - 1)
    return ret + jnp.where(n < max_exact, n, large)


def _bias_band_kernel(t_ref, o_ref, *, tq, nb):
    u = jnp.broadcast_to(t_ref[0], (tq, t_ref.shape[2]))
    y = pltpu.roll(u, 1, 1, stride=1, stride_axis=0)
    for jb in range(nb):
        o_ref[0, jb] = y[:, tq + jb * LANES:tq + (jb + 1) * LANES]


def _bias_band(rel_table, s, tq):
    h = rel_table.shape[1]
    delta = jnp.arange(-(s - 1), s)
    t = rel_table[_t5_bucket(delta)].T.astype(F32) * LOG2E
    u = jnp.concatenate([t, jnp.zeros((h, 1), F32)], axis=1).reshape(h, 1, 2 * s)
    nb = (2 * s - tq) // LANES
    return pl.pallas_call(
        functools.partial(_bias_band_kernel, tq=tq, nb=nb),
        grid=(h,),
        in_specs=[pl.BlockSpec((1, 1, 2 * s), lambda i: (i, 0, 0))],
        out_specs=pl.BlockSpec((1, nb, tq, LANES), lambda i: (i, 0, 0, 0)),
        out_shape=jax.ShapeDtypeStruct((h, nb, tq, LANES), F32),
        compiler_params=_params("parallel"),
    )(u)


def _group_rms(x, gain, half):
    sq = x * x
    lane = lax.broadcasted_iota(jnp.int32, x.shape, 1)
    lo = lane < half
    s_lo = jnp.sum(jnp.where(lo, sq, 0.0), axis=-1, keepdims=True)
    s_all = jnp.sum(sq, axis=-1, keepdims=True)
    ms = jnp.where(lo, s_lo, s_all - s_lo) * (1.0 / half)
    return x * lax.rsqrt(ms + EPS) * gain


def _diff_attn_kernel(lam_ref, q_ref, k_ref, v_ref, qg_ref, kg_ref, sg_ref, band_ref, o_ref, kn_ref, ve_ref,
                      *, tq, s, dh, out_scale):
    qi = pl.program_id(2)
    nq = pl.num_programs(2)
    dv = 2 * dh

    @pl.when(qi == 0)
    def _():
        kn_ref[...] = _group_rms(k_ref[0].astype(F32), kg_ref[...], dh).astype(BF16)
        ve_ref[:, :dv] = v_ref[0]
        ve_ref[:, dv:] = jnp.ones((s, dv), BF16)

    lam = lam_ref[0]
    q = _group_rms(q_ref[0].astype(F32), qg_ref[...], dh) * (dh ** -0.5 * LOG2E)
    lane = lax.broadcasted_iota(jnp.int32, q.shape, 1)
    q0 = jnp.where(lane < dh, q, 0.0).astype(BF16)
    q1 = jnp.where(lane < dh, 0.0, q).astype(BF16)
    jb0 = (nq - 1 - qi) * (tq // LANES)
    nkb = s // LANES

    def softmax_av(qm, r0, r1):
        sc = lax.dot_general(qm[r0:r1], kn_ref[...], (((1,), (1,)), ((), ())), preferred_element_type=F32)
        sc = sc + jnp.concatenate([band_ref[0, jb0 + kb, r0:r1, :] for kb in range(nkb)], axis=1)
        p = jnp.exp2(sc - jnp.max(sc, axis=-1, keepdims=True)).astype(BF16)
        oe = jnp.dot(p, ve_ref[...], preferred_element_type=F32)
        return oe[:, :dv], oe[:, dv:dv + 1]

    rb = ATTN_ROW_BLOCK
    for r0 in range(0, tq, rb):
        n0, l0 = softmax_av(q0, r0, r0 + rb)
        n1, l1 = softmax_av(q1, r0, r0 + rb)
        o = n0 * (1.0 / l0) - n1 * (lam / l1)
        ms = jnp.mean(o * o, axis=-1, keepdims=True)
        o_ref[0, r0:r0 + rb, :] = (o * lax.rsqrt(ms + EPS) * sg_ref[...] * out_scale).astype(o_ref.dtype)


def _diff_attention(qkv, q_gain, k_gain, subln_g, lam, lam_init, rel_table):
    b, s, _ = qkv.shape
    h, dh = DA_HEADS, DA_HEAD_DIM
    tq = min(ATTN_Q_TILE, s)
    band = _bias_band(rel_table, s, tq)
    nb = band.shape[1]
    qg = jnp.tile(q_gain, 2).reshape(1, 2 * dh)
    kg = jnp.tile(k_gain, 2).reshape(1, 2 * dh)
    kern = functools.partial(_diff_attn_kernel, tq=tq, s=s, dh=dh, out_scale=1.0 - lam_init)
    grid_spec = pltpu.PrefetchScalarGridSpec(
        num_scalar_prefetch=1,
        grid=(b, h, s // tq),
        in_specs=[
            pl.BlockSpec((1, tq, 2 * dh), lambda bi, hi, qi, lam: (bi, qi, hi)),
            pl.BlockSpec((1, s, 2 * dh), lambda bi, hi, qi, lam: (bi, 0, h + hi)),
            pl.BlockSpec((1, s, 2 * dh), lambda bi, hi, qi, lam: (bi, 0, 2 * h + hi)),
            pl.BlockSpec((1, 2 * dh), lambda bi, hi, qi, lam: (0, 0)),
            pl.BlockSpec((1, 2 * dh), lambda bi, hi, qi, lam: (0, 0)),
            pl.BlockSpec((1, 2 * dh), lambda bi, hi, qi, lam: (0, 0)),
            pl.BlockSpec((1, nb, tq, LANES), lambda bi, hi, qi, lam: (hi, 0, 0, 0),
                         pipeline_mode=pl.Buffered(1)),
        ],
        out_specs=pl.BlockSpec((1, tq, 2 * dh), lambda bi, hi, qi, lam: (bi, qi, hi)),
        scratch_shapes=[pltpu.VMEM((s, 2 * dh), BF16), pltpu.VMEM((s, 4 * dh), BF16)],
    )
    return pl.pallas_call(
        kern,
        grid_spec=grid_spec,
        out_shape=jax.ShapeDtypeStruct((b, s, h * 2 * dh), BF16),
        compiler_params=_params("parallel", "parallel", "arbitrary"),
    )(lam.reshape(1), qkv, qkv, qkv, qg, kg, subln_g.reshape(1, 2 * dh), band)


def _mlstm_pre_kernel(xm_ref, cw_ref, cb_ref, wq_ref, wk_ref, wv_ref, gq_ref, gk_ref, gv_ref, gb_ref,
                      q_ref, k_ref, v_ref, xc_ref, pre_ref, pad_ref, *, s, halo):
    j = pl.program_id(1)
    xm = xm_ref[0].astype(F32)
    cbw = xm.shape[1]
    pad_ref[0:8, :] = jnp.zeros((8, cbw), F32)
    pad_ref[8 + s:16 + s, :] = jnp.zeros((8, cbw), F32)
    pad_ref[8:8 + s, :] = xm

    @pl.when(j == 0)
    def _():
        pre_ref[0] = jnp.broadcast_to(gb_ref[...], pre_ref.shape[1:])

    rc = min(MLSTM_ROW_CHUNK, s)
    for r0 in range(0, s, rc):
        rows = slice(r0, r0 + rc)
        acc = cb_ref[...] + cw_ref[0:1, :] * pad_ref[8 - halo + r0:8 - halo + r0 + rc, :]
        for t in range(1, 2 * halo + 1):
            acc = acc + cw_ref[t:t + 1, :] * pad_ref[8 - halo + t + r0:8 - halo + t + r0 + rc, :]
        xcb = _silu(acc).astype(BF16)
        q = jnp.dot(xcb, wq_ref[0], preferred_element_type=F32).astype(BF16)
        k = jnp.dot(xcb, wk_ref[0], preferred_element_type=F32).astype(BF16)
        v = jnp.dot(xm_ref[0, rows, :], wv_ref[0], preferred_element_type=F32).astype(BF16)
        q_ref[0, rows, :] = q
        k_ref[0, rows, :] = k
        v_ref[0, rows, :] = v
        xc_ref[0, rows, :] = xcb
        pre_ref[0, rows, :] += (jnp.dot(q, gq_ref[...], preferred_element_type=F32)
                                + jnp.dot(k, gk_ref[...], preferred_element_type=F32)
                                + jnp.dot(v, gv_ref[...], preferred_element_type=F32))


def _blockdiag_dense(w, cb):
    nblk, blk, _ = w.shape
    per = cb // blk
    w4 = w.reshape(nblk // per, per, blk, blk)
    eye = jnp.eye(per, dtype=w.dtype)
    return jnp.einsum('jnio,nm->jnimo', w4, eye).reshape(nblk // per, cb, cb).astype(BF16)


def _mlstm_pre(up, conv_w, conv_b, wq, wk, wv, gate_w, gate_b):
    b, s, c2 = up.shape
    c = c2 // 2
    cb = MLSTM_COL_TILE
    ncb = c // cb
    kw = conv_w.shape[0]
    ng = gate_w.shape[0] * gate_w.shape[-1]

    def gate_mat(i):
        g = gate_w[:, i].transpose(1, 0, 2).reshape(c, ng)
        return jnp.pad(g, ((0, 0), (0, LANES - ng))).astype(BF16)

    gb = jnp.pad(gate_b.reshape(1, ng), ((0, 0), (0, LANES - ng)))
    col = lambda i, j: (i, 0, j)
    out_bf = jax.ShapeDtypeStruct((b, s, c), BF16)
    return pl.pallas_call(
        functools.partial(_mlstm_pre_kernel, s=s, halo=kw // 2),
        grid=(b, ncb),
        in_specs=[
            pl.BlockSpec((1, s, cb), col),
            pl.BlockSpec((kw, cb), lambda i, j: (0, j)),
            pl.BlockSpec((1, cb), lambda i, j: (0, j)),
            pl.BlockSpec((1, cb, cb), lambda i, j: (j, 0, 0)),
            pl.BlockSpec((1, cb, cb), lambda i, j: (j, 0, 0)),
            pl.BlockSpec((1, cb, cb), lambda i, j: (j, 0, 0)),
            pl.BlockSpec((cb, LANES), lambda i, j: (j, 0)),
            pl.BlockSpec((cb, LANES), lambda i, j: (j, 0)),
            pl.BlockSpec((cb, LANES), lambda i, j: (j, 0)),
            pl.BlockSpec((1, LANES), lambda i, j: (0, 0)),
        ],
        out_specs=[
            pl.BlockSpec((1, s, cb), col),
            pl.BlockSpec((1, s, cb), col),
            pl.BlockSpec((1, s, cb), col),
            pl.BlockSpec((1, s, cb), col),
            pl.BlockSpec((1, s, LANES), lambda i, j: (i, 0, 0)),
        ],
        out_shape=[out_bf, out_bf, out_bf, out_bf, jax.ShapeDtypeStruct((b, s, LANES), F32)],
        scratch_shapes=[pltpu.VMEM((s + 16, cb), F32)],
        compiler_params=_params("parallel", "arbitrary"),
    )(up, conv_w, conv_b.reshape(1, c), _blockdiag_dense(wq, cb), _blockdiag_dense(wk, cb),
      _blockdiag_dense(wv, cb), gate_mat(0), gate_mat(1), gate_mat(2), gb)


def _mlstm_scan_kernel(q_ref, k_ref, v_ref, gcol_ref, grow_ref, og_ref, o_ref,
                       cf_ref, cbk_ref, hf_ref, hb_ref, *, L, nc, dk):
    scale = dk ** -0.5
    row = lax.broadcasted_iota(jnp.int32, (L, L), 0)
    coli = lax.broadcasted_iota(jnp.int32, (L, L), 1)
    lower = coli <= row
    upper = coli >= row
    cf_ref[...] = jnp.zeros_like(cf_ref)
    cbk_ref[...] = jnp.zeros_like(cbk_ref)

    def chunk_step(c, c_ref, h_ref, n, m, fwd):
        mask, mask_t = (lower, upper) if fwd else (upper, lower)
        gi = 0 if fwd else 2
        sl = pl.ds(pl.multiple_of(c * L, L), L)
        qb = q_ref[0, sl, :]
        kb = k_ref[0, sl, :]
        vb = v_ref[0, sl, :]
        gc = gcol_ref[0, 0, c]
        gr = grow_ref[0, 0, c]
        ii_col = gc[:, gi:gi + 1]
        lf_col = _log_sigmoid(gc[:, gi + 1:gi + 2])
        ii_row = gr[gi:gi + 1, :]
        lf_row = _log_sigmoid(gr[gi + 1:gi + 2, :])
        b_col = jnp.sum(jnp.where(mask, lf_row, 0.0), axis=1, keepdims=True)
        b_row = jnp.sum(jnp.where(mask_t, lf_col, 0.0), axis=0, keepdims=True)
        dmat = jnp.where(mask, b_col - b_row + ii_row, -jnp.inf)
        inter = b_col + m
        m_t = jnp.maximum(inter, jnp.max(dmat, axis=1, keepdims=True))
        w_intra = jnp.exp(dmat - m_t)
        w_inter = jnp.exp(inter - m_t)
        sqk = lax.dot_general(qb, kb, (((1,), (1,)), ((), ())), preferred_element_type=F32)
        sw = sqk * scale * w_intra
        cmat = c_ref[...]
        num = (w_inter * jnp.dot(qb, cmat.astype(BF16), preferred_element_type=F32)
               + jnp.dot(sw.astype(BF16), vb, preferred_element_type=F32))
        den = (w_inter * jnp.sum(qb.astype(F32) * n, axis=1, keepdims=True)
               + jnp.sum(sw, axis=1, keepdims=True))
        h_ref[sl, :] = num / jnp.maximum(jnp.abs(den), jnp.exp(-m_t))
        b_tot = jnp.sum(lf_row, axis=1, keepdims=True)
        g_col = b_tot - b_col + ii_col
        m_new = jnp.maximum(b_tot + m, jnp.max(g_col, axis=0, keepdims=True))
        decay = jnp.exp(b_tot + m - m_new)
        kw = kb.astype(F32) * scale * jnp.exp(g_col - m_new)
        c_ref[...] = decay * cmat + lax.dot_general(
            kw.astype(BF16), vb, (((0,), (0,)), ((), ())), preferred_element_type=F32)
        n_new = decay * n + jnp.sum(kw, axis=0, keepdims=True)
        return n_new, m_new

    def body(i, carry):
        nf, mf, nb_, mb = carry
        nf, mf = chunk_step(i, cf_ref, hf_ref, nf, mf, True)
        nb_, mb = chunk_step(nc - 1 - i, cbk_ref, hb_ref, nb_, mb, False)
        return nf, mf, nb_, mb

    z_n = jnp.zeros((1, dk), F32)
    z_m = jnp.zeros((1, 1), F32)
    lax.fori_loop(0, nc, body, (z_n, z_m, z_n, z_m))

    def finish(c, carry):
        sl = pl.ds(pl.multiple_of(c * L, L), L)
        hs = hf_ref[sl, :] + hb_ref[sl, :]
        ms = jnp.mean(hs * hs, axis=-1, keepdims=True)
        o_ref[0, sl, :] = (hs * lax.rsqrt(ms + EPS) * og_ref[0]).astype(o_ref.dtype)
        return carry

    lax.fori_loop(0, nc, finish, 0)


def _mlstm_scan(q, k, v, pre, outnorm_g):
    b, s, c = q.shape
    hh = ML_HEADS
    dk = c // hh
    L = min(ML_CHUNK, s)
    nc = s // L
    idx = jnp.array([[0 * 2 * hh + h, 0 * 2 * hh + hh + h, 2 * hh + h, 2 * hh + hh + h] for h in range(hh)])
    g = pre[:, :, idx]
    gcol = g.transpose(0, 2, 1, 3).reshape(b, hh, nc, L, 4)
    grow = gcol.transpose(0, 1, 2, 4, 3)
    head = lambda i, j: (i, 0, j)
    return pl.pallas_call(
        functools.partial(_mlstm_scan_kernel, L=L, nc=nc, dk=dk),
        grid=(b, hh),
        in_specs=[
            pl.BlockSpec((1, s, dk), head),
            pl.BlockSpec((1, s, dk), head),
            pl.BlockSpec((1, s, dk), head),
            pl.BlockSpec((1, 1, nc, L, 4), lambda i, j: (i, j, 0, 0, 0)),
            pl.BlockSpec((1, 1, nc, 4, L), lambda i, j: (i, j, 0, 0, 0)),
            pl.BlockSpec((1, 1, dk), lambda i, j: (j, 0, 0)),
        ],
        out_specs=pl.BlockSpec((1, s, dk), head),
        out_shape=jax.ShapeDtypeStruct((b, s, c), BF16),
        scratch_shapes=[pltpu.VMEM((dk, dk), F32), pltpu.VMEM((dk, dk), F32),
                        pltpu.VMEM((s, dk), F32), pltpu.VMEM((s, dk), F32)],
        compiler_params=_params("parallel", "parallel"),
    )(q, k, v, gcol, grow, outnorm_g.reshape(hh, 1, dk))


def _mlstm_out_kernel(hn_ref, xc_ref, z_ref, skip_ref, w_ref, x_ref, gate_ref, o_ref):
    a = (hn_ref[0].astype(F32) + skip_ref[...] * xc_ref[0].astype(F32)) * _silu(z_ref[0].astype(F32))
    y = jnp.dot(a.astype(BF16), w_ref[...], preferred_element_type=F32)
    o_ref[0] = x_ref[0] + gate_ref[0] * y


def _mlstm_out(hn, xc, up, skip, w, x, gate):
    b, s, c = hn.shape
    d = w.shape[1]
    ts = min(ROW_TILE, s)
    row = lambda i, j: (i, j, 0)
    return pl.pallas_call(
        _mlstm_out_kernel,
        grid=(b, s // ts),
        in_specs=[
            pl.BlockSpec((1, ts, c), row),
            pl.BlockSpec((1, ts, c), row),
            pl.BlockSpec((1, ts, c), lambda i, j: (i, j, 1)),
            pl.BlockSpec((1, c), lambda i, j: (0, 0)),
            pl.BlockSpec((c, d), lambda i, j: (0, 0)),
            pl.BlockSpec((1, ts, d), row),
            pl.BlockSpec((1, 1, d), lambda i, j: (i, 0, 0)),
        ],
        out_specs=pl.BlockSpec((1, ts, d), row),
        out_shape=jax.ShapeDtypeStruct((b, s, d), F32),
        compiler_params=_params("parallel", "parallel"),
    )(hn, xc, up, skip.reshape(1, c), w, x, gate)


def _router_kernel(x_ref, g_ref, sc_ref, sh_ref, w_ref, lpc_ref, lpr_ref, wgt_ref, stats_ref, cnt_ref, run_ref):
    first = (pl.program_id(0) == 0) & (pl.program_id(1) == 0)

    @pl.when(first)
    def _():
        run_ref[...] = jnp.zeros_like(run_ref)

    hf = _modulated_norm(x_ref[0], g_ref[...], sc_ref[0], sh_ref[0])
    logits = jnp.dot(hf, w_ref[...], preferred_element_type=F32, precision=lax.Precision.HIGHEST)
    tm = logits.shape[0]
    lane = lax.broadcasted_iota(jnp.int32, logits.shape, 1).astype(F32)
    neg = -jnp.inf
    is_g = lane < N_GROUPS
    gl = jnp.where(is_g, logits, neg)
    gmax = jnp.max(gl, axis=-1, keepdims=True)
    g_sel = jnp.min(jnp.where(gl == gmax, lane, float(LANES)), axis=-1, keepdims=True)
    p_g = 1.0 / jnp.sum(jnp.where(is_g, jnp.exp(gl - gmax), 0.0), axis=-1, keepdims=True)
    e_lane = lane - N_GROUPS
    in_grp = (e_lane >= g_sel * EXPERTS_PER_GROUP) & (e_lane < (g_sel + 1) * EXPERTS_PER_GROUP)
    el = jnp.where(in_grp, logits, neg)
    emax = jnp.max(el, axis=-1, keepdims=True)
    i1 = jnp.min(jnp.where(el == emax, e_lane, float(LANES)), axis=-1, keepdims=True)
    el2 = jnp.where(e_lane == i1, neg, el)
    emax2 = jnp.max(el2, axis=-1, keepdims=True)
    i2 = jnp.min(jnp.where(el2 == emax2, e_lane, float(LANES)), axis=-1, keepdims=True)
    t2 = jnp.exp(emax2 - emax)
    w1 = p_g / (1.0 + t2)
    w2 = p_g * t2 / (1.0 + t2)
    a = jnp.where(e_lane == i1, 1.0, jnp.where(e_lane == i2, 1.0, 0.0))
    r = lax.broadcasted_iota(jnp.int32, (tm, tm), 0)
    cc = lax.broadcasted_iota(jnp.int32, (tm, tm), 1)
    tri = jnp.where(cc < r, 1.0, 0.0).astype(BF16)
    rank = jnp.dot(tri, a.astype(BF16), preferred_element_type=F32)
    k8 = jnp.ceil(jnp.sum(a, axis=0, keepdims=True) * (1.0 / SUBLANES))
    ur = lax.broadcasted_iota(jnp.int32, (LANES, LANES), 0)
    uc = lax.broadcasted_iota(jnp.int32, (LANES, LANES), 1)
    upper = jnp.where(ur < uc, 1.0, 0.0).astype(BF16)
    k8_rows = jnp.broadcast_to(k8, (SUBLANES, LANES)).astype(BF16)
    off = jnp.dot(k8_rows, upper, preferred_element_type=F32)[0:1] * SUBLANES
    pos = off + rank
    lp0 = jnp.sum(jnp.where(e_lane == i1, pos, 0.0), axis=-1, keepdims=True)
    lp1 = jnp.sum(jnp.where(e_lane == i2, pos, 0.0), axis=-1, keepdims=True)
    run_old = run_ref[...]
    run_new = run_old + k8 * SUBLANES
    run_ref[...] = run_new
    cnt_ref[...] = run_new.astype(jnp.int32)
    srow = lax.broadcasted_iota(jnp.int32, (SUBLANES, LANES), 0)
    stats_ref[0] = jnp.where(srow == 0, k8, jnp.where(srow == 1, off, jnp.where(
        srow == 2, run_old, 0.0))).astype(jnp.int32)
    lp = jnp.where(lane == 0, lp0, jnp.where(lane == 1, lp1, 0.0))
    lpc_ref[0] = lp.astype(jnp.int32)
    lpr_ref[0, 0] = lp.T[:SUBLANES].astype(jnp.int32)
    wgt_ref[0] = jnp.where(lane == 0, w1, jnp.where(lane == 1, w2, 0.0))


def _router(x, g, sc, sh, w_group, w_router):
    b, s, d = x.shape
    ts = min(MOE_TOKEN_TILE, s)
    nt = s // ts
    w = jnp.concatenate([w_group, w_router], axis=1)
    w = jnp.pad(w, ((0, 0), (0, LANES - w.shape[1])))
    row = lambda i, j: (i, j, 0)
    return pl.pallas_call(
        _router_kernel,
        grid=(b, nt),
        in_specs=[
            pl.BlockSpec((1, ts, d), row),
            pl.BlockSpec((1, d), lambda i, j: (0, 0)),
            pl.BlockSpec((1, 1, d), lambda i, j: (i, 0, 0)),
            pl.BlockSpec((1, 1, d), lambda i, j: (i, 0, 0)),
            pl.BlockSpec((d, LANES), lambda i, j: (0, 0)),
        ],
        out_specs=[
            pl.BlockSpec((1, ts, LANES), row),
            pl.BlockSpec((1, 1, SUBLANES, ts), lambda i, j: (i, j, 0, 0)),
            pl.BlockSpec((1, ts, LANES), row),
            pl.BlockSpec((1, SUBLANES, LANES), lambda i, j: (i * nt + j, 0, 0)),
            pl.BlockSpec((1, LANES), lambda i, j: (0, 0)),
        ],
        out_shape=[jax.ShapeDtypeStruct((b, s, LANES), jnp.int32),
                   jax.ShapeDtypeStruct((b, nt, SUBLANES, ts), jnp.int32),
                   jax.ShapeDtypeStruct((b, s, LANES), F32),
                   jax.ShapeDtypeStruct((b * nt, SUBLANES, LANES), jnp.int32),
                   jax.ShapeDtypeStruct((1, LANES), jnp.int32)],
        scratch_shapes=[pltpu.VMEM((1, LANES), F32)],
        compiler_params=_params("arbitrary", "arbitrary"),
    )(x, g.reshape(1, d), sc, sh, w)


def _chunk_copies(meta_ref, blk, local_ref, hbm_ref, sem, *, to_hbm, start, max_groups):
    def per_expert(e, carry):
        at = blk * LANES + e
        k = meta_ref[at]
        lo = meta_ref[at + N_EXPERTS]
        hi = meta_ref[at + 2 * N_EXPERTS]
        done = jnp.int32(0)
        for bit in range(max_groups.bit_length()):
            rows = SUBLANES << bit
            take = (k >> bit) & 1

            @pl.when(take == 1)
            def _():
                loc = local_ref.at[pl.ds(pl.multiple_of(lo + done, SUBLANES), rows)]
                far = hbm_ref.at[pl.ds(pl.multiple_of(hi + done, SUBLANES), rows)]
                cp = pltpu.make_async_copy(loc, far, sem) if to_hbm else pltpu.make_async_copy(far, loc, sem)
                if start:
                    cp.start()
                else:
                    cp.wait()

            done = done + take * rows
        return carry

    lax.fori_loop(0, N_EXPERTS, per_expert, 0)


def _dispatch_kernel(last_ref, meta_ref, lpr_ref, x_ref, g_ref, sc_ref, sh_ref, xs_ref, loc_ref, zero_ref,
                     sem, zsem):
    tm = x_ref.shape[1]
    zrows = zero_ref.shape[0]
    blk = pl.program_id(0) * pl.num_programs(1) + pl.program_id(1)
    n_blk = pl.num_programs(0) * pl.num_programs(1)
    slot = blk % 2

    @pl.when((pl.program_id(0) == 0) & (pl.program_id(1) == 0))
    def _():
        zero_ref[...] = jnp.zeros_like(zero_ref)

        def fill(e):
            start = pl.multiple_of(last_ref[e], zrows)
            return pltpu.make_async_copy(zero_ref, xs_ref.at[pl.ds(start, zrows)], zsem)

        for e in range(N_EXPERTS):
            @pl.when(last_ref[e] >= 0)
            def _():
                fill(e).start()
        for e in range(N_EXPERTS):
            @pl.when(last_ref[e] >= 0)
            def _():
                fill(e).wait()

        def tail(k):
            start = pl.multiple_of(k * zrows, zrows)
            return pltpu.make_async_copy(zero_ref, xs_ref.at[pl.ds(start, zrows)], zsem)

        n_used = last_ref[N_EXPERTS]
        n_tiles = xs_ref.shape[0] // zrows
        lax.fori_loop(n_used, n_tiles, lambda k, c: (tail(k).start(), c)[1], 0)
        lax.fori_loop(n_used, n_tiles, lambda k, c: (tail(k).wait(), c)[1], 0)

    hf = _modulated_norm(x_ref[0], g_ref[...], sc_ref[0], sh_ref[0]).astype(BF16)
    rows = lax.broadcasted_iota(jnp.int32, (loc_ref.shape[1], tm), 0)
    sel = jnp.where(rows == lpr_ref[0, 0, 0:1, :], 1.0,
                    jnp.where(rows == lpr_ref[0, 0, 1:2, :], 1.0, 0.0)).astype(BF16)
    loc_ref[slot] = jnp.dot(sel, hf, preferred_element_type=F32)

    def copies(b, sl, start):
        _chunk_copies(meta_ref, b, loc_ref.at[sl], xs_ref, sem.at[sl], to_hbm=True, start=start,
                      max_groups=tm // SUBLANES)

    copies(blk, slot, True)

    @pl.when(blk > 0)
    def _():
        copies(blk - 1, 1 - slot, False)

    @pl.when(blk == n_blk - 1)
    def _():
        copies(blk, slot, False)


def _local_rows(tm):
    return 2 * tm + N_EXPERTS * SUBLANES


def _dispatch(x, g, sc, sh, meta, lpr, last_tile_row, n_rows):
    b, s, d = x.shape
    tm = min(MOE_TOKEN_TILE, s)
    nt = s // tm
    row = lambda i, j, last, meta: (i, j, 0)
    grid_spec = pltpu.PrefetchScalarGridSpec(
        num_scalar_prefetch=2,
        grid=(b, nt),
        in_specs=[
            pl.BlockSpec((1, 1, SUBLANES, tm), lambda i, j, last, meta: (i, j, 0, 0)),
            pl.BlockSpec((1, tm, d), row),
            pl.BlockSpec((1, d), lambda i, j, last, meta: (0, 0)),
            pl.BlockSpec((1, 1, d), lambda i, j, last, meta: (i, 0, 0)),
            pl.BlockSpec((1, 1, d), lambda i, j, last, meta: (i, 0, 0)),
        ],
        out_specs=pl.BlockSpec(memory_space=pl.ANY),
        scratch_shapes=[pltpu.VMEM((2, _local_rows(tm), d), F32), pltpu.VMEM((MOE_ROW_TILE, d), F32),
                        pltpu.SemaphoreType.DMA((2,)), pltpu.SemaphoreType.DMA(())],
    )
    return pl.pallas_call(
        _dispatch_kernel,
        grid_spec=grid_spec,
        out_shape=jax.ShapeDtypeStruct((n_rows, d), F32),
        compiler_params=_params("arbitrary", "arbitrary"),
    )(last_tile_row, meta, lpr, x, g.reshape(1, d), sc, sh)


def _expert_kernel(te_ref, nu_ref, xs_ref, w1_ref, w3_ref, w2_ref, ys_ref, w1b_ref, w3b_ref, w2b_ref):
    i = pl.program_id(0)

    @pl.when((i == 0) | (te_ref[i] != te_ref[jnp.maximum(i - 1, 0)]))
    def _():
        w1b_ref[...] = w1_ref[0, 0].astype(BF16)
        w3b_ref[...] = w3_ref[0, 0].astype(BF16)
        w2b_ref[...] = w2_ref[0, 0].astype(BF16)

    @pl.when(i < nu_ref[0])
    def _():
        xb = xs_ref[...].astype(BF16)
        a = jnp.dot(xb, w1b_ref[...], preferred_element_type=F32)
        bb = jnp.dot(xb, w3b_ref[...], preferred_element_type=F32)
        hmid = (_silu(a) * bb).astype(BF16)
        ys_ref[...] = jnp.dot(hmid, w2b_ref[...], preferred_element_type=F32)

    @pl.when(i >= nu_ref[0])
    def _():
        ys_ref[...] = jnp.zeros_like(ys_ref)


def _expert_ffn(xs, tile_expert, n_used, w1, w3, w2, layer):
    p, d = xs.shape
    de = w1.shape[3]
    tm = MOE_ROW_TILE
    grid_spec = pltpu.PrefetchScalarGridSpec(
        num_scalar_prefetch=2,
        grid=(p // tm,),
        in_specs=[
            pl.BlockSpec((tm, d), lambda i, te, nu: (jnp.maximum(jnp.minimum(i, nu[0] - 1), 0), 0)),
            pl.BlockSpec((1, 1, d, de), lambda i, te, nu: (layer, te[i], 0, 0)),
            pl.BlockSpec((1, 1, d, de), lambda i, te, nu: (layer, te[i], 0, 0)),
            pl.BlockSpec((1, 1, de, d), lambda i, te, nu: (layer, te[i], 0, 0)),
        ],
        out_specs=pl.BlockSpec((tm, d), lambda i, te, nu: (i, 0)),
        scratch_shapes=[pltpu.VMEM((d, de), BF16), pltpu.VMEM((d, de), BF16), pltpu.VMEM((de, d), BF16)],
    )
    return pl.pallas_call(
        _expert_kernel,
        grid_spec=grid_spec,
        out_shape=jax.ShapeDtypeStruct((p, d), F32),
        compiler_params=_params("arbitrary"),
    )(tile_expert, n_used, xs, w1, w3, w2)


def _combine_kernel(meta_ref, ys_ref, lpc_ref, wgt_ref, x_ref, gate_ref, o_ref, loc_ref, sem):
    tm = x_ref.shape[1]
    n_loc = loc_ref.shape[1]
    blk = pl.program_id(0) * pl.num_programs(1) + pl.program_id(1)
    n_blk = pl.num_programs(0) * pl.num_programs(1)
    slot = blk % 2

    def fetch(b, sl):
        loc_ref[sl, 2 * tm:, :] = jnp.zeros((n_loc - 2 * tm, loc_ref.shape[2]), F32)
        _chunk_copies(meta_ref, b, loc_ref.at[sl], ys_ref, sem.at[sl], to_hbm=False, start=True,
                      max_groups=tm // SUBLANES)

    @pl.when(blk == 0)
    def _():
        fetch(blk, slot)

    @pl.when(blk + 1 < n_blk)
    def _():
        fetch(blk + 1, 1 - slot)

    _chunk_copies(meta_ref, blk, loc_ref.at[slot], ys_ref, sem.at[slot], to_hbm=False, start=False,
                  max_groups=tm // SUBLANES)
    cols = lax.broadcasted_iota(jnp.int32, (tm, n_loc), 1)
    lp = lpc_ref[0]
    wgt = wgt_ref[0]
    wm = jnp.where(cols == lp[:, 0:1], wgt[:, 0:1], jnp.where(cols == lp[:, 1:2], wgt[:, 1:2], 0.0))
    w_hi = wm.astype(BF16)
    w_lo = (wm - w_hi.astype(F32)).astype(BF16)
    yb = loc_ref[slot].astype(BF16)
    y = jnp.dot(w_hi, yb, preferred_element_type=F32) + jnp.dot(w_lo, yb, preferred_element_type=F32)
    o_ref[0] = x_ref[0] + gate_ref[0] * y


def _combine(ys, meta, lpc, wgt, x, gate):
    b, s, d = x.shape
    tm = min(MOE_TOKEN_TILE, s)
    nt = s // tm
    row = lambda i, j, meta: (i, j, 0)
    grid_spec = pltpu.PrefetchScalarGridSpec(
        num_scalar_prefetch=1,
        grid=(b, nt),
        in_specs=[
            pl.BlockSpec(memory_space=pl.ANY),
            pl.BlockSpec((1, tm, LANES), row),
            pl.BlockSpec((1, tm, LANES), row),
            pl.BlockSpec((1, tm, d), row),
            pl.BlockSpec((1, 1, d), lambda i, j, meta: (i, 0, 0)),
        ],
        out_specs=pl.BlockSpec((1, tm, d), row),
        scratch_shapes=[pltpu.VMEM((2, _local_rows(tm), d), F32), pltpu.SemaphoreType.DMA((2,))],
    )
    return pl.pallas_call(
        _combine_kernel,
        grid_spec=grid_spec,
        out_shape=jax.ShapeDtypeStruct((b, s, d), F32),
        compiler_params=_params("arbitrary", "arbitrary"),
    )(meta, ys, lpc, wgt, x, gate)


def _hier_moe_residual(x, g, sc, sh, gate, w_group, w_router, w1, w3, w2, layer):
    b, s, d = x.shape
    t = b * s
    lpc, lpr, wgt, stats, cnt = _router(x, g, sc, sh, w_group, w_router)
    tm = MOE_ROW_TILE
    ex = slice(N_GROUPS, N_GROUPS + N_EXPERTS)
    counts = cnt[0, ex]
    tiles = (counts + tm - 1) // tm
    tile_end = jnp.cumsum(tiles)
    base = (tile_end - tiles) * tm
    n_blocks = stats.shape[0]
    n_tiles = (2 * t + n_blocks * N_EXPERTS * SUBLANES) // tm + N_EXPERTS
    tile_expert = jnp.minimum(
        jnp.sum(tile_end[None, :] <= jnp.arange(n_tiles)[:, None], axis=1), N_EXPERTS - 1).astype(jnp.int32)
    n_used = tile_end[-1:].astype(jnp.int32)
    meta = jnp.concatenate([stats[:, 0, ex], stats[:, 1, ex], base[None, :] + stats[:, 2, ex],
                            jnp.zeros((n_blocks, LANES - 3 * N_EXPERTS), jnp.int32)], axis=1)
    meta = meta.astype(jnp.int32).reshape(n_blocks * LANES)
    last_tile_row = jnp.where(tiles > 0, (tile_end - 1) * tm, -1).astype(jnp.int32)
    last_tile_row = jnp.concatenate([last_tile_row, n_used])
    xs = _dispatch(x, g, sc, sh, meta, lpr, last_tile_row, n_tiles * tm)
    ys = _expert_ffn(xs, tile_expert, n_used, w1, w3, w2, layer)
    return _combine(ys, meta, lpc, wgt, x, gate)


def kernel(x, c, rel_table, ada_w, ada_b, norm_mix_g, norm_ffn_g, da_w_in, da_w_out, da_q_gain, da_k_gain, da_lam_q1, da_lam_k1, da_lam_q2, da_lam_k2, da_subln_g, ml_w_in, ml_conv_w, ml_conv_b, ml_wq, ml_wk, ml_wv, ml_gate_w, ml_gate_b, ml_outnorm_g, ml_skip, ml_w_out, moe_w_group, moe_w_router, moe_w1, moe_w3, moe_w2):
    depth = ada_w.shape[0]
    d = x.shape[-1]
    mod = _ada_mod(c, ada_w, ada_b)
    for i in range(depth):
        sh1, sc1, g1, sh2, sc2, g2 = [mod[i, :, None, k * d:(k + 1) * d] for k in range(6)]
        j = i // N_MIXERS
        if i % N_MIXERS == 0:
            qkv = _prenorm_matmul(x, norm_mix_g[i], sc1, sh1, da_w_in[j].astype(BF16))
            lam_init = 0.8 - 0.6 * math.exp(-0.3 * i)
            lam = (jnp.exp(jnp.sum(da_lam_q1[j] * da_lam_k1[j])) - jnp.exp(jnp.sum(da_lam_q2[j] * da_lam_k2[j]))
                   + lam_init).astype(F32)
            o = _diff_attention(qkv, da_q_gain[j], da_k_gain[j], da_subln_g[j], lam, lam_init, rel_table)
            x = _proj_residual(o, da_w_out[j].astype(BF16), x, g1)
        else:
            up = _prenorm_matmul(x, norm_mix_g[i], sc1, sh1, ml_w_in[j].astype(BF16))
            q, k, v, xc, pre = _mlstm_pre(up, ml_conv_w[j], ml_conv_b[j], ml_wq[j], ml_wk[j], ml_wv[j],
                                          ml_gate_w[j], ml_gate_b[j])
            hn = _mlstm_scan(q, k, v, pre, ml_outnorm_g[j])
            x = _mlstm_out(hn, xc, up, ml_skip[j], ml_w_out[j].astype(BF16), x, g1)
        x = _hier_moe_residual(x, norm_ffn_g[i], sc2, sh2, g2, moe_w_group[i], moe_w_router[i],
                               moe_w1, moe_w3, moe_w2, i)
    return x
```

```python
import functools
import math

import jax
import jax.numpy as jnp
from jax import lax
from jax.experimental import pallas as pl
from jax.experimental.pallas import tpu as pltpu

EPS = 1e-6
N_MIXERS = 2
DA_HEADS = 8
DA_HEAD_DIM = 64
N_BUCKETS = 32
MAX_DISTANCE = 128
ML_HEADS = 4
ML_CHUNK = 256
ML_QKV_BLOCK = 4
N_GROUPS = 4
EXPERTS_PER_GROUP = 8
N_EXPERTS = N_GROUPS * EXPERTS_PER_GROUP

LANES = 128
SUBLANES = 8
ROW_GROUP = 16
VMEM_LIMIT = 48 * 1024 * 1024
ATTN_Q_TILE = 512
ATTN_ROW_BLOCK = 128
ROW_TILE = 512
MOE_ROW_TILE = 512
MOE_TOKEN_TILE = 512
MLSTM_COL_TILE = 256
MLSTM_ROW_CHUNK = 512

F32 = jnp.float32
BF16 = jnp.bfloat16
LOG2E = math.log2(math.e)


def _params(*sem):
    return pltpu.CompilerParams(dimension_semantics=sem, vmem_limit_bytes=VMEM_LIMIT)


def _silu(x):
    return x / (1.0 + jnp.exp(-x))


def _log_sigmoid(x):
    return jnp.minimum(x, 0.0) - jnp.log(1.0 + jnp.exp(-jnp.abs(x)))


def _modulated_norm(x, g, sc, sh):
    ms = jnp.mean(x * x, axis=-1, keepdims=True)
    return x * lax.rsqrt(ms + EPS) * g * (1.0 + sc) + sh


def _ada_kernel(c_ref, w_ref, b_ref, o_ref):
    ca = _silu(c_ref[...]).astype(BF16)
    o_ref[0] = jnp.dot(ca, w_ref[0].astype(BF16), preferred_element_type=F32) + b_ref[0]


def _ada_mod(c, ada_w, ada_b):
    depth, d, n = ada_w.shape
    b = c.shape[0]
    tn = 1536
    return pl.pallas_call(
        _ada_kernel,
        grid=(depth, n // tn),
        in_specs=[
            pl.BlockSpec((b, d), lambda i, j: (0, 0)),
            pl.BlockSpec((1, d, tn), lambda i, j: (i, 0, j)),
            pl.BlockSpec((1, 1, tn), lambda i, j: (i, 0, j)),
        ],
        out_specs=pl.BlockSpec((1, b, tn), lambda i, j: (i, 0, j)),
        out_shape=jax.ShapeDtypeStruct((depth, b, n), F32),
        compiler_params=_params("parallel", "parallel"),
    )(c, ada_w, ada_b.reshape(depth, 1, n))


def _prenorm_matmul_kernel(x_ref, g_ref, sc_ref, sh_ref, w_ref, o_ref, *, n_chunk):
    y = _modulated_norm(x_ref[0], g_ref[...], sc_ref[0], sh_ref[0]).astype(BF16)
    n = w_ref.shape[1]
    for n0 in range(0, n, n_chunk):
        o_ref[0, :, n0:n0 + n_chunk] = jnp.dot(
            y, w_ref[:, n0:n0 + n_chunk], preferred_element_type=F32).astype(o_ref.dtype)


def _prenorm_matmul(x, g, sc, sh, w):
    b, s, d = x.shape
    n = w.shape[1]
    ts = min(ROW_TILE, s)
    return pl.pallas_call(
        functools.partial(_prenorm_matmul_kernel, n_chunk=1024),
        grid=(b, s // ts),
        in_specs=[
            pl.BlockSpec((1, ts, d), lambda i, j: (i, j, 0)),
            pl.BlockSpec((1, d), lambda i, j: (0, 0)),
            pl.BlockSpec((1, 1, d), lambda i, j: (i, 0, 0)),
            pl.BlockSpec((1, 1, d), lambda i, j: (i, 0, 0)),
            pl.BlockSpec((d, n), lambda i, j: (0, 0)),
        ],
        out_specs=pl.BlockSpec((1, ts, n), lambda i, j: (i, j, 0)),
        out_shape=jax.ShapeDtypeStruct((b, s, n), BF16),
        compiler_params=_params("parallel", "parallel"),
    )(x, g.reshape(1, d), sc, sh, w)


def _proj_residual_kernel(a_ref, w_ref, x_ref, gate_ref, o_ref):
    y = jnp.dot(a_ref[0], w_ref[...], preferred_element_type=F32)
    o_ref[0] = x_ref[0] + gate_ref[0] * y


def _proj_residual(a, w, x, gate):
    b, s, k = a.shape
    d = w.shape[1]
    ts = min(ROW_TILE, s)
    return pl.pallas_call(
        _proj_residual_kernel,
        grid=(b, s // ts),
        in_specs=[
            pl.BlockSpec((1, ts, k), lambda i, j: (i, j, 0)),
            pl.BlockSpec((k, d), lambda i, j: (0, 0)),
            pl.BlockSpec((1, ts, d), lambda i, j: (i, j, 0)),
            pl.BlockSpec((1, 1, d), lambda i, j: (i, 0, 0)),
        ],
        out_specs=pl.BlockSpec((1, ts, d), lambda i, j: (i, j, 0)),
        out_shape=jax.ShapeDtypeStruct((b, s, d), F32),
        compiler_params=_params("parallel", "parallel"),
    )(a, w, x, gate)


def _t5_bucket(rel):
    nb = N_BUCKETS // 2
    max_exact = nb // 2
    ret = jnp.where(rel > 0, nb, 0)
    n = jnp.abs(rel)
    nf = jnp.maximum(n, 1).astype(F32)
    large = max_exact + (jnp.log(nf / max_exact) / math.log(MAX_DISTANCE / max_exact)
                         * (nb - max_exact)).astype(jnp.int32)
    large = jnp.minimum(large, nb - 1)
    return ret + jnp.where(n < max_exact, n, large)


def _bias_band_kernel(t_ref, o_ref, *, tq, nb):
    u = jnp.broadcast_to(t_ref[0], (tq, t_ref.shape[2]))
    y = pltpu.roll(u, 1, 1, stride=1, stride_axis=0)
    for jb in range(nb):
        o_ref[0, jb] = y[:, tq + jb * LANES:tq + (jb + 1) * LANES]


def _bias_band(rel_table, s, tq):
    h = rel_table.shape[1]
    delta = jnp.arange(-(s - 1), s)
    t = rel_table[_t5_bucket(delta)].T.astype(F32) * LOG2E
    u = jnp.concatenate([t, jnp.zeros((h, 1), F32)], axis=1).reshape(h, 1, 2 * s)
    nb = (2 * s - tq) // LANES
    return pl.pallas_call(
        functools.partial(_bias_band_kernel, tq=tq, nb=nb),
        grid=(h,),
        in_specs=[pl.BlockSpec((1, 1, 2 * s), lambda i: (i, 0, 0))],
        out_specs=pl.BlockSpec((1, nb, tq, LANES), lambda i: (i, 0, 0, 0)),
        out_shape=jax.ShapeDtypeStruct((h, nb, tq, LANES), F32),
        compiler_params=_params("parallel"),
    )(u)


def _group_rms(x, gain, half):
    sq = x * x
    lane = lax.broadcasted_iota(jnp.int32, x.shape, 1)
    lo = lane < half
    s_lo = jnp.sum(jnp.where(lo, sq, 0.0), axis=-1, keepdims=True)
    s_all = jnp.sum(sq, axis=-1, keepdims=True)
    ms = jnp.where(lo, s_lo, s_all - s_lo) * (1.0 / half)
    return x * lax.rsqrt(ms + EPS) * gain


def _diff_attn_kernel(lam_ref, q_ref, k_ref, v_ref, qg_ref, kg_ref, sg_ref, band_ref, o_ref, kn_ref, ve_ref,
                      *, tq, s, dh, out_scale):
    qi = pl.program_id(2)
    nq = pl.num_programs(2)
    dv = 2 * dh

    @pl.when(qi == 0)
    def _():
        kn_ref[...] = _group_rms(k_ref[0].astype(F32), kg_ref[...], dh).astype(BF16)
        ve_ref[:, :dv] = v_ref[0]
        ve_ref[:, dv:] = jnp.ones((s, dv), BF16)

    lam = lam_ref[0]
    q = _group_rms(q_ref[0].astype(F32), qg_ref[...], dh) * (dh ** -0.5 * LOG2E)
    lane = lax.broadcasted_iota(jnp.int32, q.shape, 1)
    q0 = jnp.where(lane < dh, q, 0.0).astype(BF16)
    q1 = jnp.where(lane < dh, 0.0, q).astype(BF16)
    jb0 = (nq - 1 - qi) * (tq // LANES)
    nkb = s // LANES

    def softmax_av(qm, r0, r1):
        sc = lax.dot_general(qm[r0:r1], kn_ref[...], (((1,), (1,)), ((), ())), preferred_element_type=F32)
        sc = sc + jnp.concatenate([band_ref[0, jb0 + kb, r0:r1, :] for kb in range(nkb)], axis=1)
        p = jnp.exp2(sc - jnp.max(sc, axis=-1, keepdims=True)).astype(BF16)
        oe = jnp.dot(p, ve_ref[...], preferred_element_type=F32)
        return oe[:, :dv], oe[:, dv:dv + 1]

    rb = ATTN_ROW_BLOCK
    for r0 in range(0, tq, rb):
        n0, l0 = softmax_av(q0, r0, r0 + rb)
        n1, l1 = softmax_av(q1, r0, r0 + rb)
        o = n0 * (1.0 / l0) - n1 * (lam / l1)
        ms = jnp.mean(o * o, axis=-1, keepdims=True)
        o_ref[0, r0:r0 + rb, :] = (o * lax.rsqrt(ms + EPS) * sg_ref[...] * out_scale).astype(o_ref.dtype)


def _diff_attention(qkv, q_gain, k_gain, subln_g, lam, lam_init, rel_table):
    b, s, _ = qkv.shape
    h, dh = DA_HEADS, DA_HEAD_DIM
    tq = min(ATTN_Q_TILE, s)
    band = _bias_band(rel_table, s, tq)
    nb = band.shape[1]
    qg = jnp.tile(q_gain, 2).reshape(1, 2 * dh)
    kg = jnp.tile(k_gain, 2).reshape(1, 2 * dh)
    kern = functools.partial(_diff_attn_kernel, tq=tq, s=s, dh=dh, out_scale=1.0 - lam_init)
    grid_spec = pltpu.PrefetchScalarGridSpec(
        num_scalar_prefetch=1,
        grid=(b, h, s // tq),
        in_specs=[
            pl.BlockSpec((1, tq, 2 * dh), lambda bi, hi, qi, lam: (bi, qi, hi)),
            pl.BlockSpec((1, s, 2 * dh), lambda bi, hi, qi, lam: (bi, 0, h + hi)),
            pl.BlockSpec((1, s, 2 * dh), lambda bi, hi, qi, lam: (bi, 0, 2 * h + hi)),
            pl.BlockSpec((1, 2 * dh), lambda bi, hi, qi, lam: (0, 0)),
            pl.BlockSpec((1, 2 * dh), lambda bi, hi, qi, lam: (0, 0)),
            pl.BlockSpec((1, 2 * dh), lambda bi, hi, qi, lam: (0, 0)),
            pl.BlockSpec((1, nb, tq, LANES), lambda bi, hi, qi, lam: (hi, 0, 0, 0),
                         pipeline_mode=pl.Buffered(1)),
        ],
        out_specs=pl.BlockSpec((1, tq, 2 * dh), lambda bi, hi, qi, lam: (bi, qi, hi)),
        scratch_shapes=[pltpu.VMEM((s, 2 * dh), BF16), pltpu.VMEM((s, 4 * dh), BF16)],
    )
    return pl.pallas_call(
        kern,
        grid_spec=grid_spec,
        out_shape=jax.ShapeDtypeStruct((b, s, h * 2 * dh), BF16),
        compiler_params=_params("parallel", "parallel", "arbitrary"),
    )(lam.reshape(1), qkv, qkv, qkv, qg, kg, subln_g.reshape(1, 2 * dh), band)


def _mlstm_pre_kernel(xm_ref, cw_ref, cb_ref, wq_ref, wk_ref, wv_ref, gq_ref, gk_ref, gv_ref, gb_ref,
                      q_ref, k_ref, v_ref, xc_ref, pre_ref, pad_ref, *, s, halo):
    j = pl.program_id(1)
    xm = xm_ref[0].astype(F32)
    cbw = xm.shape[1]
    pad_ref[0:8, :] = jnp.zeros((8, cbw), F32)
    pad_ref[8 + s:16 + s, :] = jnp.zeros((8, cbw), F32)
    pad_ref[8:8 + s, :] = xm

    @pl.when(j == 0)
    def _():
        pre_ref[0] = jnp.broadcast_to(gb_ref[...], pre_ref.shape[1:])

    rc = min(MLSTM_ROW_CHUNK, s)
    for r0 in range(0, s, rc):
        rows = slice(r0, r0 + rc)
        acc = cb_ref[...] + cw_ref[0:1, :] * pad_ref[8 - halo + r0:8 - halo + r0 + rc, :]
        for t in range(1, 2 * halo + 1):
            acc = acc + cw_ref[t:t + 1, :] * pad_ref[8 - halo + t + r0:8 - halo + t + r0 + rc, :]
        xcb = _silu(acc).astype(BF16)
        q = jnp.dot(xcb, wq_ref[0], preferred_element_type=F32).astype(BF16)
        k = jnp.dot(xcb, wk_ref[0], preferred_element_type=F32).astype(BF16)
        v = jnp.dot(xm_ref[0, rows, :], wv_ref[0], preferred_element_type=F32).astype(BF16)
        q_ref[0, rows, :] = q
        k_ref[0, rows, :] = k
        v_ref[0, rows, :] = v
        xc_ref[0, rows, :] = xcb
        pre_ref[0, rows, :] += (jnp.dot(q, gq_ref[...], preferred_element_type=F32)
                                + jnp.dot(k, gk_ref[...], preferred_element_type=F32)
                                + jnp.dot(v, gv_ref[...], preferred_element_type=F32))


def _blockdiag_dense(w, cb):
    nblk, blk, _ = w.shape
    per = cb // blk
    w4 = w.reshape(nblk // per, per, blk, blk)
    eye = jnp.eye(per, dtype=w.dtype)
    return jnp.einsum('jnio,nm->jnimo', w4, eye).reshape(nblk // per, cb, cb).astype(BF16)


def _mlstm_pre(up, conv_w, conv_b, wq, wk, wv, gate_w, gate_b):
    b, s, c2 = up.shape
    c = c2 // 2
    cb = MLSTM_COL_TILE
    ncb = c // cb
    kw = conv_w.shape[0]
    ng = gate_w.shape[0] * gate_w.shape[-1]

    def gate_mat(i):
        g = gate_w[:, i].transpose(1, 0, 2).reshape(c, ng)
        return jnp.pad(g, ((0, 0), (0, LANES - ng))).astype(BF16)

    gb = jnp.pad(gate_b.reshape(1, ng), ((0, 0), (0, LANES - ng)))
    col = lambda i, j: (i, 0, j)
    out_bf = jax.ShapeDtypeStruct((b, s, c), BF16)
    return pl.pallas_call(
        functools.partial(_mlstm_pre_kernel, s=s, halo=kw // 2),
        grid=(b, ncb),
        in_specs=[
            pl.BlockSpec((1, s, cb), col),
            pl.BlockSpec((kw, cb), lambda i, j: (0, j)),
            pl.BlockSpec((1, cb), lambda i, j: (0, j)),
            pl.BlockSpec((1, cb, cb), lambda i, j: (j, 0, 0)),
            pl.BlockSpec((1, cb, cb), lambda i, j: (j, 0, 0)),
            pl.BlockSpec((1, cb, cb), lambda i, j: (j, 0, 0)),
            pl.BlockSpec((cb, LANES), lambda i, j: (j, 0)),
            pl.BlockSpec((cb, LANES), lambda i, j: (j, 0)),
            pl.BlockSpec((cb, LANES), lambda i, j: (j, 0)),
            pl.BlockSpec((1, LANES), lambda i, j: (0, 0)),
        ],
        out_specs=[
            pl.BlockSpec((1, s, cb), col),
            pl.BlockSpec((1, s, cb), col),
            pl.BlockSpec((1, s, cb), col),
            pl.BlockSpec((1, s, cb), col),
            pl.BlockSpec((1, s, LANES), lambda i, j: (i, 0, 0)),
        ],
        out_shape=[out_bf, out_bf, out_bf, out_bf, jax.ShapeDtypeStruct((b, s, LANES), F32)],
        scratch_shapes=[pltpu.VMEM((s + 16, cb), F32)],
        compiler_params=_params("parallel", "arbitrary"),
    )(up, conv_w, conv_b.reshape(1, c), _blockdiag_dense(wq, cb), _blockdiag_dense(wk, cb),
      _blockdiag_dense(wv, cb), gate_mat(0), gate_mat(1), gate_mat(2), gb)


def _mlstm_scan_kernel(q_ref, k_ref, v_ref, gcol_ref, grow_ref, og_ref, o_ref,
                       cf_ref, cbk_ref, hf_ref, hb_ref, *, L, nc, dk):
    scale = dk ** -0.5
    row = lax.broadcasted_iota(jnp.int32, (L, L), 0)
    coli = lax.broadcasted_iota(jnp.int32, (L, L), 1)
    lower = coli <= row
    upper = coli >= row
    cf_ref[...] = jnp.zeros_like(cf_ref)
    cbk_ref[...] = jnp.zeros_like(cbk_ref)

    def chunk_step(c, c_ref, h_ref, n, m, fwd):
        mask, mask_t = (lower, upper) if fwd else (upper, lower)
        gi = 0 if fwd else 2
        sl = pl.ds(pl.multiple_of(c * L, L), L)
        qb = q_ref[0, sl, :]
        kb = k_ref[0, sl, :]
        vb = v_ref[0, sl, :]
        gc = gcol_ref[0, 0, c]
        gr = grow_ref[0, 0, c]
        ii_col = gc[:, gi:gi + 1]
        lf_col = _log_sigmoid(gc[:, gi + 1:gi + 2])
        ii_row = gr[gi:gi + 1, :]
        lf_row = _log_sigmoid(gr[gi + 1:gi + 2, :])
        b_col = jnp.sum(jnp.where(mask, lf_row, 0.0), axis=1, keepdims=True)
        b_row = jnp.sum(jnp.where(mask_t, lf_col, 0.0), axis=0, keepdims=True)
        dmat = jnp.where(mask, b_col - b_row + ii_row, -jnp.inf)
        inter = b_col + m
        m_t = jnp.maximum(inter, jnp.max(dmat, axis=1, keepdims=True))
        w_intra = jnp.exp(dmat - m_t)
        w_inter = jnp.exp(inter - m_t)
        sqk = lax.dot_general(qb, kb, (((1,), (1,)), ((), ())), preferred_element_type=F32)
        sw = sqk * scale * w_intra
        cmat = c_ref[...]
        num = (w_inter * jnp.dot(qb, cmat.astype(BF16), preferred_element_type=F32)
               + jnp.dot(sw.astype(BF16), vb, preferred_element_type=F32))
        den = (w_inter * jnp.sum(qb.astype(F32) * n, axis=1, keepdims=True)
               + jnp.sum(sw, axis=1, keepdims=True))
        h_ref[sl, :] = num / jnp.maximum(jnp.abs(den), jnp.exp(-m_t))
        b_tot = jnp.sum(lf_row, axis=1, keepdims=True)
        g_col = b_tot - b_col + ii_col
        m_new = jnp.maximum(b_tot + m, jnp.max(g_col, axis=0, keepdims=True))
        decay = jnp.exp(b_tot + m - m_new)
        kw = kb.astype(F32) * scale * jnp.exp(g_col - m_new)
        c_ref[...] = decay * cmat + lax.dot_general(
            kw.astype(BF16), vb, (((0,), (0,)), ((), ())), preferred_element_type=F32)
        n_new = decay * n + jnp.sum(kw, axis=0, keepdims=True)
        return n_new, m_new

    def body(i, carry):
        nf, mf, nb_, mb = carry
        nf, mf = chunk_step(i, cf_ref, hf_ref, nf, mf, True)
        nb_, mb = chunk_step(nc - 1 - i, cbk_ref, hb_ref, nb_, mb, False)
        return nf, mf, nb_, mb

    z_n = jnp.zeros((1, dk), F32)
    z_m = jnp.zeros((1, 1), F32)
    lax.fori_loop(0, nc, body, (z_n, z_m, z_n, z_m))

    def finish(c, carry):
        sl = pl.ds(pl.multiple_of(c * L, L), L)
        hs = hf_ref[sl, :] + hb_ref[sl, :]
        ms = jnp.mean(hs * hs, axis=-1, keepdims=True)
        o_ref[0, sl, :] = (hs * lax.rsqrt(ms + EPS) * og_ref[0]).astype(o_ref.dtype)
        return carry

    lax.fori_loop(0, nc, finish, 0)


def _mlstm_scan(q, k, v, pre, outnorm_g):
    b, s, c = q.shape
    hh = ML_HEADS
    dk = c // hh
    L = min(ML_CHUNK, s)
    nc = s // L
    idx = jnp.array([[0 * 2 * hh + h, 0 * 2 * hh + hh + h, 2 * hh + h, 2 * hh + hh + h] for h in range(hh)])
    g = pre[:, :, idx]
    gcol = g.transpose(0, 2, 1, 3).reshape(b, hh, nc, L, 4)
    grow = gcol.transpose(0, 1, 2, 4, 3)
    head = lambda i, j: (i, 0, j)
    return pl.pallas_call(
        functools.partial(_mlstm_scan_kernel, L=L, nc=nc, dk=dk),
        grid=(b, hh),
        in_specs=[
            pl.BlockSpec((1, s, dk), head),
            pl.BlockSpec((1, s, dk), head),
            pl.BlockSpec((1, s, dk), head),
            pl.BlockSpec((1, 1, nc, L, 4), lambda i, j: (i, j, 0, 0, 0)),
            pl.BlockSpec((1, 1, nc, 4, L), lambda i, j: (i, j, 0, 0, 0)),
            pl.BlockSpec((1, 1, dk), lambda i, j: (j, 0, 0)),
        ],
        out_specs=pl.BlockSpec((1, s, dk), head),
        out_shape=jax.ShapeDtypeStruct((b, s, c), BF16),
        scratch_shapes=[pltpu.VMEM((dk, dk), F32), pltpu.VMEM((dk, dk), F32),
                        pltpu.VMEM((s, dk), F32), pltpu.VMEM((s, dk), F32)],
        compiler_params=_params("parallel", "parallel"),
    )(q, k, v, gcol, grow, outnorm_g.reshape(hh, 1, dk))


def _mlstm_out_kernel(hn_ref, xc_ref, z_ref, skip_ref, w_ref, x_ref, gate_ref, o_ref):
    a = (hn_ref[0].astype(F32) + skip_ref[...] * xc_ref[0].astype(F32)) * _silu(z_ref[0].astype(F32))
    y = jnp.dot(a.astype(BF16), w_ref[...], preferred_element_type=F32)
    o_ref[0] = x_ref[0] + gate_ref[0] * y


def _mlstm_out(hn, xc, up, skip, w, x, gate):
    b, s, c = hn.shape
    d = w.shape[1]
    ts = min(ROW_TILE, s)
    row = lambda i, j: (i, j, 0)
    return pl.pallas_call(
        _mlstm_out_kernel,
        grid=(b, s // ts),
        in_specs=[
            pl.BlockSpec((1, ts, c), row),
            pl.BlockSpec((1, ts, c), row),
            pl.BlockSpec((1, ts, c), lambda i, j: (i, j, 1)),
            pl.BlockSpec((1, c), lambda i, j: (0, 0)),
            pl.BlockSpec((c, d), lambda i, j: (0, 0)),
            pl.BlockSpec((1, ts, d), row),
            pl.BlockSpec((1, 1, d), lambda i, j: (i, 0, 0)),
        ],
        out_specs=pl.BlockSpec((1, ts, d), row),
        out_shape=jax.ShapeDtypeStruct((b, s, d), F32),
        compiler_params=_params("parallel", "parallel"),
    )(hn, xc, up, skip.reshape(1, c), w, x, gate)


def _router_kernel(x_ref, g_ref, sc_ref, sh_ref, w_ref, lpc_ref, lpr_ref, wgt_ref, stats_ref, cnt_ref, run_ref):
    first = (pl.program_id(0) == 0) & (pl.program_id(1) == 0)

    @pl.when(first)
    def _():
        run_ref[...] = jnp.zeros_like(run_ref)

    hf = _modulated_norm(x_ref[0], g_ref[...], sc_ref[0], sh_ref[0])
    logits = jnp.dot(hf, w_ref[...], preferred_element_type=F32, precision=lax.Precision.HIGHEST)
    tm = logits.shape[0]
    lane = lax.broadcasted_iota(jnp.int32, logits.shape, 1).astype(F32)
    neg = -jnp.inf
    is_g = lane < N_GROUPS
    gl = jnp.where(is_g, logits, neg)
    gmax = jnp.max(gl, axis=-1, keepdims=True)
    g_sel = jnp.min(jnp.where(gl == gmax, lane, float(LANES)), axis=-1, keepdims=True)
    p_g = 1.0 / jnp.sum(jnp.where(is_g, jnp.exp(gl - gmax), 0.0), axis=-1, keepdims=True)
    e_lane = lane - N_GROUPS
    in_grp = (e_lane >= g_sel * EXPERTS_PER_GROUP) & (e_lane < (g_sel + 1) * EXPERTS_PER_GROUP)
    el = jnp.where(in_grp, logits, neg)
    emax = jnp.max(el, axis=-1, keepdims=True)
    i1 = jnp.min(jnp.where(el == emax, e_lane, float(LANES)), axis=-1, keepdims=True)
    el2 = jnp.where(e_lane == i1, neg, el)
    emax2 = jnp.max(el2, axis=-1, keepdims=True)
    i2 = jnp.min(jnp.where(el2 == emax2, e_lane, float(LANES)), axis=-1, keepdims=True)
    t2 = jnp.exp(emax2 - emax)
    w1 = p_g / (1.0 + t2)
    w2 = p_g * t2 / (1.0 + t2)
    a = jnp.where(e_lane == i1, 1.0, jnp.where(e_lane == i2, 1.0, 0.0))
    r = lax.broadcasted_iota(jnp.int32, (tm, tm), 0)
    cc = lax.broadcasted_iota(jnp.int32, (tm, tm), 1)
    tri = jnp.where(cc < r, 1.0, 0.0).astype(BF16)
    rank = jnp.dot(tri, a.astype(BF16), preferred_element_type=F32)
    k8 = jnp.ceil(jnp.sum(a, axis=0, keepdims=True) * (1.0 / ROW_GROUP))
    ur = lax.broadcasted_iota(jnp.int32, (LANES, LANES), 0)
    uc = lax.broadcasted_iota(jnp.int32, (LANES, LANES), 1)
    upper = jnp.where(ur < uc, 1.0, 0.0).astype(BF16)
    k8_rows = jnp.broadcast_to(k8, (SUBLANES, LANES)).astype(BF16)
    off = jnp.dot(k8_rows, upper, preferred_element_type=F32)[0:1] * ROW_GROUP
    pos = off + rank
    lp0 = jnp.sum(jnp.where(e_lane == i1, pos, 0.0), axis=-1, keepdims=True)
    lp1 = jnp.sum(jnp.where(e_lane == i2, pos, 0.0), axis=-1, keepdims=True)
    run_old = run_ref[...]
    run_new = run_old + k8 * ROW_GROUP
    run_ref[...] = run_new
    cnt_ref[...] = run_new.astype(jnp.int32)
    srow = lax.broadcasted_iota(jnp.int32, (SUBLANES, LANES), 0)
    stats_ref[0] = jnp.where(srow == 0, k8, jnp.where(srow == 1, off, jnp.where(
        srow == 2, run_old, 0.0))).astype(jnp.int32)
    lp = jnp.where(lane == 0, lp0, jnp.where(lane == 1, lp1, 0.0))
    lpc_ref[0] = lp.astype(jnp.int32)
    lpr_ref[0, 0] = lp.T[:SUBLANES].astype(jnp.int32)
    wgt_ref[0] = jnp.where(lane == 0, w1, jnp.where(lane == 1, w2, 0.0))


def _router(x, g, sc, sh, w_group, w_router):
    b, s, d = x.shape
    ts = min(MOE_TOKEN_TILE, s)
    nt = s // ts
    w = jnp.concatenate([w_group, w_router], axis=1)
    w = jnp.pad(w, ((0, 0), (0, LANES - w.shape[1])))
    row = lambda i, j: (i, j, 0)
    return pl.pallas_call(
        _router_kernel,
        grid=(b, nt),
        in_specs=[
            pl.BlockSpec((1, ts, d), row),
            pl.BlockSpec((1, d), lambda i, j: (0, 0)),
            pl.BlockSpec((1, 1, d), lambda i, j: (i, 0, 0)),
            pl.BlockSpec((1, 1, d), lambda i, j: (i, 0, 0)),
            pl.BlockSpec((d, LANES), lambda i, j: (0, 0)),
        ],
        out_specs=[
            pl.BlockSpec((1, ts, LANES), row),
            pl.BlockSpec((1, 1, SUBLANES, ts), lambda i, j: (i, j, 0, 0)),
            pl.BlockSpec((1, ts, LANES), row),
            pl.BlockSpec((1, SUBLANES, LANES), lambda i, j: (i * nt + j, 0, 0)),
            pl.BlockSpec((1, LANES), lambda i, j: (0, 0)),
        ],
        out_shape=[jax.ShapeDtypeStruct((b, s, LANES), jnp.int32),
                   jax.ShapeDtypeStruct((b, nt, SUBLANES, ts), jnp.int32),
                   jax.ShapeDtypeStruct((b, s, LANES), F32),
                   jax.ShapeDtypeStruct((b * nt, SUBLANES, LANES), jnp.int32),
                   jax.ShapeDtypeStruct((1, LANES), jnp.int32)],
        scratch_shapes=[pltpu.VMEM((1, LANES), F32)],
        compiler_params=_params("arbitrary", "arbitrary"),
    )(x, g.reshape(1, d), sc, sh, w)


def _chunk_copies(meta_ref, blk, local_ref, hbm_ref, sem, *, to_hbm, start, max_groups):
    def per_expert(e, carry):
        at = blk * LANES + e
        k = meta_ref[at]
        lo = meta_ref[at + N_EXPERTS]
        hi = meta_ref[at + 2 * N_EXPERTS]
        done = jnp.int32(0)
        for bit in range(max_groups.bit_length()):
            rows = ROW_GROUP << bit
            take = (k >> bit) & 1

            @pl.when(take == 1)
            def _():
                loc = local_ref.at[pl.ds(pl.multiple_of(lo + done, ROW_GROUP), rows)]
                far = hbm_ref.at[pl.ds(pl.multiple_of(hi + done, ROW_GROUP), rows)]
                cp = pltpu.make_async_copy(loc, far, sem) if to_hbm else pltpu.make_async_copy(far, loc, sem)
                if start:
                    cp.start()
                else:
                    cp.wait()

            done = done + take * rows
        return carry

    lax.fori_loop(0, N_EXPERTS, per_expert, 0)


def _dispatch_kernel(last_ref, meta_ref, lpr_ref, x_ref, g_ref, sc_ref, sh_ref, xs_ref, loc_ref, zero_ref,
                     sem, zsem):
    tm = x_ref.shape[1]
    zrows = zero_ref.shape[0]
    blk = pl.program_id(0) * pl.num_programs(1) + pl.program_id(1)
    n_blk = pl.num_programs(0) * pl.num_programs(1)
    slot = blk % 2

    @pl.when((pl.program_id(0) == 0) & (pl.program_id(1) == 0))
    def _():
        zero_ref[...] = jnp.zeros_like(zero_ref)

        def fill(e):
            start = pl.multiple_of(last_ref[e], zrows)
            return pltpu.make_async_copy(zero_ref, xs_ref.at[pl.ds(start, zrows)], zsem)

        for e in range(N_EXPERTS):
            @pl.when(last_ref[e] >= 0)
            def _():
                fill(e).start()
        for e in range(N_EXPERTS):
            @pl.when(last_ref[e] >= 0)
            def _():
                fill(e).wait()

        def tail(k):
            start = pl.multiple_of(k * zrows, zrows)
            return pltpu.make_async_copy(zero_ref, xs_ref.at[pl.ds(start, zrows)], zsem)

        n_used = last_ref[N_EXPERTS]
        n_tiles = xs_ref.shape[0] // zrows
        lax.fori_loop(n_used, n_tiles, lambda k, c: (tail(k).start(), c)[1], 0)
        lax.fori_loop(n_used, n_tiles, lambda k, c: (tail(k).wait(), c)[1], 0)

    hf = _modulated_norm(x_ref[0], g_ref[...], sc_ref[0], sh_ref[0]).astype(BF16)
    rows = lax.broadcasted_iota(jnp.int32, (loc_ref.shape[1], tm), 0)
    sel = jnp.where(rows == lpr_ref[0, 0, 0:1, :], 1.0,
                    jnp.where(rows == lpr_ref[0, 0, 1:2, :], 1.0, 0.0)).astype(BF16)
    loc_ref[slot] = jnp.dot(sel, hf, preferred_element_type=F32).astype(BF16)

    def copies(b, sl, start):
        _chunk_copies(meta_ref, b, loc_ref.at[sl], xs_ref, sem.at[sl], to_hbm=True, start=start,
                      max_groups=tm // ROW_GROUP)

    copies(blk, slot, True)

    @pl.when(blk > 0)
    def _():
        copies(blk - 1, 1 - slot, False)

    @pl.when(blk == n_blk - 1)
    def _():
        copies(blk, slot, False)


def _local_rows(tm):
    return 2 * tm + N_EXPERTS * ROW_GROUP


def _dispatch(x, g, sc, sh, meta, lpr, last_tile_row, n_rows):
    b, s, d = x.shape
    tm = min(MOE_TOKEN_TILE, s)
    nt = s // tm
    row = lambda i, j, last, meta: (i, j, 0)
    grid_spec = pltpu.PrefetchScalarGridSpec(
        num_scalar_prefetch=2,
        grid=(b, nt),
        in_specs=[
            pl.BlockSpec((1, 1, SUBLANES, tm), lambda i, j, last, meta: (i, j, 0, 0)),
            pl.BlockSpec((1, tm, d), row),
            pl.BlockSpec((1, d), lambda i, j, last, meta: (0, 0)),
            pl.BlockSpec((1, 1, d), lambda i, j, last, meta: (i, 0, 0)),
            pl.BlockSpec((1, 1, d), lambda i, j, last, meta: (i, 0, 0)),
        ],
        out_specs=pl.BlockSpec(memory_space=pl.ANY),
        scratch_shapes=[pltpu.VMEM((2, _local_rows(tm), d), BF16), pltpu.VMEM((MOE_ROW_TILE, d), BF16),
                        pltpu.SemaphoreType.DMA((2,)), pltpu.SemaphoreType.DMA(())],
    )
    return pl.pallas_call(
        _dispatch_kernel,
        grid_spec=grid_spec,
        out_shape=jax.ShapeDtypeStruct((n_rows, d), BF16),
        compiler_params=_params("arbitrary", "arbitrary"),
    )(last_tile_row, meta, lpr, x, g.reshape(1, d), sc, sh)


def _expert_kernel(te_ref, nu_ref, xs_ref, w1_ref, w3_ref, w2_ref, ys_ref, w1b_ref, w3b_ref, w2b_ref):
    i = pl.program_id(0)

    @pl.when((i == 0) | (te_ref[i] != te_ref[jnp.maximum(i - 1, 0)]))
    def _():
        w1b_ref[...] = w1_ref[0, 0].astype(BF16)
        w3b_ref[...] = w3_ref[0, 0].astype(BF16)
        w2b_ref[...] = w2_ref[0, 0].astype(BF16)

    @pl.when(i < nu_ref[0])
    def _():
        xb = xs_ref[...]
        a = jnp.dot(xb, w1b_ref[...], preferred_element_type=F32)
        bb = jnp.dot(xb, w3b_ref[...], preferred_element_type=F32)
        hmid = (_silu(a) * bb).astype(BF16)
        ys_ref[...] = jnp.dot(hmid, w2b_ref[...], preferred_element_type=F32).astype(ys_ref.dtype)

    @pl.when(i >= nu_ref[0])
    def _():
        ys_ref[...] = jnp.zeros_like(ys_ref)


def _expert_ffn(xs, tile_expert, n_used, w1, w3, w2, layer):
    p, d = xs.shape
    de = w1.shape[3]
    tm = MOE_ROW_TILE
    grid_spec = pltpu.PrefetchScalarGridSpec(
        num_scalar_prefetch=2,
        grid=(p // tm,),
        in_specs=[
            pl.BlockSpec((tm, d), lambda i, te, nu: (jnp.maximum(jnp.minimum(i, nu[0] - 1), 0), 0)),
            pl.BlockSpec((1, 1, d, de), lambda i, te, nu: (layer, te[i], 0, 0)),
            pl.BlockSpec((1, 1, d, de), lambda i, te, nu: (layer, te[i], 0, 0)),
            pl.BlockSpec((1, 1, de, d), lambda i, te, nu: (layer, te[i], 0, 0)),
        ],
        out_specs=pl.BlockSpec((tm, d), lambda i, te, nu: (i, 0)),
        scratch_shapes=[pltpu.VMEM((d, de), BF16), pltpu.VMEM((d, de), BF16), pltpu.VMEM((de, d), BF16)],
    )
    return pl.pallas_call(
        _expert_kernel,
        grid_spec=grid_spec,
        out_shape=jax.ShapeDtypeStruct((p, d), BF16),
        compiler_params=_params("arbitrary"),
    )(tile_expert, n_used, xs, w1, w3, w2)


def _combine_kernel(meta_ref, ys_ref, lpc_ref, wgt_ref, x_ref, gate_ref, o_ref, loc_ref, sem):
    tm = x_ref.shape[1]
    n_loc = loc_ref.shape[1]
    blk = pl.program_id(0) * pl.num_programs(1) + pl.program_id(1)
    n_blk = pl.num_programs(0) * pl.num_programs(1)
    slot = blk % 2

    def fetch(b, sl):
        loc_ref[sl, 2 * tm:, :] = jnp.zeros((n_loc - 2 * tm, loc_ref.shape[2]), BF16)
        _chunk_copies(meta_ref, b, loc_ref.at[sl], ys_ref, sem.at[sl], to_hbm=False, start=True,
                      max_groups=tm // ROW_GROUP)

    @pl.when(blk == 0)
    def _():
        fetch(blk, slot)

    @pl.when(blk + 1 < n_blk)
    def _():
        fetch(blk + 1, 1 - slot)

    _chunk_copies(meta_ref, blk, loc_ref.at[slot], ys_ref, sem.at[slot], to_hbm=False, start=False,
                  max_groups=tm // ROW_GROUP)
    cols = lax.broadcasted_iota(jnp.int32, (tm, n_loc), 1)
    lp = lpc_ref[0]
    wgt = wgt_ref[0]
    wm = jnp.where(cols == lp[:, 0:1], wgt[:, 0:1], jnp.where(cols == lp[:, 1:2], wgt[:, 1:2], 0.0))
    w_hi = wm.astype(BF16)
    w_lo = (wm - w_hi.astype(F32)).astype(BF16)
    yb = loc_ref[slot]
    y = jnp.dot(w_hi, yb, preferred_element_type=F32) + jnp.dot(w_lo, yb, preferred_element_type=F32)
    o_ref[0] = x_ref[0] + gate_ref[0] * y


def _combine(ys, meta, lpc, wgt, x, gate):
    b, s, d = x.shape
    tm = min(MOE_TOKEN_TILE, s)
    nt = s // tm
    row = lambda i, j, meta: (i, j, 0)
    grid_spec = pltpu.PrefetchScalarGridSpec(
        num_scalar_prefetch=1,
        grid=(b, nt),
        in_specs=[
            pl.BlockSpec(memory_space=pl.ANY),
            pl.BlockSpec((1, tm, LANES), row),
            pl.BlockSpec((1, tm, LANES), row),
            pl.BlockSpec((1, tm, d), row),
            pl.BlockSpec((1, 1, d), lambda i, j, meta: (i, 0, 0)),
        ],
        out_specs=pl.BlockSpec((1, tm, d), row),
        scratch_shapes=[pltpu.VMEM((2, _local_rows(tm), d), BF16), pltpu.SemaphoreType.DMA((2,))],
    )
    return pl.pallas_call(
        _combine_kernel,
        grid_spec=grid_spec,
        out_shape=jax.ShapeDtypeStruct((b, s, d), F32),
        compiler_params=_params("arbitrary", "arbitrary"),
    )(meta, ys, lpc, wgt, x, gate)


def _hier_moe_residual(x, g, sc, sh, gate, w_group, w_router, w1, w3, w2, layer):
    b, s, d = x.shape
    t = b * s
    lpc, lpr, wgt, stats, cnt = _router(x, g, sc, sh, w_group, w_router)
    tm = MOE_ROW_TILE
    ex = slice(N_GROUPS, N_GROUPS + N_EXPERTS)
    counts = cnt[0, ex]
    tiles = (counts + tm - 1) // tm
    tile_end = jnp.cumsum(tiles)
    base = (tile_end - tiles) * tm
    n_blocks = stats.shape[0]
    n_tiles = (2 * t + n_blocks * N_EXPERTS * ROW_GROUP) // tm + N_EXPERTS
    tile_expert = jnp.minimum(
        jnp.sum(tile_end[None, :] <= jnp.arange(n_tiles)[:, None], axis=1), N_EXPERTS - 1).astype(jnp.int32)
    n_used = tile_end[-1:].astype(jnp.int32)
    meta = jnp.concatenate([stats[:, 0, ex], stats[:, 1, ex], base[None, :] + stats[:, 2, ex],
                            jnp.zeros((n_blocks, LANES - 3 * N_EXPERTS), jnp.int32)], axis=1)
    meta = meta.astype(jnp.int32).reshape(n_blocks * LANES)
    last_tile_row = jnp.where(tiles > 0, (tile_end - 1) * tm, -1).astype(jnp.int32)
    last_tile_row = jnp.concatenate([last_tile_row, n_used])
    xs = _dispatch(x, g, sc, sh, meta, lpr, last_tile_row, n_tiles * tm)
    ys = _expert_ffn(xs, tile_expert, n_used, w1, w3, w2, layer)
    return _combine(ys, meta, lpc, wgt, x, gate)


def kernel(x, c, rel_table, ada_w, ada_b, norm_mix_g, norm_ffn_g, da_w_in, da_w_out, da_q_gain, da_k_gain, da_lam_q1, da_lam_k1, da_lam_q2, da_lam_k2, da_subln_g, ml_w_in, ml_conv_w, ml_conv_b, ml_wq, ml_wk, ml_wv, ml_gate_w, ml_gate_b, ml_outnorm_g, ml_skip, ml_w_out, moe_w_group, moe_w_router, moe_w1, moe_w3, moe_w2):
    depth = ada_w.shape[0]
    d = x.shape[-1]
    mod = _ada_mod(c, ada_w, ada_b)
    for i in range(depth):
        sh1, sc1, g1, sh2, sc2, g2 = [mod[i, :, None, k * d:(k + 1) * d] for k in range(6)]
        j = i // N_MIXERS
        if i % N_MIXERS == 0:
            qkv = _prenorm_matmul(x, norm_mix_g[i], sc1, sh1, da_w_in[j].astype(BF16))
            lam_init = 0.8 - 0.6 * math.exp(-0.3 * i)
            lam = (jnp.exp(jnp.sum(da_lam_q1[j] * da_lam_k1[j])) - jnp.exp(jnp.sum(da_lam_q2[j] * da_lam_k2[j]))
                   + lam_init).astype(F32)
            o = _diff_attention(qkv, da_q_gain[j], da_k_gain[j], da_subln_g[j], lam, lam_init, rel_table)
            x = _proj_residual(o, da_w_out[j].astype(BF16), x, g1)
        else:
            up = _prenorm_matmul(x, norm_mix_g[i], sc1, sh1, ml_w_in[j].astype(BF16))
            q, k, v, xc, pre = _mlstm_pre(up, ml_conv_w[j], ml_conv_b[j], ml_wq[j], ml_wk[j], ml_wv[j],
                                          ml_gate_w[j], ml_gate_b[j])
            hn = _mlstm_scan(q, k, v, pre, ml_outnorm_g[j])
            x = _mlstm_out(hn, xc, up, ml_skip[j], ml_w_out[j].astype(BF16), x, g1)
        x = _hier_moe_residual(x, norm_ffn_g[i], sc2, sh2, g2, moe_w_group[i], moe_w_router[i],
                               moe_w1, moe_w3, moe_w2, i)
    return x
```

```python
import functools
import math

import jax
import jax.numpy as jnp
from jax import lax
from jax.experimental import pallas as pl
from jax.experimental.pallas import tpu as pltpu

EPS = 1e-6
N_MIXERS = 2
DA_HEADS = 8
DA_HEAD_DIM = 64
N_BUCKETS = 32
MAX_DISTANCE = 128
ML_HEADS = 4
ML_CHUNK = 256
ML_QKV_BLOCK = 4
N_GROUPS = 4
EXPERTS_PER_GROUP = 8
N_EXPERTS = N_GROUPS * EXPERTS_PER_GROUP

LANES = 128
SUBLANES = 8
ROW_GROUP = 16
VMEM_LIMIT = 48 * 1024 * 1024
ATTN_Q_TILE = 2048
ATTN_ROW_BLOCK = 128
ROW_TILE = 512
MOE_ROW_TILE = 512
MOE_TOKEN_TILE = 512
MLSTM_COL_TILE = 256
MLSTM_ROW_CHUNK = 512

F32 = jnp.float32
BF16 = jnp.bfloat16
LOG2E = math.log2(math.e)


def _params(*sem):
    return pltpu.CompilerParams(dimension_semantics=sem, vmem_limit_bytes=VMEM_LIMIT)


def _silu(x):
    return x / (1.0 + jnp.exp(-x))


def _log_sigmoid(x):
    return jnp.minimum(x, 0.0) - jnp.log(1.0 + jnp.exp(-jnp.abs(x)))


def _modulated_norm(x, g, sc, sh):
    ms = jnp.mean(x * x, axis=-1, keepdims=True)
    return x * lax.rsqrt(ms + EPS) * g * (1.0 + sc) + sh


def _ada_kernel(c_ref, w_ref, b_ref, o_ref):
    ca = _silu(c_ref[...]).astype(BF16)
    o_ref[0] = jnp.dot(ca, w_ref[0].astype(BF16), preferred_element_type=F32) + b_ref[0]


def _ada_mod(c, ada_w, ada_b):
    depth, d, n = ada_w.shape
    b = c.shape[0]
    tn = 1536
    return pl.pallas_call(
        _ada_kernel,
        grid=(depth, n // tn),
        in_specs=[
            pl.BlockSpec((b, d), lambda i, j: (0, 0)),
            pl.BlockSpec((1, d, tn), lambda i, j: (i, 0, j)),
            pl.BlockSpec((1, 1, tn), lambda i, j: (i, 0, j)),
        ],
        out_specs=pl.BlockSpec((1, b, tn), lambda i, j: (i, 0, j)),
        out_shape=jax.ShapeDtypeStruct((depth, b, n), F32),
        compiler_params=_params("parallel", "parallel"),
    )(c, ada_w, ada_b.reshape(depth, 1, n))


def _prenorm_matmul_kernel(x_ref, g_ref, sc_ref, sh_ref, w_ref, o_ref, *, n_chunk):
    y = _modulated_norm(x_ref[0], g_ref[...], sc_ref[0], sh_ref[0]).astype(BF16)
    n = w_ref.shape[1]
    for n0 in range(0, n, n_chunk):
        o_ref[0, :, n0:n0 + n_chunk] = jnp.dot(
            y, w_ref[:, n0:n0 + n_chunk], preferred_element_type=F32).astype(o_ref.dtype)


def _prenorm_matmul(x, g, sc, sh, w):
    b, s, d = x.shape
    n = w.shape[1]
    ts = min(ROW_TILE, s)
    return pl.pallas_call(
        functools.partial(_prenorm_matmul_kernel, n_chunk=1024),
        grid=(b, s // ts),
        in_specs=[
            pl.BlockSpec((1, ts, d), lambda i, j: (i, j, 0)),
            pl.BlockSpec((1, d), lambda i, j: (0, 0)),
            pl.BlockSpec((1, 1, d), lambda i, j: (i, 0, 0)),
            pl.BlockSpec((1, 1, d), lambda i, j: (i, 0, 0)),
            pl.BlockSpec((d, n), lambda i, j: (0, 0)),
        ],
        out_specs=pl.BlockSpec((1, ts, n), lambda i, j: (i, j, 0)),
        out_shape=jax.ShapeDtypeStruct((b, s, n), BF16),
        compiler_params=_params("parallel", "parallel"),
    )(x, g.reshape(1, d), sc, sh, w)


def _proj_residual_kernel(a_ref, w_ref, x_ref, gate_ref, o_ref):
    y = jnp.dot(a_ref[0], w_ref[...], preferred_element_type=F32)
    o_ref[0] = x_ref[0] + gate_ref[0] * y


def _proj_residual(a, w, x, gate):
    b, s, k = a.shape
    d = w.shape[1]
    ts = min(ROW_TILE, s)
    return pl.pallas_call(
        _proj_residual_kernel,
        grid=(b, s // ts),
        in_specs=[
            pl.BlockSpec((1, ts, k), lambda i, j: (i, j, 0)),
            pl.BlockSpec((k, d), lambda i, j: (0, 0)),
            pl.BlockSpec((1, ts, d), lambda i, j: (i, j, 0)),
            pl.BlockSpec((1, 1, d), lambda i, j: (i, 0, 0)),
        ],
        out_specs=pl.BlockSpec((1, ts, d), lambda i, j: (i, j, 0)),
        out_shape=jax.ShapeDtypeStruct((b, s, d), F32),
        compiler_params=_params("parallel", "parallel"),
    )(a, w, x, gate)


def _t5_bucket(rel):
    nb = N_BUCKETS // 2
    max_exact = nb // 2
    ret = jnp.where(rel > 0, nb, 0)
    n = jnp.abs(rel)
    nf = jnp.maximum(n, 1).astype(F32)
    large = max_exact + (jnp.log(nf / max_exact) / math.log(MAX_DISTANCE / max_exact)
                         * (nb - max_exact)).astype(jnp.int32)
    large = jnp.minimum(large, nb - 1)
    return ret + jnp.where(n < max_exact, n, large)


def _bias_band_kernel(t_ref, o_ref, *, tq, nb):
    u = jnp.broadcast_to(t_ref[0], (tq, t_ref.shape[2]))
    y = pltpu.roll(u, 1, 1, stride=1, stride_axis=0)
    for jb in range(nb):
        o_ref[0, jb] = y[:, tq + jb * LANES:tq + (jb + 1) * LANES]


def _bias_band(rel_table, s, tq):
    h = rel_table.shape[1]
    delta = jnp.arange(-(s - 1), s)
    t = rel_table[_t5_bucket(delta)].T.astype(F32) * LOG2E
    u = jnp.concatenate([t, jnp.zeros((h, 1), F32)], axis=1).reshape(h, 1, 2 * s)
    nb = (2 * s - tq) // LANES
    return pl.pallas_call(
        functools.partial(_bias_band_kernel, tq=tq, nb=nb),
        grid=(h,),
        in_specs=[pl.BlockSpec((1, 1, 2 * s), lambda i: (i, 0, 0))],
        out_specs=pl.BlockSpec((1, nb, tq, LANES), lambda i: (i, 0, 0, 0)),
        out_shape=jax.ShapeDtypeStruct((h, nb, tq, LANES), F32),
        compiler_params=_params("parallel"),
    )(u)


def _group_rms(x, gain, half):
    sq = x * x
    lane = lax.broadcasted_iota(jnp.int32, x.shape, 1)
    lo = lane < half
    s_lo = jnp.sum(jnp.where(lo, sq, 0.0), axis=-1, keepdims=True)
    s_all = jnp.sum(sq, axis=-1, keepdims=True)
    ms = jnp.where(lo, s_lo, s_all - s_lo) * (1.0 / half)
    return x * lax.rsqrt(ms + EPS) * gain


def _diff_attn_kernel(lam_ref, q_ref, k_ref, v_ref, qg_ref, kg_ref, sg_ref, band_ref, o_ref, kn_ref, ve_ref,
                      *, tq, s, dh, out_scale):
    qi = pl.program_id(2)
    nq = pl.num_programs(2)
    dv = 2 * dh

    @pl.when(qi == 0)
    def _():
        kn_ref[...] = _group_rms(k_ref[0].astype(F32), kg_ref[...], dh).astype(BF16)
        ve_ref[:, :dv] = v_ref[0]
        ve_ref[:, dv:] = jnp.ones((s, dv), BF16)

    lam = lam_ref[0]
    q = _group_rms(q_ref[0].astype(F32), qg_ref[...], dh) * (dh ** -0.5 * LOG2E)
    lane = lax.broadcasted_iota(jnp.int32, q.shape, 1)
    q0 = jnp.where(lane < dh, q, 0.0).astype(BF16)
    q1 = jnp.where(lane < dh, 0.0, q).astype(BF16)
    jb0 = (nq - 1 - qi) * (tq // LANES)
    nkb = s // LANES

    def softmax_av(qm, r0, r1):
        sc = lax.dot_general(qm[r0:r1], kn_ref[...], (((1,), (1,)), ((), ())), preferred_element_type=F32)
        sc = sc + jnp.concatenate([band_ref[0, jb0 + kb, r0:r1, :] for kb in range(nkb)], axis=1)
        p = jnp.exp2(sc - jnp.max(sc, axis=-1, keepdims=True)).astype(BF16)
        oe = jnp.dot(p, ve_ref[...], preferred_element_type=F32)
        return oe[:, :dv], oe[:, dv:dv + 1]

    rb = ATTN_ROW_BLOCK
    for r0 in range(0, tq, rb):
        n0, l0 = softmax_av(q0, r0, r0 + rb)
        n1, l1 = softmax_av(q1, r0, r0 + rb)
        o = n0 * (1.0 / l0) - n1 * (lam / l1)
        ms = jnp.mean(o * o, axis=-1, keepdims=True)
        o_ref[0, r0:r0 + rb, :] = (o * lax.rsqrt(ms + EPS) * sg_ref[...] * out_scale).astype(o_ref.dtype)


def _diff_attention(qkv, q_gain, k_gain, subln_g, lam, lam_init, rel_table):
    b, s, _ = qkv.shape
    h, dh = DA_HEADS, DA_HEAD_DIM
    tq = min(ATTN_Q_TILE, s)
    band = _bias_band(rel_table, s, tq)
    nb = band.shape[1]
    qg = jnp.tile(q_gain, 2).reshape(1, 2 * dh)
    kg = jnp.tile(k_gain, 2).reshape(1, 2 * dh)
    kern = functools.partial(_diff_attn_kernel, tq=tq, s=s, dh=dh, out_scale=1.0 - lam_init)
    grid_spec = pltpu.PrefetchScalarGridSpec(
        num_scalar_prefetch=1,
        grid=(h, b, s // tq),
        in_specs=[
            pl.BlockSpec((1, tq, 2 * dh), lambda hi, bi, qi, lam: (bi, qi, hi)),
            pl.BlockSpec((1, s, 2 * dh), lambda hi, bi, qi, lam: (bi, 0, h + hi)),
            pl.BlockSpec((1, s, 2 * dh), lambda hi, bi, qi, lam: (bi, 0, 2 * h + hi)),
            pl.BlockSpec((1, 2 * dh), lambda hi, bi, qi, lam: (0, 0)),
            pl.BlockSpec((1, 2 * dh), lambda hi, bi, qi, lam: (0, 0)),
            pl.BlockSpec((1, 2 * dh), lambda hi, bi, qi, lam: (0, 0)),
            pl.BlockSpec((1, nb, tq, LANES), lambda hi, bi, qi, lam: (hi, 0, 0, 0),
                         pipeline_mode=pl.Buffered(1)),
        ],
        out_specs=pl.BlockSpec((1, tq, 2 * dh), lambda hi, bi, qi, lam: (bi, qi, hi)),
        scratch_shapes=[pltpu.VMEM((s, 2 * dh), BF16), pltpu.VMEM((s, 4 * dh), BF16)],
    )
    return pl.pallas_call(
        kern,
        grid_spec=grid_spec,
        out_shape=jax.ShapeDtypeStruct((b, s, h * 2 * dh), BF16),
        compiler_params=_params("parallel", "parallel", "arbitrary"),
    )(lam.reshape(1), qkv, qkv, qkv, qg, kg, subln_g.reshape(1, 2 * dh), band)


def _mlstm_pre_kernel(xm_ref, cw_ref, cb_ref, wq_ref, wk_ref, wv_ref, gq_ref, gk_ref, gv_ref, gb_ref,
                      q_ref, k_ref, v_ref, xc_ref, pre_ref, pad_ref, *, s, halo, n_heads):
    j = pl.program_id(1)
    xm = xm_ref[0].astype(F32)
    cbw = xm.shape[1]
    pad_ref[0:8, :] = jnp.zeros((8, cbw), F32)
    pad_ref[8 + s:16 + s, :] = jnp.zeros((8, cbw), F32)
    pad_ref[8:8 + s, :] = xm

    @pl.when(j == 0)
    def _():
        pre_ref[0] = jnp.broadcast_to(gb_ref[...], pre_ref.shape[1:])

    rc = min(MLSTM_ROW_CHUNK, s)
    for r0 in range(0, s, rc):
        rows = slice(r0, r0 + rc)
        acc = cb_ref[...] + cw_ref[0:1, :] * pad_ref[8 - halo + r0:8 - halo + r0 + rc, :]
        for t in range(1, 2 * halo + 1):
            acc = acc + cw_ref[t:t + 1, :] * pad_ref[8 - halo + t + r0:8 - halo + t + r0 + rc, :]
        xcb = _silu(acc).astype(BF16)
        q = jnp.dot(xcb, wq_ref[0], preferred_element_type=F32).astype(BF16)
        k = jnp.dot(xcb, wk_ref[0], preferred_element_type=F32).astype(BF16)
        v = jnp.dot(xm_ref[0, rows, :], wv_ref[0], preferred_element_type=F32).astype(BF16)
        q_ref[0, rows, :] = q
        k_ref[0, rows, :] = k
        v_ref[0, rows, :] = v
        xc_ref[0, rows, :] = xcb
        pre_ref[0, rows, :] += (jnp.dot(q, gq_ref[...], preferred_element_type=F32)
                                + jnp.dot(k, gk_ref[...], preferred_element_type=F32)
                                + jnp.dot(v, gv_ref[...], preferred_element_type=F32))

    @pl.when(j == pl.num_programs(1) - 1)
    def _():
        pre = pre_ref[0]
        lane = lax.broadcasted_iota(jnp.int32, pre.shape, 1)
        is_forget = (lane % (2 * n_heads)) >= n_heads
        pre_ref[0] = jnp.where(is_forget, _log_sigmoid(pre), pre)


def _blockdiag_dense(w, cb):
    nblk, blk, _ = w.shape
    per = cb // blk
    w4 = w.reshape(nblk // per, per, blk, blk)
    eye = jnp.eye(per, dtype=w.dtype)
    return jnp.einsum('jnio,nm->jnimo', w4, eye).reshape(nblk // per, cb, cb).astype(BF16)


def _mlstm_pre(up, conv_w, conv_b, wq, wk, wv, gate_w, gate_b):
    b, s, c2 = up.shape
    c = c2 // 2
    cb = MLSTM_COL_TILE
    ncb = c // cb
    kw = conv_w.shape[0]
    ng = gate_w.shape[0] * gate_w.shape[-1]

    def gate_mat(i):
        g = gate_w[:, i].transpose(1, 0, 2).reshape(c, ng)
        return jnp.pad(g, ((0, 0), (0, LANES - ng))).astype(BF16)

    gb = jnp.pad(gate_b.reshape(1, ng), ((0, 0), (0, LANES - ng)))
    col = lambda i, j: (i, 0, j)
    out_bf = jax.ShapeDtypeStruct((b, s, c), BF16)
    return pl.pallas_call(
        functools.partial(_mlstm_pre_kernel, s=s, halo=kw // 2, n_heads=gate_w.shape[-1] // 2),
        grid=(b, ncb),
        in_specs=[
            pl.BlockSpec((1, s, cb), col),
            pl.BlockSpec((kw, cb), lambda i, j: (0, j)),
            pl.BlockSpec((1, cb), lambda i, j: (0, j)),
            pl.BlockSpec((1, cb, cb), lambda i, j: (j, 0, 0)),
            pl.BlockSpec((1, cb, cb), lambda i, j: (j, 0, 0)),
            pl.BlockSpec((1, cb, cb), lambda i, j: (j, 0, 0)),
            pl.BlockSpec((cb, LANES), lambda i, j: (j, 0)),
            pl.BlockSpec((cb, LANES), lambda i, j: (j, 0)),
            pl.BlockSpec((cb, LANES), lambda i, j: (j, 0)),
            pl.BlockSpec((1, LANES), lambda i, j: (0, 0)),
        ],
        out_specs=[
            pl.BlockSpec((1, s, cb), col),
            pl.BlockSpec((1, s, cb), col),
            pl.BlockSpec((1, s, cb), col),
            pl.BlockSpec((1, s, cb), col),
            pl.BlockSpec((1, s, LANES), lambda i, j: (i, 0, 0)),
        ],
        out_shape=[out_bf, out_bf, out_bf, out_bf, jax.ShapeDtypeStruct((b, s, LANES), F32)],
        scratch_shapes=[pltpu.VMEM((s + 16, cb), F32)],
        compiler_params=_params("parallel", "arbitrary"),
    )(up, conv_w, conv_b.reshape(1, c), _blockdiag_dense(wq, cb), _blockdiag_dense(wk, cb),
      _blockdiag_dense(wv, cb), gate_mat(0), gate_mat(1), gate_mat(2), gb)


def _mlstm_scan_kernel(q_ref, k_ref, v_ref, gcol_ref, grow_ref, og_ref, o_ref,
                       cf_ref, cbk_ref, hf_ref, hb_ref, *, L, nc, dk):
    scale = dk ** -0.5
    row = lax.broadcasted_iota(jnp.int32, (L, L), 0)
    coli = lax.broadcasted_iota(jnp.int32, (L, L), 1)
    lower = coli <= row
    upper = coli >= row
    cf_ref[...] = jnp.zeros_like(cf_ref)
    cbk_ref[...] = jnp.zeros_like(cbk_ref)

    def chunk_step(c, c_ref, h_ref, n, m, fwd):
        mask, mask_t = (lower, upper) if fwd else (upper, lower)
        gi = 0 if fwd else 2
        sl = pl.ds(pl.multiple_of(c * L, L), L)
        qb = q_ref[0, sl, :]
        kb = k_ref[0, sl, :]
        vb = v_ref[0, sl, :]
        gc = gcol_ref[0, 0, c]
        gr = grow_ref[0, 0, c]
        ii_col = gc[:, gi:gi + 1]
        lf_col = gc[:, gi + 1:gi + 2]
        ii_row = gr[gi:gi + 1, :]
        lf_row = gr[gi + 1:gi + 2, :]
        b_col = jnp.sum(jnp.where(mask, lf_row, 0.0), axis=1, keepdims=True)
        b_row = jnp.sum(jnp.where(mask_t, lf_col, 0.0), axis=0, keepdims=True)
        dmat = jnp.where(mask, b_col - b_row + ii_row, -jnp.inf)
        inter = b_col + m
        m_t = jnp.maximum(inter, jnp.max(dmat, axis=1, keepdims=True))
        w_intra = jnp.exp(dmat - m_t)
        w_inter = jnp.exp(inter - m_t)
        sqk = lax.dot_general(qb, kb, (((1,), (1,)), ((), ())), preferred_element_type=F32)
        sw = sqk * scale * w_intra
        cmat = c_ref[...]
        num = (w_inter * jnp.dot(qb, cmat.astype(BF16), preferred_element_type=F32)
               + jnp.dot(sw.astype(BF16), vb, preferred_element_type=F32))
        den = (w_inter * jnp.sum(qb.astype(F32) * n, axis=1, keepdims=True)
               + jnp.sum(sw, axis=1, keepdims=True))
        h_ref[sl, :] = num / jnp.maximum(jnp.abs(den), jnp.exp(-m_t))
        b_tot = jnp.sum(lf_row, axis=1, keepdims=True)
        g_col = b_tot - b_col + ii_col
        m_new = jnp.maximum(b_tot + m, jnp.max(g_col, axis=0, keepdims=True))
        decay = jnp.exp(b_tot + m - m_new)
        kw = kb.astype(F32) * scale * jnp.exp(g_col - m_new)
        c_ref[...] = decay * cmat + lax.dot_general(
            kw.astype(BF16), vb, (((0,), (0,)), ((), ())), preferred_element_type=F32)
        n_new = decay * n + jnp.sum(kw, axis=0, keepdims=True)
        return n_new, m_new

    def body(i, carry):
        nf, mf, nb_, mb = carry
        nf, mf = chunk_step(i, cf_ref, hf_ref, nf, mf, True)
        nb_, mb = chunk_step(nc - 1 - i, cbk_ref, hb_ref, nb_, mb, False)
        return nf, mf, nb_, mb

    z_n = jnp.zeros((1, dk), F32)
    z_m = jnp.zeros((1, 1), F32)
    lax.fori_loop(0, nc, body, (z_n, z_m, z_n, z_m))

    def finish(c, carry):
        sl = pl.ds(pl.multiple_of(c * L, L), L)
        hs = hf_ref[sl, :] + hb_ref[sl, :]
        ms = jnp.mean(hs * hs, axis=-1, keepdims=True)
        o_ref[0, sl, :] = (hs * lax.rsqrt(ms + EPS) * og_ref[0]).astype(o_ref.dtype)
        return carry

    lax.fori_loop(0, nc, finish, 0)


def _mlstm_scan(q, k, v, pre, outnorm_g):
    b, s, c = q.shape
    hh = ML_HEADS
    dk = c // hh
    L = min(ML_CHUNK, s)
    nc = s // L
    idx = jnp.array([[0 * 2 * hh + h, 0 * 2 * hh + hh + h, 2 * hh + h, 2 * hh + hh + h] for h in range(hh)])
    g = pre[:, :, idx]
    gcol = g.transpose(0, 2, 1, 3).reshape(b, hh, nc, L, 4)
    grow = gcol.transpose(0, 1, 2, 4, 3)
    head = lambda i, j: (i, 0, j)
    return pl.pallas_call(
        functools.partial(_mlstm_scan_kernel, L=L, nc=nc, dk=dk),
        grid=(b, hh),
        in_specs=[
            pl.BlockSpec((1, s, dk), head),
            pl.BlockSpec((1, s, dk), head),
            pl.BlockSpec((1, s, dk), head),
            pl.BlockSpec((1, 1, nc, L, 4), lambda i, j: (i, j, 0, 0, 0)),
            pl.BlockSpec((1, 1, nc, 4, L), lambda i, j: (i, j, 0, 0, 0)),
            pl.BlockSpec((1, 1, dk), lambda i, j: (j, 0, 0)),
        ],
        out_specs=pl.BlockSpec((1, s, dk), head),
        out_shape=jax.ShapeDtypeStruct((b, s, c), BF16),
        scratch_shapes=[pltpu.VMEM((dk, dk), F32), pltpu.VMEM((dk, dk), F32),
                        pltpu.VMEM((s, dk), F32), pltpu.VMEM((s, dk), F32)],
        compiler_params=_params("parallel", "parallel"),
    )(q, k, v, gcol, grow, outnorm_g.reshape(hh, 1, dk))


def _mlstm_out_kernel(hn_ref, xc_ref, z_ref, skip_ref, w_ref, x_ref, gate_ref, o_ref):
    a = (hn_ref[0].astype(F32) + skip_ref[...] * xc_ref[0].astype(F32)) * _silu(z_ref[0].astype(F32))
    y = jnp.dot(a.astype(BF16), w_ref[...], preferred_element_type=F32)
    o_ref[0] = x_ref[0] + gate_ref[0] * y


def _mlstm_out(hn, xc, up, skip, w, x, gate):
    b, s, c = hn.shape
    d = w.shape[1]
    ts = min(ROW_TILE, s)
    row = lambda i, j: (i, j, 0)
    return pl.pallas_call(
        _mlstm_out_kernel,
        grid=(b, s // ts),
        in_specs=[
            pl.BlockSpec((1, ts, c), row),
            pl.BlockSpec((1, ts, c), row),
            pl.BlockSpec((1, ts, c), lambda i, j: (i, j, 1)),
            pl.BlockSpec((1, c), lambda i, j: (0, 0)),
            pl.BlockSpec((c, d), lambda i, j: (0, 0)),
            pl.BlockSpec((1, ts, d), row),
            pl.BlockSpec((1, 1, d), lambda i, j: (i, 0, 0)),
        ],
        out_specs=pl.BlockSpec((1, ts, d), row),
        out_shape=jax.ShapeDtypeStruct((b, s, d), F32),
        compiler_params=_params("parallel", "parallel"),
    )(hn, xc, up, skip.reshape(1, c), w, x, gate)


def _router_kernel(x_ref, g_ref, sc_ref, sh_ref, w_ref, lpc_ref, lpr_ref, wgt_ref, stats_ref, cnt_ref, run_ref):
    first = (pl.program_id(0) == 0) & (pl.program_id(1) == 0)

    @pl.when(first)
    def _():
        run_ref[...] = jnp.zeros_like(run_ref)

    hf = _modulated_norm(x_ref[0], g_ref[...], sc_ref[0], sh_ref[0])
    w = w_ref[...]
    w_hi = w.astype(BF16)
    w_lo = (w - w_hi.astype(F32)).astype(BF16)
    hf_hi = hf.astype(BF16)
    hf_lo = (hf - hf_hi.astype(F32)).astype(BF16)
    logits = (jnp.dot(hf_hi, w_hi, preferred_element_type=F32)
              + (jnp.dot(hf_hi, w_lo, preferred_element_type=F32)
                 + jnp.dot(hf_lo, w_hi, preferred_element_type=F32)))
    tm = logits.shape[0]
    lane = lax.broadcasted_iota(jnp.int32, logits.shape, 1).astype(F32)
    neg = -jnp.inf
    is_g = lane < N_GROUPS
    gl = jnp.where(is_g, logits, neg)
    gmax = jnp.max(gl, axis=-1, keepdims=True)
    g_sel = jnp.min(jnp.where(gl == gmax, lane, float(LANES)), axis=-1, keepdims=True)
    p_g = 1.0 / jnp.sum(jnp.where(is_g, jnp.exp(gl - gmax), 0.0), axis=-1, keepdims=True)
    e_lane = lane - N_GROUPS
    in_grp = (e_lane >= g_sel * EXPERTS_PER_GROUP) & (e_lane < (g_sel + 1) * EXPERTS_PER_GROUP)
    el = jnp.where(in_grp, logits, neg)
    emax = jnp.max(el, axis=-1, keepdims=True)
    i1 = jnp.min(jnp.where(el == emax, e_lane, float(LANES)), axis=-1, keepdims=True)
    el2 = jnp.where(e_lane == i1, neg, el)
    emax2 = jnp.max(el2, axis=-1, keepdims=True)
    i2 = jnp.min(jnp.where(el2 == emax2, e_lane, float(LANES)), axis=-1, keepdims=True)
    t2 = jnp.exp(emax2 - emax)
    w1 = p_g / (1.0 + t2)
    w2 = p_g * t2 / (1.0 + t2)
    a = jnp.where(e_lane == i1, 1.0, jnp.where(e_lane == i2, 1.0, 0.0))
    r = lax.broadcasted_iota(jnp.int32, (tm, tm), 0)
    cc = lax.broadcasted_iota(jnp.int32, (tm, tm), 1)
    tri = jnp.where(cc < r, 1.0, 0.0).astype(BF16)
    rank = jnp.dot(tri, a.astype(BF16), preferred_element_type=F32)
    k8 = jnp.ceil(jnp.sum(a, axis=0, keepdims=True) * (1.0 / ROW_GROUP))
    ur = lax.broadcasted_iota(jnp.int32, (LANES, LANES), 0)
    uc = lax.broadcasted_iota(jnp.int32, (LANES, LANES), 1)
    upper = jnp.where(ur < uc, 1.0, 0.0).astype(BF16)
    k8_rows = jnp.broadcast_to(k8, (SUBLANES, LANES)).astype(BF16)
    off = jnp.dot(k8_rows, upper, preferred_element_type=F32)[0:1] * ROW_GROUP
    pos = off + rank
    lp0 = jnp.sum(jnp.where(e_lane == i1, pos, 0.0), axis=-1, keepdims=True)
    lp1 = jnp.sum(jnp.where(e_lane == i2, pos, 0.0), axis=-1, keepdims=True)
    run_old = run_ref[...]
    run_new = run_old + k8 * ROW_GROUP
    run_ref[...] = run_new
    cnt_ref[...] = run_new.astype(jnp.int32)
    srow = lax.broadcasted_iota(jnp.int32, (SUBLANES, LANES), 0)
    stats_ref[0] = jnp.where(srow == 0, k8, jnp.where(srow == 1, off, jnp.where(
        srow == 2, run_old, 0.0))).astype(jnp.int32)
    lp = jnp.where(lane == 0, lp0, jnp.where(lane == 1, lp1, 0.0))
    lpc_ref[0] = lp.astype(jnp.int32)
    lpr_ref[0, 0] = lp.T[:SUBLANES].astype(jnp.int32)
    wgt_ref[0] = jnp.where(lane == 0, w1, jnp.where(lane == 1, w2, 0.0))


def _router(x, g, sc, sh, w_group, w_router):
    b, s, d = x.shape
    ts = min(MOE_TOKEN_TILE, s)
    nt = s // ts
    w = jnp.concatenate([w_group, w_router], axis=1)
    w = jnp.pad(w, ((0, 0), (0, LANES - w.shape[1])))
    row = lambda i, j: (i, j, 0)
    return pl.pallas_call(
        _router_kernel,
        grid=(b, nt),
        in_specs=[
            pl.BlockSpec((1, ts, d), row),
            pl.BlockSpec((1, d), lambda i, j: (0, 0)),
            pl.BlockSpec((1, 1, d), lambda i, j: (i, 0, 0)),
            pl.BlockSpec((1, 1, d), lambda i, j: (i, 0, 0)),
            pl.BlockSpec((d, LANES), lambda i, j: (0, 0)),
        ],
        out_specs=[
            pl.BlockSpec((1, ts, LANES), row),
            pl.BlockSpec((1, 1, SUBLANES, ts), lambda i, j: (i, j, 0, 0)),
            pl.BlockSpec((1, ts, LANES), row),
            pl.BlockSpec((1, SUBLANES, LANES), lambda i, j: (i * nt + j, 0, 0)),
            pl.BlockSpec((1, LANES), lambda i, j: (0, 0)),
        ],
        out_shape=[jax.ShapeDtypeStruct((b, s, LANES), jnp.int32),
                   jax.ShapeDtypeStruct((b, nt, SUBLANES, ts), jnp.int32),
                   jax.ShapeDtypeStruct((b, s, LANES), F32),
                   jax.ShapeDtypeStruct((b * nt, SUBLANES, LANES), jnp.int32),
                   jax.ShapeDtypeStruct((1, LANES), jnp.int32)],
        scratch_shapes=[pltpu.VMEM((1, LANES), F32)],
        compiler_params=_params("arbitrary", "arbitrary"),
    )(x, g.reshape(1, d), sc, sh, w)


def _chunk_copies(meta_ref, blk, local_ref, hbm_ref, sem, *, to_hbm, start, max_groups):
    def per_expert(e, carry):
        at = blk * LANES + e
        k = meta_ref[at]
        lo = meta_ref[at + N_EXPERTS]
        hi = meta_ref[at + 2 * N_EXPERTS]
        done = jnp.int32(0)
        for bit in range(max_groups.bit_length()):
            rows = ROW_GROUP << bit
            take = (k >> bit) & 1

            @pl.when(take == 1)
            def _():
                loc = local_ref.at[pl.ds(pl.multiple_of(lo + done, ROW_GROUP), rows)]
                far = hbm_ref.at[pl.ds(pl.multiple_of(hi + done, ROW_GROUP), rows)]
                cp = pltpu.make_async_copy(loc, far, sem) if to_hbm else pltpu.make_async_copy(far, loc, sem)
                if start:
                    cp.start()
                else:
                    cp.wait()

            done = done + take * rows
        return carry

    lax.fori_loop(0, N_EXPERTS, per_expert, 0)


def _dispatch_kernel(last_ref, meta_ref, lpr_ref, x_ref, g_ref, sc_ref, sh_ref, xs_ref, loc_ref, zero_ref,
                     sem, zsem):
    tm = x_ref.shape[1]
    zrows = zero_ref.shape[0]
    blk = pl.program_id(0) * pl.num_programs(1) + pl.program_id(1)
    n_blk = pl.num_programs(0) * pl.num_programs(1)
    slot = blk % 2

    @pl.when((pl.program_id(0) == 0) & (pl.program_id(1) == 0))
    def _():
        zero_ref[...] = jnp.zeros_like(zero_ref)

        def fill(e):
            start = pl.multiple_of(last_ref[e], zrows)
            return pltpu.make_async_copy(zero_ref, xs_ref.at[pl.ds(start, zrows)], zsem)

        for e in range(N_EXPERTS):
            @pl.when(last_ref[e] >= 0)
            def _():
                fill(e).start()
        for e in range(N_EXPERTS):
            @pl.when(last_ref[e] >= 0)
            def _():
                fill(e).wait()

        def tail(k):
            start = pl.multiple_of(k * zrows, zrows)
            return pltpu.make_async_copy(zero_ref, xs_ref.at[pl.ds(start, zrows)], zsem)

        n_used = last_ref[N_EXPERTS]
        n_tiles = xs_ref.shape[0] // zrows
        lax.fori_loop(n_used, n_tiles, lambda k, c: (tail(k).start(), c)[1], 0)
        lax.fori_loop(n_used, n_tiles, lambda k, c: (tail(k).wait(), c)[1], 0)

    hf = _modulated_norm(x_ref[0], g_ref[...], sc_ref[0], sh_ref[0]).astype(BF16)
    rows = lax.broadcasted_iota(jnp.int32, (loc_ref.shape[1], tm), 0)
    sel = jnp.where(rows == lpr_ref[0, 0, 0:1, :], 1.0,
                    jnp.where(rows == lpr_ref[0, 0, 1:2, :], 1.0, 0.0)).astype(BF16)
    loc_ref[slot] = jnp.dot(sel, hf, preferred_element_type=F32).astype(BF16)

    def copies(b, sl, start):
        _chunk_copies(meta_ref, b, loc_ref.at[sl], xs_ref, sem.at[sl], to_hbm=True, start=start,
                      max_groups=tm // ROW_GROUP)

    copies(blk, slot, True)

    @pl.when(blk > 0)
    def _():
        copies(blk - 1, 1 - slot, False)

    @pl.when(blk == n_blk - 1)
    def _():
        copies(blk, slot, False)


def _local_rows(tm):
    return 2 * tm + N_EXPERTS * ROW_GROUP


def _dispatch(x, g, sc, sh, meta, lpr, last_tile_row, n_rows):
    b, s, d = x.shape
    tm = min(MOE_TOKEN_TILE, s)
    nt = s // tm
    row = lambda i, j, last, meta: (i, j, 0)
    grid_spec = pltpu.PrefetchScalarGridSpec(
        num_scalar_prefetch=2,
        grid=(b, nt),
        in_specs=[
            pl.BlockSpec((1, 1, SUBLANES, tm), lambda i, j, last, meta: (i, j, 0, 0)),
            pl.BlockSpec((1, tm, d), row),
            pl.BlockSpec((1, d), lambda i, j, last, meta: (0, 0)),
            pl.BlockSpec((1, 1, d), lambda i, j, last, meta: (i, 0, 0)),
            pl.BlockSpec((1, 1, d), lambda i, j, last, meta: (i, 0, 0)),
        ],
        out_specs=pl.BlockSpec(memory_space=pl.ANY),
        scratch_shapes=[pltpu.VMEM((2, _local_rows(tm), d), BF16), pltpu.VMEM((MOE_ROW_TILE, d), BF16),
                        pltpu.SemaphoreType.DMA((2,)), pltpu.SemaphoreType.DMA(())],
    )
    return pl.pallas_call(
        _dispatch_kernel,
        grid_spec=grid_spec,
        out_shape=jax.ShapeDtypeStruct((n_rows, d), BF16),
        compiler_params=_params("arbitrary", "arbitrary"),
    )(last_tile_row, meta, lpr, x, g.reshape(1, d), sc, sh)


def _expert_kernel(te_ref, nu_ref, xs_ref, w1_ref, w3_ref, w2_ref, ys_ref, w1b_ref, w3b_ref, w2b_ref):
    i = pl.program_id(0)

    @pl.when((i == 0) | (te_ref[i] != te_ref[jnp.maximum(i - 1, 0)]))
    def _():
        w1b_ref[...] = w1_ref[0, 0].astype(BF16)
        w3b_ref[...] = w3_ref[0, 0].astype(BF16)
        w2b_ref[...] = w2_ref[0, 0].astype(BF16)

    @pl.when(i < nu_ref[0])
    def _():
        xb = xs_ref[...]
        a = jnp.dot(xb, w1b_ref[...], preferred_element_type=F32)
        bb = jnp.dot(xb, w3b_ref[...], preferred_element_type=F32)
        hmid = (_silu(a) * bb).astype(BF16)
        ys_ref[...] = jnp.dot(hmid, w2b_ref[...], preferred_element_type=F32).astype(ys_ref.dtype)

    @pl.when(i >= nu_ref[0])
    def _():
        ys_ref[...] = jnp.zeros_like(ys_ref)


def _expert_ffn(xs, tile_expert, n_used, w1, w3, w2, layer):
    p, d = xs.shape
    de = w1.shape[3]
    tm = MOE_ROW_TILE
    grid_spec = pltpu.PrefetchScalarGridSpec(
        num_scalar_prefetch=2,
        grid=(p // tm,),
        in_specs=[
            pl.BlockSpec((tm, d), lambda i, te, nu: (jnp.maximum(jnp.minimum(i, nu[0] - 1), 0), 0)),
            pl.BlockSpec((1, 1, d, de), lambda i, te, nu: (layer, te[i], 0, 0)),
            pl.BlockSpec((1, 1, d, de), lambda i, te, nu: (layer, te[i], 0, 0)),
            pl.BlockSpec((1, 1, de, d), lambda i, te, nu: (layer, te[i], 0, 0)),
        ],
        out_specs=pl.BlockSpec((tm, d), lambda i, te, nu: (i, 0)),
        scratch_shapes=[pltpu.VMEM((d, de), BF16), pltpu.VMEM((d, de), BF16), pltpu.VMEM((de, d), BF16)],
    )
    return pl.pallas_call(
        _expert_kernel,
        grid_spec=grid_spec,
        out_shape=jax.ShapeDtypeStruct((p, d), BF16),
        compiler_params=_params("arbitrary"),
    )(tile_expert, n_used, xs, w1, w3, w2)


def _combine_kernel(meta_ref, ys_ref, lpc_ref, wgt_ref, x_ref, gate_ref, o_ref, loc_ref, sem):
    tm = x_ref.shape[1]
    n_loc = loc_ref.shape[1]
    blk = pl.program_id(0) * pl.num_programs(1) + pl.program_id(1)
    n_blk = pl.num_programs(0) * pl.num_programs(1)
    slot = blk % 2

    def fetch(b, sl):
        loc_ref[sl, 2 * tm:, :] = jnp.zeros((n_loc - 2 * tm, loc_ref.shape[2]), BF16)
        _chunk_copies(meta_ref, b, loc_ref.at[sl], ys_ref, sem.at[sl], to_hbm=False, start=True,
                      max_groups=tm // ROW_GROUP)

    @pl.when(blk == 0)
    def _():
        fetch(blk, slot)

    @pl.when(blk + 1 < n_blk)
    def _():
        fetch(blk + 1, 1 - slot)

    _chunk_copies(meta_ref, blk, loc_ref.at[slot], ys_ref, sem.at[slot], to_hbm=False, start=False,
                  max_groups=tm // ROW_GROUP)
    cols = lax.broadcasted_iota(jnp.int32, (tm, n_loc), 1)
    lp = lpc_ref[0]
    wgt = wgt_ref[0]
    wm = jnp.where(cols == lp[:, 0:1], wgt[:, 0:1], jnp.where(cols == lp[:, 1:2], wgt[:, 1:2], 0.0))
    w_hi = wm.astype(BF16)
    w_lo = (wm - w_hi.astype(F32)).astype(BF16)
    yb = loc_ref[slot]
    y = jnp.dot(w_hi, yb, preferred_element_type=F32) + jnp.dot(w_lo, yb, preferred_element_type=F32)
    o_ref[0] = x_ref[0] + gate_ref[0] * y


def _combine(ys, meta, lpc, wgt, x, gate):
    b, s, d = x.shape
    tm = min(MOE_TOKEN_TILE, s)
    nt = s // tm
    row = lambda i, j, meta: (i, j, 0)
    grid_spec = pltpu.PrefetchScalarGridSpec(
        num_scalar_prefetch=1,
        grid=(b, nt),
        in_specs=[
            pl.BlockSpec(memory_space=pl.ANY),
            pl.BlockSpec((1, tm, LANES), row),
            pl.BlockSpec((1, tm, LANES), row),
            pl.BlockSpec((1, tm, d), row),
            pl.BlockSpec((1, 1, d), lambda i, j, meta: (i, 0, 0)),
        ],
        out_specs=pl.BlockSpec((1, tm, d), row),
        scratch_shapes=[pltpu.VMEM((2, _local_rows(tm), d), BF16), pltpu.SemaphoreType.DMA((2,))],
    )
    return pl.pallas_call(
        _combine_kernel,
        grid_spec=grid_spec,
        out_shape=jax.ShapeDtypeStruct((b, s, d), F32),
        compiler_params=_params("arbitrary", "arbitrary"),
    )(meta, ys, lpc, wgt, x, gate)


def _hier_moe_residual(x, g, sc, sh, gate, w_group, w_router, w1, w3, w2, layer):
    b, s, d = x.shape
    t = b * s
    lpc, lpr, wgt, stats, cnt = _router(x, g, sc, sh, w_group, w_router)
    tm = MOE_ROW_TILE
    ex = slice(N_GROUPS, N_GROUPS + N_EXPERTS)
    counts = cnt[0, ex]
    tiles = (counts + tm - 1) // tm
    tile_end = jnp.cumsum(tiles)
    base = (tile_end - tiles) * tm
    n_blocks = stats.shape[0]
    n_tiles = (2 * t + n_blocks * N_EXPERTS * ROW_GROUP) // tm + N_EXPERTS
    tile_expert = jnp.minimum(
        jnp.sum(tile_end[None, :] <= jnp.arange(n_tiles)[:, None], axis=1), N_EXPERTS - 1).astype(jnp.int32)
    n_used = tile_end[-1:].astype(jnp.int32)
    meta = jnp.concatenate([stats[:, 0, ex], stats[:, 1, ex], base[None, :] + stats[:, 2, ex],
                            jnp.zeros((n_blocks, LANES - 3 * N_EXPERTS), jnp.int32)], axis=1)
    meta = meta.astype(jnp.int32).reshape(n_blocks * LANES)
    last_tile_row = jnp.where(tiles > 0, (tile_end - 1) * tm, -1).astype(jnp.int32)
    last_tile_row = jnp.concatenate([last_tile_row, n_used])
    xs = _dispatch(x, g, sc, sh, meta, lpr, last_tile_row, n_tiles * tm)
    ys = _expert_ffn(xs, tile_expert, n_used, w1, w3, w2, layer)
    return _combine(ys, meta, lpc, wgt, x, gate)


def kernel(x, c, rel_table, ada_w, ada_b, norm_mix_g, norm_ffn_g, da_w_in, da_w_out, da_q_gain, da_k_gain, da_lam_q1, da_lam_k1, da_lam_q2, da_lam_k2, da_subln_g, ml_w_in, ml_conv_w, ml_conv_b, ml_wq, ml_wk, ml_wv, ml_gate_w, ml_gate_b, ml_outnorm_g, ml_skip, ml_w_out, moe_w_group, moe_w_router, moe_w1, moe_w3, moe_w2):
    depth = ada_w.shape[0]
    d = x.shape[-1]
    mod = _ada_mod(c, ada_w, ada_b)
    for i in range(depth):
        sh1, sc1, g1, sh2, sc2, g2 = [mod[i, :, None, k * d:(k + 1) * d] for k in range(6)]
        j = i // N_MIXERS
        if i % N_MIXERS == 0:
            qkv = _prenorm_matmul(x, norm_mix_g[i], sc1, sh1, da_w_in[j].astype(BF16))
            lam_init = 0.8 - 0.6 * math.exp(-0.3 * i)
            lam = (jnp.exp(jnp.sum(da_lam_q1[j] * da_lam_k1[j])) - jnp.exp(jnp.sum(da_lam_q2[j] * da_lam_k2[j]))
                   + lam_init).astype(F32)
            o = _diff_attention(qkv, da_q_gain[j], da_k_gain[j], da_subln_g[j], lam, lam_init, rel_table)
            x = _proj_residual(o, da_w_out[j].astype(BF16), x, g1)
        else:
            up = _prenorm_matmul(x, norm_mix_g[i], sc1, sh1, ml_w_in[j].astype(BF16))
            q, k, v, xc, pre = _mlstm_pre(up, ml_conv_w[j], ml_conv_b[j], ml_wq[j], ml_wk[j], ml_wv[j],
                                          ml_gate_w[j], ml_gate_b[j])
            hn = _mlstm_scan(q, k, v, pre, ml_outnorm_g[j])
            x = _mlstm_out(hn, xc, up, ml_skip[j], ml_w_out[j].astype(BF16), x, g1)
        x = _hier_moe_residual(x, norm_ffn_g[i], sc2, sh2, g2, moe_w_group[i], moe_w_router[i],
                               moe_w1, moe_w3, moe_w2, i)
    return x
```

```python
import functools
import math

import jax
import jax.numpy as jnp
from jax import lax
from jax.experimental import pallas as pl
from jax.experimental.pallas import tpu as pltpu

EPS = 1e-6
N_MIXERS = 2
DA_HEADS = 8
DA_HEAD_DIM = 64
N_BUCKETS = 32
MAX_DISTANCE = 128
ML_HEADS = 4
ML_CHUNK = 256
ML_QKV_BLOCK = 4
N_GROUPS = 4
EXPERTS_PER_GROUP = 8
N_EXPERTS = N_GROUPS * EXPERTS_PER_GROUP

LANES = 128
SUBLANES = 8
ROW_GROUP = 16
VMEM_LIMIT = 48 * 1024 * 1024
ATTN_Q_TILE = 2048
ATTN_ROW_BLOCK = 128
ROW_TILE = 512
MOE_ROW_TILE = 512
MOE_TOKEN_TILE = 512
MLSTM_COL_TILE = 256
MLSTM_ROW_CHUNK = 512

F32 = jnp.float32
BF16 = jnp.bfloat16
LOG2E = math.log2(math.e)


def _params(*sem):
    return pltpu.CompilerParams(dimension_semantics=sem, vmem_limit_bytes=VMEM_LIMIT)


def _silu(x):
    return x / (1.0 + jnp.exp(-x))


def _log_sigmoid(x):
    return jnp.minimum(x, 0.0) - jnp.log(1.0 + jnp.exp(-jnp.abs(x)))


def _modulated_norm(x, g, sc, sh):
    ms = jnp.mean(x * x, axis=-1, keepdims=True)
    return x * lax.rsqrt(ms + EPS) * g * (1.0 + sc) + sh


def _ada_kernel(c_ref, w_ref, b_ref, o_ref):
    ca = _silu(c_ref[...]).astype(BF16)
    o_ref[0] = jnp.dot(ca, w_ref[0].astype(BF16), preferred_element_type=F32) + b_ref[0]


def _ada_mod(c, ada_w, ada_b):
    depth, d, n = ada_w.shape
    b = c.shape[0]
    tn = 1536
    return pl.pallas_call(
        _ada_kernel,
        grid=(depth, n // tn),
        in_specs=[
            pl.BlockSpec((b, d), lambda i, j: (0, 0)),
            pl.BlockSpec((1, d, tn), lambda i, j: (i, 0, j)),
            pl.BlockSpec((1, 1, tn), lambda i, j: (i, 0, j)),
        ],
        out_specs=pl.BlockSpec((1, b, tn), lambda i, j: (i, 0, j)),
        out_shape=jax.ShapeDtypeStruct((depth, b, n), F32),
        compiler_params=_params("parallel", "parallel"),
    )(c, ada_w, ada_b.reshape(depth, 1, n))


def _prenorm_matmul_kernel(x_ref, g_ref, sc_ref, sh_ref, w_ref, hg_ref, o_ref, *, n_chunk, n_norm, half):
    y = _modulated_norm(x_ref[0], g_ref[...], sc_ref[0], sh_ref[0]).astype(BF16)
    n = w_ref.shape[1]
    for n0 in range(0, n, n_chunk):
        acc = jnp.dot(y, w_ref[:, n0:n0 + n_chunk], preferred_element_type=F32)
        if n0 < n_norm:
            for c0 in range(n0, n0 + n_chunk, 2 * half):
                blk = _group_rms(acc[:, c0 - n0:c0 - n0 + 2 * half], hg_ref[:, c0:c0 + 2 * half], half)
                o_ref[0, :, c0:c0 + 2 * half] = blk.astype(o_ref.dtype)
        else:
            o_ref[0, :, n0:n0 + n_chunk] = acc.astype(o_ref.dtype)


def _prenorm_matmul(x, g, sc, sh, w, head_gain=None, half=LANES // 2):
    b, s, d = x.shape
    n = w.shape[1]
    ts = min(ROW_TILE, s)
    n_norm = 0 if head_gain is None else head_gain.shape[1]
    hg = jnp.ones((1, n), F32) if head_gain is None else jnp.pad(head_gain, ((0, 0), (0, n - n_norm)))
    return pl.pallas_call(
        functools.partial(_prenorm_matmul_kernel, n_chunk=1024, n_norm=n_norm, half=half),
        grid=(b, s // ts),
        in_specs=[
            pl.BlockSpec((1, ts, d), lambda i, j: (i, j, 0)),
            pl.BlockSpec((1, d), lambda i, j: (0, 0)),
            pl.BlockSpec((1, 1, d), lambda i, j: (i, 0, 0)),
            pl.BlockSpec((1, 1, d), lambda i, j: (i, 0, 0)),
            pl.BlockSpec((d, n), lambda i, j: (0, 0)),
            pl.BlockSpec((1, n), lambda i, j: (0, 0)),
        ],
        out_specs=pl.BlockSpec((1, ts, n), lambda i, j: (i, j, 0)),
        out_shape=jax.ShapeDtypeStruct((b, s, n), BF16),
        compiler_params=_params("parallel", "parallel"),
    )(x, g.reshape(1, d), sc, sh, w, hg)


def _proj_residual_kernel(a_ref, w_ref, x_ref, gate_ref, o_ref):
    y = jnp.dot(a_ref[0], w_ref[...], preferred_element_type=F32)
    o_ref[0] = x_ref[0] + gate_ref[0] * y


def _proj_residual(a, w, x, gate):
    b, s, k = a.shape
    d = w.shape[1]
    ts = min(ROW_TILE, s)
    return pl.pallas_call(
        _proj_residual_kernel,
        grid=(b, s // ts),
        in_specs=[
            pl.BlockSpec((1, ts, k), lambda i, j: (i, j, 0)),
            pl.BlockSpec((k, d), lambda i, j: (0, 0)),
            pl.BlockSpec((1, ts, d), lambda i, j: (i, j, 0)),
            pl.BlockSpec((1, 1, d), lambda i, j: (i, 0, 0)),
        ],
        out_specs=pl.BlockSpec((1, ts, d), lambda i, j: (i, j, 0)),
        out_shape=jax.ShapeDtypeStruct((b, s, d), F32),
        compiler_params=_params("parallel", "parallel"),
    )(a, w, x, gate)


def _t5_bucket(rel):
    nb = N_BUCKETS // 2
    max_exact = nb // 2
    ret = jnp.where(rel > 0, nb, 0)
    n = jnp.abs(rel)
    nf = jnp.maximum(n, 1).astype(F32)
    large = max_exact + (jnp.log(nf / max_exact) / math.log(MAX_DISTANCE / max_exact)
                         * (nb - max_exact)).astype(jnp.int32)
    large = jnp.minimum(large, nb - 1)
    return ret + jnp.where(n < max_exact, n, large)


def _bias_band_kernel(t_ref, o_ref, *, tq, nb):
    u = jnp.broadcast_to(t_ref[0], (tq, t_ref.shape[2]))
    y = pltpu.roll(u, 1, 1, stride=1, stride_axis=0)
    for jb in range(nb):
        o_ref[0, jb] = y[:, tq + jb * LANES:tq + (jb + 1) * LANES]


def _bias_band(rel_table, s, tq):
    h = rel_table.shape[1]
    delta = jnp.arange(-(s - 1), s)
    t = rel_table[_t5_bucket(delta)].T.astype(F32) * LOG2E
    u = jnp.concatenate([t, jnp.zeros((h, 1), F32)], axis=1).reshape(h, 1, 2 * s)
    nb = (2 * s - tq) // LANES
    return pl.pallas_call(
        functools.partial(_bias_band_kernel, tq=tq, nb=nb),
        grid=(h,),
        in_specs=[pl.BlockSpec((1, 1, 2 * s), lambda i: (i, 0, 0))],
        out_specs=pl.BlockSpec((1, nb, tq, LANES), lambda i: (i, 0, 0, 0)),
        out_shape=jax.ShapeDtypeStruct((h, nb, tq, LANES), F32),
        compiler_params=_params("parallel"),
    )(u)


def _group_rms(x, gain, half):
    sq = x * x
    lane = lax.broadcasted_iota(jnp.int32, x.shape, 1)
    lo = lane < half
    s_lo = jnp.sum(jnp.where(lo, sq, 0.0), axis=-1, keepdims=True)
    s_all = jnp.sum(sq, axis=-1, keepdims=True)
    ms = jnp.where(lo, s_lo, s_all - s_lo) * (1.0 / half)
    return x * lax.rsqrt(ms + EPS) * gain


def _diff_attn_kernel(lam_ref, q_ref, k_ref, v_ref, sg_ref, band_ref, o_ref, ve_ref, *, tq, s, dh, out_scale):
    qi = pl.program_id(2)
    nq = pl.num_programs(2)
    dv = 2 * dh

    @pl.when(qi == 0)
    def _():
        ve_ref[:, :dv] = v_ref[0]
        ve_ref[:, dv:] = jnp.ones((s, dv), BF16)

    lam = lam_ref[0]
    q = q_ref[0]
    lane = lax.broadcasted_iota(jnp.int32, q.shape, 1)
    zero = jnp.zeros_like(q)
    q0 = jnp.where(lane < dh, q, zero)
    q1 = jnp.where(lane < dh, zero, q)
    jb0 = (nq - 1 - qi) * (tq // LANES)
    nkb = s // LANES

    def softmax_av(qm, r0, r1):
        sc = lax.dot_general(qm[r0:r1], k_ref[0], (((1,), (1,)), ((), ())), preferred_element_type=F32)
        sc = sc + jnp.concatenate([band_ref[0, jb0 + kb, r0:r1, :] for kb in range(nkb)], axis=1)
        p = jnp.exp2(sc - jnp.max(sc, axis=-1, keepdims=True)).astype(BF16)
        oe = jnp.dot(p, ve_ref[...], preferred_element_type=F32)
        return oe[:, :dv], oe[:, dv:dv + 1]

    rb = ATTN_ROW_BLOCK
    for r0 in range(0, tq, rb):
        n0, l0 = softmax_av(q0, r0, r0 + rb)
        n1, l1 = softmax_av(q1, r0, r0 + rb)
        o = n0 * (1.0 / l0) - n1 * (lam / l1)
        ms = jnp.mean(o * o, axis=-1, keepdims=True)
        o_ref[0, r0:r0 + rb, :] = (o * lax.rsqrt(ms + EPS) * sg_ref[...] * out_scale).astype(o_ref.dtype)


def _qk_head_gains(q_gain, k_gain):
    h, dh = DA_HEADS, DA_HEAD_DIM
    qg = jnp.tile(q_gain * (dh ** -0.5 * LOG2E), 2 * h)
    kg = jnp.tile(k_gain, 2 * h)
    return jnp.concatenate([qg, kg]).reshape(1, 4 * h * dh).astype(F32)


def _diff_attention(qkv, subln_g, lam, lam_init, rel_table):
    b, s, _ = qkv.shape
    h, dh = DA_HEADS, DA_HEAD_DIM
    tq = min(ATTN_Q_TILE, s)
    band = _bias_band(rel_table, s, tq)
    nb = band.shape[1]
    kern = functools.partial(_diff_attn_kernel, tq=tq, s=s, dh=dh, out_scale=1.0 - lam_init)
    grid_spec = pltpu.PrefetchScalarGridSpec(
        num_scalar_prefetch=1,
        grid=(h, b, s // tq),
        in_specs=[
            pl.BlockSpec((1, tq, 2 * dh), lambda hi, bi, qi, lam: (bi, qi, hi)),
            pl.BlockSpec((1, s, 2 * dh), lambda hi, bi, qi, lam: (bi, 0, h + hi)),
            pl.BlockSpec((1, s, 2 * dh), lambda hi, bi, qi, lam: (bi, 0, 2 * h + hi)),
            pl.BlockSpec((1, 2 * dh), lambda hi, bi, qi, lam: (0, 0)),
            pl.BlockSpec((1, nb, tq, LANES), lambda hi, bi, qi, lam: (hi, 0, 0, 0),
                         pipeline_mode=pl.Buffered(1)),
        ],
        out_specs=pl.BlockSpec((1, tq, 2 * dh), lambda hi, bi, qi, lam: (bi, qi, hi)),
        scratch_shapes=[pltpu.VMEM((s, 4 * dh), BF16)],
    )
    return pl.pallas_call(
        kern,
        grid_spec=grid_spec,
        out_shape=jax.ShapeDtypeStruct((b, s, h * 2 * dh), BF16),
        compiler_params=_params("parallel", "parallel", "arbitrary"),
    )(lam.reshape(1), qkv, qkv, qkv, subln_g.reshape(1, 2 * dh), band)


def _mlstm_pre_kernel(xm_ref, cw_ref, cb_ref, wq_ref, wk_ref, wv_ref, gq_ref, gk_ref, gv_ref, gb_ref,
                      q_ref, k_ref, v_ref, xc_ref, pre_ref, pad_ref, *, s, halo, n_heads):
    j = pl.program_id(1)
    xm = xm_ref[0].astype(F32)
    cbw = xm.shape[1]
    pad_ref[0:8, :] = jnp.zeros((8, cbw), F32)
    pad_ref[8 + s:16 + s, :] = jnp.zeros((8, cbw), F32)
    pad_ref[8:8 + s, :] = xm

    @pl.when(j == 0)
    def _():
        pre_ref[0] = jnp.broadcast_to(gb_ref[...], pre_ref.shape[1:])

    rc = min(MLSTM_ROW_CHUNK, s)
    for r0 in range(0, s, rc):
        rows = slice(r0, r0 + rc)
        acc = cb_ref[...] + cw_ref[0:1, :] * pad_ref[8 - halo + r0:8 - halo + r0 + rc, :]
        for t in range(1, 2 * halo + 1):
            acc = acc + cw_ref[t:t + 1, :] * pad_ref[8 - halo + t + r0:8 - halo + t + r0 + rc, :]
        xcb = _silu(acc).astype(BF16)
        q = jnp.dot(xcb, wq_ref[0], preferred_element_type=F32).astype(BF16)
        k = jnp.dot(xcb, wk_ref[0], preferred_element_type=F32).astype(BF16)
        v = jnp.dot(xm_ref[0, rows, :], wv_ref[0], preferred_element_type=F32).astype(BF16)
        q_ref[0, rows, :] = q
        k_ref[0, rows, :] = k
        v_ref[0, rows, :] = v
        xc_ref[0, rows, :] = xcb
        pre_ref[0, rows, :] += (jnp.dot(q, gq_ref[...], preferred_element_type=F32)
                                + jnp.dot(k, gk_ref[...], preferred_element_type=F32)
                                + jnp.dot(v, gv_ref[...], preferred_element_type=F32))

    @pl.when(j == pl.num_programs(1) - 1)
    def _():
        pre = pre_ref[0]
        lane = lax.broadcasted_iota(jnp.int32, pre.shape, 1)
        is_forget = (lane % (2 * n_heads)) >= n_heads
        pre_ref[0] = jnp.where(is_forget, _log_sigmoid(pre), pre)


def _blockdiag_dense(w, cb):
    nblk, blk, _ = w.shape
    per = cb // blk
    w4 = w.reshape(nblk // per, per, blk, blk)
    eye = jnp.eye(per, dtype=w.dtype)
    return jnp.einsum('jnio,nm->jnimo', w4, eye).reshape(nblk // per, cb, cb).astype(BF16)


def _mlstm_pre(up, conv_w, conv_b, wq, wk, wv, gate_w, gate_b):
    b, s, c2 = up.shape
    c = c2 // 2
    cb = MLSTM_COL_TILE
    ncb = c // cb
    kw = conv_w.shape[0]
    ng = gate_w.shape[0] * gate_w.shape[-1]

    def gate_mat(i):
        g = gate_w[:, i].transpose(1, 0, 2).reshape(c, ng)
        return jnp.pad(g, ((0, 0), (0, LANES - ng))).astype(BF16)

    gb = jnp.pad(gate_b.reshape(1, ng), ((0, 0), (0, LANES - ng)))
    col = lambda i, j: (i, 0, j)
    out_bf = jax.ShapeDtypeStruct((b, s, c), BF16)
    return pl.pallas_call(
        functools.partial(_mlstm_pre_kernel, s=s, halo=kw // 2, n_heads=gate_w.shape[-1] // 2),
        grid=(b, ncb),
        in_specs=[
            pl.BlockSpec((1, s, cb), col),
            pl.BlockSpec((kw, cb), lambda i, j: (0, j)),
            pl.BlockSpec((1, cb), lambda i, j: (0, j)),
            pl.BlockSpec((1, cb, cb), lambda i, j: (j, 0, 0)),
            pl.BlockSpec((1, cb, cb), lambda i, j: (j, 0, 0)),
            pl.BlockSpec((1, cb, cb), lambda i, j: (j, 0, 0)),
            pl.BlockSpec((cb, LANES), lambda i, j: (j, 0)),
            pl.BlockSpec((cb, LANES), lambda i, j: (j, 0)),
            pl.BlockSpec((cb, LANES), lambda i, j: (j, 0)),
            pl.BlockSpec((1, LANES), lambda i, j: (0, 0)),
        ],
        out_specs=[
            pl.BlockSpec((1, s, cb), col),
            pl.BlockSpec((1, s, cb), col),
            pl.BlockSpec((1, s, cb), col),
            pl.BlockSpec((1, s, cb), col),
            pl.BlockSpec((1, s, LANES), lambda i, j: (i, 0, 0)),
        ],
        out_shape=[out_bf, out_bf, out_bf, out_bf, jax.ShapeDtypeStruct((b, s, LANES), F32)],
        scratch_shapes=[pltpu.VMEM((s + 16, cb), F32)],
        compiler_params=_params("parallel", "arbitrary"),
    )(up, conv_w, conv_b.reshape(1, c), _blockdiag_dense(wq, cb), _blockdiag_dense(wk, cb),
      _blockdiag_dense(wv, cb), gate_mat(0), gate_mat(1), gate_mat(2), gb)


def _mlstm_scan_kernel(q_ref, k_ref, v_ref, gcol_ref, grow_ref, og_ref, o_ref,
                       cf_ref, cbk_ref, hf_ref, hb_ref, *, L, nc, dk):
    scale = dk ** -0.5
    row = lax.broadcasted_iota(jnp.int32, (L, L), 0)
    coli = lax.broadcasted_iota(jnp.int32, (L, L), 1)
    lower = coli <= row
    upper = coli >= row
    cf_ref[...] = jnp.zeros_like(cf_ref)
    cbk_ref[...] = jnp.zeros_like(cbk_ref)

    def chunk_step(c, cmat, n, m, fwd):
        mask, mask_t = (lower, upper) if fwd else (upper, lower)
        gi = 0 if fwd else 2
        sl = pl.ds(pl.multiple_of(c * L, L), L)
        qb = q_ref[0, sl, :]
        kb = k_ref[0, sl, :]
        vb = v_ref[0, sl, :]
        gc = gcol_ref[0, 0, c]
        gr = grow_ref[0, 0, c]
        ii_col = gc[:, gi:gi + 1]
        lf_col = gc[:, gi + 1:gi + 2]
        ii_row = gr[gi:gi + 1, :]
        lf_row = gr[gi + 1:gi + 2, :]
        b_col = jnp.sum(jnp.where(mask, lf_row, 0.0), axis=1, keepdims=True)
        b_row = jnp.sum(jnp.where(mask_t, lf_col, 0.0), axis=0, keepdims=True)
        dmat = jnp.where(mask, b_col - b_row + ii_row, -jnp.inf)
        inter = b_col + m
        m_t = jnp.maximum(inter, jnp.max(dmat, axis=1, keepdims=True))
        w_intra = jnp.exp(dmat - m_t)
        w_inter = jnp.exp(inter - m_t)
        sqk = lax.dot_general(qb, kb, (((1,), (1,)), ((), ())), preferred_element_type=F32)
        sw = sqk * scale * w_intra
        num = (w_inter * jnp.dot(qb, cmat.astype(BF16), preferred_element_type=F32)
               + jnp.dot(sw.astype(BF16), vb, preferred_element_type=F32))
        den = (w_inter * jnp.sum(qb.astype(F32) * n, axis=1, keepdims=True)
               + jnp.sum(sw, axis=1, keepdims=True))
        h = num / jnp.maximum(jnp.abs(den), jnp.exp(-m_t))
        b_tot = jnp.sum(lf_row, axis=1, keepdims=True)
        g_col = b_tot - b_col + ii_col
        m_new = jnp.maximum(b_tot + m, jnp.max(g_col, axis=0, keepdims=True))
        decay = jnp.exp(b_tot + m - m_new)
        kw = kb.astype(F32) * scale * jnp.exp(g_col - m_new)
        c_new = decay * cmat + lax.dot_general(
            kw.astype(BF16), vb, (((0,), (0,)), ((), ())), preferred_element_type=F32)
        n_new = decay * n + jnp.sum(kw, axis=0, keepdims=True)
        return sl, h, c_new, n_new, m_new

    def body(i, carry):
        nf, mf, nb_, mb = carry
        sl_f, h_f, c_f, nf, mf = chunk_step(i, cf_ref[...], nf, mf, True)
        sl_b, h_b, c_b, nb_, mb = chunk_step(nc - 1 - i, cbk_ref[...], nb_, mb, False)
        hf_ref[sl_f, :] = h_f
        hb_ref[sl_b, :] = h_b
        cf_ref[...] = c_f
        cbk_ref[...] = c_b
        return nf, mf, nb_, mb

    z_n = jnp.zeros((1, dk), F32)
    z_m = jnp.zeros((1, 1), F32)
    lax.fori_loop(0, nc, body, (z_n, z_m, z_n, z_m))

    def finish(c, carry):
        sl = pl.ds(pl.multiple_of(c * L, L), L)
        hs = hf_ref[sl, :] + hb_ref[sl, :]
        ms = jnp.mean(hs * hs, axis=-1, keepdims=True)
        o_ref[0, sl, :] = (hs * lax.rsqrt(ms + EPS) * og_ref[0]).astype(o_ref.dtype)
        return carry

    lax.fori_loop(0, nc, finish, 0)


def _mlstm_scan(q, k, v, pre, outnorm_g):
    b, s, c = q.shape
    hh = ML_HEADS
    dk = c // hh
    L = min(ML_CHUNK, s)
    nc = s // L
    idx = jnp.array([[0 * 2 * hh + h, 0 * 2 * hh + hh + h, 2 * hh + h, 2 * hh + hh + h] for h in range(hh)])
    g = pre[:, :, idx]
    gcol = g.transpose(0, 2, 1, 3).reshape(b, hh, nc, L, 4)
    grow = gcol.transpose(0, 1, 2, 4, 3)
    head = lambda i, j: (i, 0, j)
    return pl.pallas_call(
        functools.partial(_mlstm_scan_kernel, L=L, nc=nc, dk=dk),
        grid=(b, hh),
        in_specs=[
            pl.BlockSpec((1, s, dk), head),
            pl.BlockSpec((1, s, dk), head),
            pl.BlockSpec((1, s, dk), head),
            pl.BlockSpec((1, 1, nc, L, 4), lambda i, j: (i, j, 0, 0, 0)),
            pl.BlockSpec((1, 1, nc, 4, L), lambda i, j: (i, j, 0, 0, 0)),
            pl.BlockSpec((1, 1, dk), lambda i, j: (j, 0, 0)),
        ],
        out_specs=pl.BlockSpec((1, s, dk), head),
        out_shape=jax.ShapeDtypeStruct((b, s, c), BF16),
        scratch_shapes=[pltpu.VMEM((dk, dk), F32), pltpu.VMEM((dk, dk), F32),
                        pltpu.VMEM((s, dk), F32), pltpu.VMEM((s, dk), F32)],
        compiler_params=_params("parallel", "parallel"),
    )(q, k, v, gcol, grow, outnorm_g.reshape(hh, 1, dk))


def _mlstm_out_kernel(hn_ref, xc_ref, z_ref, skip_ref, w_ref, x_ref, gate_ref, o_ref):
    a = (hn_ref[0].astype(F32) + skip_ref[...] * xc_ref[0].astype(F32)) * _silu(z_ref[0].astype(F32))
    y = jnp.dot(a.astype(BF16), w_ref[...], preferred_element_type=F32)
    o_ref[0] = x_ref[0] + gate_ref[0] * y


def _mlstm_out(hn, xc, up, skip, w, x, gate):
    b, s, c = hn.shape
    d = w.shape[1]
    ts = min(ROW_TILE, s)
    row = lambda i, j: (i, j, 0)
    return pl.pallas_call(
        _mlstm_out_kernel,
        grid=(b, s // ts),
        in_specs=[
            pl.BlockSpec((1, ts, c), row),
            pl.BlockSpec((1, ts, c), row),
            pl.BlockSpec((1, ts, c), lambda i, j: (i, j, 1)),
            pl.BlockSpec((1, c), lambda i, j: (0, 0)),
            pl.BlockSpec((c, d), lambda i, j: (0, 0)),
            pl.BlockSpec((1, ts, d), row),
            pl.BlockSpec((1, 1, d), lambda i, j: (i, 0, 0)),
        ],
        out_specs=pl.BlockSpec((1, ts, d), row),
        out_shape=jax.ShapeDtypeStruct((b, s, d), F32),
        compiler_params=_params("parallel", "parallel"),
    )(hn, xc, up, skip.reshape(1, c), w, x, gate)


def _router_kernel(x_ref, g_ref, sc_ref, sh_ref, w_ref, lpc_ref, lpr_ref, wgt_ref, stats_ref, cnt_ref, run_ref):
    first = (pl.program_id(0) == 0) & (pl.program_id(1) == 0)

    @pl.when(first)
    def _():
        run_ref[...] = jnp.zeros_like(run_ref)

    hf = _modulated_norm(x_ref[0], g_ref[...], sc_ref[0], sh_ref[0])
    w = w_ref[...]
    w_hi = w.astype(BF16)
    w_lo = (w - w_hi.astype(F32)).astype(BF16)
    hf_hi = hf.astype(BF16)
    hf_lo = (hf - hf_hi.astype(F32)).astype(BF16)
    logits = (jnp.dot(hf_hi, w_hi, preferred_element_type=F32)
              + (jnp.dot(hf_hi, w_lo, preferred_element_type=F32)
                 + jnp.dot(hf_lo, w_hi, preferred_element_type=F32)))
    tm = logits.shape[0]
    lane = lax.broadcasted_iota(jnp.int32, logits.shape, 1).astype(F32)
    neg = -jnp.inf
    is_g = lane < N_GROUPS
    gl = jnp.where(is_g, logits, neg)
    gmax = jnp.max(gl, axis=-1, keepdims=True)
    g_sel = jnp.min(jnp.where(gl == gmax, lane, float(LANES)), axis=-1, keepdims=True)
    p_g = 1.0 / jnp.sum(jnp.where(is_g, jnp.exp(gl - gmax), 0.0), axis=-1, keepdims=True)
    e_lane = lane - N_GROUPS
    in_grp = (e_lane >= g_sel * EXPERTS_PER_GROUP) & (e_lane < (g_sel + 1) * EXPERTS_PER_GROUP)
    el = jnp.where(in_grp, logits, neg)
    emax = jnp.max(el, axis=-1, keepdims=True)
    i1 = jnp.min(jnp.where(el == emax, e_lane, float(LANES)), axis=-1, keepdims=True)
    el2 = jnp.where(e_lane == i1, neg, el)
    emax2 = jnp.max(el2, axis=-1, keepdims=True)
    i2 = jnp.min(jnp.where(el2 == emax2, e_lane, float(LANES)), axis=-1, keepdims=True)
    t2 = jnp.exp(emax2 - emax)
    w1 = p_g / (1.0 + t2)
    w2 = p_g * t2 / (1.0 + t2)
    a = jnp.where(e_lane == i1, 1.0, jnp.where(e_lane == i2, 1.0, 0.0))
    r = lax.broadcasted_iota(jnp.int32, (tm, tm), 0)
    cc = lax.broadcasted_iota(jnp.int32, (tm, tm), 1)
    tri = jnp.where(cc < r, 1.0, 0.0).astype(BF16)
    rank = jnp.dot(tri, a.astype(BF16), preferred_element_type=F32)
    k8 = jnp.ceil(jnp.sum(a, axis=0, keepdims=True) * (1.0 / ROW_GROUP))
    ur = lax.broadcasted_iota(jnp.int32, (LANES, LANES), 0)
    uc = lax.broadcasted_iota(jnp.int32, (LANES, LANES), 1)
    upper = jnp.where(ur < uc, 1.0, 0.0).astype(BF16)
    k8_rows = jnp.broadcast_to(k8, (SUBLANES, LANES)).astype(BF16)
    off = jnp.dot(k8_rows, upper, preferred_element_type=F32)[0:1] * ROW_GROUP
    pos = off + rank
    lp0 = jnp.sum(jnp.where(e_lane == i1, pos, 0.0), axis=-1, keepdims=True)
    lp1 = jnp.sum(jnp.where(e_lane == i2, pos, 0.0), axis=-1, keepdims=True)
    run_old = run_ref[...]
    run_new = run_old + k8 * ROW_GROUP
    run_ref[...] = run_new
    cnt_ref[...] = run_new.astype(jnp.int32)
    srow = lax.broadcasted_iota(jnp.int32, (SUBLANES, LANES), 0)
    stats_ref[0] = jnp.where(srow == 0, k8, jnp.where(srow == 1, off, jnp.where(
        srow == 2, run_old, 0.0))).astype(jnp.int32)
    lp = jnp.where(lane == 0, lp0, jnp.where(lane == 1, lp1, 0.0))
    lpc_ref[0] = lp.astype(jnp.int32)
    lpr_ref[0, 0] = lp.T[:SUBLANES].astype(jnp.int32)
    wgt_ref[0] = jnp.where(lane == 0, w1, jnp.where(lane == 1, w2, 0.0))


def _router(x, g, sc, sh, w_group, w_router):
    b, s, d = x.shape
    ts = min(MOE_TOKEN_TILE, s)
    nt = s // ts
    w = jnp.concatenate([w_group, w_router], axis=1)
    w = jnp.pad(w, ((0, 0), (0, LANES - w.shape[1])))
    row = lambda i, j: (i, j, 0)
    return pl.pallas_call(
        _router_kernel,
        grid=(b, nt),
        in_specs=[
            pl.BlockSpec((1, ts, d), row),
            pl.BlockSpec((1, d), lambda i, j: (0, 0)),
            pl.BlockSpec((1, 1, d), lambda i, j: (i, 0, 0)),
            pl.BlockSpec((1, 1, d), lambda i, j: (i, 0, 0)),
            pl.BlockSpec((d, LANES), lambda i, j: (0, 0)),
        ],
        out_specs=[
            pl.BlockSpec((1, ts, LANES), row),
            pl.BlockSpec((1, 1, SUBLANES, ts), lambda i, j: (i, j, 0, 0)),
            pl.BlockSpec((1, ts, LANES), row),
            pl.BlockSpec((1, SUBLANES, LANES), lambda i, j: (i * nt + j, 0, 0)),
            pl.BlockSpec((1, LANES), lambda i, j: (0, 0)),
        ],
        out_shape=[jax.ShapeDtypeStruct((b, s, LANES), jnp.int32),
                   jax.ShapeDtypeStruct((b, nt, SUBLANES, ts), jnp.int32),
                   jax.ShapeDtypeStruct((b, s, LANES), F32),
                   jax.ShapeDtypeStruct((b * nt, SUBLANES, LANES), jnp.int32),
                   jax.ShapeDtypeStruct((1, LANES), jnp.int32)],
        scratch_shapes=[pltpu.VMEM((1, LANES), F32)],
        compiler_params=_params("arbitrary", "arbitrary"),
    )(x, g.reshape(1, d), sc, sh, w)


def _chunk_copies(meta_ref, blk, local_ref, hbm_ref, sem, *, to_hbm, start, max_groups):
    def per_expert(e, carry):
        at = blk * LANES + e
        k = meta_ref[at]
        lo = meta_ref[at + N_EXPERTS]
        hi = meta_ref[at + 2 * N_EXPERTS]
        done = jnp.int32(0)
        for bit in range(max_groups.bit_length()):
            rows = ROW_GROUP << bit
            take = (k >> bit) & 1

            @pl.when(take == 1)
            def _():
                loc = local_ref.at[pl.ds(pl.multiple_of(lo + done, ROW_GROUP), rows)]
                far = hbm_ref.at[pl.ds(pl.multiple_of(hi + done, ROW_GROUP), rows)]
                cp = pltpu.make_async_copy(loc, far, sem) if to_hbm else pltpu.make_async_copy(far, loc, sem)
                if start:
                    cp.start()
                else:
                    cp.wait()

            done = done + take * rows
        return carry

    lax.fori_loop(0, N_EXPERTS, per_expert, 0)


def _dispatch_kernel(last_ref, meta_ref, lpr_ref, x_ref, g_ref, sc_ref, sh_ref, xs_ref, loc_ref, zero_ref,
                     sem, zsem):
    tm = x_ref.shape[1]
    zrows = zero_ref.shape[0]
    blk = pl.program_id(0) * pl.num_programs(1) + pl.program_id(1)
    n_blk = pl.num_programs(0) * pl.num_programs(1)
    slot = blk % 2

    @pl.when((pl.program_id(0) == 0) & (pl.program_id(1) == 0))
    def _():
        zero_ref[...] = jnp.zeros_like(zero_ref)

        def fill(e):
            start = pl.multiple_of(last_ref[e], zrows)
            return pltpu.make_async_copy(zero_ref, xs_ref.at[pl.ds(start, zrows)], zsem)

        for e in range(N_EXPERTS):
            @pl.when(last_ref[e] >= 0)
            def _():
                fill(e).start()
        for e in range(N_EXPERTS):
            @pl.when(last_ref[e] >= 0)
            def _():
                fill(e).wait()

        def tail(k):
            start = pl.multiple_of(k * zrows, zrows)
            return pltpu.make_async_copy(zero_ref, xs_ref.at[pl.ds(start, zrows)], zsem)

        n_used = last_ref[N_EXPERTS]
        n_tiles = xs_ref.shape[0] // zrows
        lax.fori_loop(n_used, n_tiles, lambda k, c: (tail(k).start(), c)[1], 0)
        lax.fori_loop(n_used, n_tiles, lambda k, c: (tail(k).wait(), c)[1], 0)

    hf = _modulated_norm(x_ref[0], g_ref[...], sc_ref[0], sh_ref[0]).astype(BF16)
    rows = lax.broadcasted_iota(jnp.int32, (loc_ref.shape[1], tm), 0)
    sel = jnp.where(rows == lpr_ref[0, 0, 0:1, :], 1.0,
                    jnp.where(rows == lpr_ref[0, 0, 1:2, :], 1.0, 0.0)).astype(BF16)
    loc_ref[slot] = jnp.dot(sel, hf, preferred_element_type=F32).astype(BF16)

    def copies(b, sl, start):
        _chunk_copies(meta_ref, b, loc_ref.at[sl], xs_ref, sem.at[sl], to_hbm=True, start=start,
                      max_groups=tm // ROW_GROUP)

    copies(blk, slot, True)

    @pl.when(blk > 0)
    def _():
        copies(blk - 1, 1 - slot, False)

    @pl.when(blk == n_blk - 1)
    def _():
        copies(blk, slot, False)


def _local_rows(tm):
    return 2 * tm + N_EXPERTS * ROW_GROUP


def _dispatch(x, g, sc, sh, meta, lpr, last_tile_row, n_rows):
    b, s, d = x.shape
    tm = min(MOE_TOKEN_TILE, s)
    nt = s // tm
    row = lambda i, j, last, meta: (i, j, 0)
    grid_spec = pltpu.PrefetchScalarGridSpec(
        num_scalar_prefetch=2,
        grid=(b, nt),
        in_specs=[
            pl.BlockSpec((1, 1, SUBLANES, tm), lambda i, j, last, meta: (i, j, 0, 0)),
            pl.BlockSpec((1, tm, d), row),
            pl.BlockSpec((1, d), lambda i, j, last, meta: (0, 0)),
            pl.BlockSpec((1, 1, d), lambda i, j, last, meta: (i, 0, 0)),
            pl.BlockSpec((1, 1, d), lambda i, j, last, meta: (i, 0, 0)),
        ],
        out_specs=pl.BlockSpec(memory_space=pl.ANY),
        scratch_shapes=[pltpu.VMEM((2, _local_rows(tm), d), BF16), pltpu.VMEM((MOE_ROW_TILE, d), BF16),
                        pltpu.SemaphoreType.DMA((2,)), pltpu.SemaphoreType.DMA(())],
    )
    return pl.pallas_call(
        _dispatch_kernel,
        grid_spec=grid_spec,
        out_shape=jax.ShapeDtypeStruct((n_rows, d), BF16),
        compiler_params=_params("arbitrary", "arbitrary"),
    )(last_tile_row, meta, lpr, x, g.reshape(1, d), sc, sh)


def _expert_kernel(te_ref, nu_ref, xs_ref, w1_ref, w3_ref, w2_ref, ys_ref, w1b_ref, w3b_ref, w2b_ref):
    i = pl.program_id(0)

    @pl.when((i == 0) | (te_ref[i] != te_ref[jnp.maximum(i - 1, 0)]))
    def _():
        w1b_ref[...] = w1_ref[0, 0].astype(BF16)
        w3b_ref[...] = w3_ref[0, 0].astype(BF16)
        w2b_ref[...] = w2_ref[0, 0].astype(BF16)

    @pl.when(i < nu_ref[0])
    def _():
        xb = xs_ref[...]
        a = jnp.dot(xb, w1b_ref[...], preferred_element_type=F32)
        bb = jnp.dot(xb, w3b_ref[...], preferred_element_type=F32)
        hmid = (_silu(a) * bb).astype(BF16)
        ys_ref[...] = jnp.dot(hmid, w2b_ref[...], preferred_element_type=F32).astype(ys_ref.dtype)

    @pl.when(i >= nu_ref[0])
    def _():
        ys_ref[...] = jnp.zeros_like(ys_ref)


def _expert_ffn(xs, tile_expert, n_used, w1, w3, w2, layer):
    p, d = xs.shape
    de = w1.shape[3]
    tm = MOE_ROW_TILE
    grid_spec = pltpu.PrefetchScalarGridSpec(
        num_scalar_prefetch=2,
        grid=(p // tm,),
        in_specs=[
            pl.BlockSpec((tm, d), lambda i, te, nu: (jnp.maximum(jnp.minimum(i, nu[0] - 1), 0), 0)),
            pl.BlockSpec((1, 1, d, de), lambda i, te, nu: (layer, te[i], 0, 0)),
            pl.BlockSpec((1, 1, d, de), lambda i, te, nu: (layer, te[i], 0, 0)),
            pl.BlockSpec((1, 1, de, d), lambda i, te, nu: (layer, te[i], 0, 0)),
        ],
        out_specs=pl.BlockSpec((tm, d), lambda i, te, nu: (i, 0)),
        scratch_shapes=[pltpu.VMEM((d, de), BF16), pltpu.VMEM((d, de), BF16), pltpu.VMEM((de, d), BF16)],
    )
    return pl.pallas_call(
        _expert_kernel,
        grid_spec=grid_spec,
        out_shape=jax.ShapeDtypeStruct((p, d), BF16),
        compiler_params=_params("arbitrary"),
    )(tile_expert, n_used, xs, w1, w3, w2)


def _combine_kernel(meta_ref, ys_ref, lpc_ref, wgt_ref, x_ref, gate_ref, o_ref, loc_ref, sem):
    tm = x_ref.shape[1]
    n_loc = loc_ref.shape[1]
    blk = pl.program_id(0) * pl.num_programs(1) + pl.program_id(1)
    n_blk = pl.num_programs(0) * pl.num_programs(1)
    slot = blk % 2

    def fetch(b, sl):
        loc_ref[sl, 2 * tm:, :] = jnp.zeros((n_loc - 2 * tm, loc_ref.shape[2]), BF16)
        _chunk_copies(meta_ref, b, loc_ref.at[sl], ys_ref, sem.at[sl], to_hbm=False, start=True,
                      max_groups=tm // ROW_GROUP)

    @pl.when(blk == 0)
    def _():
        fetch(blk, slot)

    @pl.when(blk + 1 < n_blk)
    def _():
        fetch(blk + 1, 1 - slot)

    _chunk_copies(meta_ref, blk, loc_ref.at[slot], ys_ref, sem.at[slot], to_hbm=False, start=False,
                  max_groups=tm // ROW_GROUP)
    cols = lax.broadcasted_iota(jnp.int32, (tm, n_loc), 1)
    lp = lpc_ref[0]
    wgt = wgt_ref[0]
    wm = jnp.where(cols == lp[:, 0:1], wgt[:, 0:1], jnp.where(cols == lp[:, 1:2], wgt[:, 1:2], 0.0))
    w_hi = wm.astype(BF16)
    w_lo = (wm - w_hi.astype(F32)).astype(BF16)
    yb = loc_ref[slot]
    y = jnp.dot(w_hi, yb, preferred_element_type=F32) + jnp.dot(w_lo, yb, preferred_element_type=F32)
    o_ref[0] = x_ref[0] + gate_ref[0] * y


def _combine(ys, meta, lpc, wgt, x, gate):
    b, s, d = x.shape
    tm = min(MOE_TOKEN_TILE, s)
    nt = s // tm
    row = lambda i, j, meta: (i, j, 0)
    grid_spec = pltpu.PrefetchScalarGridSpec(
        num_scalar_prefetch=1,
        grid=(b, nt),
        in_specs=[
            pl.BlockSpec(memory_space=pl.ANY),
            pl.BlockSpec((1, tm, LANES), row),
            pl.BlockSpec((1, tm, LANES), row),
            pl.BlockSpec((1, tm, d), row),
            pl.BlockSpec((1, 1, d), lambda i, j, meta: (i, 0, 0)),
        ],
        out_specs=pl.BlockSpec((1, tm, d), row),
        scratch_shapes=[pltpu.VMEM((2, _local_rows(tm), d), BF16), pltpu.SemaphoreType.DMA((2,))],
    )
    return pl.pallas_call(
        _combine_kernel,
        grid_spec=grid_spec,
        out_shape=jax.ShapeDtypeStruct((b, s, d), F32),
        compiler_params=_params("arbitrary", "arbitrary"),
    )(meta, ys, lpc, wgt, x, gate)


def _hier_moe_residual(x, g, sc, sh, gate, w_group, w_router, w1, w3, w2, layer):
    b, s, d = x.shape
    t = b * s
    lpc, lpr, wgt, stats, cnt = _router(x, g, sc, sh, w_group, w_router)
    tm = MOE_ROW_TILE
    ex = slice(N_GROUPS, N_GROUPS + N_EXPERTS)
    counts = cnt[0, ex]
    tiles = (counts + tm - 1) // tm
    tile_end = jnp.cumsum(tiles)
    base = (tile_end - tiles) * tm
    n_blocks = stats.shape[0]
    n_tiles = (2 * t + n_blocks * N_EXPERTS * ROW_GROUP) // tm + N_EXPERTS
    tile_expert = jnp.minimum(
        jnp.sum(tile_end[None, :] <= jnp.arange(n_tiles)[:, None], axis=1), N_EXPERTS - 1).astype(jnp.int32)
    n_used = tile_end[-1:].astype(jnp.int32)
    meta = jnp.concatenate([stats[:, 0, ex], stats[:, 1, ex], base[None, :] + stats[:, 2, ex],
                            jnp.zeros((n_blocks, LANES - 3 * N_EXPERTS), jnp.int32)], axis=1)
    meta = meta.astype(jnp.int32).reshape(n_blocks * LANES)
    last_tile_row = jnp.where(tiles > 0, (tile_end - 1) * tm, -1).astype(jnp.int32)
    last_tile_row = jnp.concatenate([last_tile_row, n_used])
    xs = _dispatch(x, g, sc, sh, meta, lpr, last_tile_row, n_tiles * tm)
    ys = _expert_ffn(xs, tile_expert, n_used, w1, w3, w2, layer)
    return _combine(ys, meta, lpc, wgt, x, gate)


def kernel(x, c, rel_table, ada_w, ada_b, norm_mix_g, norm_ffn_g, da_w_in, da_w_out, da_q_gain, da_k_gain, da_lam_q1, da_lam_k1, da_lam_q2, da_lam_k2, da_subln_g, ml_w_in, ml_conv_w, ml_conv_b, ml_wq, ml_wk, ml_wv, ml_gate_w, ml_gate_b, ml_outnorm_g, ml_skip, ml_w_out, moe_w_group, moe_w_router, moe_w1, moe_w3, moe_w2):
    depth = ada_w.shape[0]
    d = x.shape[-1]
    mod = _ada_mod(c, ada_w, ada_b)
    for i in range(depth):
        sh1, sc1, g1, sh2, sc2, g2 = [mod[i, :, None, k * d:(k + 1) * d] for k in range(6)]
        j = i // N_MIXERS
        if i % N_MIXERS == 0:
            qkv = _prenorm_matmul(x, norm_mix_g[i], sc1, sh1, da_w_in[j].astype(BF16),
                                  head_gain=_qk_head_gains(da_q_gain[j], da_k_gain[j]), half=DA_HEAD_DIM)
            lam_init = 0.8 - 0.6 * math.exp(-0.3 * i)
            lam = (jnp.exp(jnp.sum(da_lam_q1[j] * da_lam_k1[j])) - jnp.exp(jnp.sum(da_lam_q2[j] * da_lam_k2[j]))
                   + lam_init).astype(F32)
            o = _diff_attention(qkv, da_subln_g[j], lam, lam_init, rel_table)
            x = _proj_residual(o, da_w_out[j].astype(BF16), x, g1)
        else:
            up = _prenorm_matmul(x, norm_mix_g[i], sc1, sh1, ml_w_in[j].astype(BF16))
            q, k, v, xc, pre = _mlstm_pre(up, ml_conv_w[j], ml_conv_b[j], ml_wq[j], ml_wk[j], ml_wv[j],
                                          ml_gate_w[j], ml_gate_b[j])
            hn = _mlstm_scan(q, k, v, pre, ml_outnorm_g[j])
            x = _mlstm_out(hn, xc, up, ml_skip[j], ml_w_out[j].astype(BF16), x, g1)
        x = _hier_moe_residual(x, norm_ffn_g[i], sc2, sh2, g2, moe_w_group[i], moe_w_router[i],
                               moe_w1, moe_w3, moe_w2, i)
    return x
```

```python
import functools
import math

import jax
import jax.numpy as jnp
from jax import lax
from jax.experimental import pallas as pl
from jax.experimental.pallas import tpu as pltpu

EPS = 1e-6
N_MIXERS = 2
DA_HEADS = 8
DA_HEAD_DIM = 64
N_BUCKETS = 32
MAX_DISTANCE = 128
ML_HEADS = 4
ML_CHUNK = 256
ML_QKV_BLOCK = 4
N_GROUPS = 4
EXPERTS_PER_GROUP = 8
N_EXPERTS = N_GROUPS * EXPERTS_PER_GROUP

LANES = 128
SUBLANES = 8
ROW_GROUP = 16
VMEM_LIMIT = 48 * 1024 * 1024
ATTN_Q_TILE = 2048
ATTN_ROW_BLOCK = 128
ROW_TILE = 512
MOE_ROW_TILE = 512
MOE_TOKEN_TILE = 512
MLSTM_COL_TILE = 256
MLSTM_ROW_CHUNK = 512

F32 = jnp.float32
BF16 = jnp.bfloat16
LOG2E = math.log2(math.e)


def _params(*sem):
    return pltpu.CompilerParams(dimension_semantics=sem, vmem_limit_bytes=VMEM_LIMIT)


def _silu(x):
    return x / (1.0 + jnp.exp(-x))


def _log_sigmoid(x):
    return jnp.minimum(x, 0.0) - jnp.log(1.0 + jnp.exp(-jnp.abs(x)))


def _modulated_norm(x, g, sc, sh):
    ms = jnp.mean(x * x, axis=-1, keepdims=True)
    return x * lax.rsqrt(ms + EPS) * g * (1.0 + sc) + sh


def _ada_kernel(c_ref, w_ref, b_ref, o_ref):
    ca = _silu(c_ref[...]).astype(BF16)
    o_ref[0] = jnp.dot(ca, w_ref[0].astype(BF16), preferred_element_type=F32) + b_ref[0]


def _ada_mod(c, ada_w, ada_b):
    depth, d, n = ada_w.shape
    b = c.shape[0]
    tn = 1536
    return pl.pallas_call(
        _ada_kernel,
        grid=(depth, n // tn),
        in_specs=[
            pl.BlockSpec((b, d), lambda i, j: (0, 0)),
            pl.BlockSpec((1, d, tn), lambda i, j: (i, 0, j)),
            pl.BlockSpec((1, 1, tn), lambda i, j: (i, 0, j)),
        ],
        out_specs=pl.BlockSpec((1, b, tn), lambda i, j: (i, 0, j)),
        out_shape=jax.ShapeDtypeStruct((depth, b, n), F32),
        compiler_params=_params("parallel", "parallel"),
    )(c, ada_w, ada_b.reshape(depth, 1, n))


def _prenorm_matmul_kernel(x_ref, g_ref, sc_ref, sh_ref, w_ref, hg_ref, o_ref, *, n_chunk, n_norm, half):
    y = _modulated_norm(x_ref[0], g_ref[...], sc_ref[0], sh_ref[0]).astype(BF16)
    n = w_ref.shape[1]
    for n0 in range(0, n, n_chunk):
        acc = jnp.dot(y, w_ref[:, n0:n0 + n_chunk], preferred_element_type=F32)
        if n0 < n_norm:
            for c0 in range(n0, n0 + n_chunk, 2 * half):
                blk = _group_rms(acc[:, c0 - n0:c0 - n0 + 2 * half], hg_ref[:, c0:c0 + 2 * half], half)
                o_ref[0, :, c0:c0 + 2 * half] = blk.astype(o_ref.dtype)
        else:
            o_ref[0, :, n0:n0 + n_chunk] = acc.astype(o_ref.dtype)


def _prenorm_matmul(x, g, sc, sh, w, head_gain=None, half=LANES // 2):
    b, s, d = x.shape
    n = w.shape[1]
    ts = min(ROW_TILE, s)
    n_norm = 0 if head_gain is None else head_gain.shape[1]
    hg = jnp.ones((1, n), F32) if head_gain is None else jnp.pad(head_gain, ((0, 0), (0, n - n_norm)))
    return pl.pallas_call(
        functools.partial(_prenorm_matmul_kernel, n_chunk=1024, n_norm=n_norm, half=half),
        grid=(b, s // ts),
        in_specs=[
            pl.BlockSpec((1, ts, d), lambda i, j: (i, j, 0)),
            pl.BlockSpec((1, d), lambda i, j: (0, 0)),
            pl.BlockSpec((1, 1, d), lambda i, j: (i, 0, 0)),
            pl.BlockSpec((1, 1, d), lambda i, j: (i, 0, 0)),
            pl.BlockSpec((d, n), lambda i, j: (0, 0)),
            pl.BlockSpec((1, n), lambda i, j: (0, 0)),
        ],
        out_specs=pl.BlockSpec((1, ts, n), lambda i, j: (i, j, 0)),
        out_shape=jax.ShapeDtypeStruct((b, s, n), BF16),
        compiler_params=_params("parallel", "parallel"),
    )(x, g.reshape(1, d), sc, sh, w, hg)


def _proj_residual_kernel(a_ref, w_ref, x_ref, gate_ref, o_ref):
    y = jnp.dot(a_ref[0], w_ref[...], preferred_element_type=F32)
    o_ref[0] = x_ref[0] + gate_ref[0] * y


def _proj_residual(a, w, x, gate):
    b, s, k = a.shape
    d = w.shape[1]
    ts = min(ROW_TILE, s)
    return pl.pallas_call(
        _proj_residual_kernel,
        grid=(b, s // ts),
        in_specs=[
            pl.BlockSpec((1, ts, k), lambda i, j: (i, j, 0)),
            pl.BlockSpec((k, d), lambda i, j: (0, 0)),
            pl.BlockSpec((1, ts, d), lambda i, j: (i, j, 0)),
            pl.BlockSpec((1, 1, d), lambda i, j: (i, 0, 0)),
        ],
        out_specs=pl.BlockSpec((1, ts, d), lambda i, j: (i, j, 0)),
        out_shape=jax.ShapeDtypeStruct((b, s, d), F32),
        compiler_params=_params("parallel", "parallel"),
    )(a, w, x, gate)


def _t5_bucket(rel):
    nb = N_BUCKETS // 2
    max_exact = nb // 2
    ret = jnp.where(rel > 0, nb, 0)
    n = jnp.abs(rel)
    nf = jnp.maximum(n, 1).astype(F32)
    large = max_exact + (jnp.log(nf / max_exact) / math.log(MAX_DISTANCE / max_exact)
                         * (nb - max_exact)).astype(jnp.int32)
    large = jnp.minimum(large, nb - 1)
    return ret + jnp.where(n < max_exact, n, large)


def _bias_band_kernel(t_ref, o_ref, *, tq, nb):
    u = jnp.broadcast_to(t_ref[0], (tq, t_ref.shape[2]))
    y = pltpu.roll(u, 1, 1, stride=1, stride_axis=0)
    for jb in range(nb):
        o_ref[0, jb] = y[:, tq + jb * LANES:tq + (jb + 1) * LANES]


def _bias_band(rel_table, s, tq):
    h = rel_table.shape[1]
    delta = jnp.arange(-(s - 1), s)
    t = rel_table[_t5_bucket(delta)].T.astype(F32) * LOG2E
    u = jnp.concatenate([t, jnp.zeros((h, 1), F32)], axis=1).reshape(h, 1, 2 * s)
    nb = (2 * s - tq) // LANES
    return pl.pallas_call(
        functools.partial(_bias_band_kernel, tq=tq, nb=nb),
        grid=(h,),
        in_specs=[pl.BlockSpec((1, 1, 2 * s), lambda i: (i, 0, 0))],
        out_specs=pl.BlockSpec((1, nb, tq, LANES), lambda i: (i, 0, 0, 0)),
        out_shape=jax.ShapeDtypeStruct((h, nb, tq, LANES), F32),
        compiler_params=_params("parallel"),
    )(u)


def _group_rms(x, gain, half):
    sq = x * x
    lane = lax.broadcasted_iota(jnp.int32, x.shape, 1)
    lo = lane < half
    s_lo = jnp.sum(jnp.where(lo, sq, 0.0), axis=-1, keepdims=True)
    s_all = jnp.sum(sq, axis=-1, keepdims=True)
    ms = jnp.where(lo, s_lo, s_all - s_lo) * (1.0 / half)
    return x * lax.rsqrt(ms + EPS) * gain


def _diff_attn_kernel(lam_ref, q_ref, k_ref, v_ref, sg_ref, band_ref, o_ref, ve_ref, *, tq, s, dh, out_scale):
    qi = pl.program_id(2)
    nq = pl.num_programs(2)
    dv = 2 * dh

    @pl.when(qi == 0)
    def _():
        ve_ref[:, :dv] = v_ref[0]
        ve_ref[:, dv:] = jnp.ones((s, dv), BF16)

    lam = lam_ref[0]
    q = q_ref[0]
    lane = lax.broadcasted_iota(jnp.int32, q.shape, 1)
    zero = jnp.zeros_like(q)
    q0 = jnp.where(lane < dh, q, zero)
    q1 = jnp.where(lane < dh, zero, q)
    jb0 = (nq - 1 - qi) * (tq // LANES)
    nkb = s // LANES

    def softmax_av(qm, r0, r1):
        sc = lax.dot_general(qm[r0:r1], k_ref[0], (((1,), (1,)), ((), ())), preferred_element_type=F32)
        sc = sc + jnp.concatenate([band_ref[0, jb0 + kb, r0:r1, :] for kb in range(nkb)], axis=1)
        p = jnp.exp2(sc - jnp.max(sc, axis=-1, keepdims=True)).astype(BF16)
        oe = jnp.dot(p, ve_ref[...], preferred_element_type=F32)
        return oe[:, :dv], oe[:, dv:dv + 1]

    rb = ATTN_ROW_BLOCK
    for r0 in range(0, tq, rb):
        n0, l0 = softmax_av(q0, r0, r0 + rb)
        n1, l1 = softmax_av(q1, r0, r0 + rb)
        o = n0 * (1.0 / l0) - n1 * (lam / l1)
        ms = jnp.mean(o * o, axis=-1, keepdims=True)
        o_ref[0, r0:r0 + rb, :] = (o * lax.rsqrt(ms + EPS) * sg_ref[...] * out_scale).astype(o_ref.dtype)


def _qk_head_gains(q_gain, k_gain):
    h, dh = DA_HEADS, DA_HEAD_DIM
    qg = jnp.tile(q_gain * (dh ** -0.5 * LOG2E), 2 * h)
    kg = jnp.tile(k_gain, 2 * h)
    return jnp.concatenate([qg, kg]).reshape(1, 4 * h * dh).astype(F32)


def _diff_attention(qkv, subln_g, lam, lam_init, rel_table):
    b, s, _ = qkv.shape
    h, dh = DA_HEADS, DA_HEAD_DIM
    tq = min(ATTN_Q_TILE, s)
    band = _bias_band(rel_table, s, tq)
    nb = band.shape[1]
    kern = functools.partial(_diff_attn_kernel, tq=tq, s=s, dh=dh, out_scale=1.0 - lam_init)
    grid_spec = pltpu.PrefetchScalarGridSpec(
        num_scalar_prefetch=1,
        grid=(h, b, s // tq),
        in_specs=[
            pl.BlockSpec((1, tq, 2 * dh), lambda hi, bi, qi, lam: (bi, qi, hi)),
            pl.BlockSpec((1, s, 2 * dh), lambda hi, bi, qi, lam: (bi, 0, h + hi)),
            pl.BlockSpec((1, s, 2 * dh), lambda hi, bi, qi, lam: (bi, 0, 2 * h + hi)),
            pl.BlockSpec((1, 2 * dh), lambda hi, bi, qi, lam: (0, 0)),
            pl.BlockSpec((1, nb, tq, LANES), lambda hi, bi, qi, lam: (hi, 0, 0, 0),
                         pipeline_mode=pl.Buffered(1)),
        ],
        out_specs=pl.BlockSpec((1, tq, 2 * dh), lambda hi, bi, qi, lam: (bi, qi, hi)),
        scratch_shapes=[pltpu.VMEM((s, 4 * dh), BF16)],
    )
    return pl.pallas_call(
        kern,
        grid_spec=grid_spec,
        out_shape=jax.ShapeDtypeStruct((b, s, h * 2 * dh), BF16),
        compiler_params=_params("parallel", "parallel", "arbitrary"),
    )(lam.reshape(1), qkv, qkv, qkv, subln_g.reshape(1, 2 * dh), band)


def _mlstm_pre_kernel(xm_ref, cw_ref, cb_ref, wq_ref, wk_ref, wv_ref, gq_ref, gk_ref, gv_ref, gb_ref,
                      q_ref, k_ref, v_ref, xc_ref, pre_ref, pad_ref, *, s, halo, n_heads):
    j = pl.program_id(1)
    xm = xm_ref[0].astype(F32)
    cbw = xm.shape[1]
    pad_ref[0:8, :] = jnp.zeros((8, cbw), F32)
    pad_ref[8 + s:16 + s, :] = jnp.zeros((8, cbw), F32)
    pad_ref[8:8 + s, :] = xm

    @pl.when(j == 0)
    def _():
        pre_ref[0] = jnp.broadcast_to(gb_ref[...], pre_ref.shape[1:])

    rc = min(MLSTM_ROW_CHUNK, s)
    for r0 in range(0, s, rc):
        rows = slice(r0, r0 + rc)
        acc = cb_ref[...] + cw_ref[0:1, :] * pad_ref[8 - halo + r0:8 - halo + r0 + rc, :]
        for t in range(1, 2 * halo + 1):
            acc = acc + cw_ref[t:t + 1, :] * pad_ref[8 - halo + t + r0:8 - halo + t + r0 + rc, :]
        xcb = _silu(acc).astype(BF16)
        q = jnp.dot(xcb, wq_ref[0], preferred_element_type=F32).astype(BF16)
        k = jnp.dot(xcb, wk_ref[0], preferred_element_type=F32).astype(BF16)
        v = jnp.dot(xm_ref[0, rows, :], wv_ref[0], preferred_element_type=F32).astype(BF16)
        q_ref[0, rows, :] = q
        k_ref[0, rows, :] = k
        v_ref[0, rows, :] = v
        xc_ref[0, rows, :] = xcb
        pre_ref[0, rows, :] += (jnp.dot(q, gq_ref[...], preferred_element_type=F32)
                                + jnp.dot(k, gk_ref[...], preferred_element_type=F32)
                                + jnp.dot(v, gv_ref[...], preferred_element_type=F32))

    @pl.when(j == pl.num_programs(1) - 1)
    def _():
        pre = pre_ref[0]
        lane = lax.broadcasted_iota(jnp.int32, pre.shape, 1)
        is_forget = (lane % (2 * n_heads)) >= n_heads
        pre_ref[0] = jnp.where(is_forget, _log_sigmoid(pre), pre)


def _blockdiag_dense(w, cb):
    nblk, blk, _ = w.shape
    per = cb // blk
    w4 = w.reshape(nblk // per, per, blk, blk)
    eye = jnp.eye(per, dtype=w.dtype)
    return jnp.einsum('jnio,nm->jnimo', w4, eye).reshape(nblk // per, cb, cb).astype(BF16)


def _mlstm_pre(up, conv_w, conv_b, wq, wk, wv, gate_w, gate_b):
    b, s, c2 = up.shape
    c = c2 // 2
    cb = MLSTM_COL_TILE
    ncb = c // cb
    kw = conv_w.shape[0]
    ng = gate_w.shape[0] * gate_w.shape[-1]

    def gate_mat(i):
        g = gate_w[:, i].transpose(1, 0, 2).reshape(c, ng)
        return jnp.pad(g, ((0, 0), (0, LANES - ng))).astype(BF16)

    gb = jnp.pad(gate_b.reshape(1, ng), ((0, 0), (0, LANES - ng)))
    col = lambda i, j: (i, 0, j)
    out_bf = jax.ShapeDtypeStruct((b, s, c), BF16)
    return pl.pallas_call(
        functools.partial(_mlstm_pre_kernel, s=s, halo=kw // 2, n_heads=gate_w.shape[-1] // 2),
        grid=(b, ncb),
        in_specs=[
            pl.BlockSpec((1, s, cb), col),
            pl.BlockSpec((kw, cb), lambda i, j: (0, j)),
            pl.BlockSpec((1, cb), lambda i, j: (0, j)),
            pl.BlockSpec((1, cb, cb), lambda i, j: (j, 0, 0)),
            pl.BlockSpec((1, cb, cb), lambda i, j: (j, 0, 0)),
            pl.BlockSpec((1, cb, cb), lambda i, j: (j, 0, 0)),
            pl.BlockSpec((cb, LANES), lambda i, j: (j, 0)),
            pl.BlockSpec((cb, LANES), lambda i, j: (j, 0)),
            pl.BlockSpec((cb, LANES), lambda i, j: (j, 0)),
            pl.BlockSpec((1, LANES), lambda i, j: (0, 0)),
        ],
        out_specs=[
            pl.BlockSpec((1, s, cb), col),
            pl.BlockSpec((1, s, cb), col),
            pl.BlockSpec((1, s, cb), col),
            pl.BlockSpec((1, s, cb), col),
            pl.BlockSpec((1, s, LANES), lambda i, j: (i, 0, 0)),
        ],
        out_shape=[out_bf, out_bf, out_bf, out_bf, jax.ShapeDtypeStruct((b, s, LANES), F32)],
        scratch_shapes=[pltpu.VMEM((s + 16, cb), F32)],
        compiler_params=_params("parallel", "arbitrary"),
    )(up, conv_w, conv_b.reshape(1, c), _blockdiag_dense(wq, cb), _blockdiag_dense(wk, cb),
      _blockdiag_dense(wv, cb), gate_mat(0), gate_mat(1), gate_mat(2), gb)


def _mlstm_scan_kernel(q_ref, k_ref, v_ref, gcol_ref, grow_ref, og_ref, o_ref,
                       cf_ref, cbk_ref, hf_ref, hb_ref, *, L, nc, dk):
    scale = dk ** -0.5
    dirn = lax.broadcasted_iota(jnp.int32, (2, L, L), 0)
    row = lax.broadcasted_iota(jnp.int32, (2, L, L), 1)
    coli = lax.broadcasted_iota(jnp.int32, (2, L, L), 2)
    lo = jnp.where(dirn == 0, coli, row)
    hi = jnp.where(dirn == 0, row, coli)
    mask = lo <= hi
    mask_t = hi <= lo
    cf_ref[...] = jnp.zeros_like(cf_ref)
    cbk_ref[...] = jnp.zeros_like(cbk_ref)
    bdot = functools.partial(lax.dot_general, preferred_element_type=F32)

    def body(i, carry):
        n, m = carry
        sl_f = pl.ds(pl.multiple_of(i * L, L), L)
        sl_b = pl.ds(pl.multiple_of((nc - 1 - i) * L, L), L)
        qb = jnp.stack([q_ref[0, sl_f, :], q_ref[0, sl_b, :]])
        kb = jnp.stack([k_ref[0, sl_f, :], k_ref[0, sl_b, :]])
        vb = jnp.stack([v_ref[0, sl_f, :], v_ref[0, sl_b, :]])
        gcf, gcb = gcol_ref[0, 0, i], gcol_ref[0, 0, nc - 1 - i]
        grf, grb = grow_ref[0, 0, i], grow_ref[0, 0, nc - 1 - i]
        ii_col = jnp.stack([gcf[:, 0:1], gcb[:, 2:3]])
        lf_col = jnp.stack([gcf[:, 1:2], gcb[:, 3:4]])
        ii_row = jnp.stack([grf[0:1, :], grb[2:3, :]])
        lf_row = jnp.stack([grf[1:2, :], grb[3:4, :]])
        b_col = jnp.sum(jnp.where(mask, lf_row, 0.0), axis=2, keepdims=True)
        b_row = jnp.sum(jnp.where(mask_t, lf_col, 0.0), axis=1, keepdims=True)
        dmat = jnp.where(mask, b_col - b_row + ii_row, -jnp.inf)
        inter = b_col + m
        m_t = jnp.maximum(inter, jnp.max(dmat, axis=2, keepdims=True))
        w_intra = jnp.exp(dmat - m_t)
        w_inter = jnp.exp(inter - m_t)
        sqk = bdot(qb, kb, (((2,), (2,)), ((0,), (0,))))
        sw = sqk * scale * w_intra
        cmat = jnp.stack([cf_ref[...], cbk_ref[...]])
        num = (w_inter * bdot(qb, cmat.astype(BF16), (((2,), (1,)), ((0,), (0,))))
               + bdot(sw.astype(BF16), vb, (((2,), (1,)), ((0,), (0,)))))
        den = (w_inter * jnp.sum(qb.astype(F32) * n, axis=2, keepdims=True)
               + jnp.sum(sw, axis=2, keepdims=True))
        h = num / jnp.maximum(jnp.abs(den), jnp.exp(-m_t))
        b_tot = jnp.sum(lf_row, axis=2, keepdims=True)
        g_col = b_tot - b_col + ii_col
        m_new = jnp.maximum(b_tot + m, jnp.max(g_col, axis=1, keepdims=True))
        decay = jnp.exp(b_tot + m - m_new)
        kw = kb.astype(F32) * scale * jnp.exp(g_col - m_new)
        c_new = decay * cmat + bdot(kw.astype(BF16), vb, (((1,), (1,)), ((0,), (0,))))
        n_new = decay * n + jnp.sum(kw, axis=1, keepdims=True)
        hf_ref[sl_f, :] = h[0]
        hb_ref[sl_b, :] = h[1]
        cf_ref[...] = c_new[0]
        cbk_ref[...] = c_new[1]
        return n_new, m_new

    lax.fori_loop(0, nc, body, (jnp.zeros((2, 1, dk), F32), jnp.zeros((2, 1, 1), F32)))

    def finish(c, carry):
        sl = pl.ds(pl.multiple_of(c * L, L), L)
        hs = hf_ref[sl, :] + hb_ref[sl, :]
        ms = jnp.mean(hs * hs, axis=-1, keepdims=True)
        o_ref[0, sl, :] = (hs * lax.rsqrt(ms + EPS) * og_ref[0]).astype(o_ref.dtype)
        return carry

    lax.fori_loop(0, nc, finish, 0)


def _mlstm_scan(q, k, v, pre, outnorm_g):
    b, s, c = q.shape
    hh = ML_HEADS
    dk = c // hh
    L = min(ML_CHUNK, s)
    nc = s // L
    idx = jnp.array([[0 * 2 * hh + h, 0 * 2 * hh + hh + h, 2 * hh + h, 2 * hh + hh + h] for h in range(hh)])
    g = pre[:, :, idx]
    gcol = g.transpose(0, 2, 1, 3).reshape(b, hh, nc, L, 4)
    grow = gcol.transpose(0, 1, 2, 4, 3)
    head = lambda i, j: (i, 0, j)
    return pl.pallas_call(
        functools.partial(_mlstm_scan_kernel, L=L, nc=nc, dk=dk),
        grid=(b, hh),
        in_specs=[
            pl.BlockSpec((1, s, dk), head),
            pl.BlockSpec((1, s, dk), head),
            pl.BlockSpec((1, s, dk), head),
            pl.BlockSpec((1, 1, nc, L, 4), lambda i, j: (i, j, 0, 0, 0)),
            pl.BlockSpec((1, 1, nc, 4, L), lambda i, j: (i, j, 0, 0, 0)),
            pl.BlockSpec((1, 1, dk), lambda i, j: (j, 0, 0)),
        ],
        out_specs=pl.BlockSpec((1, s, dk), head),
        out_shape=jax.ShapeDtypeStruct((b, s, c), BF16),
        scratch_shapes=[pltpu.VMEM((dk, dk), F32), pltpu.VMEM((dk, dk), F32),
                        pltpu.VMEM((s, dk), F32), pltpu.VMEM((s, dk), F32)],
        compiler_params=_params("parallel", "parallel"),
    )(q, k, v, gcol, grow, outnorm_g.reshape(hh, 1, dk))


def _mlstm_out_kernel(hn_ref, xc_ref, z_ref, skip_ref, w_ref, x_ref, gate_ref, o_ref):
    a = (hn_ref[0].astype(F32) + skip_ref[...] * xc_ref[0].astype(F32)) * _silu(z_ref[0].astype(F32))
    y = jnp.dot(a.astype(BF16), w_ref[...], preferred_element_type=F32)
    o_ref[0] = x_ref[0] + gate_ref[0] * y


def _mlstm_out(hn, xc, up, skip, w, x, gate):
    b, s, c = hn.shape
    d = w.shape[1]
    ts = min(ROW_TILE, s)
    row = lambda i, j: (i, j, 0)
    return pl.pallas_call(
        _mlstm_out_kernel,
        grid=(b, s // ts),
        in_specs=[
            pl.BlockSpec((1, ts, c), row),
            pl.BlockSpec((1, ts, c), row),
            pl.BlockSpec((1, ts, c), lambda i, j: (i, j, 1)),
            pl.BlockSpec((1, c), lambda i, j: (0, 0)),
            pl.BlockSpec((c, d), lambda i, j: (0, 0)),
            pl.BlockSpec((1, ts, d), row),
            pl.BlockSpec((1, 1, d), lambda i, j: (i, 0, 0)),
        ],
        out_specs=pl.BlockSpec((1, ts, d), row),
        out_shape=jax.ShapeDtypeStruct((b, s, d), F32),
        compiler_params=_params("parallel", "parallel"),
    )(hn, xc, up, skip.reshape(1, c), w, x, gate)


def _router_kernel(x_ref, g_ref, sc_ref, sh_ref, w_ref, lpc_ref, lpr_ref, wgt_ref, stats_ref, cnt_ref, run_ref):
    first = (pl.program_id(0) == 0) & (pl.program_id(1) == 0)

    @pl.when(first)
    def _():
        run_ref[...] = jnp.zeros_like(run_ref)

    hf = _modulated_norm(x_ref[0], g_ref[...], sc_ref[0], sh_ref[0])
    w = w_ref[...]
    w_hi = w.astype(BF16)
    w_lo = (w - w_hi.astype(F32)).astype(BF16)
    hf_hi = hf.astype(BF16)
    hf_lo = (hf - hf_hi.astype(F32)).astype(BF16)
    logits = (jnp.dot(hf_hi, w_hi, preferred_element_type=F32)
              + (jnp.dot(hf_hi, w_lo, preferred_element_type=F32)
                 + jnp.dot(hf_lo, w_hi, preferred_element_type=F32)))
    tm = logits.shape[0]
    lane = lax.broadcasted_iota(jnp.int32, logits.shape, 1).astype(F32)
    neg = -jnp.inf
    is_g = lane < N_GROUPS
    gl = jnp.where(is_g, logits, neg)
    gmax = jnp.max(gl, axis=-1, keepdims=True)
    g_sel = jnp.min(jnp.where(gl == gmax, lane, float(LANES)), axis=-1, keepdims=True)
    p_g = 1.0 / jnp.sum(jnp.where(is_g, jnp.exp(gl - gmax), 0.0), axis=-1, keepdims=True)
    e_lane = lane - N_GROUPS
    in_grp = (e_lane >= g_sel * EXPERTS_PER_GROUP) & (e_lane < (g_sel + 1) * EXPERTS_PER_GROUP)
    el = jnp.where(in_grp, logits, neg)
    emax = jnp.max(el, axis=-1, keepdims=True)
    i1 = jnp.min(jnp.where(el == emax, e_lane, float(LANES)), axis=-1, keepdims=True)
    el2 = jnp.where(e_lane == i1, neg, el)
    emax2 = jnp.max(el2, axis=-1, keepdims=True)
    i2 = jnp.min(jnp.where(el2 == emax2, e_lane, float(LANES)), axis=-1, keepdims=True)
    t2 = jnp.exp(emax2 - emax)
    w1 = p_g / (1.0 + t2)
    w2 = p_g * t2 / (1.0 + t2)
    a = jnp.where(e_lane == i1, 1.0, jnp.where(e_lane == i2, 1.0, 0.0))
    r = lax.broadcasted_iota(jnp.int32, (tm, tm), 0)
    cc = lax.broadcasted_iota(jnp.int32, (tm, tm), 1)
    tri = jnp.where(cc < r, 1.0, 0.0).astype(BF16)
    rank = jnp.dot(tri, a.astype(BF16), preferred_element_type=F32)
    k8 = jnp.ceil(jnp.sum(a, axis=0, keepdims=True) * (1.0 / ROW_GROUP))
    ur = lax.broadcasted_iota(jnp.int32, (LANES, LANES), 0)
    uc = lax.broadcasted_iota(jnp.int32, (LANES, LANES), 1)
    upper = jnp.where(ur < uc, 1.0, 0.0).astype(BF16)
    k8_rows = jnp.broadcast_to(k8, (SUBLANES, LANES)).astype(BF16)
    off = jnp.dot(k8_rows, upper, preferred_element_type=F32)[0:1] * ROW_GROUP
    pos = off + rank
    lp0 = jnp.sum(jnp.where(e_lane == i1, pos, 0.0), axis=-1, keepdims=True)
    lp1 = jnp.sum(jnp.where(e_lane == i2, pos, 0.0), axis=-1, keepdims=True)
    run_old = run_ref[...]
    run_new = run_old + k8 * ROW_GROUP
    run_ref[...] = run_new
    cnt_ref[...] = run_new.astype(jnp.int32)
    srow = lax.broadcasted_iota(jnp.int32, (SUBLANES, LANES), 0)
    stats_ref[0] = jnp.where(srow == 0, k8, jnp.where(srow == 1, off, jnp.where(
        srow == 2, run_old, 0.0))).astype(jnp.int32)
    lp = jnp.where(lane == 0, lp0, jnp.where(lane == 1, lp1, 0.0))
    lpc_ref[0] = lp.astype(jnp.int32)
    lpr_ref[0, 0] = lp.T[:SUBLANES].astype(jnp.int32)
    wgt_ref[0] = jnp.where(lane == 0, w1, jnp.where(lane == 1, w2, 0.0))


def _router(x, g, sc, sh, w_group, w_router):
    b, s, d = x.shape
    ts = min(MOE_TOKEN_TILE, s)
    nt = s // ts
    w = jnp.concatenate([w_group, w_router], axis=1)
    w = jnp.pad(w, ((0, 0), (0, LANES - w.shape[1])))
    row = lambda i, j: (i, j, 0)
    return pl.pallas_call(
        _router_kernel,
        grid=(b, nt),
        in_specs=[
            pl.BlockSpec((1, ts, d), row),
            pl.BlockSpec((1, d), lambda i, j: (0, 0)),
            pl.BlockSpec((1, 1, d), lambda i, j: (i, 0, 0)),
            pl.BlockSpec((1, 1, d), lambda i, j: (i, 0, 0)),
            pl.BlockSpec((d, LANES), lambda i, j: (0, 0)),
        ],
        out_specs=[
            pl.BlockSpec((1, ts, LANES), row),
            pl.BlockSpec((1, 1, SUBLANES, ts), lambda i, j: (i, j, 0, 0)),
            pl.BlockSpec((1, ts, LANES), row),
            pl.BlockSpec((1, SUBLANES, LANES), lambda i, j: (i * nt + j, 0, 0)),
            pl.BlockSpec((1, LANES), lambda i, j: (0, 0)),
        ],
        out_shape=[jax.ShapeDtypeStruct((b, s, LANES), jnp.int32),
                   jax.ShapeDtypeStruct((b, nt, SUBLANES, ts), jnp.int32),
                   jax.ShapeDtypeStruct((b, s, LANES), F32),
                   jax.ShapeDtypeStruct((b * nt, SUBLANES, LANES), jnp.int32),
                   jax.ShapeDtypeStruct((1, LANES), jnp.int32)],
        scratch_shapes=[pltpu.VMEM((1, LANES), F32)],
        compiler_params=_params("arbitrary", "arbitrary"),
    )(x, g.reshape(1, d), sc, sh, w)


def _chunk_copies(meta_ref, blk, local_ref, hbm_ref, sem, *, to_hbm, start, max_groups):
    def per_expert(e, carry):
        at = blk * LANES + e
        k = meta_ref[at]
        lo = meta_ref[at + N_EXPERTS]
        hi = meta_ref[at + 2 * N_EXPERTS]
        done = jnp.int32(0)
        for bit in range(max_groups.bit_length()):
            rows = ROW_GROUP << bit
            take = (k >> bit) & 1

            @pl.when(take == 1)
            def _():
                loc = local_ref.at[pl.ds(pl.multiple_of(lo + done, ROW_GROUP), rows)]
                far = hbm_ref.at[pl.ds(pl.multiple_of(hi + done, ROW_GROUP), rows)]
                cp = pltpu.make_async_copy(loc, far, sem) if to_hbm else pltpu.make_async_copy(far, loc, sem)
                if start:
                    cp.start()
                else:
                    cp.wait()

            done = done + take * rows
        return carry

    lax.fori_loop(0, N_EXPERTS, per_expert, 0)


def _dispatch_kernel(last_ref, meta_ref, lpr_ref, x_ref, g_ref, sc_ref, sh_ref, xs_ref, loc_ref, zero_ref,
                     sem, zsem):
    tm = x_ref.shape[1]
    zrows = zero_ref.shape[0]
    blk = pl.program_id(0) * pl.num_programs(1) + pl.program_id(1)
    n_blk = pl.num_programs(0) * pl.num_programs(1)
    slot = blk % 2

    @pl.when((pl.program_id(0) == 0) & (pl.program_id(1) == 0))
    def _():
        zero_ref[...] = jnp.zeros_like(zero_ref)

        def fill(e):
            start = pl.multiple_of(last_ref[e], zrows)
            return pltpu.make_async_copy(zero_ref, xs_ref.at[pl.ds(start, zrows)], zsem)

        for e in range(N_EXPERTS):
            @pl.when(last_ref[e] >= 0)
            def _():
                fill(e).start()
        for e in range(N_EXPERTS):
            @pl.when(last_ref[e] >= 0)
            def _():
                fill(e).wait()

        def tail(k):
            start = pl.multiple_of(k * zrows, zrows)
            return pltpu.make_async_copy(zero_ref, xs_ref.at[pl.ds(start, zrows)], zsem)

        n_used = last_ref[N_EXPERTS]
        n_tiles = xs_ref.shape[0] // zrows
        lax.fori_loop(n_used, n_tiles, lambda k, c: (tail(k).start(), c)[1], 0)
        lax.fori_loop(n_used, n_tiles, lambda k, c: (tail(k).wait(), c)[1], 0)

    hf = _modulated_norm(x_ref[0], g_ref[...], sc_ref[0], sh_ref[0]).astype(BF16)
    rows = lax.broadcasted_iota(jnp.int32, (loc_ref.shape[1], tm), 0)
    sel = jnp.where(rows == lpr_ref[0, 0, 0:1, :], 1.0,
                    jnp.where(rows == lpr_ref[0, 0, 1:2, :], 1.0, 0.0)).astype(BF16)
    loc_ref[slot] = jnp.dot(sel, hf, preferred_element_type=F32).astype(BF16)

    def copies(b, sl, start):
        _chunk_copies(meta_ref, b, loc_ref.at[sl], xs_ref, sem.at[sl], to_hbm=True, start=start,
                      max_groups=tm // ROW_GROUP)

    copies(blk, slot, True)

    @pl.when(blk > 0)
    def _():
        copies(blk - 1, 1 - slot, False)

    @pl.when(blk == n_blk - 1)
    def _():
        copies(blk, slot, False)


def _local_rows(tm):
    return 2 * tm + N_EXPERTS * ROW_GROUP


def _dispatch(x, g, sc, sh, meta, lpr, last_tile_row, n_rows):
    b, s, d = x.shape
    tm = min(MOE_TOKEN_TILE, s)
    nt = s // tm
    row = lambda i, j, last, meta: (i, j, 0)
    grid_spec = pltpu.PrefetchScalarGridSpec(
        num_scalar_prefetch=2,
        grid=(b, nt),
        in_specs=[
            pl.BlockSpec((1, 1, SUBLANES, tm), lambda i, j, last, meta: (i, j, 0, 0)),
            pl.BlockSpec((1, tm, d), row),
            pl.BlockSpec((1, d), lambda i, j, last, meta: (0, 0)),
            pl.BlockSpec((1, 1, d), lambda i, j, last, meta: (i, 0, 0)),
            pl.BlockSpec((1, 1, d), lambda i, j, last, meta: (i, 0, 0)),
        ],
        out_specs=pl.BlockSpec(memory_space=pl.ANY),
        scratch_shapes=[pltpu.VMEM((2, _local_rows(tm), d), BF16), pltpu.VMEM((MOE_ROW_TILE, d), BF16),
                        pltpu.SemaphoreType.DMA((2,)), pltpu.SemaphoreType.DMA(())],
    )
    return pl.pallas_call(
        _dispatch_kernel,
        grid_spec=grid_spec,
        out_shape=jax.ShapeDtypeStruct((n_rows, d), BF16),
        compiler_params=_params("arbitrary", "arbitrary"),
    )(last_tile_row, meta, lpr, x, g.reshape(1, d), sc, sh)


def _expert_kernel(te_ref, nu_ref, xs_ref, w1_ref, w3_ref, w2_ref, ys_ref, w1b_ref, w3b_ref, w2b_ref):
    i = pl.program_id(0)

    @pl.when((i == 0) | (te_ref[i] != te_ref[jnp.maximum(i - 1, 0)]))
    def _():
        w1b_ref[...] = w1_ref[0, 0].astype(BF16)
        w3b_ref[...] = w3_ref[0, 0].astype(BF16)
        w2b_ref[...] = w2_ref[0, 0].astype(BF16)

    @pl.when(i < nu_ref[0])
    def _():
        xb = xs_ref[...]
        a = jnp.dot(xb, w1b_ref[...], preferred_element_type=F32)
        bb = jnp.dot(xb, w3b_ref[...], preferred_element_type=F32)
        hmid = (_silu(a) * bb).astype(BF16)
        ys_ref[...] = jnp.dot(hmid, w2b_ref[...], preferred_element_type=F32).astype(ys_ref.dtype)

    @pl.when(i >= nu_ref[0])
    def _():
        ys_ref[...] = jnp.zeros_like(ys_ref)


def _expert_ffn(xs, tile_expert, n_used, w1, w3, w2, layer):
    p, d = xs.shape
    de = w1.shape[3]
    tm = MOE_ROW_TILE
    grid_spec = pltpu.PrefetchScalarGridSpec(
        num_scalar_prefetch=2,
        grid=(p // tm,),
        in_specs=[
            pl.BlockSpec((tm, d), lambda i, te, nu: (jnp.maximum(jnp.minimum(i, nu[0] - 1), 0), 0)),
            pl.BlockSpec((1, 1, d, de), lambda i, te, nu: (layer, te[i], 0, 0)),
            pl.BlockSpec((1, 1, d, de), lambda i, te, nu: (layer, te[i], 0, 0)),
            pl.BlockSpec((1, 1, de, d), lambda i, te, nu: (layer, te[i], 0, 0)),
        ],
        out_specs=pl.BlockSpec((tm, d), lambda i, te, nu: (i, 0)),
        scratch_shapes=[pltpu.VMEM((d, de), BF16), pltpu.VMEM((d, de), BF16), pltpu.VMEM((de, d), BF16)],
    )
    return pl.pallas_call(
        _expert_kernel,
        grid_spec=grid_spec,
        out_shape=jax.ShapeDtypeStruct((p, d), BF16),
        compiler_params=_params("arbitrary"),
    )(tile_expert, n_used, xs, w1, w3, w2)


def _combine_kernel(meta_ref, ys_ref, lpc_ref, wgt_ref, x_ref, gate_ref, o_ref, loc_ref, sem):
    tm = x_ref.shape[1]
    n_loc = loc_ref.shape[1]
    blk = pl.program_id(0) * pl.num_programs(1) + pl.program_id(1)
    n_blk = pl.num_programs(0) * pl.num_programs(1)
    slot = blk % 2

    def fetch(b, sl):
        loc_ref[sl, 2 * tm:, :] = jnp.zeros((n_loc - 2 * tm, loc_ref.shape[2]), BF16)
        _chunk_copies(meta_ref, b, loc_ref.at[sl], ys_ref, sem.at[sl], to_hbm=False, start=True,
                      max_groups=tm // ROW_GROUP)

    @pl.when(blk == 0)
    def _():
        fetch(blk, slot)

    @pl.when(blk + 1 < n_blk)
    def _():
        fetch(blk + 1, 1 - slot)

    _chunk_copies(meta_ref, blk, loc_ref.at[slot], ys_ref, sem.at[slot], to_hbm=False, start=False,
                  max_groups=tm // ROW_GROUP)
    cols = lax.broadcasted_iota(jnp.int32, (tm, n_loc), 1)
    lp = lpc_ref[0]
    wgt = wgt_ref[0]
    wm = jnp.where(cols == lp[:, 0:1], wgt[:, 0:1], jnp.where(cols == lp[:, 1:2], wgt[:, 1:2], 0.0))
    w_hi = wm.astype(BF16)
    w_lo = (wm - w_hi.astype(F32)).astype(BF16)
    yb = loc_ref[slot]
    y = jnp.dot(w_hi, yb, preferred_element_type=F32) + jnp.dot(w_lo, yb, preferred_element_type=F32)
    o_ref[0] = x_ref[0] + gate_ref[0] * y


def _combine(ys, meta, lpc, wgt, x, gate):
    b, s, d = x.shape
    tm = min(MOE_TOKEN_TILE, s)
    nt = s // tm
    row = lambda i, j, meta: (i, j, 0)
    grid_spec = pltpu.PrefetchScalarGridSpec(
        num_scalar_prefetch=1,
        grid=(b, nt),
        in_specs=[
            pl.BlockSpec(memory_space=pl.ANY),
            pl.BlockSpec((1, tm, LANES), row),
            pl.BlockSpec((1, tm, LANES), row),
            pl.BlockSpec((1, tm, d), row),
            pl.BlockSpec((1, 1, d), lambda i, j, meta: (i, 0, 0)),
        ],
        out_specs=pl.BlockSpec((1, tm, d), row),
        scratch_shapes=[pltpu.VMEM((2, _local_rows(tm), d), BF16), pltpu.SemaphoreType.DMA((2,))],
    )
    return pl.pallas_call(
        _combine_kernel,
        grid_spec=grid_spec,
        out_shape=jax.ShapeDtypeStruct((b, s, d), F32),
        compiler_params=_params("arbitrary", "arbitrary"),
    )(meta, ys, lpc, wgt, x, gate)


def _hier_moe_residual(x, g, sc, sh, gate, w_group, w_router, w1, w3, w2, layer):
    b, s, d = x.shape
    t = b * s
    lpc, lpr, wgt, stats, cnt = _router(x, g, sc, sh, w_group, w_router)
    tm = MOE_ROW_TILE
    ex = slice(N_GROUPS, N_GROUPS + N_EXPERTS)
    counts = cnt[0, ex]
    tiles = (counts + tm - 1) // tm
    tile_end = jnp.cumsum(tiles)
    base = (tile_end - tiles) * tm
    n_blocks = stats.shape[0]
    n_tiles = (2 * t + n_blocks * N_EXPERTS * ROW_GROUP) // tm + N_EXPERTS
    tile_expert = jnp.minimum(
        jnp.sum(tile_end[None, :] <= jnp.arange(n_tiles)[:, None], axis=1), N_EXPERTS - 1).astype(jnp.int32)
    n_used = tile_end[-1:].astype(jnp.int32)
    meta = jnp.concatenate([stats[:, 0, ex], stats[:, 1, ex], base[None, :] + stats[:, 2, ex],
                            jnp.zeros((n_blocks, LANES - 3 * N_EXPERTS), jnp.int32)], axis=1)
    meta = meta.astype(jnp.int32).reshape(n_blocks * LANES)
    last_tile_row = jnp.where(tiles > 0, (tile_end - 1) * tm, -1).astype(jnp.int32)
    last_tile_row = jnp.concatenate([last_tile_row, n_used])
    xs = _dispatch(x, g, sc, sh, meta, lpr, last_tile_row, n_tiles * tm)
    ys = _expert_ffn(xs, tile_expert, n_used, w1, w3, w2, layer)
    return _combine(ys, meta, lpc, wgt, x, gate)


def kernel(x, c, rel_table, ada_w, ada_b, norm_mix_g, norm_ffn_g, da_w_in, da_w_out, da_q_gain, da_k_gain, da_lam_q1, da_lam_k1, da_lam_q2, da_lam_k2, da_subln_g, ml_w_in, ml_conv_w, ml_conv_b, ml_wq, ml_wk, ml_wv, ml_gate_w, ml_gate_b, ml_outnorm_g, ml_skip, ml_w_out, moe_w_group, moe_w_router, moe_w1, moe_w3, moe_w2):
    depth = ada_w.shape[0]
    d = x.shape[-1]
    mod = _ada_mod(c, ada_w, ada_b)
    for i in range(depth):
        sh1, sc1, g1, sh2, sc2, g2 = [mod[i, :, None, k * d:(k + 1) * d] for k in range(6)]
        j = i // N_MIXERS
        if i % N_MIXERS == 0:
            qkv = _prenorm_matmul(x, norm_mix_g[i], sc1, sh1, da_w_in[j].astype(BF16),
                                  head_gain=_qk_head_gains(da_q_gain[j], da_k_gain[j]), half=DA_HEAD_DIM)
            lam_init = 0.8 - 0.6 * math.exp(-0.3 * i)
            lam = (jnp.exp(jnp.sum(da_lam_q1[j] * da_lam_k1[j])) - jnp.exp(jnp.sum(da_lam_q2[j] * da_lam_k2[j]))
                   + lam_init).astype(F32)
            o = _diff_attention(qkv, da_subln_g[j], lam, lam_init, rel_table)
            x = _proj_residual(o, da_w_out[j].astype(BF16), x, g1)
        else:
            up = _prenorm_matmul(x, norm_mix_g[i], sc1, sh1, ml_w_in[j].astype(BF16))
            q, k, v, xc, pre = _mlstm_pre(up, ml_conv_w[j], ml_conv_b[j], ml_wq[j], ml_wk[j], ml_wv[j],
                                          ml_gate_w[j], ml_gate_b[j])
            hn = _mlstm_scan(q, k, v, pre, ml_outnorm_g[j])
            x = _mlstm_out(hn, xc, up, ml_skip[j], ml_w_out[j].astype(BF16), x, g1)
        x = _hier_moe_residual(x, norm_ffn_g[i], sc2, sh2, g2, moe_w_group[i], moe_w_router[i],
                               moe_w1, moe_w3, moe_w2, i)
    return x
```

```python
import functools
import math

import jax
import jax.numpy as jnp
from jax import lax
from jax.experimental import pallas as pl
from jax.experimental.pallas import tpu as pltpu

EPS = 1e-6
N_MIXERS = 2
DA_HEADS = 8
DA_HEAD_DIM = 64
N_BUCKETS = 32
MAX_DISTANCE = 128
ML_HEADS = 4
ML_CHUNK = 256
ML_QKV_BLOCK = 4
N_GROUPS = 4
EXPERTS_PER_GROUP = 8
N_EXPERTS = N_GROUPS * EXPERTS_PER_GROUP

LANES = 128
SUBLANES = 8
ROW_GROUP = 16
VMEM_LIMIT = 48 * 1024 * 1024
ATTN_Q_TILE = 2048
ATTN_ROW_BLOCK = 128
ROW_TILE = 512
MOE_ROW_TILE = 512
MOE_TOKEN_TILE = 512
MLSTM_COL_TILE = 256
MLSTM_ROW_CHUNK = 512

F32 = jnp.float32
BF16 = jnp.bfloat16
LOG2E = math.log2(math.e)


def _params(*sem):
    return pltpu.CompilerParams(dimension_semantics=sem, vmem_limit_bytes=VMEM_LIMIT)


def _silu(x):
    return x / (1.0 + jnp.exp(-x))


def _log_sigmoid(x):
    return jnp.minimum(x, 0.0) - jnp.log(1.0 + jnp.exp(-jnp.abs(x)))


def _modulated_norm(x, g, sc, sh):
    ms = jnp.mean(x * x, axis=-1, keepdims=True)
    return x * lax.rsqrt(ms + EPS) * g * (1.0 + sc) + sh


def _ada_kernel(c_ref, w_ref, b_ref, o_ref):
    ca = _silu(c_ref[...]).astype(BF16)
    o_ref[0] = jnp.dot(ca, w_ref[0].astype(BF16), preferred_element_type=F32) + b_ref[0]


def _ada_mod(c, ada_w, ada_b):
    depth, d, n = ada_w.shape
    b = c.shape[0]
    tn = 1536
    return pl.pallas_call(
        _ada_kernel,
        grid=(depth, n // tn),
        in_specs=[
            pl.BlockSpec((b, d), lambda i, j: (0, 0)),
            pl.BlockSpec((1, d, tn), lambda i, j: (i, 0, j)),
            pl.BlockSpec((1, 1, tn), lambda i, j: (i, 0, j)),
        ],
        out_specs=pl.BlockSpec((1, b, tn), lambda i, j: (i, 0, j)),
        out_shape=jax.ShapeDtypeStruct((depth, b, n), F32),
        compiler_params=_params("parallel", "parallel"),
    )(c, ada_w, ada_b.reshape(depth, 1, n))


def _prenorm_matmul_kernel(x_ref, g_ref, sc_ref, sh_ref, w_ref, hg_ref, o_ref, *, n_chunk, n_norm, half):
    y = _modulated_norm(x_ref[0], g_ref[...], sc_ref[0], sh_ref[0]).astype(BF16)
    n = w_ref.shape[1]
    for n0 in range(0, n, n_chunk):
        acc = jnp.dot(y, w_ref[:, n0:n0 + n_chunk], preferred_element_type=F32)
        if n0 < n_norm:
            for c0 in range(n0, n0 + n_chunk, 2 * half):
                blk = _group_rms(acc[:, c0 - n0:c0 - n0 + 2 * half], hg_ref[:, c0:c0 + 2 * half], half)
                o_ref[0, :, c0:c0 + 2 * half] = blk.astype(o_ref.dtype)
        else:
            o_ref[0, :, n0:n0 + n_chunk] = acc.astype(o_ref.dtype)


def _prenorm_matmul(x, g, sc, sh, w, head_gain=None, half=LANES // 2):
    b, s, d = x.shape
    n = w.shape[1]
    ts = min(ROW_TILE, s)
    n_norm = 0 if head_gain is None else head_gain.shape[1]
    hg = jnp.ones((1, n), F32) if head_gain is None else jnp.pad(head_gain, ((0, 0), (0, n - n_norm)))
    return pl.pallas_call(
        functools.partial(_prenorm_matmul_kernel, n_chunk=1024, n_norm=n_norm, half=half),
        grid=(b, s // ts),
        in_specs=[
            pl.BlockSpec((1, ts, d), lambda i, j: (i, j, 0)),
            pl.BlockSpec((1, d), lambda i, j: (0, 0)),
            pl.BlockSpec((1, 1, d), lambda i, j: (i, 0, 0)),
            pl.BlockSpec((1, 1, d), lambda i, j: (i, 0, 0)),
            pl.BlockSpec((d, n), lambda i, j: (0, 0)),
            pl.BlockSpec((1, n), lambda i, j: (0, 0)),
        ],
        out_specs=pl.BlockSpec((1, ts, n), lambda i, j: (i, j, 0)),
        out_shape=jax.ShapeDtypeStruct((b, s, n), BF16),
        compiler_params=_params("parallel", "parallel"),
    )(x, g.reshape(1, d), sc, sh, w, hg)


def _proj_residual_kernel(a_ref, w_ref, x_ref, gate_ref, o_ref):
    y = jnp.dot(a_ref[0], w_ref[...], preferred_element_type=F32)
    o_ref[0] = x_ref[0] + gate_ref[0] * y


def _proj_residual(a, w, x, gate):
    b, s, k = a.shape
    d = w.shape[1]
    ts = min(ROW_TILE, s)
    return pl.pallas_call(
        _proj_residual_kernel,
        grid=(b, s // ts),
        in_specs=[
            pl.BlockSpec((1, ts, k), lambda i, j: (i, j, 0)),
            pl.BlockSpec((k, d), lambda i, j: (0, 0)),
            pl.BlockSpec((1, ts, d), lambda i, j: (i, j, 0)),
            pl.BlockSpec((1, 1, d), lambda i, j: (i, 0, 0)),
        ],
        out_specs=pl.BlockSpec((1, ts, d), lambda i, j: (i, j, 0)),
        out_shape=jax.ShapeDtypeStruct((b, s, d), F32),
        compiler_params=_params("parallel", "parallel"),
    )(a, w, x, gate)


def _t5_bucket(rel):
    nb = N_BUCKETS // 2
    max_exact = nb // 2
    ret = jnp.where(rel > 0, nb, 0)
    n = jnp.abs(rel)
    nf = jnp.maximum(n, 1).astype(F32)
    large = max_exact + (jnp.log(nf / max_exact) / math.log(MAX_DISTANCE / max_exact)
                         * (nb - max_exact)).astype(jnp.int32)
    large = jnp.minimum(large, nb - 1)
    return ret + jnp.where(n < max_exact, n, large)


def _bias_band_kernel(t_ref, o_ref, *, tq, nb):
    u = jnp.broadcast_to(t_ref[0], (tq, t_ref.shape[2]))
    y = pltpu.roll(u, 1, 1, stride=1, stride_axis=0)
    for jb in range(nb):
        o_ref[0, jb] = y[:, tq + jb * LANES:tq + (jb + 1) * LANES]


def _bias_band(rel_table, s, tq):
    h = rel_table.shape[1]
    delta = jnp.arange(-(s - 1), s)
    t = rel_table[_t5_bucket(delta)].T.astype(F32) * LOG2E
    u = jnp.concatenate([t, jnp.zeros((h, 1), F32)], axis=1).reshape(h, 1, 2 * s)
    nb = (2 * s - tq) // LANES
    return pl.pallas_call(
        functools.partial(_bias_band_kernel, tq=tq, nb=nb),
        grid=(h,),
        in_specs=[pl.BlockSpec((1, 1, 2 * s), lambda i: (i, 0, 0))],
        out_specs=pl.BlockSpec((1, nb, tq, LANES), lambda i: (i, 0, 0, 0)),
        out_shape=jax.ShapeDtypeStruct((h, nb, tq, LANES), F32),
        compiler_params=_params("parallel"),
    )(u)


def _group_rms(x, gain, half):
    sq = x * x
    lane = lax.broadcasted_iota(jnp.int32, x.shape, 1)
    lo = lane < half
    s_lo = jnp.sum(jnp.where(lo, sq, 0.0), axis=-1, keepdims=True)
    s_all = jnp.sum(sq, axis=-1, keepdims=True)
    ms = jnp.where(lo, s_lo, s_all - s_lo) * (1.0 / half)
    return x * lax.rsqrt(ms + EPS) * gain


def _diff_attn_kernel(lam_ref, q_ref, k_ref, v_ref, sg_ref, band_ref, o_ref, ve_ref, *, tq, s, dh, out_scale):
    qi = pl.program_id(2)
    nq = pl.num_programs(2)
    dv = 2 * dh

    @pl.when(qi == 0)
    def _():
        ve_ref[:, :dv] = v_ref[0]
        ve_ref[:, dv:] = jnp.ones((s, dv), BF16)

    lam = lam_ref[0]
    q = q_ref[0]
    lane = lax.broadcasted_iota(jnp.int32, q.shape, 1)
    zero = jnp.zeros_like(q)
    q0 = jnp.where(lane < dh, q, zero)
    q1 = jnp.where(lane < dh, zero, q)
    jb0 = (nq - 1 - qi) * (tq // LANES)
    nkb = s // LANES

    def softmax_av(qm, r0, r1):
        sc = lax.dot_general(qm[r0:r1], k_ref[0], (((1,), (1,)), ((), ())), preferred_element_type=F32)
        sc = sc + jnp.concatenate([band_ref[0, jb0 + kb, r0:r1, :] for kb in range(nkb)], axis=1)
        p = jnp.exp2(sc - jnp.max(sc, axis=-1, keepdims=True)).astype(BF16)
        oe = jnp.dot(p, ve_ref[...], preferred_element_type=F32)
        return oe[:, :dv], oe[:, dv:dv + 1]

    rb = ATTN_ROW_BLOCK
    for r0 in range(0, tq, rb):
        n0, l0 = softmax_av(q0, r0, r0 + rb)
        n1, l1 = softmax_av(q1, r0, r0 + rb)
        o = n0 * (1.0 / l0) - n1 * (lam / l1)
        ms = jnp.mean(o * o, axis=-1, keepdims=True)
        o_ref[0, r0:r0 + rb, :] = (o * lax.rsqrt(ms + EPS) * sg_ref[...] * out_scale).astype(o_ref.dtype)


def _qk_head_gains(q_gain, k_gain):
    h, dh = DA_HEADS, DA_HEAD_DIM
    qg = jnp.tile(q_gain * (dh ** -0.5 * LOG2E), 2 * h)
    kg = jnp.tile(k_gain, 2 * h)
    return jnp.concatenate([qg, kg]).reshape(1, 4 * h * dh).astype(F32)


def _diff_attention(qkv, subln_g, lam, lam_init, rel_table):
    b, s, _ = qkv.shape
    h, dh = DA_HEADS, DA_HEAD_DIM
    tq = min(ATTN_Q_TILE, s)
    band = _bias_band(rel_table, s, tq)
    nb = band.shape[1]
    kern = functools.partial(_diff_attn_kernel, tq=tq, s=s, dh=dh, out_scale=1.0 - lam_init)
    grid_spec = pltpu.PrefetchScalarGridSpec(
        num_scalar_prefetch=1,
        grid=(h, b, s // tq),
        in_specs=[
            pl.BlockSpec((1, tq, 2 * dh), lambda hi, bi, qi, lam: (bi, qi, hi)),
            pl.BlockSpec((1, s, 2 * dh), lambda hi, bi, qi, lam: (bi, 0, h + hi)),
            pl.BlockSpec((1, s, 2 * dh), lambda hi, bi, qi, lam: (bi, 0, 2 * h + hi)),
            pl.BlockSpec((1, 2 * dh), lambda hi, bi, qi, lam: (0, 0)),
            pl.BlockSpec((1, nb, tq, LANES), lambda hi, bi, qi, lam: (hi, 0, 0, 0),
                         pipeline_mode=pl.Buffered(1)),
        ],
        out_specs=pl.BlockSpec((1, tq, 2 * dh), lambda hi, bi, qi, lam: (bi, qi, hi)),
        scratch_shapes=[pltpu.VMEM((s, 4 * dh), BF16)],
    )
    return pl.pallas_call(
        kern,
        grid_spec=grid_spec,
        out_shape=jax.ShapeDtypeStruct((b, s, h * 2 * dh), BF16),
        compiler_params=_params("parallel", "parallel", "arbitrary"),
    )(lam.reshape(1), qkv, qkv, qkv, subln_g.reshape(1, 2 * dh), band)


def _mlstm_pre_kernel(xm_ref, cw_ref, cb_ref, wq_ref, wk_ref, wv_ref, gq_ref, gk_ref, gv_ref, gb_ref,
                      q_ref, k_ref, v_ref, xc_ref, pre_ref, pad_ref, *, s, halo, n_heads):
    j = pl.program_id(1)
    xm = xm_ref[0].astype(F32)
    cbw = xm.shape[1]
    pad_ref[0:8, :] = jnp.zeros((8, cbw), F32)
    pad_ref[8 + s:16 + s, :] = jnp.zeros((8, cbw), F32)
    pad_ref[8:8 + s, :] = xm

    @pl.when(j == 0)
    def _():
        pre_ref[0] = jnp.broadcast_to(gb_ref[...], pre_ref.shape[1:])

    rc = min(MLSTM_ROW_CHUNK, s)
    for r0 in range(0, s, rc):
        rows = slice(r0, r0 + rc)
        acc = cb_ref[...] + cw_ref[0:1, :] * pad_ref[8 - halo + r0:8 - halo + r0 + rc, :]
        for t in range(1, 2 * halo + 1):
            acc = acc + cw_ref[t:t + 1, :] * pad_ref[8 - halo + t + r0:8 - halo + t + r0 + rc, :]
        xcb = _silu(acc).astype(BF16)
        q = jnp.dot(xcb, wq_ref[0], preferred_element_type=F32).astype(BF16)
        k = jnp.dot(xcb, wk_ref[0], preferred_element_type=F32).astype(BF16)
        v = jnp.dot(xm_ref[0, rows, :], wv_ref[0], preferred_element_type=F32).astype(BF16)
        q_ref[0, rows, :] = q
        k_ref[0, rows, :] = k
        v_ref[0, rows, :] = v
        xc_ref[0, rows, :] = xcb
        pre_ref[0, rows, :] += (jnp.dot(q, gq_ref[...], preferred_element_type=F32)
                                + jnp.dot(k, gk_ref[...], preferred_element_type=F32)
                                + jnp.dot(v, gv_ref[...], preferred_element_type=F32))

    @pl.when(j == pl.num_programs(1) - 1)
    def _():
        pre = pre_ref[0]
        lane = lax.broadcasted_iota(jnp.int32, pre.shape, 1)
        is_forget = (lane % (2 * n_heads)) >= n_heads
        pre_ref[0] = jnp.where(is_forget, _log_sigmoid(pre), pre)


def _blockdiag_dense(w, cb):
    nblk, blk, _ = w.shape
    per = cb // blk
    w4 = w.reshape(nblk // per, per, blk, blk)
    eye = jnp.eye(per, dtype=w.dtype)
    return jnp.einsum('jnio,nm->jnimo', w4, eye).reshape(nblk // per, cb, cb).astype(BF16)


def _mlstm_pre(up, conv_w, conv_b, wq, wk, wv, gate_w, gate_b):
    b, s, c2 = up.shape
    c = c2 // 2
    cb = MLSTM_COL_TILE
    ncb = c // cb
    kw = conv_w.shape[0]
    ng = gate_w.shape[0] * gate_w.shape[-1]

    def gate_mat(i):
        g = gate_w[:, i].transpose(1, 0, 2).reshape(c, ng)
        return jnp.pad(g, ((0, 0), (0, LANES - ng))).astype(BF16)

    gb = jnp.pad(gate_b.reshape(1, ng), ((0, 0), (0, LANES - ng)))
    col = lambda i, j: (i, 0, j)
    out_bf = jax.ShapeDtypeStruct((b, s, c), BF16)
    return pl.pallas_call(
        functools.partial(_mlstm_pre_kernel, s=s, halo=kw // 2, n_heads=gate_w.shape[-1] // 2),
        grid=(b, ncb),
        in_specs=[
            pl.BlockSpec((1, s, cb), col),
            pl.BlockSpec((kw, cb), lambda i, j: (0, j)),
            pl.BlockSpec((1, cb), lambda i, j: (0, j)),
            pl.BlockSpec((1, cb, cb), lambda i, j: (j, 0, 0)),
            pl.BlockSpec((1, cb, cb), lambda i, j: (j, 0, 0)),
            pl.BlockSpec((1, cb, cb), lambda i, j: (j, 0, 0)),
            pl.BlockSpec((cb, LANES), lambda i, j: (j, 0)),
            pl.BlockSpec((cb, LANES), lambda i, j: (j, 0)),
            pl.BlockSpec((cb, LANES), lambda i, j: (j, 0)),
            pl.BlockSpec((1, LANES), lambda i, j: (0, 0)),
        ],
        out_specs=[
            pl.BlockSpec((1, s, cb), col),
            pl.BlockSpec((1, s, cb), col),
            pl.BlockSpec((1, s, cb), col),
            pl.BlockSpec((1, s, cb), col),
            pl.BlockSpec((1, s, LANES), lambda i, j: (i, 0, 0)),
        ],
        out_shape=[out_bf, out_bf, out_bf, out_bf, jax.ShapeDtypeStruct((b, s, LANES), F32)],
        scratch_shapes=[pltpu.VMEM((s + 16, cb), F32)],
        compiler_params=_params("parallel", "arbitrary"),
    )(up, conv_w, conv_b.reshape(1, c), _blockdiag_dense(wq, cb), _blockdiag_dense(wk, cb),
      _blockdiag_dense(wv, cb), gate_mat(0), gate_mat(1), gate_mat(2), gb)


def _mlstm_scan_kernel(q_ref, k_ref, v_ref, gcol_ref, grow_ref, og_ref, o_ref,
                       cf_ref, cbk_ref, hf_ref, hb_ref, *, L, nc, dk):
    scale = dk ** -0.5
    dirn = lax.broadcasted_iota(jnp.int32, (2, L, L), 0)
    row = lax.broadcasted_iota(jnp.int32, (2, L, L), 1)
    coli = lax.broadcasted_iota(jnp.int32, (2, L, L), 2)
    lo = jnp.where(dirn == 0, coli, row)
    hi = jnp.where(dirn == 0, row, coli)
    mask = lo <= hi
    mask_t = hi <= lo
    cf_ref[...] = jnp.zeros_like(cf_ref)
    cbk_ref[...] = jnp.zeros_like(cbk_ref)
    bdot = functools.partial(lax.dot_general, preferred_element_type=F32)

    def body(i, carry):
        n, m = carry
        sl_f = pl.ds(pl.multiple_of(i * L, L), L)
        sl_b = pl.ds(pl.multiple_of((nc - 1 - i) * L, L), L)
        qb = jnp.stack([q_ref[0, sl_f, :], q_ref[0, sl_b, :]])
        kb = jnp.stack([k_ref[0, sl_f, :], k_ref[0, sl_b, :]])
        vb = jnp.stack([v_ref[0, sl_f, :], v_ref[0, sl_b, :]])
        gcf, gcb = gcol_ref[0, 0, i], gcol_ref[0, 0, nc - 1 - i]
        grf, grb = grow_ref[0, 0, i], grow_ref[0, 0, nc - 1 - i]
        ii_col = jnp.stack([gcf[:, 0:1], gcb[:, 2:3]])
        lf_col = jnp.stack([gcf[:, 1:2], gcb[:, 3:4]])
        ii_row = jnp.stack([grf[0:1, :], grb[2:3, :]])
        lf_row = jnp.stack([grf[1:2, :], grb[3:4, :]])
        b_col = jnp.sum(jnp.where(mask, lf_row, 0.0), axis=2, keepdims=True)
        b_row = jnp.sum(jnp.where(mask_t, lf_col, 0.0), axis=1, keepdims=True)
        dmat = jnp.where(mask, b_col - b_row + ii_row, -jnp.inf)
        inter = b_col + m
        m_t = jnp.maximum(inter, jnp.max(dmat, axis=2, keepdims=True))
        w_intra = jnp.exp(dmat - m_t)
        w_inter = jnp.exp(inter - m_t)
        sqk = bdot(qb, kb, (((2,), (2,)), ((0,), (0,))))
        sw = sqk * scale * w_intra
        cmat = jnp.stack([cf_ref[...], cbk_ref[...]])
        num = (w_inter * bdot(qb, cmat.astype(BF16), (((2,), (1,)), ((0,), (0,))))
               + bdot(sw.astype(BF16), vb, (((2,), (1,)), ((0,), (0,)))))
        den = (w_inter * jnp.sum(qb.astype(F32) * n, axis=2, keepdims=True)
               + jnp.sum(sw, axis=2, keepdims=True))
        h = num / jnp.maximum(jnp.abs(den), jnp.exp(-m_t))
        b_tot = jnp.sum(lf_row, axis=2, keepdims=True)
        g_col = b_tot - b_col + ii_col
        m_new = jnp.maximum(b_tot + m, jnp.max(g_col, axis=1, keepdims=True))
        decay = jnp.exp(b_tot + m - m_new)
        kw = kb.astype(F32) * scale * jnp.exp(g_col - m_new)
        c_new = decay * cmat + bdot(kw.astype(BF16), vb, (((1,), (1,)), ((0,), (0,))))
        n_new = decay * n + jnp.sum(kw, axis=1, keepdims=True)
        hf_ref[sl_f, :] = h[0]
        hb_ref[sl_b, :] = h[1]
        cf_ref[...] = c_new[0]
        cbk_ref[...] = c_new[1]
        return n_new, m_new

    lax.fori_loop(0, nc, body, (jnp.zeros((2, 1, dk), F32), jnp.zeros((2, 1, 1), F32)))

    def finish(c, carry):
        sl = pl.ds(pl.multiple_of(c * L, L), L)
        hs = hf_ref[sl, :] + hb_ref[sl, :]
        ms = jnp.mean(hs * hs, axis=-1, keepdims=True)
        o_ref[0, sl, :] = (hs * lax.rsqrt(ms + EPS) * og_ref[0]).astype(o_ref.dtype)
        return carry

    lax.fori_loop(0, nc, finish, 0)


def _mlstm_scan(q, k, v, pre, outnorm_g):
    b, s, c = q.shape
    hh = ML_HEADS
    dk = c // hh
    L = min(ML_CHUNK, s)
    nc = s // L
    idx = jnp.array([[0 * 2 * hh + h, 0 * 2 * hh + hh + h, 2 * hh + h, 2 * hh + hh + h] for h in range(hh)])
    g = pre[:, :, idx]
    gcol = g.transpose(0, 2, 1, 3).reshape(b, hh, nc, L, 4)
    grow = gcol.transpose(0, 1, 2, 4, 3)
    head = lambda i, j: (i, 0, j)
    return pl.pallas_call(
        functools.partial(_mlstm_scan_kernel, L=L, nc=nc, dk=dk),
        grid=(b, hh),
        in_specs=[
            pl.BlockSpec((1, s, dk), head),
            pl.BlockSpec((1, s, dk), head),
            pl.BlockSpec((1, s, dk), head),
            pl.BlockSpec((1, 1, nc, L, 4), lambda i, j: (i, j, 0, 0, 0)),
            pl.BlockSpec((1, 1, nc, 4, L), lambda i, j: (i, j, 0, 0, 0)),
            pl.BlockSpec((1, 1, dk), lambda i, j: (j, 0, 0)),
        ],
        out_specs=pl.BlockSpec((1, s, dk), head),
        out_shape=jax.ShapeDtypeStruct((b, s, c), BF16),
        scratch_shapes=[pltpu.VMEM((dk, dk), F32), pltpu.VMEM((dk, dk), F32),
                        pltpu.VMEM((s, dk), F32), pltpu.VMEM((s, dk), F32)],
        compiler_params=_params("parallel", "parallel"),
    )(q, k, v, gcol, grow, outnorm_g.reshape(hh, 1, dk))


def _mlstm_out_kernel(hn_ref, xc_ref, z_ref, skip_ref, w_ref, x_ref, gate_ref, o_ref):
    a = (hn_ref[0].astype(F32) + skip_ref[...] * xc_ref[0].astype(F32)) * _silu(z_ref[0].astype(F32))
    y = jnp.dot(a.astype(BF16), w_ref[...], preferred_element_type=F32)
    o_ref[0] = x_ref[0] + gate_ref[0] * y


def _mlstm_out(hn, xc, up, skip, w, x, gate):
    b, s, c = hn.shape
    d = w.shape[1]
    ts = min(ROW_TILE, s)
    row = lambda i, j: (i, j, 0)
    return pl.pallas_call(
        _mlstm_out_kernel,
        grid=(b, s // ts),
        in_specs=[
            pl.BlockSpec((1, ts, c), row),
            pl.BlockSpec((1, ts, c), row),
            pl.BlockSpec((1, ts, c), lambda i, j: (i, j, 1)),
            pl.BlockSpec((1, c), lambda i, j: (0, 0)),
            pl.BlockSpec((c, d), lambda i, j: (0, 0)),
            pl.BlockSpec((1, ts, d), row),
            pl.BlockSpec((1, 1, d), lambda i, j: (i, 0, 0)),
        ],
        out_specs=pl.BlockSpec((1, ts, d), row),
        out_shape=jax.ShapeDtypeStruct((b, s, d), F32),
        compiler_params=_params("parallel", "parallel"),
    )(hn, xc, up, skip.reshape(1, c), w, x, gate)


def _router_kernel(x_ref, g_ref, sc_ref, sh_ref, w_ref, lpc_ref, lpr_ref, wgt_ref, stats_ref, cnt_ref, run_ref):
    first = (pl.program_id(0) == 0) & (pl.program_id(1) == 0)

    @pl.when(first)
    def _():
        run_ref[...] = jnp.zeros_like(run_ref)

    hf = _modulated_norm(x_ref[0], g_ref[...], sc_ref[0], sh_ref[0])
    w = w_ref[...]
    w_hi = w.astype(BF16)
    w_lo = (w - w_hi.astype(F32)).astype(BF16)
    hf_hi = hf.astype(BF16)
    hf_lo = (hf - hf_hi.astype(F32)).astype(BF16)
    logits = (jnp.dot(hf_hi, w_hi, preferred_element_type=F32)
              + (jnp.dot(hf_hi, w_lo, preferred_element_type=F32)
                 + jnp.dot(hf_lo, w_hi, preferred_element_type=F32)))
    tm = logits.shape[0]
    lane = lax.broadcasted_iota(jnp.int32, logits.shape, 1).astype(F32)
    neg = -jnp.inf
    is_g = lane < N_GROUPS
    gl = jnp.where(is_g, logits, neg)
    gmax = jnp.max(gl, axis=-1, keepdims=True)
    g_sel = jnp.min(jnp.where(gl == gmax, lane, float(LANES)), axis=-1, keepdims=True)
    p_g = 1.0 / jnp.sum(jnp.where(is_g, jnp.exp(gl - gmax), 0.0), axis=-1, keepdims=True)
    e_lane = lane - N_GROUPS
    in_grp = (e_lane >= g_sel * EXPERTS_PER_GROUP) & (e_lane < (g_sel + 1) * EXPERTS_PER_GROUP)
    el = jnp.where(in_grp, logits, neg)
    emax = jnp.max(el, axis=-1, keepdims=True)
    i1 = jnp.min(jnp.where(el == emax, e_lane, float(LANES)), axis=-1, keepdims=True)
    el2 = jnp.where(e_lane == i1, neg, el)
    emax2 = jnp.max(el2, axis=-1, keepdims=True)
    i2 = jnp.min(jnp.where(el2 == emax2, e_lane, float(LANES)), axis=-1, keepdims=True)
    t2 = jnp.exp(emax2 - emax)
    w1 = p_g / (1.0 + t2)
    w2 = p_g * t2 / (1.0 + t2)
    a = jnp.where(e_lane == i1, 1.0, jnp.where(e_lane == i2, 1.0, 0.0))
    r = lax.broadcasted_iota(jnp.int32, (tm, tm), 0)
    cc = lax.broadcasted_iota(jnp.int32, (tm, tm), 1)
    tri = jnp.where(cc < r, 1.0, 0.0).astype(BF16)
    rank = jnp.dot(tri, a.astype(BF16), preferred_element_type=F32)
    k8 = jnp.ceil(jnp.sum(a, axis=0, keepdims=True) * (1.0 / ROW_GROUP))
    ur = lax.broadcasted_iota(jnp.int32, (LANES, LANES), 0)
    uc = lax.broadcasted_iota(jnp.int32, (LANES, LANES), 1)
    upper = jnp.where(ur < uc, 1.0, 0.0).astype(BF16)
    k8_rows = jnp.broadcast_to(k8, (SUBLANES, LANES)).astype(BF16)
    off = jnp.dot(k8_rows, upper, preferred_element_type=F32)[0:1] * ROW_GROUP
    pos = off + rank
    lp0 = jnp.sum(jnp.where(e_lane == i1, pos, 0.0), axis=-1, keepdims=True)
    lp1 = jnp.sum(jnp.where(e_lane == i2, pos, 0.0), axis=-1, keepdims=True)
    run_old = run_ref[...]
    run_new = run_old + k8 * ROW_GROUP
    run_ref[...] = run_new
    cnt_ref[...] = run_new.astype(jnp.int32)
    srow = lax.broadcasted_iota(jnp.int32, (SUBLANES, LANES), 0)
    stats_ref[0] = jnp.where(srow == 0, k8, jnp.where(srow == 1, off, jnp.where(
        srow == 2, run_old, 0.0))).astype(jnp.int32)
    lp = jnp.where(lane == 0, lp0, jnp.where(lane == 1, lp1, 0.0))
    lpc_ref[0] = lp.astype(jnp.int32)
    lpr_ref[0, 0] = lp.T[:SUBLANES].astype(jnp.int32)
    wgt_ref[0] = jnp.where(lane == 0, w1, jnp.where(lane == 1, w2, 0.0))


def _router(x, g, sc, sh, w_group, w_router):
    b, s, d = x.shape
    ts = min(MOE_TOKEN_TILE, s)
    nt = s // ts
    w = jnp.concatenate([w_group, w_router], axis=1)
    w = jnp.pad(w, ((0, 0), (0, LANES - w.shape[1])))
    row = lambda i, j: (i, j, 0)
    return pl.pallas_call(
        _router_kernel,
        grid=(b, nt),
        in_specs=[
            pl.BlockSpec((1, ts, d), row),
            pl.BlockSpec((1, d), lambda i, j: (0, 0)),
            pl.BlockSpec((1, 1, d), lambda i, j: (i, 0, 0)),
            pl.BlockSpec((1, 1, d), lambda i, j: (i, 0, 0)),
            pl.BlockSpec((d, LANES), lambda i, j: (0, 0)),
        ],
        out_specs=[
            pl.BlockSpec((1, ts, LANES), row),
            pl.BlockSpec((1, 1, SUBLANES, ts), lambda i, j: (i, j, 0, 0)),
            pl.BlockSpec((1, ts, LANES), row),
            pl.BlockSpec((1, SUBLANES, LANES), lambda i, j: (i * nt + j, 0, 0)),
            pl.BlockSpec((1, LANES), lambda i, j: (0, 0)),
        ],
        out_shape=[jax.ShapeDtypeStruct((b, s, LANES), jnp.int32),
                   jax.ShapeDtypeStruct((b, nt, SUBLANES, ts), jnp.int32),
                   jax.ShapeDtypeStruct((b, s, LANES), F32),
                   jax.ShapeDtypeStruct((b * nt, SUBLANES, LANES), jnp.int32),
                   jax.ShapeDtypeStruct((1, LANES), jnp.int32)],
        scratch_shapes=[pltpu.VMEM((1, LANES), F32)],
        compiler_params=_params("arbitrary", "arbitrary"),
    )(x, g.reshape(1, d), sc, sh, w)


def _chunk_copies(meta_ref, blk, local_ref, hbm_ref, sem, *, to_hbm, start):
    def piece(lo, hi, rows):
        loc = local_ref.at[pl.ds(pl.multiple_of(lo, ROW_GROUP), rows)]
        far = hbm_ref.at[pl.ds(pl.multiple_of(hi, ROW_GROUP), rows)]
        cp = pltpu.make_async_copy(loc, far, sem) if to_hbm else pltpu.make_async_copy(far, loc, sem)
        if start:
            cp.start()
        else:
            cp.wait()

    def per_expert(e, carry):
        at = blk * LANES + e
        k = meta_ref[at]
        lo = meta_ref[at + N_EXPERTS]
        hi = meta_ref[at + 2 * N_EXPERTS]
        pairs = k >> 1

        def pair(i, c):
            piece(lo + i * (2 * ROW_GROUP), hi + i * (2 * ROW_GROUP), 2 * ROW_GROUP)
            return c

        lax.fori_loop(0, pairs, pair, 0)

        @pl.when((k & 1) == 1)
        def _():
            piece(lo + pairs * (2 * ROW_GROUP), hi + pairs * (2 * ROW_GROUP), ROW_GROUP)

        return carry

    lax.fori_loop(0, N_EXPERTS, per_expert, 0)


def _dispatch_kernel(last_ref, meta_ref, lpr_ref, x_ref, g_ref, sc_ref, sh_ref, xs_ref, loc_ref, zero_ref,
                     sem, zsem):
    tm = x_ref.shape[1]
    zrows = zero_ref.shape[0]
    blk = pl.program_id(0) * pl.num_programs(1) + pl.program_id(1)
    n_blk = pl.num_programs(0) * pl.num_programs(1)
    slot = blk % 2

    @pl.when((pl.program_id(0) == 0) & (pl.program_id(1) == 0))
    def _():
        zero_ref[...] = jnp.zeros_like(zero_ref)

        def fill(e):
            start = pl.multiple_of(last_ref[e], zrows)
            return pltpu.make_async_copy(zero_ref, xs_ref.at[pl.ds(start, zrows)], zsem)

        for e in range(N_EXPERTS):
            @pl.when(last_ref[e] >= 0)
            def _():
                fill(e).start()
        for e in range(N_EXPERTS):
            @pl.when(last_ref[e] >= 0)
            def _():
                fill(e).wait()

        def tail(k):
            start = pl.multiple_of(k * zrows, zrows)
            return pltpu.make_async_copy(zero_ref, xs_ref.at[pl.ds(start, zrows)], zsem)

        n_used = last_ref[N_EXPERTS]
        n_tiles = xs_ref.shape[0] // zrows
        lax.fori_loop(n_used, n_tiles, lambda k, c: (tail(k).start(), c)[1], 0)
        lax.fori_loop(n_used, n_tiles, lambda k, c: (tail(k).wait(), c)[1], 0)

    hf = _modulated_norm(x_ref[0], g_ref[...], sc_ref[0], sh_ref[0]).astype(BF16)
    rows = lax.broadcasted_iota(jnp.int32, (loc_ref.shape[1], tm), 0)
    sel = jnp.where(rows == lpr_ref[0, 0, 0:1, :], 1.0,
                    jnp.where(rows == lpr_ref[0, 0, 1:2, :], 1.0, 0.0)).astype(BF16)
    loc_ref[slot] = jnp.dot(sel, hf, preferred_element_type=F32).astype(BF16)

    def copies(b, sl, start):
        _chunk_copies(meta_ref, b, loc_ref.at[sl], xs_ref, sem.at[sl], to_hbm=True, start=start)

    copies(blk, slot, True)

    @pl.when(blk > 0)
    def _():
        copies(blk - 1, 1 - slot, False)

    @pl.when(blk == n_blk - 1)
    def _():
        copies(blk, slot, False)


def _local_rows(tm):
    return 2 * tm + N_EXPERTS * ROW_GROUP


def _dispatch(x, g, sc, sh, meta, lpr, last_tile_row, n_rows):
    b, s, d = x.shape
    tm = min(MOE_TOKEN_TILE, s)
    nt = s // tm
    row = lambda i, j, last, meta: (i, j, 0)
    grid_spec = pltpu.PrefetchScalarGridSpec(
        num_scalar_prefetch=2,
        grid=(b, nt),
        in_specs=[
            pl.BlockSpec((1, 1, SUBLANES, tm), lambda i, j, last, meta: (i, j, 0, 0)),
            pl.BlockSpec((1, tm, d), row),
            pl.BlockSpec((1, d), lambda i, j, last, meta: (0, 0)),
            pl.BlockSpec((1, 1, d), lambda i, j, last, meta: (i, 0, 0)),
            pl.BlockSpec((1, 1, d), lambda i, j, last, meta: (i, 0, 0)),
        ],
        out_specs=pl.BlockSpec(memory_space=pl.ANY),
        scratch_shapes=[pltpu.VMEM((2, _local_rows(tm), d), BF16), pltpu.VMEM((MOE_ROW_TILE, d), BF16),
                        pltpu.SemaphoreType.DMA((2,)), pltpu.SemaphoreType.DMA(())],
    )
    return pl.pallas_call(
        _dispatch_kernel,
        grid_spec=grid_spec,
        out_shape=jax.ShapeDtypeStruct((n_rows, d), BF16),
        compiler_params=_params("arbitrary", "arbitrary"),
    )(last_tile_row, meta, lpr, x, g.reshape(1, d), sc, sh)


def _expert_kernel(te_ref, nu_ref, xs_ref, w1_ref, w3_ref, w2_ref, ys_ref, w1b_ref, w3b_ref, w2b_ref):
    i = pl.program_id(0)

    @pl.when((i == 0) | (te_ref[i] != te_ref[jnp.maximum(i - 1, 0)]))
    def _():
        w1b_ref[...] = w1_ref[0, 0].astype(BF16)
        w3b_ref[...] = w3_ref[0, 0].astype(BF16)
        w2b_ref[...] = w2_ref[0, 0].astype(BF16)

    @pl.when(i < nu_ref[0])
    def _():
        xb = xs_ref[...]
        a = jnp.dot(xb, w1b_ref[...], preferred_element_type=F32)
        bb = jnp.dot(xb, w3b_ref[...], preferred_element_type=F32)
        hmid = (_silu(a) * bb).astype(BF16)
        ys_ref[...] = jnp.dot(hmid, w2b_ref[...], preferred_element_type=F32).astype(ys_ref.dtype)

    @pl.when(i >= nu_ref[0])
    def _():
        ys_ref[...] = jnp.zeros_like(ys_ref)


def _expert_ffn(xs, tile_expert, n_used, w1, w3, w2, layer):
    p, d = xs.shape
    de = w1.shape[3]
    tm = MOE_ROW_TILE
    grid_spec = pltpu.PrefetchScalarGridSpec(
        num_scalar_prefetch=2,
        grid=(p // tm,),
        in_specs=[
            pl.BlockSpec((tm, d), lambda i, te, nu: (jnp.maximum(jnp.minimum(i, nu[0] - 1), 0), 0)),
            pl.BlockSpec((1, 1, d, de), lambda i, te, nu: (layer, te[i], 0, 0)),
            pl.BlockSpec((1, 1, d, de), lambda i, te, nu: (layer, te[i], 0, 0)),
            pl.BlockSpec((1, 1, de, d), lambda i, te, nu: (layer, te[i], 0, 0)),
        ],
        out_specs=pl.BlockSpec((tm, d), lambda i, te, nu: (i, 0)),
        scratch_shapes=[pltpu.VMEM((d, de), BF16), pltpu.VMEM((d, de), BF16), pltpu.VMEM((de, d), BF16)],
    )
    return pl.pallas_call(
        _expert_kernel,
        grid_spec=grid_spec,
        out_shape=jax.ShapeDtypeStruct((p, d), BF16),
        compiler_params=_params("arbitrary"),
    )(tile_expert, n_used, xs, w1, w3, w2)


def _combine_kernel(meta_ref, ys_ref, lpc_ref, wgt_ref, x_ref, gate_ref, o_ref, loc_ref, sem):
    tm = x_ref.shape[1]
    n_loc = loc_ref.shape[1]
    blk = pl.program_id(0) * pl.num_programs(1) + pl.program_id(1)
    n_blk = pl.num_programs(0) * pl.num_programs(1)
    slot = blk % 2

    def fetch(b, sl):
        loc_ref[sl, 2 * tm:, :] = jnp.zeros((n_loc - 2 * tm, loc_ref.shape[2]), BF16)
        _chunk_copies(meta_ref, b, loc_ref.at[sl], ys_ref, sem.at[sl], to_hbm=False, start=True)

    @pl.when(blk == 0)
    def _():
        fetch(blk, slot)

    @pl.when(blk + 1 < n_blk)
    def _():
        fetch(blk + 1, 1 - slot)

    _chunk_copies(meta_ref, blk, loc_ref.at[slot], ys_ref, sem.at[slot], to_hbm=False, start=False)
    cols = lax.broadcasted_iota(jnp.int32, (tm, n_loc), 1)
    lp = lpc_ref[0]
    wgt = wgt_ref[0]
    wm = jnp.where(cols == lp[:, 0:1], wgt[:, 0:1], jnp.where(cols == lp[:, 1:2], wgt[:, 1:2], 0.0))
    w_hi = wm.astype(BF16)
    w_lo = (wm - w_hi.astype(F32)).astype(BF16)
    yb = loc_ref[slot]
    y = jnp.dot(w_hi, yb, preferred_element_type=F32) + jnp.dot(w_lo, yb, preferred_element_type=F32)
    o_ref[0] = x_ref[0] + gate_ref[0] * y


def _combine(ys, meta, lpc, wgt, x, gate):
    b, s, d = x.shape
    tm = min(MOE_TOKEN_TILE, s)
    nt = s // tm
    row = lambda i, j, meta: (i, j, 0)
    grid_spec = pltpu.PrefetchScalarGridSpec(
        num_scalar_prefetch=1,
        grid=(b, nt),
        in_specs=[
            pl.BlockSpec(memory_space=pl.ANY),
            pl.BlockSpec((1, tm, LANES), row),
            pl.BlockSpec((1, tm, LANES), row),
            pl.BlockSpec((1, tm, d), row),
            pl.BlockSpec((1, 1, d), lambda i, j, meta: (i, 0, 0)),
        ],
        out_specs=pl.BlockSpec((1, tm, d), row),
        scratch_shapes=[pltpu.VMEM((2, _local_rows(tm), d), BF16), pltpu.SemaphoreType.DMA((2,))],
    )
    return pl.pallas_call(
        _combine_kernel,
        grid_spec=grid_spec,
        out_shape=jax.ShapeDtypeStruct((b, s, d), F32),
        compiler_params=_params("arbitrary", "arbitrary"),
    )(meta, ys, lpc, wgt, x, gate)


def _hier_moe_residual(x, g, sc, sh, gate, w_group, w_router, w1, w3, w2, layer):
    b, s, d = x.shape
    t = b * s
    lpc, lpr, wgt, stats, cnt = _router(x, g, sc, sh, w_group, w_router)
    tm = MOE_ROW_TILE
    ex = slice(N_GROUPS, N_GROUPS + N_EXPERTS)
    counts = cnt[0, ex]
    tiles = (counts + tm - 1) // tm
    tile_end = jnp.cumsum(tiles)
    base = (tile_end - tiles) * tm
    n_blocks = stats.shape[0]
    n_tiles = (2 * t + n_blocks * N_EXPERTS * ROW_GROUP) // tm + N_EXPERTS
    tile_expert = jnp.minimum(
        jnp.sum(tile_end[None, :] <= jnp.arange(n_tiles)[:, None], axis=1), N_EXPERTS - 1).astype(jnp.int32)
    n_used = tile_end[-1:].astype(jnp.int32)
    meta = jnp.concatenate([stats[:, 0, ex], stats[:, 1, ex], base[None, :] + stats[:, 2, ex],
                            jnp.zeros((n_blocks, LANES - 3 * N_EXPERTS), jnp.int32)], axis=1)
    meta = meta.astype(jnp.int32).reshape(n_blocks * LANES)
    last_tile_row = jnp.where(tiles > 0, (tile_end - 1) * tm, -1).astype(jnp.int32)
    last_tile_row = jnp.concatenate([last_tile_row, n_used])
    xs = _dispatch(x, g, sc, sh, meta, lpr, last_tile_row, n_tiles * tm)
    ys = _expert_ffn(xs, tile_expert, n_used, w1, w3, w2, layer)
    return _combine(ys, meta, lpc, wgt, x, gate)


def kernel(x, c, rel_table, ada_w, ada_b, norm_mix_g, norm_ffn_g, da_w_in, da_w_out, da_q_gain, da_k_gain, da_lam_q1, da_lam_k1, da_lam_q2, da_lam_k2, da_subln_g, ml_w_in, ml_conv_w, ml_conv_b, ml_wq, ml_wk, ml_wv, ml_gate_w, ml_gate_b, ml_outnorm_g, ml_skip, ml_w_out, moe_w_group, moe_w_router, moe_w1, moe_w3, moe_w2):
    depth = ada_w.shape[0]
    d = x.shape[-1]
    mod = _ada_mod(c, ada_w, ada_b)
    for i in range(depth):
        sh1, sc1, g1, sh2, sc2, g2 = [mod[i, :, None, k * d:(k + 1) * d] for k in range(6)]
        j = i // N_MIXERS
        if i % N_MIXERS == 0:
            qkv = _prenorm_matmul(x, norm_mix_g[i], sc1, sh1, da_w_in[j].astype(BF16),
                                  head_gain=_qk_head_gains(da_q_gain[j], da_k_gain[j]), half=DA_HEAD_DIM)
            lam_init = 0.8 - 0.6 * math.exp(-0.3 * i)
            lam = (jnp.exp(jnp.sum(da_lam_q1[j] * da_lam_k1[j])) - jnp.exp(jnp.sum(da_lam_q2[j] * da_lam_k2[j]))
                   + lam_init).astype(F32)
            o = _diff_attention(qkv, da_subln_g[j], lam, lam_init, rel_table)
            x = _proj_residual(o, da_w_out[j].astype(BF16), x, g1)
        else:
            up = _prenorm_matmul(x, norm_mix_g[i], sc1, sh1, ml_w_in[j].astype(BF16))
            q, k, v, xc, pre = _mlstm_pre(up, ml_conv_w[j], ml_conv_b[j], ml_wq[j], ml_wk[j], ml_wv[j],
                                          ml_gate_w[j], ml_gate_b[j])
            hn = _mlstm_scan(q, k, v, pre, ml_outnorm_g[j])
            x = _mlstm_out(hn, xc, up, ml_skip[j], ml_w_out[j].astype(BF16), x, g1)
        x = _hier_moe_residual(x, norm_ffn_g[i], sc2, sh2, g2, moe_w_group[i], moe_w_router[i],
                               moe_w1, moe_w3, moe_w2, i)
    return x
```

```python
import functools
import math

import jax
import jax.numpy as jnp
from jax import lax
from jax.experimental import pallas as pl
from jax.experimental.pallas import tpu as pltpu

EPS = 1e-6
N_MIXERS = 2
DA_HEADS = 8
DA_HEAD_DIM = 64
N_BUCKETS = 32
MAX_DISTANCE = 128
ML_HEADS = 4
ML_CHUNK = 256
ML_QKV_BLOCK = 4
N_GROUPS = 4
EXPERTS_PER_GROUP = 8
N_EXPERTS = N_GROUPS * EXPERTS_PER_GROUP

LANES = 128
SUBLANES = 8
ROW_GROUP = 16
VMEM_LIMIT = 48 * 1024 * 1024
ATTN_Q_TILE = 2048
ATTN_ROW_BLOCK = 128
ROW_TILE = 512
MOE_ROW_TILE = 512
MOE_TOKEN_TILE = 512
MLSTM_COL_TILE = 256
MLSTM_ROW_CHUNK = 512

F32 = jnp.float32
BF16 = jnp.bfloat16
LOG2E = math.log2(math.e)


def _params(*sem):
    return pltpu.CompilerParams(dimension_semantics=sem, vmem_limit_bytes=VMEM_LIMIT)


def _silu(x):
    return x / (1.0 + jnp.exp(-x))


def _log_sigmoid(x):
    return jnp.minimum(x, 0.0) - jnp.log(1.0 + jnp.exp(-jnp.abs(x)))


def _modulated_norm(x, g, sc, sh):
    ms = jnp.mean(x * x, axis=-1, keepdims=True)
    return x * lax.rsqrt(ms + EPS) * g * (1.0 + sc) + sh


def _ada_kernel(c_ref, w_ref, b_ref, o_ref):
    ca = _silu(c_ref[...]).astype(BF16)
    o_ref[0] = jnp.dot(ca, w_ref[0].astype(BF16), preferred_element_type=F32) + b_ref[0]


def _ada_mod(c, ada_w, ada_b):
    depth, d, n = ada_w.shape
    b = c.shape[0]
    tn = 1536
    return pl.pallas_call(
        _ada_kernel,
        grid=(depth, n // tn),
        in_specs=[
            pl.BlockSpec((b, d), lambda i, j: (0, 0)),
            pl.BlockSpec((1, d, tn), lambda i, j: (i, 0, j)),
            pl.BlockSpec((1, 1, tn), lambda i, j: (i, 0, j)),
        ],
        out_specs=pl.BlockSpec((1, b, tn), lambda i, j: (i, 0, j)),
        out_shape=jax.ShapeDtypeStruct((depth, b, n), F32),
        compiler_params=_params("parallel", "parallel"),
    )(c, ada_w, ada_b.reshape(depth, 1, n))


def _prenorm_matmul_kernel(x_ref, g_ref, sc_ref, sh_ref, w_ref, hg_ref, o_ref, *, n_chunk, n_norm, half):
    y = _modulated_norm(x_ref[0], g_ref[...], sc_ref[0], sh_ref[0]).astype(BF16)
    n = w_ref.shape[1]
    for n0 in range(0, n, n_chunk):
        acc = jnp.dot(y, w_ref[:, n0:n0 + n_chunk], preferred_element_type=F32)
        if n0 < n_norm:
            for c0 in range(n0, n0 + n_chunk, 2 * half):
                blk = _group_rms(acc[:, c0 - n0:c0 - n0 + 2 * half], hg_ref[:, c0:c0 + 2 * half], half)
                o_ref[0, :, c0:c0 + 2 * half] = blk.astype(o_ref.dtype)
        else:
            o_ref[0, :, n0:n0 + n_chunk] = acc.astype(o_ref.dtype)


def _prenorm_matmul(x, g, sc, sh, w, head_gain=None, half=LANES // 2):
    b, s, d = x.shape
    n = w.shape[1]
    ts = min(ROW_TILE, s)
    n_norm = 0 if head_gain is None else head_gain.shape[1]
    hg = jnp.ones((1, n), F32) if head_gain is None else jnp.pad(head_gain, ((0, 0), (0, n - n_norm)))
    return pl.pallas_call(
        functools.partial(_prenorm_matmul_kernel, n_chunk=1024, n_norm=n_norm, half=half),
        grid=(b, s // ts),
        in_specs=[
            pl.BlockSpec((1, ts, d), lambda i, j: (i, j, 0)),
            pl.BlockSpec((1, d), lambda i, j: (0, 0)),
            pl.BlockSpec((1, 1, d), lambda i, j: (i, 0, 0)),
            pl.BlockSpec((1, 1, d), lambda i, j: (i, 0, 0)),
            pl.BlockSpec((d, n), lambda i, j: (0, 0)),
            pl.BlockSpec((1, n), lambda i, j: (0, 0)),
        ],
        out_specs=pl.BlockSpec((1, ts, n), lambda i, j: (i, j, 0)),
        out_shape=jax.ShapeDtypeStruct((b, s, n), BF16),
        compiler_params=_params("parallel", "parallel"),
    )(x, g.reshape(1, d), sc, sh, w, hg)


def _proj_residual_kernel(a_ref, w_ref, x_ref, gate_ref, o_ref):
    y = jnp.dot(a_ref[0], w_ref[...], preferred_element_type=F32)
    o_ref[0] = x_ref[0] + gate_ref[0] * y


def _proj_residual(a, w, x, gate):
    b, s, k = a.shape
    d = w.shape[1]
    ts = min(ROW_TILE, s)
    return pl.pallas_call(
        _proj_residual_kernel,
        grid=(b, s // ts),
        in_specs=[
            pl.BlockSpec((1, ts, k), lambda i, j: (i, j, 0)),
            pl.BlockSpec((k, d), lambda i, j: (0, 0)),
            pl.BlockSpec((1, ts, d), lambda i, j: (i, j, 0)),
            pl.BlockSpec((1, 1, d), lambda i, j: (i, 0, 0)),
        ],
        out_specs=pl.BlockSpec((1, ts, d), lambda i, j: (i, j, 0)),
        out_shape=jax.ShapeDtypeStruct((b, s, d), F32),
        compiler_params=_params("parallel", "parallel"),
    )(a, w, x, gate)


def _t5_bucket(rel):
    nb = N_BUCKETS // 2
    max_exact = nb // 2
    ret = jnp.where(rel > 0, nb, 0)
    n = jnp.abs(rel)
    nf = jnp.maximum(n, 1).astype(F32)
    large = max_exact + (jnp.log(nf / max_exact) / math.log(MAX_DISTANCE / max_exact)
                         * (nb - max_exact)).astype(jnp.int32)
    large = jnp.minimum(large, nb - 1)
    return ret + jnp.where(n < max_exact, n, large)


def _bias_band_kernel(t_ref, o_ref, *, tq, nb):
    u = jnp.broadcast_to(t_ref[0], (tq, t_ref.shape[2]))
    y = pltpu.roll(u, 1, 1, stride=1, stride_axis=0)
    for jb in range(nb):
        o_ref[0, jb] = y[:, tq + jb * LANES:tq + (jb + 1) * LANES]


def _bias_band(rel_table, s, tq):
    h = rel_table.shape[1]
    delta = jnp.arange(-(s - 1), s)
    t = rel_table[_t5_bucket(delta)].T.astype(F32) * LOG2E
    u = jnp.concatenate([t, jnp.zeros((h, 1), F32)], axis=1).reshape(h, 1, 2 * s)
    nb = (2 * s - tq) // LANES
    return pl.pallas_call(
        functools.partial(_bias_band_kernel, tq=tq, nb=nb),
        grid=(h,),
        in_specs=[pl.BlockSpec((1, 1, 2 * s), lambda i: (i, 0, 0))],
        out_specs=pl.BlockSpec((1, nb, tq, LANES), lambda i: (i, 0, 0, 0)),
        out_shape=jax.ShapeDtypeStruct((h, nb, tq, LANES), F32),
        compiler_params=_params("parallel"),
    )(u)


def _group_rms(x, gain, half):
    sq = x * x
    lane = lax.broadcasted_iota(jnp.int32, x.shape, 1)
    lo = lane < half
    s_lo = jnp.sum(jnp.where(lo, sq, 0.0), axis=-1, keepdims=True)
    s_all = jnp.sum(sq, axis=-1, keepdims=True)
    ms = jnp.where(lo, s_lo, s_all - s_lo) * (1.0 / half)
    return x * lax.rsqrt(ms + EPS) * gain


def _diff_attn_kernel(lam_ref, q_ref, k_ref, v_ref, sg_ref, band_ref, o_ref, ve_ref, *, tq, s, dh, out_scale):
    qi = pl.program_id(2)
    nq = pl.num_programs(2)
    dv = 2 * dh

    @pl.when(qi == 0)
    def _():
        ve_ref[:, :dv] = v_ref[0]
        ve_ref[:, dv:] = jnp.ones((s, dv), BF16)

    lam = lam_ref[0]
    q = q_ref[0]
    lane = lax.broadcasted_iota(jnp.int32, q.shape, 1)
    zero = jnp.zeros_like(q)
    q0 = jnp.where(lane < dh, q, zero)
    q1 = jnp.where(lane < dh, zero, q)
    jb0 = (nq - 1 - qi) * (tq // LANES)
    nkb = s // LANES

    def softmax_av(qm, r0, r1):
        sc = lax.dot_general(qm[r0:r1], k_ref[0], (((1,), (1,)), ((), ())), preferred_element_type=F32)
        sc = sc + jnp.concatenate([band_ref[0, jb0 + kb, r0:r1, :] for kb in range(nkb)], axis=1)
        p = jnp.exp2(sc - jnp.max(sc, axis=-1, keepdims=True)).astype(BF16)
        oe = jnp.dot(p, ve_ref[...], preferred_element_type=F32)
        return oe[:, :dv], oe[:, dv:dv + 1]

    rb = ATTN_ROW_BLOCK
    for r0 in range(0, tq, rb):
        n0, l0 = softmax_av(q0, r0, r0 + rb)
        n1, l1 = softmax_av(q1, r0, r0 + rb)
        o = n0 * (1.0 / l0) - n1 * (lam / l1)
        ms = jnp.mean(o * o, axis=-1, keepdims=True)
        o_ref[0, r0:r0 + rb, :] = (o * lax.rsqrt(ms + EPS) * sg_ref[...] * out_scale).astype(o_ref.dtype)


def _qk_head_gains(q_gain, k_gain):
    h, dh = DA_HEADS, DA_HEAD_DIM
    qg = jnp.tile(q_gain * (dh ** -0.5 * LOG2E), 2 * h)
    kg = jnp.tile(k_gain, 2 * h)
    return jnp.concatenate([qg, kg]).reshape(1, 4 * h * dh).astype(F32)


def _diff_attention(qkv, subln_g, lam, lam_init, rel_table):
    b, s, _ = qkv.shape
    h, dh = DA_HEADS, DA_HEAD_DIM
    tq = min(ATTN_Q_TILE, s)
    band = _bias_band(rel_table, s, tq)
    nb = band.shape[1]
    kern = functools.partial(_diff_attn_kernel, tq=tq, s=s, dh=dh, out_scale=1.0 - lam_init)
    grid_spec = pltpu.PrefetchScalarGridSpec(
        num_scalar_prefetch=1,
        grid=(h, b, s // tq),
        in_specs=[
            pl.BlockSpec((1, tq, 2 * dh), lambda hi, bi, qi, lam: (bi, qi, hi)),
            pl.BlockSpec((1, s, 2 * dh), lambda hi, bi, qi, lam: (bi, 0, h + hi)),
            pl.BlockSpec((1, s, 2 * dh), lambda hi, bi, qi, lam: (bi, 0, 2 * h + hi)),
            pl.BlockSpec((1, 2 * dh), lambda hi, bi, qi, lam: (0, 0)),
            pl.BlockSpec((1, nb, tq, LANES), lambda hi, bi, qi, lam: (hi, 0, 0, 0),
                         pipeline_mode=pl.Buffered(1)),
        ],
        out_specs=pl.BlockSpec((1, tq, 2 * dh), lambda hi, bi, qi, lam: (bi, qi, hi)),
        scratch_shapes=[pltpu.VMEM((s, 4 * dh), BF16)],
    )
    return pl.pallas_call(
        kern,
        grid_spec=grid_spec,
        out_shape=jax.ShapeDtypeStruct((b, s, h * 2 * dh), BF16),
        compiler_params=_params("parallel", "parallel", "arbitrary"),
    )(lam.reshape(1), qkv, qkv, qkv, subln_g.reshape(1, 2 * dh), band)


def _mlstm_pre_kernel(xm_ref, cw_ref, cb_ref, wq_ref, wk_ref, wv_ref, gq_ref, gk_ref, gv_ref, gb_ref,
                      q_ref, k_ref, v_ref, xc_ref, pre_ref, pad_ref, *, s, halo, n_heads):
    j = pl.program_id(1)
    xm = xm_ref[0].astype(F32)
    cbw = xm.shape[1]
    pad_ref[0:8, :] = jnp.zeros((8, cbw), F32)
    pad_ref[8 + s:16 + s, :] = jnp.zeros((8, cbw), F32)
    pad_ref[8:8 + s, :] = xm

    @pl.when(j == 0)
    def _():
        pre_ref[0] = jnp.broadcast_to(gb_ref[...], pre_ref.shape[1:])

    rc = min(MLSTM_ROW_CHUNK, s)
    for r0 in range(0, s, rc):
        rows = slice(r0, r0 + rc)
        acc = cb_ref[...] + cw_ref[0:1, :] * pad_ref[8 - halo + r0:8 - halo + r0 + rc, :]
        for t in range(1, 2 * halo + 1):
            acc = acc + cw_ref[t:t + 1, :] * pad_ref[8 - halo + t + r0:8 - halo + t + r0 + rc, :]
        xcb = _silu(acc).astype(BF16)
        q = jnp.dot(xcb, wq_ref[0], preferred_element_type=F32).astype(BF16)
        k = jnp.dot(xcb, wk_ref[0], preferred_element_type=F32).astype(BF16)
        v = jnp.dot(xm_ref[0, rows, :], wv_ref[0], preferred_element_type=F32).astype(BF16)
        q_ref[0, rows, :] = q
        k_ref[0, rows, :] = k
        v_ref[0, rows, :] = v
        xc_ref[0, rows, :] = xcb
        pre_ref[0, rows, :] += (jnp.dot(q, gq_ref[...], preferred_element_type=F32)
                                + jnp.dot(k, gk_ref[...], preferred_element_type=F32)
                                + jnp.dot(v, gv_ref[...], preferred_element_type=F32))

    @pl.when(j == pl.num_programs(1) - 1)
    def _():
        pre = pre_ref[0]
        lane = lax.broadcasted_iota(jnp.int32, pre.shape, 1)
        is_forget = (lane % (2 * n_heads)) >= n_heads
        pre_ref[0] = jnp.where(is_forget, _log_sigmoid(pre), pre)


def _blockdiag_dense(w, cb):
    nblk, blk, _ = w.shape
    per = cb // blk
    w4 = w.reshape(nblk // per, per, blk, blk)
    eye = jnp.eye(per, dtype=w.dtype)
    return jnp.einsum('jnio,nm->jnimo', w4, eye).reshape(nblk // per, cb, cb).astype(BF16)


def _mlstm_pre(up, conv_w, conv_b, wq, wk, wv, gate_w, gate_b):
    b, s, c2 = up.shape
    c = c2 // 2
    cb = MLSTM_COL_TILE
    ncb = c // cb
    kw = conv_w.shape[0]
    ng = gate_w.shape[0] * gate_w.shape[-1]

    def gate_mat(i):
        g = gate_w[:, i].transpose(1, 0, 2).reshape(c, ng)
        return jnp.pad(g, ((0, 0), (0, LANES - ng))).astype(BF16)

    gb = jnp.pad(gate_b.reshape(1, ng), ((0, 0), (0, LANES - ng)))
    col = lambda i, j: (i, 0, j)
    out_bf = jax.ShapeDtypeStruct((b, s, c), BF16)
    return pl.pallas_call(
        functools.partial(_mlstm_pre_kernel, s=s, halo=kw // 2, n_heads=gate_w.shape[-1] // 2),
        grid=(b, ncb),
        in_specs=[
            pl.BlockSpec((1, s, cb), col),
            pl.BlockSpec((kw, cb), lambda i, j: (0, j)),
            pl.BlockSpec((1, cb), lambda i, j: (0, j)),
            pl.BlockSpec((1, cb, cb), lambda i, j: (j, 0, 0)),
            pl.BlockSpec((1, cb, cb), lambda i, j: (j, 0, 0)),
            pl.BlockSpec((1, cb, cb), lambda i, j: (j, 0, 0)),
            pl.BlockSpec((cb, LANES), lambda i, j: (j, 0)),
            pl.BlockSpec((cb, LANES), lambda i, j: (j, 0)),
            pl.BlockSpec((cb, LANES), lambda i, j: (j, 0)),
            pl.BlockSpec((1, LANES), lambda i, j: (0, 0)),
        ],
        out_specs=[
            pl.BlockSpec((1, s, cb), col),
            pl.BlockSpec((1, s, cb), col),
            pl.BlockSpec((1, s, cb), col),
            pl.BlockSpec((1, s, cb), col),
            pl.BlockSpec((1, s, LANES), lambda i, j: (i, 0, 0)),
        ],
        out_shape=[out_bf, out_bf, out_bf, out_bf, jax.ShapeDtypeStruct((b, s, LANES), F32)],
        scratch_shapes=[pltpu.VMEM((s + 16, cb), F32)],
        compiler_params=_params("parallel", "arbitrary"),
    )(up, conv_w, conv_b.reshape(1, c), _blockdiag_dense(wq, cb), _blockdiag_dense(wk, cb),
      _blockdiag_dense(wv, cb), gate_mat(0), gate_mat(1), gate_mat(2), gb)


def _mlstm_scan_kernel(q_ref, k_ref, v_ref, gcol_ref, grow_ref, og_ref, o_ref,
                       cf_ref, cbk_ref, hf_ref, hb_ref, *, L, nc, dk):
    scale = dk ** -0.5
    dirn = lax.broadcasted_iota(jnp.int32, (2, L, L), 0)
    row = lax.broadcasted_iota(jnp.int32, (2, L, L), 1)
    coli = lax.broadcasted_iota(jnp.int32, (2, L, L), 2)
    lo = jnp.where(dirn == 0, coli, row)
    hi = jnp.where(dirn == 0, row, coli)
    mask = lo <= hi
    mask_t = hi <= lo
    cf_ref[...] = jnp.zeros_like(cf_ref)
    cbk_ref[...] = jnp.zeros_like(cbk_ref)
    bdot = functools.partial(lax.dot_general, preferred_element_type=F32)

    def body(i, carry):
        n, m = carry
        sl_f = pl.ds(pl.multiple_of(i * L, L), L)
        sl_b = pl.ds(pl.multiple_of((nc - 1 - i) * L, L), L)
        qb = jnp.stack([q_ref[0, sl_f, :], q_ref[0, sl_b, :]])
        kb = jnp.stack([k_ref[0, sl_f, :], k_ref[0, sl_b, :]])
        vb = jnp.stack([v_ref[0, sl_f, :], v_ref[0, sl_b, :]])
        gcf, gcb = gcol_ref[0, 0, i], gcol_ref[0, 0, nc - 1 - i]
        grf, grb = grow_ref[0, 0, i], grow_ref[0, 0, nc - 1 - i]
        ii_col = jnp.stack([gcf[:, 0:1], gcb[:, 2:3]])
        lf_col = jnp.stack([gcf[:, 1:2], gcb[:, 3:4]])
        ii_row = jnp.stack([grf[0:1, :], grb[2:3, :]])
        lf_row = jnp.stack([grf[1:2, :], grb[3:4, :]])
        b_col = jnp.sum(jnp.where(mask, lf_row, 0.0), axis=2, keepdims=True)
        b_row = jnp.sum(jnp.where(mask_t, lf_col, 0.0), axis=1, keepdims=True)
        dmat = jnp.where(mask, b_col - b_row + ii_row, -jnp.inf)
        inter = b_col + m
        m_t = jnp.maximum(inter, jnp.max(dmat, axis=2, keepdims=True))
        w_intra = jnp.exp(dmat - m_t)
        w_inter = jnp.exp(inter - m_t)
        sqk = bdot(qb, kb, (((2,), (2,)), ((0,), (0,))))
        sw = sqk * scale * w_intra
        cmat = jnp.stack([cf_ref[...], cbk_ref[...]])
        num = (w_inter * bdot(qb, cmat.astype(BF16), (((2,), (1,)), ((0,), (0,))))
               + bdot(sw.astype(BF16), vb, (((2,), (1,)), ((0,), (0,)))))
        den = (w_inter * jnp.sum(qb.astype(F32) * n, axis=2, keepdims=True)
               + jnp.sum(sw, axis=2, keepdims=True))
        h = num / jnp.maximum(jnp.abs(den), jnp.exp(-m_t))
        b_tot = jnp.sum(lf_row, axis=2, keepdims=True)
        g_col = b_tot - b_col + ii_col
        m_new = jnp.maximum(b_tot + m, jnp.max(g_col, axis=1, keepdims=True))
        decay = jnp.exp(b_tot + m - m_new)
        kw = kb.astype(F32) * scale * jnp.exp(g_col - m_new)
        c_new = decay * cmat + bdot(kw.astype(BF16), vb, (((1,), (1,)), ((0,), (0,))))
        n_new = decay * n + jnp.sum(kw, axis=1, keepdims=True)
        hf_ref[sl_f, :] = h[0]
        hb_ref[sl_b, :] = h[1]
        cf_ref[...] = c_new[0]
        cbk_ref[...] = c_new[1]
        return n_new, m_new

    lax.fori_loop(0, nc, body, (jnp.zeros((2, 1, dk), F32), jnp.zeros((2, 1, 1), F32)))

    def finish(c, carry):
        sl = pl.ds(pl.multiple_of(c * L, L), L)
        hs = hf_ref[sl, :] + hb_ref[sl, :]
        ms = jnp.mean(hs * hs, axis=-1, keepdims=True)
        o_ref[0, sl, :] = (hs * lax.rsqrt(ms + EPS) * og_ref[0]).astype(o_ref.dtype)
        return carry

    lax.fori_loop(0, nc, finish, 0)


def _mlstm_scan(q, k, v, pre, outnorm_g):
    b, s, c = q.shape
    hh = ML_HEADS
    dk = c // hh
    L = min(ML_CHUNK, s)
    nc = s // L
    idx = jnp.array([[0 * 2 * hh + h, 0 * 2 * hh + hh + h, 2 * hh + h, 2 * hh + hh + h] for h in range(hh)])
    g = pre[:, :, idx]
    gcol = g.transpose(0, 2, 1, 3).reshape(b, hh, nc, L, 4)
    grow = gcol.transpose(0, 1, 2, 4, 3)
    head = lambda i, j: (i, 0, j)
    return pl.pallas_call(
        functools.partial(_mlstm_scan_kernel, L=L, nc=nc, dk=dk),
        grid=(b, hh),
        in_specs=[
            pl.BlockSpec((1, s, dk), head),
            pl.BlockSpec((1, s, dk), head),
            pl.BlockSpec((1, s, dk), head),
            pl.BlockSpec((1, 1, nc, L, 4), lambda i, j: (i, j, 0, 0, 0)),
            pl.BlockSpec((1, 1, nc, 4, L), lambda i, j: (i, j, 0, 0, 0)),
            pl.BlockSpec((1, 1, dk), lambda i, j: (j, 0, 0)),
        ],
        out_specs=pl.BlockSpec((1, s, dk), head),
        out_shape=jax.ShapeDtypeStruct((b, s, c), BF16),
        scratch_shapes=[pltpu.VMEM((dk, dk), F32), pltpu.VMEM((dk, dk), F32),
                        pltpu.VMEM((s, dk), F32), pltpu.VMEM((s, dk), F32)],
        compiler_params=_params("parallel", "parallel"),
    )(q, k, v, gcol, grow, outnorm_g.reshape(hh, 1, dk))


def _mlstm_out_kernel(hn_ref, xc_ref, z_ref, skip_ref, w_ref, x_ref, gate_ref, o_ref):
    a = (hn_ref[0].astype(F32) + skip_ref[...] * xc_ref[0].astype(F32)) * _silu(z_ref[0].astype(F32))
    y = jnp.dot(a.astype(BF16), w_ref[...], preferred_element_type=F32)
    o_ref[0] = x_ref[0] + gate_ref[0] * y


def _mlstm_out(hn, xc, up, skip, w, x, gate):
    b, s, c = hn.shape
    d = w.shape[1]
    ts = min(ROW_TILE, s)
    row = lambda i, j: (i, j, 0)
    return pl.pallas_call(
        _mlstm_out_kernel,
        grid=(b, s // ts),
        in_specs=[
            pl.BlockSpec((1, ts, c), row),
            pl.BlockSpec((1, ts, c), row),
            pl.BlockSpec((1, ts, c), lambda i, j: (i, j, 1)),
            pl.BlockSpec((1, c), lambda i, j: (0, 0)),
            pl.BlockSpec((c, d), lambda i, j: (0, 0)),
            pl.BlockSpec((1, ts, d), row),
            pl.BlockSpec((1, 1, d), lambda i, j: (i, 0, 0)),
        ],
        out_specs=pl.BlockSpec((1, ts, d), row),
        out_shape=jax.ShapeDtypeStruct((b, s, d), F32),
        compiler_params=_params("parallel", "parallel"),
    )(hn, xc, up, skip.reshape(1, c), w, x, gate)


def _router_kernel(x_ref, g_ref, sc_ref, sh_ref, w_ref, lpc_ref, lpr_ref, wr_ref, stats_ref, cnt_ref, run_ref):
    first = (pl.program_id(0) == 0) & (pl.program_id(1) == 0)

    @pl.when(first)
    def _():
        run_ref[...] = jnp.zeros_like(run_ref)

    hf = _modulated_norm(x_ref[0], g_ref[...], sc_ref[0], sh_ref[0])
    w = w_ref[...]
    w_hi = w.astype(BF16)
    w_lo = (w - w_hi.astype(F32)).astype(BF16)
    hf_hi = hf.astype(BF16)
    hf_lo = (hf - hf_hi.astype(F32)).astype(BF16)
    logits = (jnp.dot(hf_hi, w_hi, preferred_element_type=F32)
              + (jnp.dot(hf_hi, w_lo, preferred_element_type=F32)
                 + jnp.dot(hf_lo, w_hi, preferred_element_type=F32)))
    tm = logits.shape[0]
    lane = lax.broadcasted_iota(jnp.int32, logits.shape, 1).astype(F32)
    neg = -jnp.inf
    is_g = lane < N_GROUPS
    gl = jnp.where(is_g, logits, neg)
    gmax = jnp.max(gl, axis=-1, keepdims=True)
    g_sel = jnp.min(jnp.where(gl == gmax, lane, float(LANES)), axis=-1, keepdims=True)
    p_g = 1.0 / jnp.sum(jnp.where(is_g, jnp.exp(gl - gmax), 0.0), axis=-1, keepdims=True)
    e_lane = lane - N_GROUPS
    in_grp = (e_lane >= g_sel * EXPERTS_PER_GROUP) & (e_lane < (g_sel + 1) * EXPERTS_PER_GROUP)
    el = jnp.where(in_grp, logits, neg)
    emax = jnp.max(el, axis=-1, keepdims=True)
    i1 = jnp.min(jnp.where(el == emax, e_lane, float(LANES)), axis=-1, keepdims=True)
    el2 = jnp.where(e_lane == i1, neg, el)
    emax2 = jnp.max(el2, axis=-1, keepdims=True)
    i2 = jnp.min(jnp.where(el2 == emax2, e_lane, float(LANES)), axis=-1, keepdims=True)
    t2 = jnp.exp(emax2 - emax)
    w1 = p_g / (1.0 + t2)
    w2 = p_g * t2 / (1.0 + t2)
    a = jnp.where(e_lane == i1, 1.0, jnp.where(e_lane == i2, 1.0, 0.0))
    r = lax.broadcasted_iota(jnp.int32, (tm, tm), 0)
    cc = lax.broadcasted_iota(jnp.int32, (tm, tm), 1)
    tri = jnp.where(cc < r, 1.0, 0.0).astype(BF16)
    rank = jnp.dot(tri, a.astype(BF16), preferred_element_type=F32)
    k8 = jnp.ceil(jnp.sum(a, axis=0, keepdims=True) * (1.0 / ROW_GROUP))
    ur = lax.broadcasted_iota(jnp.int32, (LANES, LANES), 0)
    uc = lax.broadcasted_iota(jnp.int32, (LANES, LANES), 1)
    upper = jnp.where(ur < uc, 1.0, 0.0).astype(BF16)
    k8_rows = jnp.broadcast_to(k8, (SUBLANES, LANES)).astype(BF16)
    off = jnp.dot(k8_rows, upper, preferred_element_type=F32)[0:1] * ROW_GROUP
    pos = off + rank
    lp0 = jnp.sum(jnp.where(e_lane == i1, pos, 0.0), axis=-1, keepdims=True)
    lp1 = jnp.sum(jnp.where(e_lane == i2, pos, 0.0), axis=-1, keepdims=True)
    run_old = run_ref[...]
    run_new = run_old + k8 * ROW_GROUP
    run_ref[...] = run_new
    cnt_ref[...] = run_new.astype(jnp.int32)
    srow = lax.broadcasted_iota(jnp.int32, (SUBLANES, LANES), 0)
    stats_ref[0] = jnp.where(srow == 0, k8, jnp.where(srow == 1, off, jnp.where(
        srow == 2, run_old, 0.0))).astype(jnp.int32)
    lp = jnp.where(lane == 0, lp0, jnp.where(lane == 1, lp1, 0.0))
    lpc_ref[0] = lp.astype(jnp.int32)
    lpr_ref[0, 0] = lp.T[:SUBLANES].astype(jnp.int32)
    wr_ref[0, 0] = jnp.where(lane == 0, w1, jnp.where(lane == 1, w2, 0.0)).T[:SUBLANES]


def _router(x, g, sc, sh, w_group, w_router):
    b, s, d = x.shape
    ts = min(MOE_TOKEN_TILE, s)
    nt = s // ts
    w = jnp.concatenate([w_group, w_router], axis=1)
    w = jnp.pad(w, ((0, 0), (0, LANES - w.shape[1])))
    row = lambda i, j: (i, j, 0)
    return pl.pallas_call(
        _router_kernel,
        grid=(b, nt),
        in_specs=[
            pl.BlockSpec((1, ts, d), row),
            pl.BlockSpec((1, d), lambda i, j: (0, 0)),
            pl.BlockSpec((1, 1, d), lambda i, j: (i, 0, 0)),
            pl.BlockSpec((1, 1, d), lambda i, j: (i, 0, 0)),
            pl.BlockSpec((d, LANES), lambda i, j: (0, 0)),
        ],
        out_specs=[
            pl.BlockSpec((1, ts, LANES), row),
            pl.BlockSpec((1, 1, SUBLANES, ts), lambda i, j: (i, j, 0, 0)),
            pl.BlockSpec((1, 1, SUBLANES, ts), lambda i, j: (i, j, 0, 0)),
            pl.BlockSpec((1, SUBLANES, LANES), lambda i, j: (i * nt + j, 0, 0)),
            pl.BlockSpec((1, LANES), lambda i, j: (0, 0)),
        ],
        out_shape=[jax.ShapeDtypeStruct((b, s, LANES), jnp.int32),
                   jax.ShapeDtypeStruct((b, nt, SUBLANES, ts), jnp.int32),
                   jax.ShapeDtypeStruct((b, nt, SUBLANES, ts), F32),
                   jax.ShapeDtypeStruct((b * nt, SUBLANES, LANES), jnp.int32),
                   jax.ShapeDtypeStruct((1, LANES), jnp.int32)],
        scratch_shapes=[pltpu.VMEM((1, LANES), F32)],
        compiler_params=_params("arbitrary", "arbitrary"),
    )(x, g.reshape(1, d), sc, sh, w)


def _chunk_copies(meta_ref, blk, local_ref, hbm_ref, sem, *, to_hbm, start):
    def piece(lo, hi, rows):
        loc = local_ref.at[pl.ds(pl.multiple_of(lo, ROW_GROUP), rows)]
        far = hbm_ref.at[pl.ds(pl.multiple_of(hi, ROW_GROUP), rows)]
        cp = pltpu.make_async_copy(loc, far, sem) if to_hbm else pltpu.make_async_copy(far, loc, sem)
        if start:
            cp.start()
        else:
            cp.wait()

    def per_expert(e, carry):
        at = blk * LANES + e
        k = meta_ref[at]
        lo = meta_ref[at + N_EXPERTS]
        hi = meta_ref[at + 2 * N_EXPERTS]
        pairs = k >> 1

        def pair(i, c):
            piece(lo + i * (2 * ROW_GROUP), hi + i * (2 * ROW_GROUP), 2 * ROW_GROUP)
            return c

        lax.fori_loop(0, pairs, pair, 0)

        @pl.when((k & 1) == 1)
        def _():
            piece(lo + pairs * (2 * ROW_GROUP), hi + pairs * (2 * ROW_GROUP), ROW_GROUP)

        return carry

    lax.fori_loop(0, N_EXPERTS, per_expert, 0)


def _dispatch_kernel(last_ref, meta_ref, lpr_ref, wr_ref, x_ref, g_ref, sc_ref, sh_ref, xs_ref, loc_ref,
                     zero_ref, sem, zsem):
    tm = x_ref.shape[1]
    zrows = zero_ref.shape[0]
    blk = pl.program_id(0) * pl.num_programs(1) + pl.program_id(1)
    n_blk = pl.num_programs(0) * pl.num_programs(1)
    slot = blk % 2

    @pl.when((pl.program_id(0) == 0) & (pl.program_id(1) == 0))
    def _():
        zero_ref[...] = jnp.zeros_like(zero_ref)

        def fill(e):
            start = pl.multiple_of(last_ref[e], zrows)
            return pltpu.make_async_copy(zero_ref, xs_ref.at[pl.ds(start, zrows)], zsem)

        for e in range(N_EXPERTS):
            @pl.when(last_ref[e] >= 0)
            def _():
                fill(e).start()
        for e in range(N_EXPERTS):
            @pl.when(last_ref[e] >= 0)
            def _():
                fill(e).wait()

        def tail(k):
            start = pl.multiple_of(k * zrows, zrows)
            return pltpu.make_async_copy(zero_ref, xs_ref.at[pl.ds(start, zrows)], zsem)

        n_used = last_ref[N_EXPERTS]
        n_tiles = xs_ref.shape[0] // zrows
        lax.fori_loop(n_used, n_tiles, lambda k, c: (tail(k).start(), c)[1], 0)
        lax.fori_loop(n_used, n_tiles, lambda k, c: (tail(k).wait(), c)[1], 0)

    hf = _modulated_norm(x_ref[0], g_ref[...], sc_ref[0], sh_ref[0]).astype(BF16)
    d = hf.shape[1]
    n_loc = loc_ref.shape[1]
    rows = lax.broadcasted_iota(jnp.int32, (n_loc, tm), 0)
    pick0 = rows == lpr_ref[0, 0, 0:1, :]
    pick1 = rows == lpr_ref[0, 0, 1:2, :]
    sel = jnp.where(pick0, 1.0, jnp.where(pick1, 1.0, 0.0)).astype(BF16)
    loc_ref[slot, :, :d] = jnp.dot(sel, hf, preferred_element_type=F32).astype(BF16)
    wrow = jnp.sum(jnp.where(pick0, wr_ref[0, 0, 0:1, :], jnp.where(pick1, wr_ref[0, 0, 1:2, :], 0.0)),
                   axis=1, keepdims=True)
    w_hi = wrow.astype(BF16).astype(F32)
    lane = lax.broadcasted_iota(jnp.int32, (n_loc, LANES), 1)
    loc_ref[slot, :, d:] = jnp.where(lane == 0, w_hi, jnp.where(lane == 1, wrow - w_hi, 0.0)).astype(BF16)

    def copies(b, sl, start):
        _chunk_copies(meta_ref, b, loc_ref.at[sl], xs_ref, sem.at[sl], to_hbm=True, start=start)

    copies(blk, slot, True)

    @pl.when(blk > 0)
    def _():
        copies(blk - 1, 1 - slot, False)

    @pl.when(blk == n_blk - 1)
    def _():
        copies(blk, slot, False)


def _local_rows(tm):
    return 2 * tm + N_EXPERTS * ROW_GROUP


def _dispatch(x, g, sc, sh, meta, lpr, wr, last_tile_row, n_rows):
    b, s, d = x.shape
    tm = min(MOE_TOKEN_TILE, s)
    nt = s // tm
    dw = d + LANES
    row = lambda i, j, last, meta: (i, j, 0)
    grid_spec = pltpu.PrefetchScalarGridSpec(
        num_scalar_prefetch=2,
        grid=(b, nt),
        in_specs=[
            pl.BlockSpec((1, 1, SUBLANES, tm), lambda i, j, last, meta: (i, j, 0, 0)),
            pl.BlockSpec((1, 1, SUBLANES, tm), lambda i, j, last, meta: (i, j, 0, 0)),
            pl.BlockSpec((1, tm, d), row),
            pl.BlockSpec((1, d), lambda i, j, last, meta: (0, 0)),
            pl.BlockSpec((1, 1, d), lambda i, j, last, meta: (i, 0, 0)),
            pl.BlockSpec((1, 1, d), lambda i, j, last, meta: (i, 0, 0)),
        ],
        out_specs=pl.BlockSpec(memory_space=pl.ANY),
        scratch_shapes=[pltpu.VMEM((2, _local_rows(tm), dw), BF16), pltpu.VMEM((MOE_ROW_TILE, dw), BF16),
                        pltpu.SemaphoreType.DMA((2,)), pltpu.SemaphoreType.DMA(())],
    )
    return pl.pallas_call(
        _dispatch_kernel,
        grid_spec=grid_spec,
        out_shape=jax.ShapeDtypeStruct((n_rows, dw), BF16),
        compiler_params=_params("arbitrary", "arbitrary"),
    )(last_tile_row, meta, lpr, wr, x, g.reshape(1, d), sc, sh)


def _expert_kernel(te_ref, nu_ref, xs_ref, w1_ref, w3_ref, w2_ref, ys_ref, w1b_ref, w3b_ref, w2b_ref):
    i = pl.program_id(0)

    @pl.when((i == 0) | (te_ref[i] != te_ref[jnp.maximum(i - 1, 0)]))
    def _():
        w1b_ref[...] = w1_ref[0, 0].astype(BF16)
        w3b_ref[...] = w3_ref[0, 0].astype(BF16)
        w2b_ref[...] = w2_ref[0, 0].astype(BF16)

    @pl.when(i < nu_ref[0])
    def _():
        d = w1b_ref.shape[0]
        xb = xs_ref[:, :d]
        gate = xs_ref[:, d:d + 1].astype(F32) + xs_ref[:, d + 1:d + 2].astype(F32)
        a = jnp.dot(xb, w1b_ref[...], preferred_element_type=F32)
        bb = jnp.dot(xb, w3b_ref[...], preferred_element_type=F32)
        hmid = (_silu(a) * bb * gate).astype(BF16)
        ys_ref[...] = jnp.dot(hmid, w2b_ref[...], preferred_element_type=F32).astype(ys_ref.dtype)

    @pl.when(i >= nu_ref[0])
    def _():
        ys_ref[...] = jnp.zeros_like(ys_ref)


def _expert_ffn(xs, tile_expert, n_used, w1, w3, w2, layer):
    p, dw = xs.shape
    d, de = w1.shape[2], w1.shape[3]
    tm = MOE_ROW_TILE
    grid_spec = pltpu.PrefetchScalarGridSpec(
        num_scalar_prefetch=2,
        grid=(p // tm,),
        in_specs=[
            pl.BlockSpec((tm, dw), lambda i, te, nu: (jnp.maximum(jnp.minimum(i, nu[0] - 1), 0), 0)),
            pl.BlockSpec((1, 1, d, de), lambda i, te, nu: (layer, te[i], 0, 0)),
            pl.BlockSpec((1, 1, d, de), lambda i, te, nu: (layer, te[i], 0, 0)),
            pl.BlockSpec((1, 1, de, d), lambda i, te, nu: (layer, te[i], 0, 0)),
        ],
        out_specs=pl.BlockSpec((tm, d), lambda i, te, nu: (i, 0)),
        scratch_shapes=[pltpu.VMEM((d, de), BF16), pltpu.VMEM((d, de), BF16), pltpu.VMEM((de, d), BF16)],
    )
    return pl.pallas_call(
        _expert_kernel,
        grid_spec=grid_spec,
        out_shape=jax.ShapeDtypeStruct((p, d), BF16),
        compiler_params=_params("arbitrary"),
    )(tile_expert, n_used, xs, w1, w3, w2)


def _combine_kernel(meta_ref, ys_ref, lpc_ref, x_ref, gate_ref, o_ref, loc_ref, sem):
    tm = x_ref.shape[1]
    n_loc = loc_ref.shape[1]
    blk = pl.program_id(0) * pl.num_programs(1) + pl.program_id(1)
    n_blk = pl.num_programs(0) * pl.num_programs(1)
    slot = blk % 2

    def fetch(b, sl):
        loc_ref[sl, 2 * tm:, :] = jnp.zeros((n_loc - 2 * tm, loc_ref.shape[2]), BF16)
        _chunk_copies(meta_ref, b, loc_ref.at[sl], ys_ref, sem.at[sl], to_hbm=False, start=True)

    @pl.when(blk == 0)
    def _():
        fetch(blk, slot)

    @pl.when(blk + 1 < n_blk)
    def _():
        fetch(blk + 1, 1 - slot)

    _chunk_copies(meta_ref, blk, loc_ref.at[slot], ys_ref, sem.at[slot], to_hbm=False, start=False)
    cols = lax.broadcasted_iota(jnp.int32, (tm, n_loc), 1)
    lp = lpc_ref[0]
    sel = jnp.where(cols == lp[:, 0:1], 1.0, jnp.where(cols == lp[:, 1:2], 1.0, 0.0)).astype(BF16)
    y = jnp.dot(sel, loc_ref[slot], preferred_element_type=F32)
    o_ref[0] = x_ref[0] + gate_ref[0] * y


def _combine(ys, meta, lpc, x, gate):
    b, s, d = x.shape
    tm = min(MOE_TOKEN_TILE, s)
    nt = s // tm
    row = lambda i, j, meta: (i, j, 0)
    grid_spec = pltpu.PrefetchScalarGridSpec(
        num_scalar_prefetch=1,
        grid=(b, nt),
        in_specs=[
            pl.BlockSpec(memory_space=pl.ANY),
            pl.BlockSpec((1, tm, LANES), row),
            pl.BlockSpec((1, tm, d), row),
            pl.BlockSpec((1, 1, d), lambda i, j, meta: (i, 0, 0)),
        ],
        out_specs=pl.BlockSpec((1, tm, d), row),
        scratch_shapes=[pltpu.VMEM((2, _local_rows(tm), d), BF16), pltpu.SemaphoreType.DMA((2,))],
    )
    return pl.pallas_call(
        _combine_kernel,
        grid_spec=grid_spec,
        out_shape=jax.ShapeDtypeStruct((b, s, d), F32),
        compiler_params=_params("arbitrary", "arbitrary"),
    )(meta, ys, lpc, x, gate)


def _hier_moe_residual(x, g, sc, sh, gate, w_group, w_router, w1, w3, w2, layer):
    b, s, d = x.shape
    t = b * s
    lpc, lpr, wr, stats, cnt = _router(x, g, sc, sh, w_group, w_router)
    tm = MOE_ROW_TILE
    ex = slice(N_GROUPS, N_GROUPS + N_EXPERTS)
    counts = cnt[0, ex]
    tiles = (counts + tm - 1) // tm
    tile_end = jnp.cumsum(tiles)
    base = (tile_end - tiles) * tm
    n_blocks = stats.shape[0]
    n_tiles = (2 * t + n_blocks * N_EXPERTS * ROW_GROUP) // tm + N_EXPERTS
    tile_expert = jnp.minimum(
        jnp.sum(tile_end[None, :] <= jnp.arange(n_tiles)[:, None], axis=1), N_EXPERTS - 1).astype(jnp.int32)
    n_used = tile_end[-1:].astype(jnp.int32)
    meta = jnp.concatenate([stats[:, 0, ex], stats[:, 1, ex], base[None, :] + stats[:, 2, ex],
                            jnp.zeros((n_blocks, LANES - 3 * N_EXPERTS), jnp.int32)], axis=1)
    meta = meta.astype(jnp.int32).reshape(n_blocks * LANES)
    last_tile_row = jnp.where(tiles > 0, (tile_end - 1) * tm, -1).astype(jnp.int32)
    last_tile_row = jnp.concatenate([last_tile_row, n_used])
    xs = _dispatch(x, g, sc, sh, meta, lpr, wr, last_tile_row, n_tiles * tm)
    ys = _expert_ffn(xs, tile_expert, n_used, w1, w3, w2, layer)
    return _combine(ys, meta, lpc, x, gate)


def kernel(x, c, rel_table, ada_w, ada_b, norm_mix_g, norm_ffn_g, da_w_in, da_w_out, da_q_gain, da_k_gain, da_lam_q1, da_lam_k1, da_lam_q2, da_lam_k2, da_subln_g, ml_w_in, ml_conv_w, ml_conv_b, ml_wq, ml_wk, ml_wv, ml_gate_w, ml_gate_b, ml_outnorm_g, ml_skip, ml_w_out, moe_w_group, moe_w_router, moe_w1, moe_w3, moe_w2):
    depth = ada_w.shape[0]
    d = x.shape[-1]
    mod = _ada_mod(c, ada_w, ada_b)
    for i in range(depth):
        sh1, sc1, g1, sh2, sc2, g2 = [mod[i, :, None, k * d:(k + 1) * d] for k in range(6)]
        j = i // N_MIXERS
        if i % N_MIXERS == 0:
            qkv = _prenorm_matmul(x, norm_mix_g[i], sc1, sh1, da_w_in[j].astype(BF16),
                                  head_gain=_qk_head_gains(da_q_gain[j], da_k_gain[j]), half=DA_HEAD_DIM)
            lam_init = 0.8 - 0.6 * math.exp(-0.3 * i)
            lam = (jnp.exp(jnp.sum(da_lam_q1[j] * da_lam_k1[j])) - jnp.exp(jnp.sum(da_lam_q2[j] * da_lam_k2[j]))
                   + lam_init).astype(F32)
            o = _diff_attention(qkv, da_subln_g[j], lam, lam_init, rel_table)
            x = _proj_residual(o, da_w_out[j].astype(BF16), x, g1)
        else:
            up = _prenorm_matmul(x, norm_mix_g[i], sc1, sh1, ml_w_in[j].astype(BF16))
            q, k, v, xc, pre = _mlstm_pre(up, ml_conv_w[j], ml_conv_b[j], ml_wq[j], ml_wk[j], ml_wv[j],
                                          ml_gate_w[j], ml_gate_b[j])
            hn = _mlstm_scan(q, k, v, pre, ml_outnorm_g[j])
            x = _mlstm_out(hn, xc, up, ml_skip[j], ml_w_out[j].astype(BF16), x, g1)
        x = _hier_moe_residual(x, norm_ffn_g[i], sc2, sh2, g2, moe_w_group[i], moe_w_router[i],
                               moe_w1, moe_w3, moe_w2, i)
    return x
```

```python
import functools
import math

import jax
import jax.numpy as jnp
from jax import lax
from jax.experimental import pallas as pl
from jax.experimental.pallas import tpu as pltpu

EPS = 1e-6
N_MIXERS = 2
DA_HEADS = 8
DA_HEAD_DIM = 64
N_BUCKETS = 32
MAX_DISTANCE = 128
ML_HEADS = 4
ML_CHUNK = 256
ML_QKV_BLOCK = 4
N_GROUPS = 4
EXPERTS_PER_GROUP = 8
N_EXPERTS = N_GROUPS * EXPERTS_PER_GROUP

LANES = 128
SUBLANES = 8
ROW_GROUP = 16
VMEM_LIMIT = 48 * 1024 * 1024
ATTN_Q_TILE = 2048
ATTN_SEQS_PER_STEP = 1
ATTN_ROW_BLOCK = 128
ROW_TILE = 512
MOE_ROW_TILE = 512
MOE_TOKEN_TILE = 512
ROUTER_ROWS = 48
MLSTM_COL_TILE = 256
MLSTM_ROW_CHUNK = 512

F32 = jnp.float32
BF16 = jnp.bfloat16
LOG2E = math.log2(math.e)


def _params(*sem):
    return pltpu.CompilerParams(dimension_semantics=sem, vmem_limit_bytes=VMEM_LIMIT)


def _silu(x):
    return x / (1.0 + jnp.exp(-x))


def _log_sigmoid(x):
    return jnp.minimum(x, 0.0) - jnp.log(1.0 + jnp.exp(-jnp.abs(x)))


def _modulated_norm(x, g, sc, sh):
    ms = jnp.mean(x * x, axis=-1, keepdims=True)
    return x * lax.rsqrt(ms + EPS) * g * (1.0 + sc) + sh


def _ada_kernel(c_ref, w_ref, b_ref, o_ref):
    ca = _silu(c_ref[...]).astype(BF16)
    o_ref[0] = jnp.dot(ca, w_ref[0].astype(BF16), preferred_element_type=F32) + b_ref[0]


def _ada_mod(c, ada_w, ada_b):
    depth, d, n = ada_w.shape
    b = c.shape[0]
    tn = 1536
    return pl.pallas_call(
        _ada_kernel,
        grid=(depth, n // tn),
        in_specs=[
            pl.BlockSpec((b, d), lambda i, j: (0, 0)),
            pl.BlockSpec((1, d, tn), lambda i, j: (i, 0, j)),
            pl.BlockSpec((1, 1, tn), lambda i, j: (i, 0, j)),
        ],
        out_specs=pl.BlockSpec((1, b, tn), lambda i, j: (i, 0, j)),
        out_shape=jax.ShapeDtypeStruct((depth, b, n), F32),
        compiler_params=_params("parallel", "parallel"),
    )(c, ada_w, ada_b.reshape(depth, 1, n))


def _prenorm_matmul_kernel(x_ref, g_ref, sc_ref, sh_ref, w_ref, hg_ref, o_ref, *, n_chunk, n_norm, half):
    y = _modulated_norm(x_ref[0], g_ref[...], sc_ref[0], sh_ref[0]).astype(BF16)
    n = w_ref.shape[1]
    for n0 in range(0, n, n_chunk):
        acc = jnp.dot(y, w_ref[:, n0:n0 + n_chunk], preferred_element_type=F32)
        if n0 < n_norm:
            for c0 in range(n0, n0 + n_chunk, 2 * half):
                blk = _group_rms(acc[:, c0 - n0:c0 - n0 + 2 * half], hg_ref[:, c0:c0 + 2 * half], half)
                o_ref[0, :, c0:c0 + 2 * half] = blk.astype(o_ref.dtype)
        else:
            o_ref[0, :, n0:n0 + n_chunk] = acc.astype(o_ref.dtype)


def _prenorm_matmul(x, g, sc, sh, w, head_gain=None, half=LANES // 2):
    b, s, d = x.shape
    n = w.shape[1]
    ts = min(ROW_TILE, s)
    n_norm = 0 if head_gain is None else head_gain.shape[1]
    hg = jnp.ones((1, n), F32) if head_gain is None else jnp.pad(head_gain, ((0, 0), (0, n - n_norm)))
    return pl.pallas_call(
        functools.partial(_prenorm_matmul_kernel, n_chunk=1024, n_norm=n_norm, half=half),
        grid=(b, s // ts),
        in_specs=[
            pl.BlockSpec((1, ts, d), lambda i, j: (i, j, 0)),
            pl.BlockSpec((1, d), lambda i, j: (0, 0)),
            pl.BlockSpec((1, 1, d), lambda i, j: (i, 0, 0)),
            pl.BlockSpec((1, 1, d), lambda i, j: (i, 0, 0)),
            pl.BlockSpec((d, n), lambda i, j: (0, 0)),
            pl.BlockSpec((1, n), lambda i, j: (0, 0)),
        ],
        out_specs=pl.BlockSpec((1, ts, n), lambda i, j: (i, j, 0)),
        out_shape=jax.ShapeDtypeStruct((b, s, n), BF16),
        compiler_params=_params("parallel", "parallel"),
    )(x, g.reshape(1, d), sc, sh, w, hg)


def _proj_residual_kernel(a_ref, w_ref, x_ref, gate_ref, o_ref):
    y = jnp.dot(a_ref[0], w_ref[...], preferred_element_type=F32)
    o_ref[0] = x_ref[0] + gate_ref[0] * y


def _proj_residual(a, w, x, gate):
    b, s, k = a.shape
    d = w.shape[1]
    ts = min(ROW_TILE, s)
    return pl.pallas_call(
        _proj_residual_kernel,
        grid=(b, s // ts),
        in_specs=[
            pl.BlockSpec((1, ts, k), lambda i, j: (i, j, 0)),
            pl.BlockSpec((k, d), lambda i, j: (0, 0)),
            pl.BlockSpec((1, ts, d), lambda i, j: (i, j, 0)),
            pl.BlockSpec((1, 1, d), lambda i, j: (i, 0, 0)),
        ],
        out_specs=pl.BlockSpec((1, ts, d), lambda i, j: (i, j, 0)),
        out_shape=jax.ShapeDtypeStruct((b, s, d), F32),
        compiler_params=_params("parallel", "parallel"),
    )(a, w, x, gate)


def _t5_bucket(rel):
    nb = N_BUCKETS // 2
    max_exact = nb // 2
    ret = jnp.where(rel > 0, nb, 0)
    n = jnp.abs(rel)
    nf = jnp.maximum(n, 1).astype(F32)
    large = max_exact + (jnp.log(nf / max_exact) / math.log(MAX_DISTANCE / max_exact)
                         * (nb - max_exact)).astype(jnp.int32)
    large = jnp.minimum(large, nb - 1)
    return ret + jnp.where(n < max_exact, n, large)


def _bias_band_kernel(t_ref, o_ref, *, tq, nb):
    u = jnp.broadcast_to(t_ref[0], (tq, t_ref.shape[2]))
    y = pltpu.roll(u, 1, 1, stride=1, stride_axis=0)
    for jb in range(nb):
        o_ref[0, jb] = y[:, tq + jb * LANES:tq + (jb + 1) * LANES]


def _bias_band(rel_table, s, tq):
    h = rel_table.shape[1]
    delta = jnp.arange(-(s - 1), s)
    t = rel_table[_t5_bucket(delta)].T.astype(F32) * LOG2E
    u = jnp.concatenate([t, jnp.zeros((h, 1), F32)], axis=1).reshape(h, 1, 2 * s)
    nb = (2 * s - tq) // LANES
    return pl.pallas_call(
        functools.partial(_bias_band_kernel, tq=tq, nb=nb),
        grid=(h,),
        in_specs=[pl.BlockSpec((1, 1, 2 * s), lambda i: (i, 0, 0))],
        out_specs=pl.BlockSpec((1, nb, tq, LANES), lambda i: (i, 0, 0, 0)),
        out_shape=jax.ShapeDtypeStruct((h, nb, tq, LANES), F32),
        compiler_params=_params("parallel"),
    )(u)


def _group_rms(x, gain, half):
    sq = x * x
    lane = lax.broadcasted_iota(jnp.int32, x.shape, 1)
    lo = lane < half
    s_lo = jnp.sum(jnp.where(lo, sq, 0.0), axis=-1, keepdims=True)
    s_all = jnp.sum(sq, axis=-1, keepdims=True)
    ms = jnp.where(lo, s_lo, s_all - s_lo) * (1.0 / half)
    return x * lax.rsqrt(ms + EPS) * gain


def _diff_attn_kernel(lam_ref, q_ref, k_ref, v_ref, sg_ref, band_ref, o_ref, ve_ref, *, tq, s, dh, out_scale):
    qi = pl.program_id(2)
    nq = pl.num_programs(2)
    dv = 2 * dh

    n_seq = q_ref.shape[0]

    @pl.when(qi == 0)
    def _():
        for bb in range(n_seq):
            ve_ref[bb, :, :dv] = v_ref[bb]
            ve_ref[bb, :, dv:] = jnp.ones((s, dv), BF16)

    lam = lam_ref[0]
    jb0 = (nq - 1 - qi) * (tq // LANES)
    nkb = s // LANES
    rb = ATTN_ROW_BLOCK
    for bb in range(n_seq):
        q = q_ref[bb]
        lane = lax.broadcasted_iota(jnp.int32, q.shape, 1)
        zero = jnp.zeros_like(q)
        q0 = jnp.where(lane < dh, q, zero)
        q1 = jnp.where(lane < dh, zero, q)

        def softmax_av(qm, r0, r1):
            sc = lax.dot_general(qm[r0:r1], k_ref[bb], (((1,), (1,)), ((), ())), preferred_element_type=F32)
            sc = sc + jnp.concatenate([band_ref[0, jb0 + kb, r0:r1, :] for kb in range(nkb)], axis=1)
            p = jnp.exp2(sc - jnp.max(sc, axis=-1, keepdims=True)).astype(BF16)
            oe = jnp.dot(p, ve_ref[bb], preferred_element_type=F32)
            return oe[:, :dv], oe[:, dv:dv + 1]

        for r0 in range(0, tq, rb):
            n0, l0 = softmax_av(q0, r0, r0 + rb)
            n1, l1 = softmax_av(q1, r0, r0 + rb)
            o = n0 * (1.0 / l0) - n1 * (lam / l1)
            ms = jnp.mean(o * o, axis=-1, keepdims=True)
            o_ref[bb, r0:r0 + rb, :] = (o * lax.rsqrt(ms + EPS) * sg_ref[...] * out_scale).astype(o_ref.dtype)


def _qk_head_gains(q_gain, k_gain):
    h, dh = DA_HEADS, DA_HEAD_DIM
    qg = jnp.tile(q_gain * (dh ** -0.5 * LOG2E), 2 * h)
    kg = jnp.tile(k_gain, 2 * h)
    return jnp.concatenate([qg, kg]).reshape(1, 4 * h * dh).astype(F32)


def _diff_attention(qkv, subln_g, lam, lam_init, rel_table):
    b, s, _ = qkv.shape
    h, dh = DA_HEADS, DA_HEAD_DIM
    tq = min(ATTN_Q_TILE, s)
    band = _bias_band(rel_table, s, tq)
    nb = band.shape[1]
    nsq = math.gcd(ATTN_SEQS_PER_STEP, b)
    kern = functools.partial(_diff_attn_kernel, tq=tq, s=s, dh=dh, out_scale=1.0 - lam_init)
    grid_spec = pltpu.PrefetchScalarGridSpec(
        num_scalar_prefetch=1,
        grid=(h, b // nsq, s // tq),
        in_specs=[
            pl.BlockSpec((nsq, tq, 2 * dh), lambda hi, bi, qi, lam: (bi, qi, hi)),
            pl.BlockSpec((nsq, s, 2 * dh), lambda hi, bi, qi, lam: (bi, 0, h + hi)),
            pl.BlockSpec((nsq, s, 2 * dh), lambda hi, bi, qi, lam: (bi, 0, 2 * h + hi)),
            pl.BlockSpec((1, 2 * dh), lambda hi, bi, qi, lam: (0, 0)),
            pl.BlockSpec((1, nb, tq, LANES), lambda hi, bi, qi, lam: (hi, 0, 0, 0),
                         pipeline_mode=pl.Buffered(1)),
        ],
        out_specs=pl.BlockSpec((nsq, tq, 2 * dh), lambda hi, bi, qi, lam: (bi, qi, hi)),
        scratch_shapes=[pltpu.VMEM((nsq, s, 4 * dh), BF16)],
    )
    return pl.pallas_call(
        kern,
        grid_spec=grid_spec,
        out_shape=jax.ShapeDtypeStruct((b, s, h * 2 * dh), BF16),
        compiler_params=_params("parallel", "parallel", "arbitrary"),
    )(lam.reshape(1), qkv, qkv, qkv, subln_g.reshape(1, 2 * dh), band)


def _mlstm_pre_kernel(xm_ref, cw_ref, cb_ref, wq_ref, wk_ref, wv_ref, gq_ref, gk_ref, gv_ref, gb_ref,
                      q_ref, k_ref, v_ref, xc_ref, pre_ref, pad_ref, *, s, halo, n_heads):
    j = pl.program_id(1)
    xm = xm_ref[0].astype(F32)
    cbw = xm.shape[1]
    pad_ref[0:8, :] = jnp.zeros((8, cbw), F32)
    pad_ref[8 + s:16 + s, :] = jnp.zeros((8, cbw), F32)
    pad_ref[8:8 + s, :] = xm

    @pl.when(j == 0)
    def _():
        pre_ref[0] = jnp.broadcast_to(gb_ref[...], pre_ref.shape[1:])

    rc = min(MLSTM_ROW_CHUNK, s)
    for r0 in range(0, s, rc):
        rows = slice(r0, r0 + rc)
        acc = cb_ref[...] + cw_ref[0:1, :] * pad_ref[8 - halo + r0:8 - halo + r0 + rc, :]
        for t in range(1, 2 * halo + 1):
            acc = acc + cw_ref[t:t + 1, :] * pad_ref[8 - halo + t + r0:8 - halo + t + r0 + rc, :]
        xcb = _silu(acc).astype(BF16)
        q = jnp.dot(xcb, wq_ref[0], preferred_element_type=F32).astype(BF16)
        k = jnp.dot(xcb, wk_ref[0], preferred_element_type=F32).astype(BF16)
        v = jnp.dot(xm_ref[0, rows, :], wv_ref[0], preferred_element_type=F32).astype(BF16)
        q_ref[0, rows, :] = q
        k_ref[0, rows, :] = k
        v_ref[0, rows, :] = v
        xc_ref[0, rows, :] = xcb
        pre_ref[0, rows, :] += (jnp.dot(q, gq_ref[...], preferred_element_type=F32)
                                + jnp.dot(k, gk_ref[...], preferred_element_type=F32)
                                + jnp.dot(v, gv_ref[...], preferred_element_type=F32))

    @pl.when(j == pl.num_programs(1) - 1)
    def _():
        pre = pre_ref[0]
        lane = lax.broadcasted_iota(jnp.int32, pre.shape, 1)
        is_forget = (lane % (2 * n_heads)) >= n_heads
        pre_ref[0] = jnp.where(is_forget, _log_sigmoid(pre), pre)


def _blockdiag_dense(w, cb):
    nblk, blk, _ = w.shape
    per = cb // blk
    w4 = w.reshape(nblk // per, per, blk, blk)
    eye = jnp.eye(per, dtype=w.dtype)
    return jnp.einsum('jnio,nm->jnimo', w4, eye).reshape(nblk // per, cb, cb).astype(BF16)


def _mlstm_pre(up, conv_w, conv_b, wq, wk, wv, gate_w, gate_b):
    b, s, c2 = up.shape
    c = c2 // 2
    cb = MLSTM_COL_TILE
    ncb = c // cb
    kw = conv_w.shape[0]
    ng = gate_w.shape[0] * gate_w.shape[-1]

    def gate_mat(i):
        g = gate_w[:, i].transpose(1, 0, 2).reshape(c, ng)
        return jnp.pad(g, ((0, 0), (0, LANES - ng))).astype(BF16)

    gb = jnp.pad(gate_b.reshape(1, ng), ((0, 0), (0, LANES - ng)))
    col = lambda i, j: (i, 0, j)
    out_bf = jax.ShapeDtypeStruct((b, s, c), BF16)
    return pl.pallas_call(
        functools.partial(_mlstm_pre_kernel, s=s, halo=kw // 2, n_heads=gate_w.shape[-1] // 2),
        grid=(b, ncb),
        in_specs=[
            pl.BlockSpec((1, s, cb), col),
            pl.BlockSpec((kw, cb), lambda i, j: (0, j)),
            pl.BlockSpec((1, cb), lambda i, j: (0, j)),
            pl.BlockSpec((1, cb, cb), lambda i, j: (j, 0, 0)),
            pl.BlockSpec((1, cb, cb), lambda i, j: (j, 0, 0)),
            pl.BlockSpec((1, cb, cb), lambda i, j: (j, 0, 0)),
            pl.BlockSpec((cb, LANES), lambda i, j: (j, 0)),
            pl.BlockSpec((cb, LANES), lambda i, j: (j, 0)),
            pl.BlockSpec((cb, LANES), lambda i, j: (j, 0)),
            pl.BlockSpec((1, LANES), lambda i, j: (0, 0)),
        ],
        out_specs=[
            pl.BlockSpec((1, s, cb), col),
            pl.BlockSpec((1, s, cb), col),
            pl.BlockSpec((1, s, cb), col),
            pl.BlockSpec((1, s, cb), col),
            pl.BlockSpec((1, s, LANES), lambda i, j: (i, 0, 0)),
        ],
        out_shape=[out_bf, out_bf, out_bf, out_bf, jax.ShapeDtypeStruct((b, s, LANES), F32)],
        scratch_shapes=[pltpu.VMEM((s + 16, cb), F32)],
        compiler_params=_params("parallel", "arbitrary"),
    )(up, conv_w, conv_b.reshape(1, c), _blockdiag_dense(wq, cb), _blockdiag_dense(wk, cb),
      _blockdiag_dense(wv, cb), gate_mat(0), gate_mat(1), gate_mat(2), gb)


def _mlstm_scan_kernel(q_ref, k_ref, v_ref, gcol_ref, grow_ref, og_ref, o_ref,
                       cf_ref, cbk_ref, hf_ref, hb_ref, *, L, nc, dk):
    scale = dk ** -0.5
    dirn = lax.broadcasted_iota(jnp.int32, (2, L, L), 0)
    row = lax.broadcasted_iota(jnp.int32, (2, L, L), 1)
    coli = lax.broadcasted_iota(jnp.int32, (2, L, L), 2)
    lo = jnp.where(dirn == 0, coli, row)
    hi = jnp.where(dirn == 0, row, coli)
    mask = lo <= hi
    mask_t = hi <= lo
    cf_ref[...] = jnp.zeros_like(cf_ref)
    cbk_ref[...] = jnp.zeros_like(cbk_ref)
    bdot = functools.partial(lax.dot_general, preferred_element_type=F32)

    def body(i, carry):
        n, m = carry
        sl_f = pl.ds(pl.multiple_of(i * L, L), L)
        sl_b = pl.ds(pl.multiple_of((nc - 1 - i) * L, L), L)
        qb = jnp.stack([q_ref[0, sl_f, :], q_ref[0, sl_b, :]])
        kb = jnp.stack([k_ref[0, sl_f, :], k_ref[0, sl_b, :]])
        vb = jnp.stack([v_ref[0, sl_f, :], v_ref[0, sl_b, :]])
        gcf, gcb = gcol_ref[0, 0, i], gcol_ref[0, 0, nc - 1 - i]
        grf, grb = grow_ref[0, 0, i], grow_ref[0, 0, nc - 1 - i]
        ii_col = jnp.stack([gcf[:, 0:1], gcb[:, 2:3]])
        lf_col = jnp.stack([gcf[:, 1:2], gcb[:, 3:4]])
        ii_row = jnp.stack([grf[0:1, :], grb[2:3, :]])
        lf_row = jnp.stack([grf[1:2, :], grb[3:4, :]])
        b_col = jnp.sum(jnp.where(mask, lf_row, 0.0), axis=2, keepdims=True)
        b_row = jnp.sum(jnp.where(mask_t, lf_col, 0.0), axis=1, keepdims=True)
        dmat = jnp.where(mask, b_col - b_row + ii_row, -jnp.inf)
        inter = b_col + m
        m_t = jnp.maximum(inter, jnp.max(dmat, axis=2, keepdims=True))
        w_intra = jnp.exp(dmat - m_t)
        w_inter = jnp.exp(inter - m_t)
        sqk = bdot(qb, kb, (((2,), (2,)), ((0,), (0,))))
        sw = sqk * scale * w_intra
        cmat = jnp.stack([cf_ref[...], cbk_ref[...]])
        num = (w_inter * bdot(qb, cmat.astype(BF16), (((2,), (1,)), ((0,), (0,))))
               + bdot(sw.astype(BF16), vb, (((2,), (1,)), ((0,), (0,)))))
        den = (w_inter * jnp.sum(qb.astype(F32) * n, axis=2, keepdims=True)
               + jnp.sum(sw, axis=2, keepdims=True))
        h = num / jnp.maximum(jnp.abs(den), jnp.exp(-m_t))
        b_tot = jnp.sum(lf_row, axis=2, keepdims=True)
        g_col = b_tot - b_col + ii_col
        m_new = jnp.maximum(b_tot + m, jnp.max(g_col, axis=1, keepdims=True))
        decay = jnp.exp(b_tot + m - m_new)
        kw = kb.astype(F32) * scale * jnp.exp(g_col - m_new)
        c_new = decay * cmat + bdot(kw.astype(BF16), vb, (((1,), (1,)), ((0,), (0,))))
        n_new = decay * n + jnp.sum(kw, axis=1, keepdims=True)
        hf_ref[sl_f, :] = h[0]
        hb_ref[sl_b, :] = h[1]
        cf_ref[...] = c_new[0]
        cbk_ref[...] = c_new[1]
        return n_new, m_new

    lax.fori_loop(0, nc, body, (jnp.zeros((2, 1, dk), F32), jnp.zeros((2, 1, 1), F32)))

    def finish(c, carry):
        sl = pl.ds(pl.multiple_of(c * L, L), L)
        hs = hf_ref[sl, :] + hb_ref[sl, :]
        ms = jnp.mean(hs * hs, axis=-1, keepdims=True)
        o_ref[0, sl, :] = (hs * lax.rsqrt(ms + EPS) * og_ref[0]).astype(o_ref.dtype)
        return carry

    lax.fori_loop(0, nc, finish, 0)


def _mlstm_scan(q, k, v, pre, outnorm_g):
    b, s, c = q.shape
    hh = ML_HEADS
    dk = c // hh
    L = min(ML_CHUNK, s)
    nc = s // L
    idx = jnp.array([[0 * 2 * hh + h, 0 * 2 * hh + hh + h, 2 * hh + h, 2 * hh + hh + h] for h in range(hh)])
    g = pre[:, :, idx]
    gcol = g.transpose(0, 2, 1, 3).reshape(b, hh, nc, L, 4)
    grow = gcol.transpose(0, 1, 2, 4, 3)
    head = lambda i, j: (i, 0, j)
    return pl.pallas_call(
        functools.partial(_mlstm_scan_kernel, L=L, nc=nc, dk=dk),
        grid=(b, hh),
        in_specs=[
            pl.BlockSpec((1, s, dk), head),
            pl.BlockSpec((1, s, dk), head),
            pl.BlockSpec((1, s, dk), head),
            pl.BlockSpec((1, 1, nc, L, 4), lambda i, j: (i, j, 0, 0, 0)),
            pl.BlockSpec((1, 1, nc, 4, L), lambda i, j: (i, j, 0, 0, 0)),
            pl.BlockSpec((1, 1, dk), lambda i, j: (j, 0, 0)),
        ],
        out_specs=pl.BlockSpec((1, s, dk), head),
        out_shape=jax.ShapeDtypeStruct((b, s, c), BF16),
        scratch_shapes=[pltpu.VMEM((dk, dk), F32), pltpu.VMEM((dk, dk), F32),
                        pltpu.VMEM((s, dk), F32), pltpu.VMEM((s, dk), F32)],
        compiler_params=_params("parallel", "parallel"),
    )(q, k, v, gcol, grow, outnorm_g.reshape(hh, 1, dk))


def _mlstm_out_kernel(hn_ref, xc_ref, z_ref, skip_ref, w_ref, x_ref, gate_ref, o_ref):
    a = (hn_ref[0].astype(F32) + skip_ref[...] * xc_ref[0].astype(F32)) * _silu(z_ref[0].astype(F32))
    y = jnp.dot(a.astype(BF16), w_ref[...], preferred_element_type=F32)
    o_ref[0] = x_ref[0] + gate_ref[0] * y


def _mlstm_out(hn, xc, up, skip, w, x, gate):
    b, s, c = hn.shape
    d = w.shape[1]
    ts = min(ROW_TILE, s)
    row = lambda i, j: (i, j, 0)
    return pl.pallas_call(
        _mlstm_out_kernel,
        grid=(b, s // ts),
        in_specs=[
            pl.BlockSpec((1, ts, c), row),
            pl.BlockSpec((1, ts, c), row),
            pl.BlockSpec((1, ts, c), lambda i, j: (i, j, 1)),
            pl.BlockSpec((1, c), lambda i, j: (0, 0)),
            pl.BlockSpec((c, d), lambda i, j: (0, 0)),
            pl.BlockSpec((1, ts, d), row),
            pl.BlockSpec((1, 1, d), lambda i, j: (i, 0, 0)),
        ],
        out_specs=pl.BlockSpec((1, ts, d), row),
        out_shape=jax.ShapeDtypeStruct((b, s, d), F32),
        compiler_params=_params("parallel", "parallel"),
    )(hn, xc, up, skip.reshape(1, c), w, x, gate)


def _router_kernel(x_ref, g_ref, sc_ref, sh_ref, w_ref, lpc_ref, lpr_ref, wr_ref, stats_ref, cnt_ref, hf_ref,
                   run_ref):
    first = (pl.program_id(0) == 0) & (pl.program_id(1) == 0)

    @pl.when(first)
    def _():
        run_ref[...] = jnp.zeros_like(run_ref)

    hf = _modulated_norm(x_ref[0], g_ref[...], sc_ref[0], sh_ref[0])
    w = w_ref[...]
    w_hi = w.astype(BF16)
    w_lo = (w - w_hi.astype(F32)).astype(BF16)
    hf_hi = hf.astype(BF16)
    hf_ref[0] = hf_hi
    hf_lo = (hf - hf_hi.astype(F32)).astype(BF16)
    nt_dot = functools.partial(lax.dot_general, dimension_numbers=(((1,), (1,)), ((), ())),
                               preferred_element_type=F32)
    logits = nt_dot(w_hi, hf_hi) + (nt_dot(w_lo, hf_hi) + nt_dot(w_hi, hf_lo))
    tm = logits.shape[1]
    nr = ROUTER_ROWS
    lt = logits[:nr]
    row = lax.broadcasted_iota(jnp.int32, (nr, tm), 0).astype(F32)
    neg = -jnp.inf
    big = float(LANES)
    is_g = row < N_GROUPS
    gl = jnp.where(is_g, lt, neg)
    gmax = jnp.max(gl, axis=0, keepdims=True)
    g_sel = jnp.min(jnp.where(gl == gmax, row, big), axis=0, keepdims=True)
    p_g = 1.0 / jnp.sum(jnp.where(is_g, jnp.exp(gl - gmax), 0.0), axis=0, keepdims=True)
    e_row = row - N_GROUPS
    in_grp = (e_row >= g_sel * EXPERTS_PER_GROUP) & (e_row < (g_sel + 1) * EXPERTS_PER_GROUP)
    el = jnp.where(in_grp, lt, neg)
    emax = jnp.max(el, axis=0, keepdims=True)
    i1 = jnp.min(jnp.where(el == emax, e_row, big), axis=0, keepdims=True)
    el2 = jnp.where(e_row == i1, neg, el)
    emax2 = jnp.max(el2, axis=0, keepdims=True)
    i2 = jnp.min(jnp.where(el2 == emax2, e_row, big), axis=0, keepdims=True)
    t2 = jnp.exp(emax2 - emax)
    w1 = p_g / (1.0 + t2)
    w2 = p_g * t2 / (1.0 + t2)
    a = jnp.where(e_row == i1, 1.0, jnp.where(e_row == i2, 1.0, 0.0)).astype(BF16)
    r = lax.broadcasted_iota(jnp.int32, (tm, tm), 0)
    cc = lax.broadcasted_iota(jnp.int32, (tm, tm), 1)
    before = jnp.where(r < cc, 1.0, 0.0).astype(BF16)
    rank = jnp.dot(a, before, preferred_element_type=F32)
    a_full = jnp.concatenate([a, jnp.zeros((LANES - nr, tm), BF16)], axis=0)
    cnt = nt_dot(jnp.ones((SUBLANES, tm), BF16), a_full)[0:1]
    k8 = jnp.ceil(cnt * (1.0 / ROW_GROUP))
    ur = lax.broadcasted_iota(jnp.int32, (LANES, LANES), 0)
    uc = lax.broadcasted_iota(jnp.int32, (LANES, LANES), 1)
    upper = jnp.where(ur < uc, 1.0, 0.0).astype(BF16)
    k8_rows = jnp.broadcast_to(k8, (SUBLANES, LANES)).astype(BF16)
    off_rows = jnp.dot(k8_rows, upper, preferred_element_type=F32) * ROW_GROUP
    off = off_rows[0:1]
    off_col = off_rows.T[:nr, 0:1]
    pos = off_col + rank
    lp0 = jnp.sum(jnp.where(e_row == i1, pos, 0.0), axis=0, keepdims=True)
    lp1 = jnp.sum(jnp.where(e_row == i2, pos, 0.0), axis=0, keepdims=True)
    run_old = run_ref[...]
    run_new = run_old + k8 * ROW_GROUP
    run_ref[...] = run_new
    cnt_ref[...] = run_new.astype(jnp.int32)
    srow = lax.broadcasted_iota(jnp.int32, (SUBLANES, LANES), 0)
    stats_ref[0] = jnp.where(srow == 0, k8, jnp.where(srow == 1, off, jnp.where(
        srow == 2, run_old, 0.0))).astype(jnp.int32)
    trow = lax.broadcasted_iota(jnp.int32, (SUBLANES, tm), 0)
    lpr_ref[0, 0] = jnp.where(trow == 0, lp0, jnp.where(trow == 1, lp1, 0.0)).astype(jnp.int32)
    wr_ref[0, 0] = jnp.where(trow == 0, w1, jnp.where(trow == 1, w2, 0.0))
    frow = lax.broadcasted_iota(jnp.int32, (LANES, tm), 0)
    lpc_ref[0] = jnp.where(frow == 0, lp0, jnp.where(frow == 1, lp1, 0.0)).T.astype(jnp.int32)


def _router(x, g, sc, sh, w_group, w_router):
    b, s, d = x.shape
    ts = min(MOE_TOKEN_TILE, s)
    nt = s // ts
    w = jnp.concatenate([w_group, w_router], axis=1).T
    w = jnp.pad(w, ((0, LANES - w.shape[0]), (0, 0)))
    row = lambda i, j: (i, j, 0)
    return pl.pallas_call(
        _router_kernel,
        grid=(b, nt),
        in_specs=[
            pl.BlockSpec((1, ts, d), row),
            pl.BlockSpec((1, d), lambda i, j: (0, 0)),
            pl.BlockSpec((1, 1, d), lambda i, j: (i, 0, 0)),
            pl.BlockSpec((1, 1, d), lambda i, j: (i, 0, 0)),
            pl.BlockSpec((LANES, d), lambda i, j: (0, 0)),
        ],
        out_specs=[
            pl.BlockSpec((1, ts, LANES), row),
            pl.BlockSpec((1, 1, SUBLANES, ts), lambda i, j: (i, j, 0, 0)),
            pl.BlockSpec((1, 1, SUBLANES, ts), lambda i, j: (i, j, 0, 0)),
            pl.BlockSpec((1, SUBLANES, LANES), lambda i, j: (i * nt + j, 0, 0)),
            pl.BlockSpec((1, LANES), lambda i, j: (0, 0)),
            pl.BlockSpec((1, ts, d), row),
        ],
        out_shape=[jax.ShapeDtypeStruct((b, s, LANES), jnp.int32),
                   jax.ShapeDtypeStruct((b, nt, SUBLANES, ts), jnp.int32),
                   jax.ShapeDtypeStruct((b, nt, SUBLANES, ts), F32),
                   jax.ShapeDtypeStruct((b * nt, SUBLANES, LANES), jnp.int32),
                   jax.ShapeDtypeStruct((1, LANES), jnp.int32),
                   jax.ShapeDtypeStruct((b, s, d), BF16)],
        scratch_shapes=[pltpu.VMEM((1, LANES), F32)],
        compiler_params=_params("arbitrary", "arbitrary"),
    )(x, g.reshape(1, d), sc, sh, w)


def _chunk_copies(meta_ref, blk, local_ref, hbm_ref, sem, *, to_hbm, start):
    def piece(lo, hi, rows):
        loc = local_ref.at[pl.ds(pl.multiple_of(lo, ROW_GROUP), rows)]
        far = hbm_ref.at[pl.ds(pl.multiple_of(hi, ROW_GROUP), rows)]
        cp = pltpu.make_async_copy(loc, far, sem) if to_hbm else pltpu.make_async_copy(far, loc, sem)
        if start:
            cp.start()
        else:
            cp.wait()

    def per_expert(e, carry):
        at = blk * LANES + e
        k = meta_ref[at]
        lo = meta_ref[at + N_EXPERTS]
        hi = meta_ref[at + 2 * N_EXPERTS]
        pairs = k >> 1

        def pair(i, c):
            piece(lo + i * (2 * ROW_GROUP), hi + i * (2 * ROW_GROUP), 2 * ROW_GROUP)
            return c

        lax.fori_loop(0, pairs, pair, 0)

        @pl.when((k & 1) == 1)
        def _():
            piece(lo + pairs * (2 * ROW_GROUP), hi + pairs * (2 * ROW_GROUP), ROW_GROUP)

        return carry

    lax.fori_loop(0, N_EXPERTS, per_expert, 0)


def _dispatch_kernel(last_ref, meta_ref, lpr_ref, wr_ref, hf_ref, xs_ref, loc_ref, zero_ref, sem, zsem):
    tm = hf_ref.shape[1]
    zrows = zero_ref.shape[0]
    blk = pl.program_id(0) * pl.num_programs(1) + pl.program_id(1)
    n_blk = pl.num_programs(0) * pl.num_programs(1)
    slot = blk % 2

    @pl.when((pl.program_id(0) == 0) & (pl.program_id(1) == 0))
    def _():
        zero_ref[...] = jnp.zeros_like(zero_ref)

        def fill(e):
            start = pl.multiple_of(last_ref[e], zrows)
            return pltpu.make_async_copy(zero_ref, xs_ref.at[pl.ds(start, zrows)], zsem)

        for e in range(N_EXPERTS):
            @pl.when(last_ref[e] >= 0)
            def _():
                fill(e).start()
        for e in range(N_EXPERTS):
            @pl.when(last_ref[e] >= 0)
            def _():
                fill(e).wait()

        def tail(k):
            start = pl.multiple_of(k * zrows, zrows)
            return pltpu.make_async_copy(zero_ref, xs_ref.at[pl.ds(start, zrows)], zsem)

        n_used = last_ref[N_EXPERTS]
        n_tiles = xs_ref.shape[0] // zrows
        lax.fori_loop(n_used, n_tiles, lambda k, c: (tail(k).start(), c)[1], 0)
        lax.fori_loop(n_used, n_tiles, lambda k, c: (tail(k).wait(), c)[1], 0)

    hf = hf_ref[0]
    d = hf.shape[1]
    n_loc = loc_ref.shape[1]
    rows = lax.broadcasted_iota(jnp.int32, (n_loc, tm), 0)
    pick0 = rows == lpr_ref[0, 0, 0:1, :]
    pick1 = rows == lpr_ref[0, 0, 1:2, :]
    sel = jnp.where(pick0, 1.0, jnp.where(pick1, 1.0, 0.0)).astype(BF16)
    loc_ref[slot, :, :d] = jnp.dot(sel, hf, preferred_element_type=F32).astype(BF16)
    wrow = jnp.sum(jnp.where(pick0, wr_ref[0, 0, 0:1, :], jnp.where(pick1, wr_ref[0, 0, 1:2, :], 0.0)),
                   axis=1, keepdims=True)
    w_hi = wrow.astype(BF16).astype(F32)
    lane = lax.broadcasted_iota(jnp.int32, (n_loc, LANES), 1)
    loc_ref[slot, :, d:] = jnp.where(lane == 0, w_hi, jnp.where(lane == 1, wrow - w_hi, 0.0)).astype(BF16)

    def copies(b, sl, start):
        _chunk_copies(meta_ref, b, loc_ref.at[sl], xs_ref, sem.at[sl], to_hbm=True, start=start)

    copies(blk, slot, True)

    @pl.when(blk > 0)
    def _():
        copies(blk - 1, 1 - slot, False)

    @pl.when(blk == n_blk - 1)
    def _():
        copies(blk, slot, False)


def _local_rows(tm):
    return 2 * tm + N_EXPERTS * ROW_GROUP


def _dispatch(hf, meta, lpr, wr, last_tile_row, n_rows):
    b, s, d = hf.shape
    tm = min(MOE_TOKEN_TILE, s)
    nt = s // tm
    dw = d + LANES
    row = lambda i, j, last, meta: (i, j, 0)
    grid_spec = pltpu.PrefetchScalarGridSpec(
        num_scalar_prefetch=2,
        grid=(b, nt),
        in_specs=[
            pl.BlockSpec((1, 1, SUBLANES, tm), lambda i, j, last, meta: (i, j, 0, 0)),
            pl.BlockSpec((1, 1, SUBLANES, tm), lambda i, j, last, meta: (i, j, 0, 0)),
            pl.BlockSpec((1, tm, d), row),
        ],
        out_specs=pl.BlockSpec(memory_space=pl.ANY),
        scratch_shapes=[pltpu.VMEM((2, _local_rows(tm), dw), BF16), pltpu.VMEM((MOE_ROW_TILE, dw), BF16),
                        pltpu.SemaphoreType.DMA((2,)), pltpu.SemaphoreType.DMA(())],
    )
    return pl.pallas_call(
        _dispatch_kernel,
        grid_spec=grid_spec,
        out_shape=jax.ShapeDtypeStruct((n_rows, dw), BF16),
        compiler_params=_params("arbitrary", "arbitrary"),
    )(last_tile_row, meta, lpr, wr, hf)


def _expert_kernel(te_ref, nu_ref, xs_ref, w1_ref, w3_ref, w2_ref, ys_ref, w1b_ref, w3b_ref, w2b_ref):
    i = pl.program_id(0)

    @pl.when((i == 0) | (te_ref[i] != te_ref[jnp.maximum(i - 1, 0)]))
    def _():
        w1b_ref[...] = w1_ref[0, 0].astype(BF16)
        w3b_ref[...] = w3_ref[0, 0].astype(BF16)
        w2b_ref[...] = w2_ref[0, 0].astype(BF16)

    @pl.when(i < nu_ref[0])
    def _():
        d = w1b_ref.shape[0]
        xb = xs_ref[:, :d]
        gate = xs_ref[:, d:d + 1].astype(F32) + xs_ref[:, d + 1:d + 2].astype(F32)
        a = jnp.dot(xb, w1b_ref[...], preferred_element_type=F32)
        bb = jnp.dot(xb, w3b_ref[...], preferred_element_type=F32)
        hmid = (_silu(a) * bb * gate).astype(BF16)
        ys_ref[...] = jnp.dot(hmid, w2b_ref[...], preferred_element_type=F32).astype(ys_ref.dtype)

    @pl.when(i >= nu_ref[0])
    def _():
        ys_ref[...] = jnp.zeros_like(ys_ref)


def _expert_ffn(xs, tile_expert, n_used, w1, w3, w2, layer):
    p, dw = xs.shape
    d, de = w1.shape[2], w1.shape[3]
    tm = MOE_ROW_TILE
    grid_spec = pltpu.PrefetchScalarGridSpec(
        num_scalar_prefetch=2,
        grid=(p // tm,),
        in_specs=[
            pl.BlockSpec((tm, dw), lambda i, te, nu: (jnp.maximum(jnp.minimum(i, nu[0] - 1), 0), 0)),
            pl.BlockSpec((1, 1, d, de), lambda i, te, nu: (layer, te[i], 0, 0)),
            pl.BlockSpec((1, 1, d, de), lambda i, te, nu: (layer, te[i], 0, 0)),
            pl.BlockSpec((1, 1, de, d), lambda i, te, nu: (layer, te[i], 0, 0)),
        ],
        out_specs=pl.BlockSpec((tm, d), lambda i, te, nu: (i, 0)),
        scratch_shapes=[pltpu.VMEM((d, de), BF16), pltpu.VMEM((d, de), BF16), pltpu.VMEM((de, d), BF16)],
    )
    return pl.pallas_call(
        _expert_kernel,
        grid_spec=grid_spec,
        out_shape=jax.ShapeDtypeStruct((p, d), BF16),
        compiler_params=_params("arbitrary"),
    )(tile_expert, n_used, xs, w1, w3, w2)


def _combine_kernel(meta_ref, ys_ref, lpc_ref, x_ref, gate_ref, o_ref, loc_ref, sem):
    tm = x_ref.shape[1]
    n_loc = loc_ref.shape[1]
    blk = pl.program_id(0) * pl.num_programs(1) + pl.program_id(1)
    n_blk = pl.num_programs(0) * pl.num_programs(1)
    slot = blk % 2

    def fetch(b, sl):
        loc_ref[sl, 2 * tm:, :] = jnp.zeros((n_loc - 2 * tm, loc_ref.shape[2]), BF16)
        _chunk_copies(meta_ref, b, loc_ref.at[sl], ys_ref, sem.at[sl], to_hbm=False, start=True)

    @pl.when(blk == 0)
    def _():
        fetch(blk, slot)

    @pl.when(blk + 1 < n_blk)
    def _():
        fetch(blk + 1, 1 - slot)

    _chunk_copies(meta_ref, blk, loc_ref.at[slot], ys_ref, sem.at[slot], to_hbm=False, start=False)
    cols = lax.broadcasted_iota(jnp.int32, (tm, n_loc), 1)
    lp = lpc_ref[0]
    sel = jnp.where(cols == lp[:, 0:1], 1.0, jnp.where(cols == lp[:, 1:2], 1.0, 0.0)).astype(BF16)
    y = jnp.dot(sel, loc_ref[slot], preferred_element_type=F32)
    o_ref[0] = x_ref[0] + gate_ref[0] * y


def _combine(ys, meta, lpc, x, gate):
    b, s, d = x.shape
    tm = min(MOE_TOKEN_TILE, s)
    nt = s // tm
    row = lambda i, j, meta: (i, j, 0)
    grid_spec = pltpu.PrefetchScalarGridSpec(
        num_scalar_prefetch=1,
        grid=(b, nt),
        in_specs=[
            pl.BlockSpec(memory_space=pl.ANY),
            pl.BlockSpec((1, tm, LANES), row),
            pl.BlockSpec((1, tm, d), row),
            pl.BlockSpec((1, 1, d), lambda i, j, meta: (i, 0, 0)),
        ],
        out_specs=pl.BlockSpec((1, tm, d), row),
        scratch_shapes=[pltpu.VMEM((2, _local_rows(tm), d), BF16), pltpu.SemaphoreType.DMA((2,))],
    )
    return pl.pallas_call(
        _combine_kernel,
        grid_spec=grid_spec,
        out_shape=jax.ShapeDtypeStruct((b, s, d), F32),
        compiler_params=_params("arbitrary", "arbitrary"),
    )(meta, ys, lpc, x, gate)


def _hier_moe_residual(x, g, sc, sh, gate, w_group, w_router, w1, w3, w2, layer):
    b, s, d = x.shape
    t = b * s
    lpc, lpr, wr, stats, cnt, hf = _router(x, g, sc, sh, w_group, w_router)
    tm = MOE_ROW_TILE
    ex = slice(N_GROUPS, N_GROUPS + N_EXPERTS)
    counts = cnt[0, ex]
    tiles = (counts + tm - 1) // tm
    tile_end = jnp.cumsum(tiles)
    base = (tile_end - tiles) * tm
    n_blocks = stats.shape[0]
    n_tiles = (2 * t + n_blocks * N_EXPERTS * ROW_GROUP) // tm + N_EXPERTS
    tile_expert = jnp.minimum(
        jnp.sum(tile_end[None, :] <= jnp.arange(n_tiles)[:, None], axis=1), N_EXPERTS - 1).astype(jnp.int32)
    n_used = tile_end[-1:].astype(jnp.int32)
    meta = jnp.concatenate([stats[:, 0, ex], stats[:, 1, ex], base[None, :] + stats[:, 2, ex],
                            jnp.zeros((n_blocks, LANES - 3 * N_EXPERTS), jnp.int32)], axis=1)
    meta = meta.astype(jnp.int32).reshape(n_blocks * LANES)
    last_tile_row = jnp.where(tiles > 0, (tile_end - 1) * tm, -1).astype(jnp.int32)
    last_tile_row = jnp.concatenate([last_tile_row, n_used])
    xs = _dispatch(hf, meta, lpr, wr, last_tile_row, n_tiles * tm)
    ys = _expert_ffn(xs, tile_expert, n_used, w1, w3, w2, layer)
    return _combine(ys, meta, lpc, x, gate)


def kernel(x, c, rel_table, ada_w, ada_b, norm_mix_g, norm_ffn_g, da_w_in, da_w_out, da_q_gain, da_k_gain, da_lam_q1, da_lam_k1, da_lam_q2, da_lam_k2, da_subln_g, ml_w_in, ml_conv_w, ml_conv_b, ml_wq, ml_wk, ml_wv, ml_gate_w, ml_gate_b, ml_outnorm_g, ml_skip, ml_w_out, moe_w_group, moe_w_router, moe_w1, moe_w3, moe_w2):
    depth = ada_w.shape[0]
    d = x.shape[-1]
    mod = _ada_mod(c, ada_w, ada_b)
    for i in range(depth):
        sh1, sc1, g1, sh2, sc2, g2 = [mod[i, :, None, k * d:(k + 1) * d] for k in range(6)]
        j = i // N_MIXERS
        if i % N_MIXERS == 0:
            qkv = _prenorm_matmul(x, norm_mix_g[i], sc1, sh1, da_w_in[j].astype(BF16),
                                  head_gain=_qk_head_gains(da_q_gain[j], da_k_gain[j]), half=DA_HEAD_DIM)
            lam_init = 0.8 - 0.6 * math.exp(-0.3 * i)
            lam = (jnp.exp(jnp.sum(da_lam_q1[j] * da_lam_k1[j])) - jnp.exp(jnp.sum(da_lam_q2[j] * da_lam_k2[j]))
                   + lam_init).astype(F32)
            o = _diff_attention(qkv, da_subln_g[j], lam, lam_init, rel_table)
            x = _proj_residual(o, da_w_out[j].astype(BF16), x, g1)
        else:
            up = _prenorm_matmul(x, norm_mix_g[i], sc1, sh1, ml_w_in[j].astype(BF16))
            q, k, v, xc, pre = _mlstm_pre(up, ml_conv_w[j], ml_conv_b[j], ml_wq[j], ml_wk[j], ml_wv[j],
                                          ml_gate_w[j], ml_gate_b[j])
            hn = _mlstm_scan(q, k, v, pre, ml_outnorm_g[j])
            x = _mlstm_out(hn, xc, up, ml_skip[j], ml_w_out[j].astype(BF16), x, g1)
        x = _hier_moe_residual(x, norm_ffn_g[i], sc2, sh2, g2, moe_w_group[i], moe_w_router[i],
                               moe_w1, moe_w3, moe_w2, i)
    return x
```

```python
import functools
import math

import jax
import jax.numpy as jnp
from jax import lax
from jax.experimental import pallas as pl
from jax.experimental.pallas import tpu as pltpu

EPS = 1e-6
N_MIXERS = 2
DA_HEADS = 8
DA_HEAD_DIM = 64
N_BUCKETS = 32
MAX_DISTANCE = 128
ML_HEADS = 4
ML_CHUNK = 256
N_GROUPS = 4
EXPERTS_PER_GROUP = 8
N_EXPERTS = N_GROUPS * EXPERTS_PER_GROUP

LANES = 128
SUBLANES = 8
ROW_GROUP = 16
VMEM_LIMIT = 48 * 1024 * 1024
ATTN_Q_TILE = 2048
ATTN_SEQS_PER_STEP = 1
ATTN_ROW_BLOCK = 128
ROW_TILE = 512
MOE_ROW_TILE = 512
MOE_TOKEN_TILE = 512
ROUTER_ROWS = -(-(N_GROUPS + N_EXPERTS) // ROW_GROUP) * ROW_GROUP
MLSTM_COL_TILE = 256
MLSTM_ROW_CHUNK = 512

F32 = jnp.float32
BF16 = jnp.bfloat16
LOG2E = math.log2(math.e)


def _params(*sem):
    return pltpu.CompilerParams(dimension_semantics=sem, vmem_limit_bytes=VMEM_LIMIT)


def _silu(x):
    return x / (1.0 + jnp.exp2(x * -LOG2E))


def _log_sigmoid(x):
    return jnp.minimum(x, 0.0) - jnp.log(1.0 + jnp.exp(-jnp.abs(x)))


def _modulated_norm(x, g, sc, sh):
    ms = jnp.mean(x * x, axis=-1, keepdims=True)
    return x * lax.rsqrt(ms + EPS) * g * (1.0 + sc) + sh


def _ada_kernel(c_ref, w_ref, b_ref, o_ref):
    ca = _silu(c_ref[...]).astype(BF16)
    o_ref[0] = jnp.dot(ca, w_ref[0].astype(BF16), preferred_element_type=F32) + b_ref[0]


def _ada_mod(c, ada_w, ada_b):
    depth, d, n = ada_w.shape
    b = c.shape[0]
    tn = 1536
    return pl.pallas_call(
        _ada_kernel,
        grid=(depth, n // tn),
        in_specs=[
            pl.BlockSpec((b, d), lambda i, j: (0, 0)),
            pl.BlockSpec((1, d, tn), lambda i, j: (i, 0, j)),
            pl.BlockSpec((1, 1, tn), lambda i, j: (i, 0, j)),
        ],
        out_specs=pl.BlockSpec((1, b, tn), lambda i, j: (i, 0, j)),
        out_shape=jax.ShapeDtypeStruct((depth, b, n), F32),
        compiler_params=_params("parallel", "parallel"),
    )(c, ada_w, ada_b.reshape(depth, 1, n))


def _prenorm_matmul_kernel(x_ref, g_ref, sc_ref, sh_ref, w_ref, hg_ref, o_ref, *, n_chunk, n_norm, half):
    y = _modulated_norm(x_ref[0], g_ref[...], sc_ref[0], sh_ref[0]).astype(BF16)
    n = w_ref.shape[1]
    for n0 in range(0, n, n_chunk):
        acc = jnp.dot(y, w_ref[:, n0:n0 + n_chunk], preferred_element_type=F32)
        if n0 < n_norm:
            for c0 in range(n0, n0 + n_chunk, 2 * half):
                blk = _group_rms(acc[:, c0 - n0:c0 - n0 + 2 * half], hg_ref[:, c0:c0 + 2 * half], half)
                o_ref[0, :, c0:c0 + 2 * half] = blk.astype(o_ref.dtype)
        else:
            o_ref[0, :, n0:n0 + n_chunk] = acc.astype(o_ref.dtype)


def _prenorm_matmul(x, g, sc, sh, w, head_gain=None, half=LANES // 2):
    b, s, d = x.shape
    n = w.shape[1]
    ts = min(ROW_TILE, s)
    n_norm = 0 if head_gain is None else head_gain.shape[1]
    hg = jnp.ones((1, n), F32) if head_gain is None else jnp.pad(head_gain, ((0, 0), (0, n - n_norm)))
    return pl.pallas_call(
        functools.partial(_prenorm_matmul_kernel, n_chunk=1024, n_norm=n_norm, half=half),
        grid=(b, s // ts),
        in_specs=[
            pl.BlockSpec((1, ts, d), lambda i, j: (i, j, 0)),
            pl.BlockSpec((1, d), lambda i, j: (0, 0)),
            pl.BlockSpec((1, 1, d), lambda i, j: (i, 0, 0)),
            pl.BlockSpec((1, 1, d), lambda i, j: (i, 0, 0)),
            pl.BlockSpec((d, n), lambda i, j: (0, 0)),
            pl.BlockSpec((1, n), lambda i, j: (0, 0)),
        ],
        out_specs=pl.BlockSpec((1, ts, n), lambda i, j: (i, j, 0)),
        out_shape=jax.ShapeDtypeStruct((b, s, n), BF16),
        compiler_params=_params("parallel", "parallel"),
    )(x, g.reshape(1, d), sc, sh, w, hg)


def _proj_residual_kernel(a_ref, w_ref, x_ref, gate_ref, o_ref):
    y = jnp.dot(a_ref[0], w_ref[...], preferred_element_type=F32)
    o_ref[0] = x_ref[0] + gate_ref[0] * y


def _proj_residual(a, w, x, gate):
    b, s, k = a.shape
    d = w.shape[1]
    ts = min(ROW_TILE, s)
    return pl.pallas_call(
        _proj_residual_kernel,
        grid=(b, s // ts),
        in_specs=[
            pl.BlockSpec((1, ts, k), lambda i, j: (i, j, 0)),
            pl.BlockSpec((k, d), lambda i, j: (0, 0)),
            pl.BlockSpec((1, ts, d), lambda i, j: (i, j, 0)),
            pl.BlockSpec((1, 1, d), lambda i, j: (i, 0, 0)),
        ],
        out_specs=pl.BlockSpec((1, ts, d), lambda i, j: (i, j, 0)),
        out_shape=jax.ShapeDtypeStruct((b, s, d), F32),
        compiler_params=_params("parallel", "parallel"),
    )(a, w, x, gate)


def _t5_bucket(rel):
    nb = N_BUCKETS // 2
    max_exact = nb // 2
    ret = jnp.where(rel > 0, nb, 0)
    n = jnp.abs(rel)
    nf = jnp.maximum(n, 1).astype(F32)
    large = max_exact + (jnp.log(nf / max_exact) / math.log(MAX_DISTANCE / max_exact)
                         * (nb - max_exact)).astype(jnp.int32)
    large = jnp.minimum(large, nb - 1)
    return ret + jnp.where(n < max_exact, n, large)


def _bias_band_kernel(t_ref, o_ref, *, tq, nb):
    u = jnp.broadcast_to(t_ref[0], (tq, t_ref.shape[2]))
    y = pltpu.roll(u, 1, 1, stride=1, stride_axis=0)
    for jb in range(nb):
        o_ref[0, jb] = y[:, tq + jb * LANES:tq + (jb + 1) * LANES]


def _bias_band(rel_table, s, tq):
    h = rel_table.shape[1]
    delta = jnp.arange(-(s - 1), s)
    t = rel_table[_t5_bucket(delta)].T.astype(F32) * LOG2E
    u = jnp.concatenate([t, jnp.zeros((h, 1), F32)], axis=1).reshape(h, 1, 2 * s)
    nb = (2 * s - tq) // LANES
    return pl.pallas_call(
        functools.partial(_bias_band_kernel, tq=tq, nb=nb),
        grid=(h,),
        in_specs=[pl.BlockSpec((1, 1, 2 * s), lambda i: (i, 0, 0))],
        out_specs=pl.BlockSpec((1, nb, tq, LANES), lambda i: (i, 0, 0, 0)),
        out_shape=jax.ShapeDtypeStruct((h, nb, tq, LANES), F32),
        compiler_params=_params("parallel"),
    )(u)


def _group_rms(x, gain, half):
    sq = x * x
    lane = lax.broadcasted_iota(jnp.int32, x.shape, 1)
    lo = lane < half
    s_lo = jnp.sum(jnp.where(lo, sq, 0.0), axis=-1, keepdims=True)
    s_all = jnp.sum(sq, axis=-1, keepdims=True)
    ms = jnp.where(lo, s_lo, s_all - s_lo) * (1.0 / half)
    return x * lax.rsqrt(ms + EPS) * gain


def _diff_attn_kernel(lam_ref, q_ref, k_ref, v_ref, sg_ref, band_ref, o_ref, ve_ref, *, tq, s, dh, out_scale):
    qi = pl.program_id(2)
    nq = pl.num_programs(2)
    dv = 2 * dh

    n_seq = q_ref.shape[0]

    @pl.when(qi == 0)
    def _():
        for bb in range(n_seq):
            ve_ref[bb, :, :dv] = v_ref[bb]
            ve_ref[bb, :, dv:] = jnp.ones((s, dv), BF16)

    lam = lam_ref[0]
    jb0 = (nq - 1 - qi) * (tq // LANES)
    nkb = s // LANES
    rb = ATTN_ROW_BLOCK
    for bb in range(n_seq):
        q = q_ref[bb]
        lane = lax.broadcasted_iota(jnp.int32, q.shape, 1)
        zero = jnp.zeros_like(q)
        q0 = jnp.where(lane < dh, q, zero)
        q1 = jnp.where(lane < dh, zero, q)

        def softmax_av(qm, r0, r1):
            sc = lax.dot_general(qm[r0:r1], k_ref[bb], (((1,), (1,)), ((), ())), preferred_element_type=F32)
            sc = sc + jnp.concatenate([band_ref[0, jb0 + kb, r0:r1, :] for kb in range(nkb)], axis=1)
            p = jnp.exp2(sc - jnp.max(sc, axis=-1, keepdims=True)).astype(BF16)
            oe = jnp.dot(p, ve_ref[bb], preferred_element_type=F32)
            return oe[:, :dv], oe[:, dv:dv + 1]

        for r0 in range(0, tq, rb):
            n0, l0 = softmax_av(q0, r0, r0 + rb)
            n1, l1 = softmax_av(q1, r0, r0 + rb)
            o = n0 * (1.0 / l0) - n1 * (lam / l1)
            ms = jnp.mean(o * o, axis=-1, keepdims=True)
            o_ref[bb, r0:r0 + rb, :] = (o * lax.rsqrt(ms + EPS) * sg_ref[...] * out_scale).astype(o_ref.dtype)


def _qk_head_gains(q_gain, k_gain):
    h, dh = DA_HEADS, DA_HEAD_DIM
    qg = jnp.tile(q_gain * (dh ** -0.5 * LOG2E), 2 * h)
    kg = jnp.tile(k_gain, 2 * h)
    return jnp.concatenate([qg, kg]).reshape(1, 4 * h * dh).astype(F32)


def _diff_attention(qkv, subln_g, lam, lam_init, rel_table):
    b, s, _ = qkv.shape
    h, dh = DA_HEADS, DA_HEAD_DIM
    tq = min(ATTN_Q_TILE, s)
    band = _bias_band(rel_table, s, tq)
    nb = band.shape[1]
    nsq = math.gcd(ATTN_SEQS_PER_STEP, b)
    kern = functools.partial(_diff_attn_kernel, tq=tq, s=s, dh=dh, out_scale=1.0 - lam_init)
    grid_spec = pltpu.PrefetchScalarGridSpec(
        num_scalar_prefetch=1,
        grid=(h, b // nsq, s // tq),
        in_specs=[
            pl.BlockSpec((nsq, tq, 2 * dh), lambda hi, bi, qi, lam: (bi, qi, hi)),
            pl.BlockSpec((nsq, s, 2 * dh), lambda hi, bi, qi, lam: (bi, 0, h + hi)),
            pl.BlockSpec((nsq, s, 2 * dh), lambda hi, bi, qi, lam: (bi, 0, 2 * h + hi)),
            pl.BlockSpec((1, 2 * dh), lambda hi, bi, qi, lam: (0, 0)),
            pl.BlockSpec((1, nb, tq, LANES), lambda hi, bi, qi, lam: (hi, 0, 0, 0),
                         pipeline_mode=pl.Buffered(1)),
        ],
        out_specs=pl.BlockSpec((nsq, tq, 2 * dh), lambda hi, bi, qi, lam: (bi, qi, hi)),
        scratch_shapes=[pltpu.VMEM((nsq, s, 4 * dh), BF16)],
    )
    return pl.pallas_call(
        kern,
        grid_spec=grid_spec,
        out_shape=jax.ShapeDtypeStruct((b, s, h * 2 * dh), BF16),
        compiler_params=_params("parallel", "parallel", "arbitrary"),
    )(lam.reshape(1), qkv, qkv, qkv, subln_g.reshape(1, 2 * dh), band)


def _mlstm_pre_kernel(xm_ref, cw_ref, cb_ref, wq_ref, wk_ref, wv_ref, gq_ref, gk_ref, gv_ref, gb_ref,
                      q_ref, k_ref, v_ref, xc_ref, pre_ref, pad_ref, *, s, halo, n_heads):
    j = pl.program_id(1)
    xm = xm_ref[0].astype(F32)
    cbw = xm.shape[1]
    pad_ref[0:8, :] = jnp.zeros((8, cbw), F32)
    pad_ref[8 + s:16 + s, :] = jnp.zeros((8, cbw), F32)
    pad_ref[8:8 + s, :] = xm

    @pl.when(j == 0)
    def _():
        pre_ref[0] = jnp.broadcast_to(gb_ref[...], pre_ref.shape[1:])

    rc = min(MLSTM_ROW_CHUNK, s)
    for r0 in range(0, s, rc):
        rows = slice(r0, r0 + rc)
        acc = cb_ref[...] + cw_ref[0:1, :] * pad_ref[8 - halo + r0:8 - halo + r0 + rc, :]
        for t in range(1, 2 * halo + 1):
            acc = acc + cw_ref[t:t + 1, :] * pad_ref[8 - halo + t + r0:8 - halo + t + r0 + rc, :]
        xcb = _silu(acc).astype(BF16)
        q = jnp.dot(xcb, wq_ref[0], preferred_element_type=F32).astype(BF16)
        k = jnp.dot(xcb, wk_ref[0], preferred_element_type=F32).astype(BF16)
        v = jnp.dot(xm_ref[0, rows, :], wv_ref[0], preferred_element_type=F32).astype(BF16)
        q_ref[0, rows, :] = q
        k_ref[0, rows, :] = k
        v_ref[0, rows, :] = v
        xc_ref[0, rows, :] = xcb
        pre_ref[0, rows, :] += (jnp.dot(q, gq_ref[...], preferred_element_type=F32)
                                + jnp.dot(k, gk_ref[...], preferred_element_type=F32)
                                + jnp.dot(v, gv_ref[...], preferred_element_type=F32))

    @pl.when(j == pl.num_programs(1) - 1)
    def _():
        pre = pre_ref[0]
        lane = lax.broadcasted_iota(jnp.int32, pre.shape, 1)
        is_forget = (lane % (2 * n_heads)) >= n_heads
        pre_ref[0] = jnp.where(is_forget, _log_sigmoid(pre), pre)


def _blockdiag_dense(w, cb):
    nblk, blk, _ = w.shape
    per = cb // blk
    w4 = w.reshape(nblk // per, per, blk, blk)
    eye = jnp.eye(per, dtype=w.dtype)
    return jnp.einsum('jnio,nm->jnimo', w4, eye).reshape(nblk // per, cb, cb).astype(BF16)


def _mlstm_pre(up, conv_w, conv_b, wq, wk, wv, gate_w, gate_b):
    b, s, c2 = up.shape
    c = c2 // 2
    cb = MLSTM_COL_TILE
    ncb = c // cb
    kw = conv_w.shape[0]
    ng = gate_w.shape[0] * gate_w.shape[-1]

    def gate_mat(i):
        g = gate_w[:, i].transpose(1, 0, 2).reshape(c, ng)
        return jnp.pad(g, ((0, 0), (0, LANES - ng))).astype(BF16)

    gb = jnp.pad(gate_b.reshape(1, ng), ((0, 0), (0, LANES - ng)))
    col = lambda i, j: (i, 0, j)
    out_bf = jax.ShapeDtypeStruct((b, s, c), BF16)
    return pl.pallas_call(
        functools.partial(_mlstm_pre_kernel, s=s, halo=kw // 2, n_heads=gate_w.shape[-1] // 2),
        grid=(b, ncb),
        in_specs=[
            pl.BlockSpec((1, s, cb), col),
            pl.BlockSpec((kw, cb), lambda i, j: (0, j)),
            pl.BlockSpec((1, cb), lambda i, j: (0, j)),
            pl.BlockSpec((1, cb, cb), lambda i, j: (j, 0, 0)),
            pl.BlockSpec((1, cb, cb), lambda i, j: (j, 0, 0)),
            pl.BlockSpec((1, cb, cb), lambda i, j: (j, 0, 0)),
            pl.BlockSpec((cb, LANES), lambda i, j: (j, 0)),
            pl.BlockSpec((cb, LANES), lambda i, j: (j, 0)),
            pl.BlockSpec((cb, LANES), lambda i, j: (j, 0)),
            pl.BlockSpec((1, LANES), lambda i, j: (0, 0)),
        ],
        out_specs=[
            pl.BlockSpec((1, s, cb), col),
            pl.BlockSpec((1, s, cb), col),
            pl.BlockSpec((1, s, cb), col),
            pl.BlockSpec((1, s, cb), col),
            pl.BlockSpec((1, s, LANES), lambda i, j: (i, 0, 0)),
        ],
        out_shape=[out_bf, out_bf, out_bf, out_bf, jax.ShapeDtypeStruct((b, s, LANES), F32)],
        scratch_shapes=[pltpu.VMEM((s + 16, cb), F32)],
        compiler_params=_params("parallel", "arbitrary"),
    )(up, conv_w, conv_b.reshape(1, c), _blockdiag_dense(wq, cb), _blockdiag_dense(wk, cb),
      _blockdiag_dense(wv, cb), gate_mat(0), gate_mat(1), gate_mat(2), gb)


def _mlstm_scan_kernel(q_ref, k_ref, v_ref, gcol_ref, grow_ref, og_ref, o_ref,
                       cf_ref, cbk_ref, hf_ref, hb_ref, *, L, nc, dk):
    ln_scale = -0.5 * math.log(dk)
    dirn = lax.broadcasted_iota(jnp.int32, (2, L, L), 0)
    row = lax.broadcasted_iota(jnp.int32, (2, L, L), 1)
    coli = lax.broadcasted_iota(jnp.int32, (2, L, L), 2)
    lo = jnp.where(dirn == 0, coli, row)
    hi = jnp.where(dirn == 0, row, coli)
    mask = lo <= hi
    mask_t = hi <= lo
    cf_ref[...] = jnp.zeros_like(cf_ref)
    cbk_ref[...] = jnp.zeros_like(cbk_ref)
    bdot = functools.partial(lax.dot_general, preferred_element_type=F32)

    def body(i, carry):
        n, m = carry
        sl_f = pl.ds(pl.multiple_of(i * L, L), L)
        sl_b = pl.ds(pl.multiple_of((nc - 1 - i) * L, L), L)
        qb = jnp.stack([q_ref[0, sl_f, :], q_ref[0, sl_b, :]])
        kb = jnp.stack([k_ref[0, sl_f, :], k_ref[0, sl_b, :]])
        vb = jnp.stack([v_ref[0, sl_f, :], v_ref[0, sl_b, :]])
        gcf, gcb = gcol_ref[0, 0, i], gcol_ref[0, 0, nc - 1 - i]
        grf, grb = grow_ref[0, 0, i], grow_ref[0, 0, nc - 1 - i]
        ii_col = jnp.stack([gcf[:, 0:1], gcb[:, 2:3]])
        lf_col = jnp.stack([gcf[:, 1:2], gcb[:, 3:4]])
        ii_row = jnp.stack([grf[0:1, :], grb[2:3, :]])
        lf_row = jnp.stack([grf[1:2, :], grb[3:4, :]])
        b_col = jnp.sum(jnp.where(mask, lf_row, 0.0), axis=2, keepdims=True)
        b_row = jnp.sum(jnp.where(mask_t, lf_col, 0.0), axis=1, keepdims=True)
        dmat = jnp.where(mask, b_col - b_row + ii_row, -jnp.inf)
        inter = b_col + m
        m_t = jnp.maximum(inter, jnp.max(dmat, axis=2, keepdims=True))
        w_intra = jnp.exp(dmat - (m_t - ln_scale))
        w_inter = jnp.exp(inter - m_t)
        sqk = bdot(qb, kb, (((2,), (2,)), ((0,), (0,))))
        sw = sqk * w_intra
        cmat = jnp.stack([cf_ref[...], cbk_ref[...]])
        num = (w_inter * bdot(qb, cmat.astype(BF16), (((2,), (1,)), ((0,), (0,))))
               + bdot(sw.astype(BF16), vb, (((2,), (1,)), ((0,), (0,)))))
        den = (w_inter * jnp.sum(qb.astype(F32) * n, axis=2, keepdims=True)
               + jnp.sum(sw, axis=2, keepdims=True))
        h = num / jnp.maximum(jnp.abs(den), jnp.exp(-m_t))
        b_tot = jnp.sum(lf_row, axis=2, keepdims=True)
        g_col = b_tot - b_col + ii_col
        m_new = jnp.maximum(b_tot + m, jnp.max(g_col, axis=1, keepdims=True))
        decay = jnp.exp(b_tot + m - m_new)
        kw = kb.astype(F32) * jnp.exp(g_col - (m_new - ln_scale))
        c_new = decay * cmat + bdot(kw.astype(BF16), vb, (((1,), (1,)), ((0,), (0,))))
        n_new = decay * n + jnp.sum(kw, axis=1, keepdims=True)
        hf_ref[sl_f, :] = h[0]
        hb_ref[sl_b, :] = h[1]
        cf_ref[...] = c_new[0]
        cbk_ref[...] = c_new[1]
        return n_new, m_new

    lax.fori_loop(0, nc, body, (jnp.zeros((2, 1, dk), F32), jnp.zeros((2, 1, 1), F32)))

    def finish(c, carry):
        sl = pl.ds(pl.multiple_of(c * L, L), L)
        hs = hf_ref[sl, :] + hb_ref[sl, :]
        ms = jnp.mean(hs * hs, axis=-1, keepdims=True)
        o_ref[0, sl, :] = (hs * lax.rsqrt(ms + EPS) * og_ref[0]).astype(o_ref.dtype)
        return carry

    lax.fori_loop(0, nc, finish, 0)


def _mlstm_scan(q, k, v, pre, outnorm_g):
    b, s, c = q.shape
    hh = ML_HEADS
    dk = c // hh
    L = min(ML_CHUNK, s)
    nc = s // L
    idx = jnp.array([[0 * 2 * hh + h, 0 * 2 * hh + hh + h, 2 * hh + h, 2 * hh + hh + h] for h in range(hh)])
    g = pre[:, :, idx]
    gcol = g.transpose(0, 2, 1, 3).reshape(b, hh, nc, L, 4)
    grow = gcol.transpose(0, 1, 2, 4, 3)
    head = lambda i, j: (i, 0, j)
    return pl.pallas_call(
        functools.partial(_mlstm_scan_kernel, L=L, nc=nc, dk=dk),
        grid=(b, hh),
        in_specs=[
            pl.BlockSpec((1, s, dk), head),
            pl.BlockSpec((1, s, dk), head),
            pl.BlockSpec((1, s, dk), head),
            pl.BlockSpec((1, 1, nc, L, 4), lambda i, j: (i, j, 0, 0, 0)),
            pl.BlockSpec((1, 1, nc, 4, L), lambda i, j: (i, j, 0, 0, 0)),
            pl.BlockSpec((1, 1, dk), lambda i, j: (j, 0, 0)),
        ],
        out_specs=pl.BlockSpec((1, s, dk), head),
        out_shape=jax.ShapeDtypeStruct((b, s, c), BF16),
        scratch_shapes=[pltpu.VMEM((dk, dk), F32), pltpu.VMEM((dk, dk), F32),
                        pltpu.VMEM((s, dk), F32), pltpu.VMEM((s, dk), F32)],
        compiler_params=_params("parallel", "parallel"),
    )(q, k, v, gcol, grow, outnorm_g.reshape(hh, 1, dk))


def _mlstm_out_kernel(hn_ref, xc_ref, z_ref, skip_ref, w_ref, x_ref, gate_ref, o_ref):
    a = (hn_ref[0].astype(F32) + skip_ref[...] * xc_ref[0].astype(F32)) * _silu(z_ref[0].astype(F32))
    y = jnp.dot(a.astype(BF16), w_ref[...], preferred_element_type=F32)
    o_ref[0] = x_ref[0] + gate_ref[0] * y


def _mlstm_out(hn, xc, up, skip, w, x, gate):
    b, s, c = hn.shape
    d = w.shape[1]
    ts = min(ROW_TILE, s)
    row = lambda i, j: (i, j, 0)
    return pl.pallas_call(
        _mlstm_out_kernel,
        grid=(b, s // ts),
        in_specs=[
            pl.BlockSpec((1, ts, c), row),
            pl.BlockSpec((1, ts, c), row),
            pl.BlockSpec((1, ts, c), lambda i, j: (i, j, 1)),
            pl.BlockSpec((1, c), lambda i, j: (0, 0)),
            pl.BlockSpec((c, d), lambda i, j: (0, 0)),
            pl.BlockSpec((1, ts, d), row),
            pl.BlockSpec((1, 1, d), lambda i, j: (i, 0, 0)),
        ],
        out_specs=pl.BlockSpec((1, ts, d), row),
        out_shape=jax.ShapeDtypeStruct((b, s, d), F32),
        compiler_params=_params("parallel", "parallel"),
    )(hn, xc, up, skip.reshape(1, c), w, x, gate)


def _router_kernel(x_ref, g_ref, sc_ref, sh_ref, w_ref, lpc_ref, lpr_ref, wr_ref, stats_ref, cnt_ref, hf_ref,
                   run_ref):
    first = (pl.program_id(0) == 0) & (pl.program_id(1) == 0)

    @pl.when(first)
    def _():
        run_ref[...] = jnp.zeros_like(run_ref)

    hf = _modulated_norm(x_ref[0], g_ref[...], sc_ref[0], sh_ref[0])
    w = w_ref[...]
    w_hi = w.astype(BF16)
    w_lo = (w - w_hi.astype(F32)).astype(BF16)
    hf_hi = hf.astype(BF16)
    hf_ref[0] = hf_hi
    hf_lo = (hf - hf_hi.astype(F32)).astype(BF16)
    nt_dot = functools.partial(lax.dot_general, dimension_numbers=(((1,), (1,)), ((), ())),
                               preferred_element_type=F32)
    logits = nt_dot(w_hi, hf_hi) + (nt_dot(w_lo, hf_hi) + nt_dot(w_hi, hf_lo))
    tm = logits.shape[1]
    nr = ROUTER_ROWS
    lt = logits[:nr]
    row = lax.broadcasted_iota(jnp.int32, (nr, tm), 0).astype(F32)
    neg = -jnp.inf
    big = float(LANES)
    is_g = row < N_GROUPS
    gl = jnp.where(is_g, lt, neg)
    gmax = jnp.max(gl, axis=0, keepdims=True)
    g_sel = jnp.min(jnp.where(gl == gmax, row, big), axis=0, keepdims=True)
    p_g = 1.0 / jnp.sum(jnp.where(is_g, jnp.exp(gl - gmax), 0.0), axis=0, keepdims=True)
    e_row = row - N_GROUPS
    in_grp = (e_row >= g_sel * EXPERTS_PER_GROUP) & (e_row < (g_sel + 1) * EXPERTS_PER_GROUP)
    el = jnp.where(in_grp, lt, neg)
    emax = jnp.max(el, axis=0, keepdims=True)
    i1 = jnp.min(jnp.where(el == emax, e_row, big), axis=0, keepdims=True)
    el2 = jnp.where(e_row == i1, neg, el)
    emax2 = jnp.max(el2, axis=0, keepdims=True)
    i2 = jnp.min(jnp.where(el2 == emax2, e_row, big), axis=0, keepdims=True)
    t2 = jnp.exp(emax2 - emax)
    w1 = p_g / (1.0 + t2)
    w2 = p_g * t2 / (1.0 + t2)
    a = jnp.where(e_row == i1, 1.0, jnp.where(e_row == i2, 1.0, 0.0)).astype(BF16)
    r = lax.broadcasted_iota(jnp.int32, (tm, tm), 0)
    cc = lax.broadcasted_iota(jnp.int32, (tm, tm), 1)
    before = jnp.where(r < cc, 1.0, 0.0).astype(BF16)
    rank = jnp.dot(a, before, preferred_element_type=F32)
    a_full = jnp.concatenate([a, jnp.zeros((LANES - nr, tm), BF16)], axis=0)
    cnt = nt_dot(jnp.ones((SUBLANES, tm), BF16), a_full)[0:1]
    k8 = jnp.ceil(cnt * (1.0 / ROW_GROUP))
    ur = lax.broadcasted_iota(jnp.int32, (LANES, LANES), 0)
    uc = lax.broadcasted_iota(jnp.int32, (LANES, LANES), 1)
    upper = jnp.where(ur < uc, 1.0, 0.0).astype(BF16)
    k8_rows = jnp.broadcast_to(k8, (SUBLANES, LANES)).astype(BF16)
    off_rows = jnp.dot(k8_rows, upper, preferred_element_type=F32) * ROW_GROUP
    off = off_rows[0:1]
    off_col = off_rows.T[:nr, 0:1]
    pos = off_col + rank
    lp0 = jnp.sum(jnp.where(e_row == i1, pos, 0.0), axis=0, keepdims=True)
    lp1 = jnp.sum(jnp.where(e_row == i2, pos, 0.0), axis=0, keepdims=True)
    run_old = run_ref[...]
    run_new = run_old + k8 * ROW_GROUP
    run_ref[...] = run_new
    cnt_ref[...] = run_new.astype(jnp.int32)
    srow = lax.broadcasted_iota(jnp.int32, (SUBLANES, LANES), 0)
    stats_ref[0] = jnp.where(srow == 0, k8, jnp.where(srow == 1, off, jnp.where(
        srow == 2, run_old, 0.0))).astype(jnp.int32)
    trow = lax.broadcasted_iota(jnp.int32, (SUBLANES, tm), 0)
    lpr_ref[0, 0] = jnp.where(trow == 0, lp0, jnp.where(trow == 1, lp1, 0.0)).astype(jnp.int32)
    wr_ref[0, 0] = jnp.where(trow == 0, w1, jnp.where(trow == 1, w2, 0.0))
    frow = lax.broadcasted_iota(jnp.int32, (LANES, tm), 0)
    lpc_ref[0] = jnp.where(frow == 0, lp0, jnp.where(frow == 1, lp1, 0.0)).T.astype(jnp.int32)


def _router(x, g, sc, sh, w_group, w_router):
    b, s, d = x.shape
    ts = min(MOE_TOKEN_TILE, s)
    nt = s // ts
    w = jnp.concatenate([w_group, w_router], axis=1).T
    w = jnp.pad(w, ((0, LANES - w.shape[0]), (0, 0)))
    row = lambda i, j: (i, j, 0)
    return pl.pallas_call(
        _router_kernel,
        grid=(b, nt),
        in_specs=[
            pl.BlockSpec((1, ts, d), row),
            pl.BlockSpec((1, d), lambda i, j: (0, 0)),
            pl.BlockSpec((1, 1, d), lambda i, j: (i, 0, 0)),
            pl.BlockSpec((1, 1, d), lambda i, j: (i, 0, 0)),
            pl.BlockSpec((LANES, d), lambda i, j: (0, 0)),
        ],
        out_specs=[
            pl.BlockSpec((1, ts, LANES), row),
            pl.BlockSpec((1, 1, SUBLANES, ts), lambda i, j: (i, j, 0, 0)),
            pl.BlockSpec((1, 1, SUBLANES, ts), lambda i, j: (i, j, 0, 0)),
            pl.BlockSpec((1, SUBLANES, LANES), lambda i, j: (i * nt + j, 0, 0)),
            pl.BlockSpec((1, LANES), lambda i, j: (0, 0)),
            pl.BlockSpec((1, ts, d), row),
        ],
        out_shape=[jax.ShapeDtypeStruct((b, s, LANES), jnp.int32),
                   jax.ShapeDtypeStruct((b, nt, SUBLANES, ts), jnp.int32),
                   jax.ShapeDtypeStruct((b, nt, SUBLANES, ts), F32),
                   jax.ShapeDtypeStruct((b * nt, SUBLANES, LANES), jnp.int32),
                   jax.ShapeDtypeStruct((1, LANES), jnp.int32),
                   jax.ShapeDtypeStruct((b, s, d), BF16)],
        scratch_shapes=[pltpu.VMEM((1, LANES), F32)],
        compiler_params=_params("arbitrary", "arbitrary"),
    )(x, g.reshape(1, d), sc, sh, w)


def _chunk_copies(meta_ref, blk, local_ref, hbm_ref, sem, *, to_hbm, start):
    def piece(lo, hi, rows):
        loc = local_ref.at[pl.ds(pl.multiple_of(lo, ROW_GROUP), rows)]
        far = hbm_ref.at[pl.ds(pl.multiple_of(hi, ROW_GROUP), rows)]
        cp = pltpu.make_async_copy(loc, far, sem) if to_hbm else pltpu.make_async_copy(far, loc, sem)
        if start:
            cp.start()
        else:
            cp.wait()

    def per_expert(e, carry):
        at = blk * LANES + e
        k = meta_ref[at]
        lo = meta_ref[at + N_EXPERTS]
        hi = meta_ref[at + 2 * N_EXPERTS]
        pairs = k >> 1

        def pair(i, c):
            piece(lo + i * (2 * ROW_GROUP), hi + i * (2 * ROW_GROUP), 2 * ROW_GROUP)
            return c

        lax.fori_loop(0, pairs, pair, 0)

        @pl.when((k & 1) == 1)
        def _():
            piece(lo + pairs * (2 * ROW_GROUP), hi + pairs * (2 * ROW_GROUP), ROW_GROUP)

        return carry

    lax.fori_loop(0, N_EXPERTS, per_expert, 0)


def _dispatch_kernel(last_ref, meta_ref, lpr_ref, wr_ref, hf_ref, xs_ref, loc_ref, zero_ref, sem, zsem):
    tm = hf_ref.shape[1]
    zrows = zero_ref.shape[0]
    blk = pl.program_id(0) * pl.num_programs(1) + pl.program_id(1)
    n_blk = pl.num_programs(0) * pl.num_programs(1)
    slot = blk % 2

    @pl.when((pl.program_id(0) == 0) & (pl.program_id(1) == 0))
    def _():
        zero_ref[...] = jnp.zeros_like(zero_ref)

        def fill(e):
            start = pl.multiple_of(last_ref[e], zrows)
            return pltpu.make_async_copy(zero_ref, xs_ref.at[pl.ds(start, zrows)], zsem)

        for e in range(N_EXPERTS):
            @pl.when(last_ref[e] >= 0)
            def _():
                fill(e).start()
        for e in range(N_EXPERTS):
            @pl.when(last_ref[e] >= 0)
            def _():
                fill(e).wait()

        def tail(k):
            start = pl.multiple_of(k * zrows, zrows)
            return pltpu.make_async_copy(zero_ref, xs_ref.at[pl.ds(start, zrows)], zsem)

        n_used = last_ref[N_EXPERTS]
        n_tiles = xs_ref.shape[0] // zrows
        lax.fori_loop(n_used, n_tiles, lambda k, c: (tail(k).start(), c)[1], 0)
        lax.fori_loop(n_used, n_tiles, lambda k, c: (tail(k).wait(), c)[1], 0)

    hf = hf_ref[0]
    d = hf.shape[1]
    n_loc = loc_ref.shape[1]
    rows = lax.broadcasted_iota(jnp.int32, (n_loc, tm), 0)
    pick0 = rows == lpr_ref[0, 0, 0:1, :]
    pick1 = rows == lpr_ref[0, 0, 1:2, :]
    sel = jnp.where(pick0, 1.0, jnp.where(pick1, 1.0, 0.0)).astype(BF16)
    loc_ref[slot, :, :d] = jnp.dot(sel, hf, preferred_element_type=F32).astype(BF16)
    wrow = jnp.sum(jnp.where(pick0, wr_ref[0, 0, 0:1, :], jnp.where(pick1, wr_ref[0, 0, 1:2, :], 0.0)),
                   axis=1, keepdims=True)
    w_hi = wrow.astype(BF16).astype(F32)
    lane = lax.broadcasted_iota(jnp.int32, (n_loc, LANES), 1)
    loc_ref[slot, :, d:] = jnp.where(lane == 0, w_hi, jnp.where(lane == 1, wrow - w_hi, 0.0)).astype(BF16)

    def copies(b, sl, start):
        _chunk_copies(meta_ref, b, loc_ref.at[sl], xs_ref, sem.at[sl], to_hbm=True, start=start)

    copies(blk, slot, True)

    @pl.when(blk > 0)
    def _():
        copies(blk - 1, 1 - slot, False)

    @pl.when(blk == n_blk - 1)
    def _():
        copies(blk, slot, False)


def _local_rows(tm):
    return 2 * tm + N_EXPERTS * ROW_GROUP


def _dispatch(hf, meta, lpr, wr, last_tile_row, n_rows):
    b, s, d = hf.shape
    tm = min(MOE_TOKEN_TILE, s)
    nt = s // tm
    dw = d + LANES
    row = lambda i, j, last, meta: (i, j, 0)
    grid_spec = pltpu.PrefetchScalarGridSpec(
        num_scalar_prefetch=2,
        grid=(b, nt),
        in_specs=[
            pl.BlockSpec((1, 1, SUBLANES, tm), lambda i, j, last, meta: (i, j, 0, 0)),
            pl.BlockSpec((1, 1, SUBLANES, tm), lambda i, j, last, meta: (i, j, 0, 0)),
            pl.BlockSpec((1, tm, d), row),
        ],
        out_specs=pl.BlockSpec(memory_space=pl.ANY),
        scratch_shapes=[pltpu.VMEM((2, _local_rows(tm), dw), BF16), pltpu.VMEM((MOE_ROW_TILE, dw), BF16),
                        pltpu.SemaphoreType.DMA((2,)), pltpu.SemaphoreType.DMA(())],
    )
    return pl.pallas_call(
        _dispatch_kernel,
        grid_spec=grid_spec,
        out_shape=jax.ShapeDtypeStruct((n_rows, dw), BF16),
        compiler_params=_params("arbitrary", "arbitrary"),
    )(last_tile_row, meta, lpr, wr, hf)


def _expert_kernel(te_ref, nu_ref, xs_ref, w1_ref, w3_ref, w2_ref, ys_ref, w1b_ref, w3b_ref, w2b_ref):
    i = pl.program_id(0)

    @pl.when((i == 0) | (te_ref[i] != te_ref[jnp.maximum(i - 1, 0)]))
    def _():
        w1b_ref[...] = w1_ref[0, 0].astype(BF16)
        w3b_ref[...] = w3_ref[0, 0].astype(BF16)
        w2b_ref[...] = w2_ref[0, 0].astype(BF16)

    @pl.when(i < nu_ref[0])
    def _():
        d = w1b_ref.shape[0]
        xb = xs_ref[:, :d]
        gate = xs_ref[:, d:d + 1].astype(F32) + xs_ref[:, d + 1:d + 2].astype(F32)
        a = jnp.dot(xb, w1b_ref[...], preferred_element_type=F32)
        bb = jnp.dot(xb, w3b_ref[...], preferred_element_type=F32)
        hmid = (_silu(a) * bb * gate).astype(BF16)
        ys_ref[...] = jnp.dot(hmid, w2b_ref[...], preferred_element_type=F32).astype(ys_ref.dtype)

    @pl.when(i >= nu_ref[0])
    def _():
        ys_ref[...] = jnp.zeros_like(ys_ref)


def _expert_ffn(xs, tile_expert, n_used, w1, w3, w2, layer):
    p, dw = xs.shape
    d, de = w1.shape[2], w1.shape[3]
    tm = MOE_ROW_TILE
    grid_spec = pltpu.PrefetchScalarGridSpec(
        num_scalar_prefetch=2,
        grid=(p // tm,),
        in_specs=[
            pl.BlockSpec((tm, dw), lambda i, te, nu: (jnp.maximum(jnp.minimum(i, nu[0] - 1), 0), 0)),
            pl.BlockSpec((1, 1, d, de), lambda i, te, nu: (layer, te[i], 0, 0)),
            pl.BlockSpec((1, 1, d, de), lambda i, te, nu: (layer, te[i], 0, 0)),
            pl.BlockSpec((1, 1, de, d), lambda i, te, nu: (layer, te[i], 0, 0)),
        ],
        out_specs=pl.BlockSpec((tm, d), lambda i, te, nu: (i, 0)),
        scratch_shapes=[pltpu.VMEM((d, de), BF16), pltpu.VMEM((d, de), BF16), pltpu.VMEM((de, d), BF16)],
    )
    return pl.pallas_call(
        _expert_kernel,
        grid_spec=grid_spec,
        out_shape=jax.ShapeDtypeStruct((p, d), BF16),
        compiler_params=_params("arbitrary"),
    )(tile_expert, n_used, xs, w1, w3, w2)


def _combine_kernel(meta_ref, ys_ref, lpc_ref, x_ref, gate_ref, o_ref, loc_ref, sem):
    tm = x_ref.shape[1]
    n_loc = loc_ref.shape[1]
    blk = pl.program_id(0) * pl.num_programs(1) + pl.program_id(1)
    n_blk = pl.num_programs(0) * pl.num_programs(1)
    slot = blk % 2

    def fetch(b, sl):
        loc_ref[sl, 2 * tm:, :] = jnp.zeros((n_loc - 2 * tm, loc_ref.shape[2]), BF16)
        _chunk_copies(meta_ref, b, loc_ref.at[sl], ys_ref, sem.at[sl], to_hbm=False, start=True)

    @pl.when(blk == 0)
    def _():
        fetch(blk, slot)

    @pl.when(blk + 1 < n_blk)
    def _():
        fetch(blk + 1, 1 - slot)

    _chunk_copies(meta_ref, blk, loc_ref.at[slot], ys_ref, sem.at[slot], to_hbm=False, start=False)
    cols = lax.broadcasted_iota(jnp.int32, (tm, n_loc), 1)
    lp = lpc_ref[0]
    sel = jnp.where(cols == lp[:, 0:1], 1.0, jnp.where(cols == lp[:, 1:2], 1.0, 0.0)).astype(BF16)
    y = jnp.dot(sel, loc_ref[slot], preferred_element_type=F32)
    o_ref[0] = x_ref[0] + gate_ref[0] * y


def _combine(ys, meta, lpc, x, gate):
    b, s, d = x.shape
    tm = min(MOE_TOKEN_TILE, s)
    nt = s // tm
    row = lambda i, j, meta: (i, j, 0)
    grid_spec = pltpu.PrefetchScalarGridSpec(
        num_scalar_prefetch=1,
        grid=(b, nt),
        in_specs=[
            pl.BlockSpec(memory_space=pl.ANY),
            pl.BlockSpec((1, tm, LANES), row),
            pl.BlockSpec((1, tm, d), row),
            pl.BlockSpec((1, 1, d), lambda i, j, meta: (i, 0, 0)),
        ],
        out_specs=pl.BlockSpec((1, tm, d), row),
        scratch_shapes=[pltpu.VMEM((2, _local_rows(tm), d), BF16), pltpu.SemaphoreType.DMA((2,))],
    )
    return pl.pallas_call(
        _combine_kernel,
        grid_spec=grid_spec,
        out_shape=jax.ShapeDtypeStruct((b, s, d), F32),
        compiler_params=_params("arbitrary", "arbitrary"),
    )(meta, ys, lpc, x, gate)


def _hier_moe_residual(x, g, sc, sh, gate, w_group, w_router, w1, w3, w2, layer):
    b, s, d = x.shape
    t = b * s
    lpc, lpr, wr, stats, cnt, hf = _router(x, g, sc, sh, w_group, w_router)
    tm = MOE_ROW_TILE
    ex = slice(N_GROUPS, N_GROUPS + N_EXPERTS)
    counts = cnt[0, ex]
    tiles = (counts + tm - 1) // tm
    tile_end = jnp.cumsum(tiles)
    base = (tile_end - tiles) * tm
    n_blocks = stats.shape[0]
    n_tiles = (2 * t + n_blocks * N_EXPERTS * ROW_GROUP) // tm + N_EXPERTS
    tile_expert = jnp.minimum(
        jnp.sum(tile_end[None, :] <= jnp.arange(n_tiles)[:, None], axis=1), N_EXPERTS - 1).astype(jnp.int32)
    n_used = tile_end[-1:].astype(jnp.int32)
    meta = jnp.concatenate([stats[:, 0, ex], stats[:, 1, ex], base[None, :] + stats[:, 2, ex],
                            jnp.zeros((n_blocks, LANES - 3 * N_EXPERTS), jnp.int32)], axis=1)
    meta = meta.astype(jnp.int32).reshape(n_blocks * LANES)
    last_tile_row = jnp.where(tiles > 0, (tile_end - 1) * tm, -1).astype(jnp.int32)
    last_tile_row = jnp.concatenate([last_tile_row, n_used])
    xs = _dispatch(hf, meta, lpr, wr, last_tile_row, n_tiles * tm)
    ys = _expert_ffn(xs, tile_expert, n_used, w1, w3, w2, layer)
    return _combine(ys, meta, lpc, x, gate)


def kernel(x, c, rel_table, ada_w, ada_b, norm_mix_g, norm_ffn_g, da_w_in, da_w_out, da_q_gain, da_k_gain, da_lam_q1, da_lam_k1, da_lam_q2, da_lam_k2, da_subln_g, ml_w_in, ml_conv_w, ml_conv_b, ml_wq, ml_wk, ml_wv, ml_gate_w, ml_gate_b, ml_outnorm_g, ml_skip, ml_w_out, moe_w_group, moe_w_router, moe_w1, moe_w3, moe_w2):
    depth = ada_w.shape[0]
    d = x.shape[-1]
    mod = _ada_mod(c, ada_w, ada_b)
    for i in range(depth):
        sh1, sc1, g1, sh2, sc2, g2 = [mod[i, :, None, k * d:(k + 1) * d] for k in range(6)]
        j = i // N_MIXERS
        if i % N_MIXERS == 0:
            qkv = _prenorm_matmul(x, norm_mix_g[i], sc1, sh1, da_w_in[j].astype(BF16),
                                  head_gain=_qk_head_gains(da_q_gain[j], da_k_gain[j]), half=DA_HEAD_DIM)
            lam_init = 0.8 - 0.6 * math.exp(-0.3 * i)
            lam = (jnp.exp(jnp.sum(da_lam_q1[j] * da_lam_k1[j])) - jnp.exp(jnp.sum(da_lam_q2[j] * da_lam_k2[j]))
                   + lam_init).astype(F32)
            o = _diff_attention(qkv, da_subln_g[j], lam, lam_init, rel_table)
            x = _proj_residual(o, da_w_out[j].astype(BF16), x, g1)
        else:
            up = _prenorm_matmul(x, norm_mix_g[i], sc1, sh1, ml_w_in[j].astype(BF16))
            q, k, v, xc, pre = _mlstm_pre(up, ml_conv_w[j], ml_conv_b[j], ml_wq[j], ml_wk[j], ml_wv[j],
                                          ml_gate_w[j], ml_gate_b[j])
            hn = _mlstm_scan(q, k, v, pre, ml_outnorm_g[j])
            x = _mlstm_out(hn, xc, up, ml_skip[j], ml_w_out[j].astype(BF16), x, g1)
        x = _hier_moe_residual(x, norm_ffn_g[i], sc2, sh2, g2, moe_w_group[i], moe_w_router[i],
                               moe_w1, moe_w3, moe_w2, i)
    return x
```
